```python
import functools
import jax
import jax.numpy as jnp
from jax import lax
import numpy as np

D_MODEL = 1024
BATCH = 16
SEQ = 4096
DEPTH = 1

GRID_W = 64
CTX_LEN = 256
D_MIX = D_MODEL
GLA_WIDTH = D_MIX // 2
GLA_HEADS = 4
GLA_DV = GLA_WIDTH // GLA_HEADS
GLA_DK = GLA_DV // 2
GLA_QK = GLA_HEADS * GLA_DK
GATE_RANK = 16
GATE_NORMALIZER = 16.0
CHUNK = 64
CONV_WIDTH = D_MIX - GLA_WIDTH
CONV_K = 3
N_EXPERTS = 32
TOP_K = 4
D_FF = D_MODEL
SWIGLU_LIMIT = 7.0
SWIGLU_ALPHA = 1.702
MOE_BLOCK = 256
EPS = 1e-6

Q_END = GLA_QK
K_END = Q_END + GLA_QK
V_END = K_END + GLA_WIDTH
R_END = V_END + GLA_WIDTH
AF_END = R_END + GATE_RANK
AB_END = AF_END + GATE_RANK
B_END = AB_END + CONV_WIDTH
C_END = B_END + CONV_WIDTH
PROJ_WIDTH = C_END + CONV_WIDTH
PROJ_SPLITS = (Q_END, K_END, V_END, R_END, AF_END, AB_END, B_END, C_END)

kernel_name = 'hybrid_gla_shortconv_moe_dit'


def rms_norm(x, w):
    x32 = x.astype(jnp.float32)
    y = x32 * lax.rsqrt(jnp.mean(x32 * x32, axis=-1, keepdims=True) + EPS)
    return (y * w.astype(jnp.float32)).astype(x.dtype)


def modulate(x, shift, scale):
    return x * (1.0 + scale[:, None, :]) + shift[:, None, :]


def rev(t):
    return jnp.flip(t, axis=1)


def heads(t, d):
    bsz, length, _ = t.shape
    return t.reshape(bsz, length, GLA_HEADS, d).astype(jnp.float32)


def zero_state(bsz):
    return jnp.zeros((bsz, GLA_HEADS, GLA_DK, GLA_DV), jnp.float32)


def log_forget_gate(z_low, up, bias):
    bsz, length, _ = z_low.shape
    z = (z_low @ up + bias).astype(jnp.float32)
    return (jax.nn.log_sigmoid(z) / GATE_NORMALIZER).reshape(bsz, length, GLA_HEADS, GLA_DK)


def to_chunks(t):
    bsz, length, nh, d = t.shape
    return t.reshape(bsz, length // CHUNK, CHUNK, nh, d)


def gla_states(k, v, log_a, s0):
    k, v, log_a = to_chunks(k), to_chunks(v), to_chunks(log_a)
    b = jnp.cumsum(log_a, axis=2)
    b_end = b[:, :, -1]
    k_to_end = k * jnp.exp(b_end[:, :, None] - b)
    chunk_kv = jnp.einsum('bnshd,bnshv->bnhdv', k_to_end, v)
    decay = jnp.exp(b_end)

    def step(s, inp):
        dec, kv = inp
        return dec[..., None] * s + kv, s

    s_final, s_start = lax.scan(step, s0, (jnp.moveaxis(decay, 1, 0), jnp.moveaxis(chunk_kv, 1, 0)))
    return b, jnp.moveaxis(s_start, 0, 1), s_final


def gla_direction(q, k, v, log_a, s0):
    bsz, length = q.shape[:2]
    b, s_start, s_final = gla_states(k, v, log_a, s0)
    qc, kc, vc = to_chunks(q), to_chunks(k), to_chunks(v)
    q_dec = qc * jnp.exp(b)
    k_inv = kc * jnp.exp(-b)
    within = jnp.tril(jnp.ones((CHUNK, CHUNK), dtype=bool))
    scores = jnp.where(within, jnp.einsum('bnthd,bnshd->bnhts', q_dec, k_inv), 0.0)
    intra = jnp.einsum('bnhts,bnshv->bnthv', scores, vc)
    inter = jnp.einsum('bnthd,bnhdv->bnthv', q_dec, s_start)
    return (intra + inter).reshape(bsz, length, GLA_HEADS, GLA_DV), s_final


def gla_mixer(pq, pk, pv, pr, paf, pab, a_up_f, a_bias_f, a_up_b, a_bias_b, gla_norm_w, s0_f, s0_b):
    bsz, length = pq.shape[:2]
    q = heads(pq, GLA_DK) * (GLA_DK ** -0.5)
    k = heads(pk, GLA_DK)
    v = heads(pv, GLA_DV)
    la_f = log_forget_gate(paf, a_up_f, a_bias_f)
    la_b = log_forget_gate(pab, a_up_b, a_bias_b)
    o_f, s_f = gla_direction(q, k, v, la_f, s0_f)
    o_b, s_b = gla_direction(rev(q), rev(k), rev(v), rev(la_b), s0_b)
    o = rms_norm(o_f + rev(o_b), gla_norm_w)
    y = o.reshape(bsz, length, GLA_WIDTH) * jax.nn.silu(pr.astype(jnp.float32))
    return y.astype(pr.dtype), s_f, s_b


def depthwise_conv(u, w):
    return lax.conv_general_dilated(
        u, w[:, None, :].astype(u.dtype), window_strides=(1,), padding='SAME',
        dimension_numbers=('NWC', 'WIO', 'NWC'), feature_group_count=u.shape[-1])


def conv_rows(u, w, rows):
    bsz, length, ch = u.shape
    return depthwise_conv(u.reshape(bsz * rows, GRID_W, ch), w).reshape(bsz, length, ch)


def token_mixer(hn, conv_fn, s0_f, s0_b, w_in, a_up_f, a_bias_f, a_up_b, a_bias_b, gla_norm_w, conv_w, w_out):
    p = hn @ w_in
    pq, pk, pv, pr, paf, pab, p_b, p_c, p_x = jnp.split(p, PROJ_SPLITS, axis=-1)
    y_gla, s_f, s_b = gla_mixer(pq, pk, pv, pr, paf, pab, a_up_f, a_bias_f, a_up_b, a_bias_b,
                                gla_norm_w, s0_f, s0_b)
    y_conv = p_b * conv_fn(p_c * p_x, conv_w)
    y = jnp.concatenate([y_gla, y_conv], axis=-1) @ w_out
    return y, s_f, s_b


def context_states(hn, w_in, a_up_f, a_bias_f, a_up_b, a_bias_b):
    k = heads(hn @ w_in[:, Q_END:K_END], GLA_DK)
    v = heads(hn @ w_in[:, K_END:V_END], GLA_DV)
    la_f = log_forget_gate(hn @ w_in[:, R_END:AF_END], a_up_f, a_bias_f)
    la_b = log_forget_gate(hn @ w_in[:, AF_END:AB_END], a_up_b, a_bias_b)
    s0 = zero_state(hn.shape[0])
    s_f = gla_states(k, v, la_f, s0)[2]
    s_b = gla_states(rev(k), rev(v), rev(la_b), s0)[2]
    return s_f, s_b


def channel_mixer(hn, router_w, router_b, w_gu, b_gu, w_dn, b_dn):
    bsz, length, d = hn.shape
    xt = hn.reshape(-1, d)
    n_tok = xt.shape[0]
    logits = (xt @ router_w + router_b).astype(jnp.float32)
    top_val, top_idx = lax.top_k(logits, TOP_K)
    weights = jax.nn.softmax(top_val, axis=-1).astype(hn.dtype)
    flat_e = top_idx.reshape(-1)
    n_assign = n_tok * TOP_K
    order = jnp.argsort(flat_e)
    sorted_e = flat_e[order]
    sorted_tok = (order // TOP_K).astype(jnp.int32)
    sorted_w = weights.reshape(-1)[order]
    counts = jnp.bincount(flat_e, length=N_EXPERTS)
    padded = (counts + MOE_BLOCK - 1) // MOE_BLOCK * MOE_BLOCK
    pad_end = jnp.cumsum(padded)
    pad_start = pad_end - padded
    grp_start = jnp.cumsum(counts) - counts
    slot = pad_start[sorted_e] + (jnp.arange(n_assign) - grp_start[sorted_e])
    n_blocks = -(-n_assign // MOE_BLOCK) + N_EXPERTS
    n_slots = n_blocks * MOE_BLOCK
    slot_tok = jnp.full((n_slots,), n_tok, jnp.int32).at[slot].set(sorted_tok)
    slot_w = jnp.zeros((n_slots,), hn.dtype).at[slot].set(sorted_w)
    block_e = jnp.minimum(jnp.searchsorted(pad_end, jnp.arange(n_blocks) * MOE_BLOCK, side='right'),
                          N_EXPERTS - 1)
    x_pad = jnp.concatenate([xt, jnp.zeros((1, d), xt.dtype)], axis=0)

    def expert_block(args):
        tok, e = args
        gu = x_pad[tok] @ w_gu[e] + b_gu[e]
        gate = jnp.minimum(gu[:, 0::2], SWIGLU_LIMIT)
        up = jnp.clip(gu[:, 1::2], -SWIGLU_LIMIT, SWIGLU_LIMIT)
        act = (up + 1.0) * (gate * jax.nn.sigmoid(SWIGLU_ALPHA * gate))
        return act @ w_dn[e] + b_dn[e]

    y_slots = lax.map(expert_block, (slot_tok.reshape(n_blocks, MOE_BLOCK), block_e))
    y = jnp.zeros((n_tok + 1, d), hn.dtype).at[slot_tok].add(y_slots.reshape(n_slots, d) * slot_w[:, None])
    return y[:n_tok].reshape(bsz, length, d)


def setup_inputs(seed: int = 0) -> dict:
    key = jax.random.key(seed)
    ks = jax.random.split(key, 23)

    def nrm(k, shape, scale):
        return jax.random.normal(k, shape, jnp.float32) * scale

    return {
        'x': nrm(ks[0], (BATCH, SEQ, D_MODEL), 1.0),
        'c': nrm(ks[1], (BATCH, D_MODEL), 1.0),
        'ctx': nrm(ks[2], (BATCH, CTX_LEN, D_MODEL), 1.0),
        'c_ctx': nrm(ks[3], (D_MODEL,), 1.0),
        'w_ada': nrm(ks[4], (DEPTH, D_MODEL, 6 * D_MODEL), D_MODEL ** -0.5),
        'b_ada': nrm(ks[5], (DEPTH, 6 * D_MODEL), 0.01),
        'norm1_w': 1.0 + nrm(ks[6], (DEPTH, D_MODEL), 0.02),
        'w_in': nrm(ks[7], (DEPTH, D_MODEL, PROJ_WIDTH), D_MODEL ** -0.5),
        'a_up_f': nrm(ks[8], (DEPTH, GATE_RANK, GLA_QK), GATE_RANK ** -0.5),
        'a_bias_f': nrm(ks[9], (DEPTH, GLA_QK), 0.5),
        'a_up_b': nrm(ks[10], (DEPTH, GATE_RANK, GLA_QK), GATE_RANK ** -0.5),
        'a_bias_b': nrm(ks[11], (DEPTH, GLA_QK), 0.5),
        'gla_norm_w': 1.0 + nrm(ks[12], (DEPTH, GLA_DV), 0.02),
        'conv_w': nrm(ks[13], (DEPTH, CONV_K, CONV_WIDTH), CONV_K ** -0.5),
        'w_out': nrm(ks[14], (DEPTH, D_MIX, D_MODEL), D_MIX ** -0.5),
        'norm2_w': 1.0 + nrm(ks[15], (DEPTH, D_MODEL), 0.02),
        'router_w': nrm(ks[16], (DEPTH, D_MODEL, N_EXPERTS), D_MODEL ** -0.5),
        'router_b': nrm(ks[17], (DEPTH, N_EXPERTS), 0.01),
        'w_gu': nrm(ks[18], (DEPTH, N_EXPERTS, D_MODEL, 2 * D_FF), D_MODEL ** -0.5),
        'b_gu': nrm(ks[19], (DEPTH, N_EXPERTS, 2 * D_FF), 0.01),
        'w_dn': nrm(ks[20], (DEPTH, N_EXPERTS, D_FF, D_MODEL), D_FF ** -0.5),
        'b_dn': nrm(ks[21], (DEPTH, N_EXPERTS, D_MODEL), 0.01),
        'final_norm_w': 1.0 + nrm(ks[22], (D_MODEL,), 0.02),
    }


def reference(x, c, ctx, c_ctx, w_ada, b_ada, norm1_w, w_in, a_up_f, a_bias_f, a_up_b, a_bias_b,
              gla_norm_w, conv_w, w_out, norm2_w, router_w, router_b, w_gu, b_gu, w_dn, b_dn,
              final_norm_w):
    rows = x.shape[1] // GRID_W
    latent_conv = functools.partial(conv_rows, rows=rows)
    h, h_ctx = x, ctx
    for l in range(DEPTH):
        mix_w = (w_in[l], a_up_f[l], a_bias_f[l], a_up_b[l], a_bias_b[l], gla_norm_w[l], conv_w[l], w_out[l])
        moe_w = (router_w[l], router_b[l], w_gu[l], b_gu[l], w_dn[l], b_dn[l])
        mod = jax.nn.silu(c) @ w_ada[l] + b_ada[l]
        mod_ctx = (jax.nn.silu(c_ctx) @ w_ada[l] + b_ada[l])[None]
        sh1, sc1, g1, sh2, sc2, g2 = jnp.split(mod, 6, axis=-1)
        csh1, csc1, cg1, csh2, csc2, cg2 = jnp.split(mod_ctx, 6, axis=-1)
        last = l == DEPTH - 1

        hn_ctx = modulate(rms_norm(h_ctx, norm1_w[l]), csh1, csc1)
        if last:
            s_f, s_b = context_states(hn_ctx, *mix_w[:5])
        else:
            zs = zero_state(h_ctx.shape[0])
            y_ctx, s_f, s_b = token_mixer(hn_ctx, depthwise_conv, zs, zs, *mix_w)

        hn = modulate(rms_norm(h, norm1_w[l]), sh1, sc1)
        y, _, _ = token_mixer(hn, latent_conv, s_f, s_b, *mix_w)
        h = h + g1[:, None, :] * y
        h = h + g2[:, None, :] * channel_mixer(modulate(rms_norm(h, norm2_w[l]), sh2, sc2), *moe_w)

        if not last:
            h_ctx = h_ctx + cg1[:, None, :] * y_ctx
            h_ctx = h_ctx + cg2[:, None, :] * channel_mixer(
                modulate(rms_norm(h_ctx, norm2_w[l]), csh2, csc2), *moe_w)
    return rms_norm(h, final_norm_w)
```

```python
import functools

import jax
import jax.numpy as jnp
from jax import lax
from jax.experimental import pallas as pl
from jax.experimental.pallas import tpu as pltpu

D_MODEL = 1024
GLA_HEADS = 4
GLA_DK = 64
GLA_DV = 128
GLA_QK = GLA_HEADS * GLA_DK
GLA_WIDTH = GLA_HEADS * GLA_DV
CONV_WIDTH = 512
GATE_RANK = 16
GATE_PAD = 128
GATE_NORMALIZER = 16.0
CHUNK = 64
GRID_W = 64
N_EXPERTS = 32
TOP_K = 4
SWIGLU_LIMIT = 7.0
SWIGLU_ALPHA = 1.702
EPS = 1e-6

TOKEN_TILE = 512
GLA_BLOCK = 512
INTRA_GROUP = 256
EXPERT_TILE = 512
DISPATCH_TILE = 512
COMBINE_TILE = 256
VMEM_LIMIT = 56 * 1024 * 1024

F32 = jnp.float32
BF16 = jnp.bfloat16
HI_MASK = 0xFFFF0000


def _cparams(*sem):
    return pltpu.CompilerParams(dimension_semantics=sem, vmem_limit_bytes=VMEM_LIMIT)


def _rms(x):
    return x * lax.rsqrt(jnp.mean(x * x, axis=-1, keepdims=True) + EPS)


def _sigmoid(x):
    return 1.0 / (1.0 + jnp.exp(-x))


def _log_sigmoid(z):
    return jnp.minimum(z, 0.0) - jnp.log(1.0 + jnp.exp(-jnp.abs(z)))


def _dot(a, b):
    return jnp.dot(a, b, preferred_element_type=F32)


def _dot_nt(a, b):
    return lax.dot_general(a, b, (((1,), (1,)), ((), ())), preferred_element_type=F32)


def _dot_tn(a, b):
    return lax.dot_general(a, b, (((0,), (0,)), ((), ())), preferred_element_type=F32)


def _split_bf16(a):
    hi = a.astype(BF16)
    lo = (a - hi.astype(F32)).astype(BF16)
    return hi, lo


def _dot3(a, b, dot):
    a_hi, a_lo = _split_bf16(a)
    b_hi, b_lo = _split_bf16(b)
    return dot(a_hi, b_hi) + (dot(a_hi, b_lo) + dot(a_lo, b_hi))


def _segment_scan(x, seg, reverse):
    n = x.shape[0]
    row = lax.broadcasted_iota(jnp.int32, x.shape, 0) & (seg - 1)
    s = 1
    while s < seg:
        if reverse:
            shifted = pltpu.roll(x, n - s, 0)
            x = x + jnp.where(row < seg - s, shifted, 0.0)
        else:
            shifted = pltpu.roll(x, s, 0)
            x = x + jnp.where(row >= s, shifted, 0.0)
        s *= 2
    return x


def _pack_halves(x):
    n = x.shape[1] // 2
    bits = pltpu.bitcast(x.astype(BF16).astype(F32), jnp.uint32)
    return (bits[:, :n] & jnp.uint32(HI_MASK)) | (bits[:, n:] >> 16)


def _unpack_halves(w):
    hi = pltpu.bitcast(w & jnp.uint32(HI_MASK), F32)
    lo = pltpu.bitcast(w << 16, F32)
    return jnp.concatenate([hi, lo], axis=1)


def _mod_kernel(c_ref, w_ref, b_ref, o_ref):
    c = c_ref[...]
    s = c * _sigmoid(c)
    o_ref[...] = _dot3(s, w_ref[...], _dot) + b_ref[...]


def _modulation(cc, w_ada, b_ada):
    rows, d = cc.shape
    n = w_ada.shape[1]
    bn = 1536
    return pl.pallas_call(
        _mod_kernel,
        out_shape=jax.ShapeDtypeStruct((rows, n), F32),
        grid=(n // bn,),
        in_specs=[pl.BlockSpec((rows, d), lambda j: (0, 0)),
                  pl.BlockSpec((d, bn), lambda j: (0, j)),
                  pl.BlockSpec((1, bn), lambda j: (0, j))],
        out_specs=pl.BlockSpec((rows, bn), lambda j: (0, j)),
        compiler_params=_cparams("arbitrary"),
        name="modulation",
    )(cc, w_ada, b_ada)


def _ctx_kernel(x_ref, sh_ref, sc_ref, nw_ref, wkv_ref, wg_ref, afb_ref, bias_ref, sf_ref, sb_ref):
    x = x_ref[...]
    n = x.shape[0]
    hn = _rms(x) * nw_ref[...] * (1.0 + sc_ref[...]) + sh_ref[...]
    hb = hn.astype(BF16)
    kv = _dot(hb, wkv_ref[...])
    g = _dot(hb, wg_ref[...])
    z = _dot(g.astype(BF16), afb_ref[...]) + bias_ref[...]
    la = _log_sigmoid(z) * (1.0 / GATE_NORMALIZER)
    p = _segment_scan(la, n, reverse=False)
    p_f, p_b = p[:, :GLA_QK], p[:, GLA_QK:]
    w_f = jnp.exp(p_f[n - 1:n, :] - p_f)
    w_b = jnp.exp(p_b - la[:, GLA_QK:])
    k = kv[:, :GLA_QK]
    vb = kv[:, GLA_QK:].astype(BF16)
    ke_f = (k * w_f).astype(BF16)
    ke_b = (k * w_b).astype(BF16)
    for h in range(GLA_HEADS):
        vh = vb[:, h * GLA_DV:(h + 1) * GLA_DV]
        sf_ref[h] = _dot_tn(vh, ke_f[:, h * GLA_DK:(h + 1) * GLA_DK])
        sb_ref[h] = _dot_tn(vh, ke_b[:, h * GLA_DK:(h + 1) * GLA_DK])


def _context_states(ctx, csh1, csc1, nw1, wkv, wg, afb, bias_fb):
    bsz, n, d = ctx.shape
    full = lambda a: pl.BlockSpec(a.shape, lambda b: (0,) * a.ndim)
    st = jax.ShapeDtypeStruct((bsz, GLA_HEADS, GLA_DV, GLA_DK), F32)
    st_spec = pl.BlockSpec((None, GLA_HEADS, GLA_DV, GLA_DK), lambda b: (b, 0, 0, 0))
    return pl.pallas_call(
        _ctx_kernel,
        out_shape=(st, st),
        grid=(bsz,),
        in_specs=[pl.BlockSpec((None, n, d), lambda b: (b, 0, 0)),
                  full(csh1), full(csc1), full(nw1), full(wkv), full(wg), full(afb), full(bias_fb)],
        out_specs=(st_spec, st_spec),
        compiler_params=_cparams("arbitrary"),
        name="context_states",
    )(ctx, csh1, csc1, nw1, wkv, wg, afb, bias_fb)


def _proj_kernel(x_ref, sh_ref, sc_ref, nw_ref, wm_ref, wg_ref, afb_ref, bias_ref, cw_ref,
                 af_ref, ab_ref, v_ref, oi_ref, sr_ref, yc_ref, dec_ref):
    x = x_ref[...]
    tm = x.shape[0]
    nc = tm // CHUNK
    hn = _rms(x) * nw_ref[...] * (1.0 + sc_ref[...]) + sh_ref[...]
    hb = hn.astype(BF16)
    p = _dot(hb, wm_ref[...])
    g = _dot(hb, wg_ref[...])
    z = _dot(g.astype(BF16), afb_ref[...]) + bias_ref[...]
    la = _log_sigmoid(z) * (1.0 / GATE_NORMALIZER)

    q = p[:, 0:GLA_QK] * (GLA_DK ** -0.5)
    k = p[:, GLA_QK:2 * GLA_QK]
    vb = p[:, 2 * GLA_QK:2 * GLA_QK + GLA_WIDTH].astype(BF16)
    v_ref[...] = vb

    b_f = _segment_scan(la[:, :GLA_QK], CHUNK, reverse=False)
    b_f3 = b_f.reshape(nc, CHUNK, GLA_QK)
    tot_f = b_f3[:, CHUNK - 1:CHUNK, :]
    qd_f = (q * jnp.exp(b_f)).astype(BF16)
    ki_f = (k * jnp.exp(-b_f)).astype(BF16)
    ke_f = (k * jnp.exp(tot_f - b_f3).reshape(tm, GLA_QK)).astype(BF16)
    b_b = _segment_scan(la[:, GLA_QK:], CHUNK, reverse=True)
    b_b3 = b_b.reshape(nc, CHUNK, GLA_QK)
    tot_b = b_b3[:, 0:1, :]
    qd_b = (q * jnp.exp(b_b)).astype(BF16)
    ki_b = (k * jnp.exp(-b_b)).astype(BF16)
    ke_b = (k * jnp.exp(tot_b - b_b3).reshape(tm, GLA_QK)).astype(BF16)

    af_ref[:, :GLA_QK] = qd_f
    af_ref[:, GLA_QK:] = ke_f
    ab_ref[:, :GLA_QK] = qd_b
    ab_ref[:, GLA_QK:] = ke_b
    dec_ref[:, :GLA_QK] = jnp.exp(tot_f).reshape(nc, GLA_QK)
    dec_ref[:, GLA_QK:] = jnp.exp(tot_b).reshape(nc, GLA_QK)

    gi = INTRA_GROUP
    ri = lax.broadcasted_iota(jnp.int32, (gi, gi), 0)
    ci = lax.broadcasted_iota(jnp.int32, (gi, gi), 1)
    same = (ri // CHUNK) == (ci // CHUNK)
    lower = same & (ci <= ri)
    upper = same & (ci >= ri)
    for gidx in range(tm // gi):
        rows = slice(gidx * gi, (gidx + 1) * gi)
        for h in range(GLA_HEADS):
            lanes = slice(h * GLA_DK, (h + 1) * GLA_DK)
            s_f = _dot_nt(qd_f[rows, lanes], ki_f[rows, lanes])
            s_b = _dot_nt(qd_b[rows, lanes], ki_b[rows, lanes])
            s = jnp.where(lower, s_f, 0.0) + jnp.where(upper, s_b, 0.0)
            oi_ref[rows, h * GLA_DV:(h + 1) * GLA_DV] = _dot(
                s.astype(BF16), vb[rows, h * GLA_DV:(h + 1) * GLA_DV])

    r = p[:, 1024:1536]
    sr_ref[...] = (r * _sigmoid(r)).astype(BF16)

    u = p[:, 2048:2560] * p[:, 2560:3072]
    row = lax.broadcasted_iota(jnp.int32, u.shape, 0) & (GRID_W - 1)
    prev = jnp.where(row >= 1, pltpu.roll(u, 1, 0), 0.0)
    nxt = jnp.where(row < GRID_W - 1, pltpu.roll(u, tm - 1, 0), 0.0)
    cw = cw_ref[...]
    conv = prev * cw[0:1, :] + u * cw[1:2, :] + nxt * cw[2:3, :]
    yc_ref[...] = (p[:, 1536:2048] * conv).astype(BF16)


def _projection(x2, sh1, sc1, nw1, wm, wg, afb, bias_fb, conv_w, seq):
    n, d = x2.shape
    tm = TOKEN_TILE
    tpb = seq // tm
    full = lambda a: pl.BlockSpec(a.shape, lambda i: (0,) * a.ndim)
    per_batch = pl.BlockSpec((None, 1, d), lambda i: (i // tpb, 0, 0))
    tok = lambda w: pl.BlockSpec((tm, w), lambda i: (i, 0))
    out_shape = (
        jax.ShapeDtypeStruct((n, 2 * GLA_QK), BF16),
        jax.ShapeDtypeStruct((n, 2 * GLA_QK), BF16),
        jax.ShapeDtypeStruct((n, GLA_WIDTH), BF16),
        jax.ShapeDtypeStruct((n, GLA_WIDTH), F32),
        jax.ShapeDtypeStruct((n, GLA_WIDTH), BF16),
        jax.ShapeDtypeStruct((n, CONV_WIDTH), BF16),
        jax.ShapeDtypeStruct((n // CHUNK, 2 * GLA_QK), F32),
    )
    out_specs = (tok(2 * GLA_QK), tok(2 * GLA_QK), tok(GLA_WIDTH), tok(GLA_WIDTH), tok(GLA_WIDTH),
                 tok(CONV_WIDTH), pl.BlockSpec((tm // CHUNK, 2 * GLA_QK), lambda i: (i, 0)))
    return pl.pallas_call(
        _proj_kernel,
        out_shape=out_shape,
        grid=(n // tm,),
        in_specs=[tok(d), per_batch, per_batch, full(nw1), full(wm), full(wg), full(afb),
                  full(bias_fb), full(conv_w)],
        out_specs=out_specs,
        compiler_params=_cparams("arbitrary"),
        name="projection",
    )(x2, sh1, sc1, nw1, wm, wg, afb, bias_fb, conv_w)


def _scan_kernel(af_ref, vf_ref, df_ref, ab_ref, vb_ref, db_ref, s0f_ref, s0b_ref,
                 of_ref, ob_ref, st_ref):
    @pl.when(pl.program_id(1) == 0)
    def _():
        st_ref[0] = s0f_ref[...]
        st_ref[1] = s0b_ref[...]

    nc = af_ref.shape[0] // CHUNK

    def one_chunk(c, a_ref, v_ref, d_ref, o_ref, direction):
        r0 = pl.multiple_of(c * CHUNK, CHUNK)
        rows = pl.ds(r0, CHUNK)
        qd = a_ref[rows, 0:GLA_QK]
        ke = a_ref[rows, GLA_QK:2 * GLA_QK]
        v = v_ref[rows, :]
        d = d_ref[pl.ds(c, 1), direction * GLA_QK:(direction + 1) * GLA_QK]
        for h in range(GLA_HEADS):
            lanes = slice(h * GLA_DK, (h + 1) * GLA_DK)
            vl = slice(h * GLA_DV, (h + 1) * GLA_DV)
            s = st_ref[direction, h]
            o_ref[rows, vl] = _dot_nt(qd[:, lanes], s.astype(BF16))
            st_ref[direction, h] = s * d[:, lanes] + _dot_tn(v[:, vl], ke[:, lanes])

    def body(c, carry):
        one_chunk(c, af_ref, vf_ref, df_ref, of_ref, 0)
        one_chunk(nc - 1 - c, ab_ref, vb_ref, db_ref, ob_ref, 1)
        return carry

    lax.fori_loop(0, nc, body, 0)


def _gla_scan(a_f, a_b, vb, dec, s0f, s0b, bsz, seq):
    n = a_f.shape[0]
    tb = GLA_BLOCK
    nb = seq // tb
    fwd = lambda w: pl.BlockSpec((tb, w), lambda b, j: (b * nb + j, 0))
    bwd = lambda w: pl.BlockSpec((tb, w), lambda b, j: (b * nb + nb - 1 - j, 0))
    dfw = pl.BlockSpec((tb // CHUNK, 2 * GLA_QK), lambda b, j: (b * nb + j, 0))
    dbw = pl.BlockSpec((tb // CHUNK, 2 * GLA_QK), lambda b, j: (b * nb + nb - 1 - j, 0))
    st_spec = pl.BlockSpec((None, GLA_HEADS, GLA_DV, GLA_DK), lambda b, j: (b, 0, 0, 0))
    out = jax.ShapeDtypeStruct((n, GLA_WIDTH), F32)
    return pl.pallas_call(
        _scan_kernel,
        out_shape=(out, out),
        grid=(bsz, nb),
        in_specs=[fwd(2 * GLA_QK), fwd(GLA_WIDTH), dfw, bwd(2 * GLA_QK), bwd(GLA_WIDTH), dbw,
                  st_spec, st_spec],
        out_specs=(fwd(GLA_WIDTH), bwd(GLA_WIDTH)),
        scratch_shapes=[pltpu.VMEM((2, GLA_HEADS, GLA_DV, GLA_DK), F32)],
        compiler_params=_cparams("arbitrary", "arbitrary"),
        name="gla_scan",
    )(a_f, vb, dec, a_b, vb, dec, s0f, s0b)


def _post_kernel(x_ref, oi_ref, of_ref, ob_ref, sr_ref, yc_ref, gnw_ref, wo_ref, g1_ref, nw2_ref,
                 sh2_ref, sc2_ref, rwt_ref, rb_ref, tri_ref,
                 h_ref, xw_ref, idx_ref, rank_ref, wt_ref, cnt_ref, run_ref):
    @pl.when(pl.program_id(0) == 0)
    def _():
        run_ref[...] = jnp.zeros_like(run_ref)

    tm = x_ref.shape[0]
    o = oi_ref[...] + of_ref[...] + ob_ref[...]
    normed = jnp.concatenate(
        [_rms(o[:, h * GLA_DV:(h + 1) * GLA_DV]) for h in range(GLA_HEADS)], axis=1)
    y_gla = (normed * gnw_ref[...] * sr_ref[...].astype(F32)).astype(BF16)
    wo = wo_ref[...]
    y = _dot(y_gla, wo[:GLA_WIDTH, :]) + _dot(yc_ref[...], wo[GLA_WIDTH:, :])
    h = x_ref[...] + g1_ref[...] * y
    h_ref[...] = h
    hn = _rms(h) * nw2_ref[...] * (1.0 + sc2_ref[...]) + sh2_ref[...]
    xw_ref[...] = _pack_halves(hn)

    logits = _dot3(rwt_ref[...], hn, _dot_nt) + rb_ref[...]
    eid = lax.broadcasted_iota(jnp.int32, logits.shape, 0)
    vals, idxs = [], []
    work = logits
    for _ in range(TOP_K):
        m = jnp.max(work, axis=0, keepdims=True)
        i = jnp.min(jnp.where(work == m, eid, N_EXPERTS), axis=0, keepdims=True)
        vals.append(m)
        idxs.append(i)
        work = jnp.where(eid == i, -jnp.inf, work)
    exps = [jnp.exp(v - vals[0]) for v in vals]
    denom = exps[0] + exps[1] + exps[2] + exps[3]
    wts = [e / denom for e in exps]

    onehot = jnp.zeros(logits.shape, F32)
    for i in idxs:
        onehot = onehot + jnp.where(eid == i, 1.0, 0.0)
    before = _dot(onehot.astype(BF16), tri_ref[...]) + run_ref[:, 0:1]
    for kk in range(TOP_K):
        idx_ref[kk:kk + 1, :] = idxs[kk]
        rk = jnp.sum(jnp.where(eid == idxs[kk], before, 0.0), axis=0, keepdims=True)
        rank_ref[kk:kk + 1, :] = rk.astype(jnp.int32)
    run = run_ref[...] + jnp.sum(onehot, axis=1, keepdims=True)
    run_ref[...] = run
    cnt_ref[...] = run.astype(jnp.int32)

    sub = lax.broadcasted_iota(jnp.int32, (GATE_PAD, tm), 0)
    wpad = jnp.zeros((GATE_PAD, tm), F32)
    for kk in range(TOP_K):
        wpad = wpad + jnp.where(sub == kk, wts[kk], 0.0)
    wt_ref[...] = wpad.T


def _post_mixer(x2, o_i, o_f, o_b, sr, yc, gnw, wo, g1, nw2, sh2, sc2, rwt, rb, tri, seq):
    n, d = x2.shape
    tm = TOKEN_TILE
    tpb = seq // tm
    full = lambda a: pl.BlockSpec(a.shape, lambda i: (0,) * a.ndim)
    per_batch = pl.BlockSpec((None, 1, d), lambda i: (i // tpb, 0, 0))
    tok = lambda w: pl.BlockSpec((tm, w), lambda i: (i, 0))
    lane_tok = pl.BlockSpec((TOP_K, tm), lambda i: (0, i))
    out_shape = (
        jax.ShapeDtypeStruct((n, d), F32),
        jax.ShapeDtypeStruct((n, d // 2), jnp.uint32),
        jax.ShapeDtypeStruct((TOP_K, n), jnp.int32),
        jax.ShapeDtypeStruct((TOP_K, n), jnp.int32),
        jax.ShapeDtypeStruct((n, GATE_PAD), F32),
        jax.ShapeDtypeStruct((N_EXPERTS, GATE_PAD), jnp.int32),
    )
    out_specs = (tok(d), tok(d // 2), lane_tok, lane_tok, tok(GATE_PAD),
                 pl.BlockSpec((N_EXPERTS, GATE_PAD), lambda i: (0, 0)))
    return pl.pallas_call(
        _post_kernel,
        out_shape=out_shape,
        grid=(n // tm,),
        in_specs=[tok(d), tok(GLA_WIDTH), tok(GLA_WIDTH), tok(GLA_WIDTH), tok(GLA_WIDTH),
                  tok(CONV_WIDTH), full(gnw), full(wo), per_batch, full(nw2), per_batch, per_batch,
                  full(rwt), full(rb), full(tri)],
        out_specs=out_specs,
        scratch_shapes=[pltpu.VMEM((N_EXPERTS, GATE_PAD), F32)],
        compiler_params=_cparams("arbitrary"),
        name="post_mixer",
    )(x2, o_i, o_f, o_b, sr, yc, gnw, wo, g1, nw2, sh2, sc2, rwt, rb, tri)


def _slot_kernel(start_ref, idx_ref, rank_ref, slot_ref):
    idx = idx_ref[...]
    base = jnp.zeros(idx.shape, jnp.int32)
    for e in range(N_EXPERTS):
        base = jnp.where(idx == e, start_ref[e], base)
    slot_ref[...] = base + rank_ref[...]


def _slots(group_start, idx_t, rank_t):
    k, n = idx_t.shape
    bn = min(n, 8192)
    blk = pl.BlockSpec((k, bn), lambda i, s: (0, i))
    return pl.pallas_call(
        _slot_kernel,
        out_shape=jax.ShapeDtypeStruct((k, n), jnp.int32),
        grid_spec=pltpu.PrefetchScalarGridSpec(
            num_scalar_prefetch=1, grid=(n // bn,), in_specs=[blk, blk], out_specs=blk),
        compiler_params=_cparams("arbitrary"),
        name="slots",
    )(group_start, idx_t, rank_t)


def _dispatch_kernel(slot_ref, xw_ref, xs_in_ref, xs_ref, sem):
    del xs_in_ref
    tt = slot_ref.shape[1]
    base = pl.program_id(0) * tt

    def row_copy(t, kk):
        return pltpu.make_async_copy(xw_ref.at[pl.ds(base + t, 1)],
                                     xs_ref.at[pl.ds(slot_ref[kk, t], 1)], sem)

    def issue(t, carry):
        for kk in range(TOP_K):
            row_copy(t, kk).start()
        return carry

    def drain(t, carry):
        for kk in range(TOP_K):
            row_copy(t, kk).wait()
        return carry

    lax.fori_loop(0, tt, issue, 0)
    lax.fori_loop(0, tt, drain, 0)


def _dispatch(slot_t, xw, n_slots):
    k, n = slot_t.shape
    tt = DISPATCH_TILE
    xs0 = jnp.zeros((n_slots, xw.shape[1]), xw.dtype)
    return pl.pallas_call(
        _dispatch_kernel,
        out_shape=jax.ShapeDtypeStruct(xs0.shape, xs0.dtype),
        grid=(n // tt,),
        in_specs=[pl.BlockSpec((k, tt), lambda i: (0, i), memory_space=pltpu.SMEM),
                  pl.BlockSpec(memory_space=pl.ANY),
                  pl.BlockSpec(memory_space=pl.ANY)],
        out_specs=pl.BlockSpec(memory_space=pl.ANY),
        scratch_shapes=[pltpu.SemaphoreType.DMA],
        input_output_aliases={2: 0},
        compiler_params=_cparams("arbitrary"),
        name="dispatch",
    )(slot_t, xw, xs0)


def _expert_kernel(te_ref, nu_ref, x_ref, wg_ref, wu_ref, wd_ref, bg_ref, bu_ref, bd_ref, y_ref):
    i = pl.program_id(0)

    @pl.when(i < nu_ref[0])
    def _():
        x = _unpack_halves(x_ref[...]).astype(BF16)
        g = _dot(x, wg_ref[...]) + bg_ref[...]
        u = _dot(x, wu_ref[...]) + bu_ref[...]
        gate = jnp.minimum(g, SWIGLU_LIMIT)
        up = jnp.clip(u, -SWIGLU_LIMIT, SWIGLU_LIMIT)
        act = (up + 1.0) * (gate * _sigmoid(SWIGLU_ALPHA * gate))
        y = _dot(act.astype(BF16), wd_ref[...]) + bd_ref[...]
        y_ref[...] = _pack_halves(y)

    @pl.when(i >= nu_ref[0])
    def _():
        y_ref[...] = jnp.zeros_like(y_ref)


def _experts(tile_expert, n_used, xs, wg, wu, wd, bg, bu, bd):
    n_slots, half = xs.shape
    tm = EXPERT_TILE
    n_tiles = n_slots // tm
    d = wg.shape[1]
    row_in = pl.BlockSpec((tm, half), lambda i, te, nu: (jnp.minimum(i, nu[0] - 1), 0))
    row_out = pl.BlockSpec((tm, half), lambda i, te, nu: (jnp.where(i < nu[0], i, n_tiles - 1), 0))
    wspec = lambda a: pl.BlockSpec((None,) + a.shape[1:], lambda i, te, nu: (te[i], 0, 0))
    return pl.pallas_call(
        _expert_kernel,
        out_shape=jax.ShapeDtypeStruct((n_slots, half), jnp.uint32),
        grid_spec=pltpu.PrefetchScalarGridSpec(
            num_scalar_prefetch=2, grid=(n_tiles,),
            in_specs=[row_in, wspec(wg), wspec(wu), wspec(wd), wspec(bg), wspec(bu), wspec(bd)],
            out_specs=row_out),
        compiler_params=_cparams("arbitrary"),
        name="experts",
    )(tile_expert, n_used, xs, wg, wu, wd, bg, bu, bd)


def _combine_kernel(slot_ref, ys_ref, h_ref, wt_ref, g2_ref, fnw_ref, o_ref, buf_ref, sem):
    tt = h_ref.shape[0]

    def row_copy(t, kk):
        return pltpu.make_async_copy(ys_ref.at[pl.ds(slot_ref[kk, t], 1)],
                                     buf_ref.at[kk, pl.ds(t, 1)], sem)

    def issue(t, carry):
        for kk in range(TOP_K):
            row_copy(t, kk).start()
        return carry

    def drain(t, carry):
        for kk in range(TOP_K):
            row_copy(t, kk).wait()
        return carry

    lax.fori_loop(0, tt, issue, 0)
    lax.fori_loop(0, tt, drain, 0)

    wt = wt_ref[...]
    acc = jnp.zeros(h_ref.shape, F32)
    for kk in range(TOP_K):
        acc = acc + wt[:, kk:kk + 1] * _unpack_halves(buf_ref[kk])
    h = h_ref[...] + g2_ref[...] * acc
    o_ref[...] = _rms(h) * fnw_ref[...]


def _combine(slot_t, ys, h, wt, g2, fnw, seq):
    n, d = h.shape
    tt = COMBINE_TILE
    tpb = seq // tt
    return pl.pallas_call(
        _combine_kernel,
        out_shape=jax.ShapeDtypeStruct((n, d), F32),
        grid=(n // tt,),
        in_specs=[pl.BlockSpec((TOP_K, tt), lambda i: (0, i), memory_space=pltpu.SMEM),
                  pl.BlockSpec(memory_space=pl.ANY),
                  pl.BlockSpec((tt, d), lambda i: (i, 0)),
                  pl.BlockSpec((tt, GATE_PAD), lambda i: (i, 0)),
                  pl.BlockSpec((None, 1, d), lambda i: (i // tpb, 0, 0)),
                  pl.BlockSpec((1, d), lambda i: (0, 0))],
        out_specs=pl.BlockSpec((tt, d), lambda i: (i, 0)),
        scratch_shapes=[pltpu.VMEM((TOP_K, tt, d // 2), jnp.uint32), pltpu.SemaphoreType.DMA],
        compiler_params=_cparams("arbitrary"),
        name="combine",
    )(slot_t, ys, h, wt, g2, fnw)


def kernel(x, c, ctx, c_ctx, w_ada, b_ada, norm1_w, w_in, a_up_f, a_bias_f, a_up_b, a_bias_b,
           gla_norm_w, conv_w, w_out, norm2_w, router_w, router_b, w_gu, b_gu, w_dn, b_dn,
           final_norm_w):
    bsz, seq, d = x.shape
    n = bsz * seq
    assert w_ada.shape[0] == 1 and d == D_MODEL
    assert seq % TOKEN_TILE == 0 and seq % GLA_BLOCK == 0 and seq % GRID_W == 0
    assert ctx.shape[1] & (ctx.shape[1] - 1) == 0

    w_in0 = w_in[0]
    v_end = 2 * GLA_QK + GLA_WIDTH
    r_end = v_end + GLA_WIDTH
    g_end = r_end + 2 * GATE_RANK
    wm = jnp.concatenate([w_in0[:, :r_end], w_in0[:, g_end:]], axis=1).astype(BF16)
    wg_low = jnp.pad(w_in0[:, r_end:g_end], ((0, 0), (0, GATE_PAD - 2 * GATE_RANK))).astype(BF16)
    afb = jnp.zeros((GATE_PAD, 2 * GLA_QK), F32)
    afb = afb.at[:GATE_RANK, :GLA_QK].set(a_up_f[0]).at[GATE_RANK:2 * GATE_RANK, GLA_QK:].set(a_up_b[0])
    afb = afb.astype(BF16)
    bias_fb = jnp.concatenate([a_bias_f[0], a_bias_b[0]])[None]
    wkv = wm[:, GLA_QK:v_end]
    row = lambda a: a.reshape(1, -1)

    mod = _modulation(jnp.concatenate([c, c_ctx[None]], axis=0), w_ada[0], row(b_ada[0]))
    sh1, sc1, g1, sh2, sc2, g2 = [mod[:bsz, i * d:(i + 1) * d].reshape(bsz, 1, d) for i in range(6)]
    csh1, csc1 = mod[bsz:, 0:d], mod[bsz:, d:2 * d]

    s0f, s0b = _context_states(ctx, csh1, csc1, row(norm1_w[0]), wkv, wg_low, afb, bias_fb)

    x2 = x.reshape(n, d)
    a_f, a_b, vb, o_i, sr, yc, dec = _projection(
        x2, sh1, sc1, row(norm1_w[0]), wm, wg_low, afb, bias_fb, conv_w[0], seq)
    o_f, o_b = _gla_scan(a_f, a_b, vb, dec, s0f, s0b, bsz, seq)

    tri = jnp.triu(jnp.ones((TOKEN_TILE, TOKEN_TILE), BF16), k=1)
    h, xw, idx_t, rank_t, wt, cnt = _post_mixer(
        x2, o_i, o_f, o_b, sr, yc, jnp.tile(gla_norm_w[0], GLA_HEADS)[None], w_out[0].astype(BF16),
        g1, row(norm2_w[0]), sh2, sc2, router_w[0].T, router_b[0][:, None], tri, seq)

    counts = cnt[:, 0]
    padded = (counts + EXPERT_TILE - 1) // EXPERT_TILE * EXPERT_TILE
    group_end = jnp.cumsum(padded)
    group_start = (group_end - padded).astype(jnp.int32)
    n_tiles = n * TOP_K // EXPERT_TILE + N_EXPERTS
    n_used = (group_end[-1] // EXPERT_TILE).astype(jnp.int32)
    tile_ids = jnp.minimum(jnp.arange(n_tiles, dtype=jnp.int32), n_used - 1) * EXPERT_TILE
    tile_expert = jnp.minimum(jnp.searchsorted(group_end, tile_ids, side='right'),
                              N_EXPERTS - 1).astype(jnp.int32)

    slot_t = _slots(group_start, idx_t, rank_t)
    xs = _dispatch(slot_t, xw, n_tiles * EXPERT_TILE)
    ys = _experts(tile_expert, n_used.reshape(1), xs,
                  w_gu[0][:, :, 0::2].astype(BF16), w_gu[0][:, :, 1::2].astype(BF16),
                  w_dn[0].astype(BF16),
                  b_gu[0][:, None, 0::2], b_gu[0][:, None, 1::2], b_dn[0][:, None, :])
    out = _combine(slot_t, ys, h, wt, g2, row(final_norm_w), seq)
    return out.reshape(bsz, seq, d)
```

```python
import functools

import jax
import jax.numpy as jnp
from jax import lax
from jax.experimental import pallas as pl
from jax.experimental.pallas import tpu as pltpu

D_MODEL = 1024
GLA_HEADS = 4
GLA_DK = 64
GLA_DV = 128
GLA_QK = GLA_HEADS * GLA_DK
GLA_WIDTH = GLA_HEADS * GLA_DV
CONV_WIDTH = 512
GATE_RANK = 16
GATE_PAD = 128
GATE_NORMALIZER = 16.0
CHUNK = 64
GRID_W = 64
N_EXPERTS = 32
TOP_K = 4
SWIGLU_LIMIT = 7.0
SWIGLU_ALPHA = 1.702
EPS = 1e-6

TOKEN_TILE = 512
GLA_BLOCK = 512
INTRA_GROUP = 256
EXPERT_TILE = 512
DISPATCH_TILE = 512
COMBINE_TILE = 256
VMEM_LIMIT = 56 * 1024 * 1024

F32 = jnp.float32
BF16 = jnp.bfloat16
HI_MASK = 0xFFFF0000


def _cparams(*sem):
    return pltpu.CompilerParams(dimension_semantics=sem, vmem_limit_bytes=VMEM_LIMIT)


def _rms(x):
    return x * lax.rsqrt(jnp.mean(x * x, axis=-1, keepdims=True) + EPS)


def _sigmoid(x):
    return 1.0 / (1.0 + jnp.exp(-x))


def _log_sigmoid(z):
    return jnp.minimum(z, 0.0) - jnp.log(1.0 + jnp.exp(-jnp.abs(z)))


def _dot(a, b):
    return jnp.dot(a, b, preferred_element_type=F32)


def _dot_nt(a, b):
    return lax.dot_general(a, b, (((1,), (1,)), ((), ())), preferred_element_type=F32)


def _dot_tn(a, b):
    return lax.dot_general(a, b, (((0,), (0,)), ((), ())), preferred_element_type=F32)


def _split_bf16(a):
    hi = a.astype(BF16)
    lo = (a - hi.astype(F32)).astype(BF16)
    return hi, lo


def _dot3(a, b, dot):
    a_hi, a_lo = _split_bf16(a)
    b_hi, b_lo = _split_bf16(b)
    return dot(a_hi, b_hi) + (dot(a_hi, b_lo) + dot(a_lo, b_hi))


def _segment_scan(x, seg, reverse):
    n = x.shape[0]
    row = lax.broadcasted_iota(jnp.int32, x.shape, 0) & (seg - 1)
    s = 1
    while s < seg:
        if reverse:
            shifted = pltpu.roll(x, n - s, 0)
            x = x + jnp.where(row < seg - s, shifted, 0.0)
        else:
            shifted = pltpu.roll(x, s, 0)
            x = x + jnp.where(row >= s, shifted, 0.0)
        s *= 2
    return x


def _pack_halves(x):
    n = x.shape[1] // 2
    bits = pltpu.bitcast(x.astype(BF16).astype(F32), jnp.uint32)
    return (bits[:, :n] & jnp.uint32(HI_MASK)) | (bits[:, n:] >> 16)


def _unpack_halves(w):
    hi = pltpu.bitcast(w & jnp.uint32(HI_MASK), F32)
    lo = pltpu.bitcast(w << 16, F32)
    return jnp.concatenate([hi, lo], axis=1)


def _mod_kernel(c_ref, w_ref, b_ref, o_ref):
    c = c_ref[...]
    s = c * _sigmoid(c)
    o_ref[...] = _dot3(s, w_ref[...], _dot) + b_ref[...]


def _modulation(cc, w_ada, b_ada):
    rows, d = cc.shape
    n = w_ada.shape[1]
    bn = 1536
    return pl.pallas_call(
        _mod_kernel,
        out_shape=jax.ShapeDtypeStruct((rows, n), F32),
        grid=(n // bn,),
        in_specs=[pl.BlockSpec((rows, d), lambda j: (0, 0)),
                  pl.BlockSpec((d, bn), lambda j: (0, j)),
                  pl.BlockSpec((1, bn), lambda j: (0, j))],
        out_specs=pl.BlockSpec((rows, bn), lambda j: (0, j)),
        compiler_params=_cparams("arbitrary"),
        name="modulation",
    )(cc, w_ada, b_ada)


def _ctx_kernel(x_ref, sh_ref, sc_ref, nw_ref, wkv_ref, wg_ref, afb_ref, bias_ref, sf_ref, sb_ref):
    x = x_ref[...]
    n = x.shape[0]
    hn = _rms(x) * nw_ref[...] * (1.0 + sc_ref[...]) + sh_ref[...]
    hb = hn.astype(BF16)
    kv = _dot(hb, wkv_ref[...])
    g = _dot(hb, wg_ref[...])
    z = _dot(g.astype(BF16), afb_ref[...]) + bias_ref[...]
    la = _log_sigmoid(z) * (1.0 / GATE_NORMALIZER)
    p = _segment_scan(la, n, reverse=False)
    p_f, p_b = p[:, :GLA_QK], p[:, GLA_QK:]
    w_f = jnp.exp(p_f[n - 1:n, :] - p_f)
    w_b = jnp.exp(p_b - la[:, GLA_QK:])
    k = kv[:, :GLA_QK]
    vb = kv[:, GLA_QK:].astype(BF16)
    ke_f = (k * w_f).astype(BF16)
    ke_b = (k * w_b).astype(BF16)
    for h in range(GLA_HEADS):
        vh = vb[:, h * GLA_DV:(h + 1) * GLA_DV]
        sf_ref[h] = _dot_tn(vh, ke_f[:, h * GLA_DK:(h + 1) * GLA_DK])
        sb_ref[h] = _dot_tn(vh, ke_b[:, h * GLA_DK:(h + 1) * GLA_DK])


def _context_states(ctx, csh1, csc1, nw1, wkv, wg, afb, bias_fb):
    bsz, n, d = ctx.shape
    full = lambda a: pl.BlockSpec(a.shape, lambda b: (0,) * a.ndim)
    st = jax.ShapeDtypeStruct((bsz, GLA_HEADS, GLA_DV, GLA_DK), F32)
    st_spec = pl.BlockSpec((None, GLA_HEADS, GLA_DV, GLA_DK), lambda b: (b, 0, 0, 0))
    return pl.pallas_call(
        _ctx_kernel,
        out_shape=(st, st),
        grid=(bsz,),
        in_specs=[pl.BlockSpec((None, n, d), lambda b: (b, 0, 0)),
                  full(csh1), full(csc1), full(nw1), full(wkv), full(wg), full(afb), full(bias_fb)],
        out_specs=(st_spec, st_spec),
        compiler_params=_cparams("arbitrary"),
        name="context_states",
    )(ctx, csh1, csc1, nw1, wkv, wg, afb, bias_fb)


def _proj_kernel(x_ref, sh_ref, sc_ref, nw_ref, wm_ref, wg_ref, afb_ref, bias_ref, cw_ref,
                 af_ref, ab_ref, v_ref, oi_ref, sr_ref, yc_ref, dec_ref):
    x = x_ref[...]
    tm = x.shape[0]
    nc = tm // CHUNK
    hn = _rms(x) * nw_ref[...] * (1.0 + sc_ref[...]) + sh_ref[...]
    hb = hn.astype(BF16)
    p = _dot(hb, wm_ref[...])
    g = _dot(hb, wg_ref[...])
    z = _dot(g.astype(BF16), afb_ref[...]) + bias_ref[...]
    la = _log_sigmoid(z) * (1.0 / GATE_NORMALIZER)

    q = p[:, 0:GLA_QK] * (GLA_DK ** -0.5)
    k = p[:, GLA_QK:2 * GLA_QK]
    vb = p[:, 2 * GLA_QK:2 * GLA_QK + GLA_WIDTH].astype(BF16)
    v_ref[...] = vb

    b_f = _segment_scan(la[:, :GLA_QK], CHUNK, reverse=False)
    b_f3 = b_f.reshape(nc, CHUNK, GLA_QK)
    tot_f = b_f3[:, CHUNK - 1:CHUNK, :]
    qd_f = (q * jnp.exp(b_f)).astype(BF16)
    ki_f = (k * jnp.exp(-b_f)).astype(BF16)
    ke_f = (k * jnp.exp(tot_f - b_f3).reshape(tm, GLA_QK)).astype(BF16)
    b_b = _segment_scan(la[:, GLA_QK:], CHUNK, reverse=True)
    b_b3 = b_b.reshape(nc, CHUNK, GLA_QK)
    tot_b = b_b3[:, 0:1, :]
    qd_b = (q * jnp.exp(b_b)).astype(BF16)
    ki_b = (k * jnp.exp(-b_b)).astype(BF16)
    ke_b = (k * jnp.exp(tot_b - b_b3).reshape(tm, GLA_QK)).astype(BF16)

    af_ref[:, :GLA_QK] = qd_f
    af_ref[:, GLA_QK:] = ke_f
    ab_ref[:, :GLA_QK] = qd_b
    ab_ref[:, GLA_QK:] = ke_b
    dec_ref[:, :GLA_QK] = jnp.exp(tot_f).reshape(nc, GLA_QK)
    dec_ref[:, GLA_QK:] = jnp.exp(tot_b).reshape(nc, GLA_QK)

    gi = INTRA_GROUP
    ri = lax.broadcasted_iota(jnp.int32, (gi, gi), 0)
    ci = lax.broadcasted_iota(jnp.int32, (gi, gi), 1)
    same = (ri // CHUNK) == (ci // CHUNK)
    lower = same & (ci <= ri)
    upper = same & (ci >= ri)
    for gidx in range(tm // gi):
        rows = slice(gidx * gi, (gidx + 1) * gi)
        for h in range(GLA_HEADS):
            lanes = slice(h * GLA_DK, (h + 1) * GLA_DK)
            s_f = _dot_nt(qd_f[rows, lanes], ki_f[rows, lanes])
            s_b = _dot_nt(qd_b[rows, lanes], ki_b[rows, lanes])
            s = jnp.where(lower, s_f, 0.0) + jnp.where(upper, s_b, 0.0)
            oi_ref[rows, h * GLA_DV:(h + 1) * GLA_DV] = _dot(
                s.astype(BF16), vb[rows, h * GLA_DV:(h + 1) * GLA_DV])

    r = p[:, 1024:1536]
    sr_ref[...] = (r * _sigmoid(r)).astype(BF16)

    u = p[:, 2048:2560] * p[:, 2560:3072]
    row = lax.broadcasted_iota(jnp.int32, u.shape, 0) & (GRID_W - 1)
    prev = jnp.where(row >= 1, pltpu.roll(u, 1, 0), 0.0)
    nxt = jnp.where(row < GRID_W - 1, pltpu.roll(u, tm - 1, 0), 0.0)
    cw = cw_ref[...]
    conv = prev * cw[0:1, :] + u * cw[1:2, :] + nxt * cw[2:3, :]
    yc_ref[...] = (p[:, 1536:2048] * conv).astype(BF16)


def _projection(x2, sh1, sc1, nw1, wm, wg, afb, bias_fb, conv_w, seq):
    n, d = x2.shape
    tm = TOKEN_TILE
    tpb = seq // tm
    full = lambda a: pl.BlockSpec(a.shape, lambda i: (0,) * a.ndim)
    per_batch = pl.BlockSpec((None, 1, d), lambda i: (i // tpb, 0, 0))
    tok = lambda w: pl.BlockSpec((tm, w), lambda i: (i, 0))
    out_shape = (
        jax.ShapeDtypeStruct((n, 2 * GLA_QK), BF16),
        jax.ShapeDtypeStruct((n, 2 * GLA_QK), BF16),
        jax.ShapeDtypeStruct((n, GLA_WIDTH), BF16),
        jax.ShapeDtypeStruct((n, GLA_WIDTH), F32),
        jax.ShapeDtypeStruct((n, GLA_WIDTH), BF16),
        jax.ShapeDtypeStruct((n, CONV_WIDTH), BF16),
        jax.ShapeDtypeStruct((n // CHUNK, 2 * GLA_QK), F32),
    )
    out_specs = (tok(2 * GLA_QK), tok(2 * GLA_QK), tok(GLA_WIDTH), tok(GLA_WIDTH), tok(GLA_WIDTH),
                 tok(CONV_WIDTH), pl.BlockSpec((tm // CHUNK, 2 * GLA_QK), lambda i: (i, 0)))
    return pl.pallas_call(
        _proj_kernel,
        out_shape=out_shape,
        grid=(n // tm,),
        in_specs=[tok(d), per_batch, per_batch, full(nw1), full(wm), full(wg), full(afb),
                  full(bias_fb), full(conv_w)],
        out_specs=out_specs,
        compiler_params=_cparams("arbitrary"),
        name="projection",
    )(x2, sh1, sc1, nw1, wm, wg, afb, bias_fb, conv_w)


def _scan_kernel(af_ref, vf_ref, df_ref, ab_ref, vb_ref, db_ref, s0f_ref, s0b_ref,
                 of_ref, ob_ref, st_ref):
    @pl.when(pl.program_id(1) == 0)
    def _():
        st_ref[0] = s0f_ref[...]
        st_ref[1] = s0b_ref[...]

    nc = af_ref.shape[0] // CHUNK

    def one_chunk(c, a_ref, v_ref, d_ref, o_ref, direction):
        r0 = pl.multiple_of(c * CHUNK, CHUNK)
        rows = pl.ds(r0, CHUNK)
        qd = a_ref[rows, 0:GLA_QK]
        ke = a_ref[rows, GLA_QK:2 * GLA_QK]
        v = v_ref[rows, :]
        d = d_ref[pl.ds(c, 1), direction * GLA_QK:(direction + 1) * GLA_QK]
        for h in range(GLA_HEADS):
            lanes = slice(h * GLA_DK, (h + 1) * GLA_DK)
            vl = slice(h * GLA_DV, (h + 1) * GLA_DV)
            s = st_ref[direction, h]
            o_ref[rows, vl] = _dot_nt(qd[:, lanes], s.astype(BF16))
            st_ref[direction, h] = s * d[:, lanes] + _dot_tn(v[:, vl], ke[:, lanes])

    def body(c, carry):
        one_chunk(c, af_ref, vf_ref, df_ref, of_ref, 0)
        one_chunk(nc - 1 - c, ab_ref, vb_ref, db_ref, ob_ref, 1)
        return carry

    lax.fori_loop(0, nc, body, 0)


def _gla_scan(a_f, a_b, vb, dec, s0f, s0b, bsz, seq):
    n = a_f.shape[0]
    tb = GLA_BLOCK
    nb = seq // tb
    fwd = lambda w: pl.BlockSpec((tb, w), lambda b, j: (b * nb + j, 0))
    bwd = lambda w: pl.BlockSpec((tb, w), lambda b, j: (b * nb + nb - 1 - j, 0))
    dfw = pl.BlockSpec((tb // CHUNK, 2 * GLA_QK), lambda b, j: (b * nb + j, 0))
    dbw = pl.BlockSpec((tb // CHUNK, 2 * GLA_QK), lambda b, j: (b * nb + nb - 1 - j, 0))
    st_spec = pl.BlockSpec((None, GLA_HEADS, GLA_DV, GLA_DK), lambda b, j: (b, 0, 0, 0))
    out = jax.ShapeDtypeStruct((n, GLA_WIDTH), F32)
    return pl.pallas_call(
        _scan_kernel,
        out_shape=(out, out),
        grid=(bsz, nb),
        in_specs=[fwd(2 * GLA_QK), fwd(GLA_WIDTH), dfw, bwd(2 * GLA_QK), bwd(GLA_WIDTH), dbw,
                  st_spec, st_spec],
        out_specs=(fwd(GLA_WIDTH), bwd(GLA_WIDTH)),
        scratch_shapes=[pltpu.VMEM((2, GLA_HEADS, GLA_DV, GLA_DK), F32)],
        compiler_params=_cparams("arbitrary", "arbitrary"),
        name="gla_scan",
    )(a_f, vb, dec, a_b, vb, dec, s0f, s0b)


def _post_kernel(x_ref, oi_ref, of_ref, ob_ref, sr_ref, yc_ref, gnw_ref, wo_ref, g1_ref, nw2_ref,
                 sh2_ref, sc2_ref, rwt_ref, rb_ref, tri_ref,
                 h_ref, xw_ref, idx_ref, rank_ref, wt_ref, cnt_ref, run_ref):
    @pl.when(pl.program_id(0) == 0)
    def _():
        run_ref[...] = jnp.zeros_like(run_ref)

    tm = x_ref.shape[0]
    o = oi_ref[...] + of_ref[...] + ob_ref[...]
    normed = jnp.concatenate(
        [_rms(o[:, h * GLA_DV:(h + 1) * GLA_DV]) for h in range(GLA_HEADS)], axis=1)
    y_gla = (normed * gnw_ref[...] * sr_ref[...].astype(F32)).astype(BF16)
    wo = wo_ref[...]
    y = _dot(y_gla, wo[:GLA_WIDTH, :]) + _dot(yc_ref[...], wo[GLA_WIDTH:, :])
    h = x_ref[...] + g1_ref[...] * y
    h_ref[...] = h
    hn = _rms(h) * nw2_ref[...] * (1.0 + sc2_ref[...]) + sh2_ref[...]
    xw_ref[...] = _pack_halves(hn)

    logits = _dot3(rwt_ref[...], hn, _dot_nt) + rb_ref[...]
    eid = lax.broadcasted_iota(jnp.int32, logits.shape, 0)
    vals, idxs = [], []
    work = logits
    for _ in range(TOP_K):
        m = jnp.max(work, axis=0, keepdims=True)
        i = jnp.min(jnp.where(work == m, eid, N_EXPERTS), axis=0, keepdims=True)
        vals.append(m)
        idxs.append(i)
        work = jnp.where(eid == i, -jnp.inf, work)
    exps = [jnp.exp(v - vals[0]) for v in vals]
    denom = exps[0] + exps[1] + exps[2] + exps[3]
    wts = [e / denom for e in exps]

    onehot = jnp.zeros(logits.shape, F32)
    for i in idxs:
        onehot = onehot + jnp.where(eid == i, 1.0, 0.0)
    before = _dot(onehot.astype(BF16), tri_ref[...]) + run_ref[:, 0:1]
    for kk in range(TOP_K):
        idx_ref[kk:kk + 1, :] = idxs[kk]
        rk = jnp.sum(jnp.where(eid == idxs[kk], before, 0.0), axis=0, keepdims=True)
        rank_ref[kk:kk + 1, :] = rk.astype(jnp.int32)
    run = run_ref[...] + jnp.sum(onehot, axis=1, keepdims=True)
    run_ref[...] = run
    cnt_ref[...] = run.astype(jnp.int32)

    sub = lax.broadcasted_iota(jnp.int32, (GATE_PAD, tm), 0)
    wpad = jnp.zeros((GATE_PAD, tm), F32)
    for kk in range(TOP_K):
        wpad = wpad + jnp.where(sub == kk, wts[kk], 0.0)
    wt_ref[...] = wpad.T


def _post_mixer(x2, o_i, o_f, o_b, sr, yc, gnw, wo, g1, nw2, sh2, sc2, rwt, rb, tri, seq):
    n, d = x2.shape
    tm = TOKEN_TILE
    tpb = seq // tm
    full = lambda a: pl.BlockSpec(a.shape, lambda i: (0,) * a.ndim)
    per_batch = pl.BlockSpec((None, 1, d), lambda i: (i // tpb, 0, 0))
    tok = lambda w: pl.BlockSpec((tm, w), lambda i: (i, 0))
    lane_tok = pl.BlockSpec((TOP_K, tm), lambda i: (0, i))
    out_shape = (
        jax.ShapeDtypeStruct((n, d), F32),
        jax.ShapeDtypeStruct((n, d // 2), jnp.uint32),
        jax.ShapeDtypeStruct((TOP_K, n), jnp.int32),
        jax.ShapeDtypeStruct((TOP_K, n), jnp.int32),
        jax.ShapeDtypeStruct((n, GATE_PAD), F32),
        jax.ShapeDtypeStruct((N_EXPERTS, GATE_PAD), jnp.int32),
    )
    out_specs = (tok(d), tok(d // 2), lane_tok, lane_tok, tok(GATE_PAD),
                 pl.BlockSpec((N_EXPERTS, GATE_PAD), lambda i: (0, 0)))
    return pl.pallas_call(
        _post_kernel,
        out_shape=out_shape,
        grid=(n // tm,),
        in_specs=[tok(d), tok(GLA_WIDTH), tok(GLA_WIDTH), tok(GLA_WIDTH), tok(GLA_WIDTH),
                  tok(CONV_WIDTH), full(gnw), full(wo), per_batch, full(nw2), per_batch, per_batch,
                  full(rwt), full(rb), full(tri)],
        out_specs=out_specs,
        scratch_shapes=[pltpu.VMEM((N_EXPERTS, GATE_PAD), F32)],
        compiler_params=_cparams("arbitrary"),
        name="post_mixer",
    )(x2, o_i, o_f, o_b, sr, yc, gnw, wo, g1, nw2, sh2, sc2, rwt, rb, tri)


def _slot_kernel(start_ref, idx_ref, rank_ref, slot_ref):
    idx = idx_ref[...]
    base = jnp.zeros(idx.shape, jnp.int32)
    for e in range(N_EXPERTS):
        base = jnp.where(idx == e, start_ref[e], base)
    slot_ref[...] = base + rank_ref[...]


def _slots(group_start, idx_t, rank_t):
    k, n = idx_t.shape
    bn = min(n, 8192)
    blk = pl.BlockSpec((k, bn), lambda i, s: (0, i))
    return pl.pallas_call(
        _slot_kernel,
        out_shape=jax.ShapeDtypeStruct((k, n), jnp.int32),
        grid_spec=pltpu.PrefetchScalarGridSpec(
            num_scalar_prefetch=1, grid=(n // bn,), in_specs=[blk, blk], out_specs=blk),
        compiler_params=_cparams("arbitrary"),
        name="slots",
    )(group_start, idx_t, rank_t)


def _dispatch_kernel(slot_ref, xw_ref, xs_in_ref, xs_ref, sem):
    del xs_in_ref
    tt = slot_ref.shape[1]

    def row_copy(t, kk):
        return pltpu.make_async_copy(xw_ref.at[pl.ds(t, 1)],
                                     xs_ref.at[pl.ds(slot_ref[kk, t], 1)], sem)

    def issue(t, carry):
        for kk in range(TOP_K):
            row_copy(t, kk).start()
        return carry

    def drain(t, carry):
        for kk in range(TOP_K):
            row_copy(t, kk).wait()
        return carry

    lax.fori_loop(0, tt, issue, 0)
    lax.fori_loop(0, tt, drain, 0)


def _dispatch(slot_t, xw, n_slots):
    k, n = slot_t.shape
    tt = DISPATCH_TILE
    xs0 = jnp.zeros((n_slots, xw.shape[1]), xw.dtype)
    return pl.pallas_call(
        _dispatch_kernel,
        out_shape=jax.ShapeDtypeStruct(xs0.shape, xs0.dtype),
        grid=(n // tt,),
        in_specs=[pl.BlockSpec((k, tt), lambda i: (0, i), memory_space=pltpu.SMEM),
                  pl.BlockSpec((tt, xw.shape[1]), lambda i: (i, 0)),
                  pl.BlockSpec(memory_space=pl.ANY)],
        out_specs=pl.BlockSpec(memory_space=pl.ANY),
        scratch_shapes=[pltpu.SemaphoreType.DMA],
        input_output_aliases={2: 0},
        compiler_params=_cparams("arbitrary"),
        name="dispatch",
    )(slot_t, xw, xs0)


PERM_BLOCK = 256


def _gu_split_kernel(w_ref, perm_ref, wg_ref, wu_ref):
    half = PERM_BLOCK // 2
    perm = perm_ref[...]
    for b in range(w_ref.shape[1] // PERM_BLOCK):
        blk = w_ref[:, b * PERM_BLOCK:(b + 1) * PERM_BLOCK].astype(BF16)
        t = _dot(blk, perm)
        wg_ref[:, b * half:(b + 1) * half] = t[:, :half].astype(BF16)
        wu_ref[:, b * half:(b + 1) * half] = t[:, half:].astype(BF16)


def _gu_split(w_gu):
    ne, d, two_f = w_gu.shape
    rows = 256
    half = PERM_BLOCK // 2
    j = jnp.arange(PERM_BLOCK)
    perm = (jnp.arange(PERM_BLOCK)[None, :] == (j // 2 + (j % 2) * half)[:, None]).astype(BF16)
    out = jax.ShapeDtypeStruct((ne, d, two_f // 2), BF16)
    ospec = pl.BlockSpec((None, rows, two_f // 2), lambda e, r: (e, r, 0))
    return pl.pallas_call(
        _gu_split_kernel,
        out_shape=(out, out),
        grid=(ne, d // rows),
        in_specs=[pl.BlockSpec((None, rows, two_f), lambda e, r: (e, r, 0)),
                  pl.BlockSpec((PERM_BLOCK, PERM_BLOCK), lambda e, r: (0, 0))],
        out_specs=(ospec, ospec),
        compiler_params=_cparams("arbitrary", "arbitrary"),
        name="gu_split",
    )(w_gu, perm)


def _expert_kernel(te_ref, nu_ref, x_ref, wg_ref, wu_ref, wd_ref, bg_ref, bu_ref, bd_ref, y_ref):
    i = pl.program_id(0)

    @pl.when(i < nu_ref[0])
    def _():
        x = _unpack_halves(x_ref[...]).astype(BF16)
        g = _dot(x, wg_ref[...]) + bg_ref[...]
        u = _dot(x, wu_ref[...]) + bu_ref[...]
        gate = jnp.minimum(g, SWIGLU_LIMIT)
        up = jnp.clip(u, -SWIGLU_LIMIT, SWIGLU_LIMIT)
        act = (up + 1.0) * (gate * _sigmoid(SWIGLU_ALPHA * gate))
        y = _dot(act.astype(BF16), wd_ref[...]) + bd_ref[...]
        y_ref[...] = _pack_halves(y)

    @pl.when(i >= nu_ref[0])
    def _():
        y_ref[...] = jnp.zeros_like(y_ref)


def _experts(tile_expert, n_used, xs, wg, wu, wd, bg, bu, bd):
    n_slots, half = xs.shape
    tm = EXPERT_TILE
    n_tiles = n_slots // tm
    d = wg.shape[1]
    row_in = pl.BlockSpec((tm, half), lambda i, te, nu: (jnp.minimum(i, nu[0] - 1), 0))
    row_out = pl.BlockSpec((tm, half), lambda i, te, nu: (jnp.where(i < nu[0], i, n_tiles - 1), 0))
    wspec = lambda a: pl.BlockSpec((None,) + a.shape[1:], lambda i, te, nu: (te[i], 0, 0))
    return pl.pallas_call(
        _expert_kernel,
        out_shape=jax.ShapeDtypeStruct((n_slots, half), jnp.uint32),
        grid_spec=pltpu.PrefetchScalarGridSpec(
            num_scalar_prefetch=2, grid=(n_tiles,),
            in_specs=[row_in, wspec(wg), wspec(wu), wspec(wd), wspec(bg), wspec(bu), wspec(bd)],
            out_specs=row_out),
        compiler_params=_cparams("arbitrary"),
        name="experts",
    )(tile_expert, n_used, xs, wg, wu, wd, bg, bu, bd)


def _combine_kernel(slot_ref, ys_ref, h_ref, wt_ref, g2_ref, fnw_ref, o_ref, buf_ref, sem):
    tt = h_ref.shape[0]

    def row_copy(t, kk):
        return pltpu.make_async_copy(ys_ref.at[pl.ds(slot_ref[kk, t], 1)],
                                     buf_ref.at[kk, pl.ds(t, 1)], sem)

    def issue(t, carry):
        for kk in range(TOP_K):
            row_copy(t, kk).start()
        return carry

    def drain(t, carry):
        for kk in range(TOP_K):
            row_copy(t, kk).wait()
        return carry

    lax.fori_loop(0, tt, issue, 0)
    lax.fori_loop(0, tt, drain, 0)

    wt = wt_ref[...]
    acc = jnp.zeros(h_ref.shape, F32)
    for kk in range(TOP_K):
        acc = acc + wt[:, kk:kk + 1] * _unpack_halves(buf_ref[kk])
    h = h_ref[...] + g2_ref[...] * acc
    o_ref[...] = _rms(h) * fnw_ref[...]


def _combine(slot_t, ys, h, wt, g2, fnw, seq):
    n, d = h.shape
    tt = COMBINE_TILE
    tpb = seq // tt
    return pl.pallas_call(
        _combine_kernel,
        out_shape=jax.ShapeDtypeStruct((n, d), F32),
        grid=(n // tt,),
        in_specs=[pl.BlockSpec((TOP_K, tt), lambda i: (0, i), memory_space=pltpu.SMEM),
                  pl.BlockSpec(memory_space=pl.ANY),
                  pl.BlockSpec((tt, d), lambda i: (i, 0)),
                  pl.BlockSpec((tt, GATE_PAD), lambda i: (i, 0)),
                  pl.BlockSpec((None, 1, d), lambda i: (i // tpb, 0, 0)),
                  pl.BlockSpec((1, d), lambda i: (0, 0))],
        out_specs=pl.BlockSpec((tt, d), lambda i: (i, 0)),
        scratch_shapes=[pltpu.VMEM((TOP_K, tt, d // 2), jnp.uint32), pltpu.SemaphoreType.DMA],
        compiler_params=_cparams("arbitrary"),
        name="combine",
    )(slot_t, ys, h, wt, g2, fnw)


def kernel(x, c, ctx, c_ctx, w_ada, b_ada, norm1_w, w_in, a_up_f, a_bias_f, a_up_b, a_bias_b,
           gla_norm_w, conv_w, w_out, norm2_w, router_w, router_b, w_gu, b_gu, w_dn, b_dn,
           final_norm_w):
    bsz, seq, d = x.shape
    n = bsz * seq
    assert w_ada.shape[0] == 1 and d == D_MODEL
    assert seq % TOKEN_TILE == 0 and seq % GLA_BLOCK == 0 and seq % GRID_W == 0
    assert ctx.shape[1] & (ctx.shape[1] - 1) == 0

    w_in0 = w_in[0]
    v_end = 2 * GLA_QK + GLA_WIDTH
    r_end = v_end + GLA_WIDTH
    g_end = r_end + 2 * GATE_RANK
    wm = jnp.concatenate([w_in0[:, :r_end], w_in0[:, g_end:]], axis=1).astype(BF16)
    wg_low = jnp.pad(w_in0[:, r_end:g_end], ((0, 0), (0, GATE_PAD - 2 * GATE_RANK))).astype(BF16)
    afb = jnp.zeros((GATE_PAD, 2 * GLA_QK), F32)
    afb = afb.at[:GATE_RANK, :GLA_QK].set(a_up_f[0]).at[GATE_RANK:2 * GATE_RANK, GLA_QK:].set(a_up_b[0])
    afb = afb.astype(BF16)
    bias_fb = jnp.concatenate([a_bias_f[0], a_bias_b[0]])[None]
    wkv = wm[:, GLA_QK:v_end]
    row = lambda a: a.reshape(1, -1)

    mod = _modulation(jnp.concatenate([c, c_ctx[None]], axis=0), w_ada[0], row(b_ada[0]))
    sh1, sc1, g1, sh2, sc2, g2 = [mod[:bsz, i * d:(i + 1) * d].reshape(bsz, 1, d) for i in range(6)]
    csh1, csc1 = mod[bsz:, 0:d], mod[bsz:, d:2 * d]

    s0f, s0b = _context_states(ctx, csh1, csc1, row(norm1_w[0]), wkv, wg_low, afb, bias_fb)

    x2 = x.reshape(n, d)
    a_f, a_b, vb, o_i, sr, yc, dec = _projection(
        x2, sh1, sc1, row(norm1_w[0]), wm, wg_low, afb, bias_fb, conv_w[0], seq)
    o_f, o_b = _gla_scan(a_f, a_b, vb, dec, s0f, s0b, bsz, seq)

    tri = jnp.triu(jnp.ones((TOKEN_TILE, TOKEN_TILE), BF16), k=1)
    h, xw, idx_t, rank_t, wt, cnt = _post_mixer(
        x2, o_i, o_f, o_b, sr, yc, jnp.tile(gla_norm_w[0], GLA_HEADS)[None], w_out[0].astype(BF16),
        g1, row(norm2_w[0]), sh2, sc2, router_w[0].T, router_b[0][:, None], tri, seq)

    counts = cnt[:, 0]
    padded = (counts + EXPERT_TILE - 1) // EXPERT_TILE * EXPERT_TILE
    group_end = jnp.cumsum(padded)
    group_start = (group_end - padded).astype(jnp.int32)
    n_tiles = n * TOP_K // EXPERT_TILE + N_EXPERTS
    n_used = (group_end[-1] // EXPERT_TILE).astype(jnp.int32)
    tile_ids = jnp.minimum(jnp.arange(n_tiles, dtype=jnp.int32), n_used - 1) * EXPERT_TILE
    tile_expert = jnp.minimum(jnp.sum(tile_ids[:, None] >= group_end[None, :], axis=1),
                              N_EXPERTS - 1).astype(jnp.int32)

    slot_t = _slots(group_start, idx_t, rank_t)
    xs = _dispatch(slot_t, xw, n_tiles * EXPERT_TILE)
    wg_e, wu_e = _gu_split(w_gu[0])
    ys = _experts(tile_expert, n_used.reshape(1), xs, wg_e, wu_e, w_dn[0].astype(BF16),
                  b_gu[0][:, None, 0::2], b_gu[0][:, None, 1::2], b_dn[0][:, None, :])
    out = _combine(slot_t, ys, h, wt, g2, row(final_norm_w), seq)
    return out.reshape(bsz, seq, d)
```

```python
import functools

import jax
import jax.numpy as jnp
from jax import lax
from jax.experimental import pallas as pl
from jax.experimental.pallas import tpu as pltpu
from jax.experimental.pallas import tpu_sc as plsc

D_MODEL = 1024
GLA_HEADS = 4
GLA_DK = 64
GLA_DV = 128
GLA_QK = GLA_HEADS * GLA_DK
GLA_WIDTH = GLA_HEADS * GLA_DV
CONV_WIDTH = 512
GATE_RANK = 16
GATE_PAD = 128
GATE_NORMALIZER = 16.0
CHUNK = 64
GRID_W = 64
N_EXPERTS = 32
TOP_K = 4
SWIGLU_LIMIT = 7.0
SWIGLU_ALPHA = 1.702
EPS = 1e-6

TOKEN_TILE = 512
GLA_BLOCK = 512
INTRA_GROUP = 256
EXPERT_TILE = 512
COMBINE_TILE = 512
VMEM_LIMIT = 56 * 1024 * 1024

F32 = jnp.float32
BF16 = jnp.bfloat16
HI_MASK = 0xFFFF0000


def _cparams(*sem):
    return pltpu.CompilerParams(dimension_semantics=sem, vmem_limit_bytes=VMEM_LIMIT)


def _rms(x):
    return x * lax.rsqrt(jnp.mean(x * x, axis=-1, keepdims=True) + EPS)


def _sigmoid(x):
    return 1.0 / (1.0 + jnp.exp(-x))


def _log_sigmoid(z):
    return jnp.minimum(z, 0.0) - jnp.log(1.0 + jnp.exp(-jnp.abs(z)))


def _dot(a, b):
    return jnp.dot(a, b, preferred_element_type=F32)


def _dot_nt(a, b):
    return lax.dot_general(a, b, (((1,), (1,)), ((), ())), preferred_element_type=F32)


def _dot_tn(a, b):
    return lax.dot_general(a, b, (((0,), (0,)), ((), ())), preferred_element_type=F32)


def _split_bf16(a):
    hi = a.astype(BF16)
    lo = (a - hi.astype(F32)).astype(BF16)
    return hi, lo


def _dot3(a, b, dot):
    a_hi, a_lo = _split_bf16(a)
    b_hi, b_lo = _split_bf16(b)
    return dot(a_hi, b_hi) + (dot(a_hi, b_lo) + dot(a_lo, b_hi))


def _segment_scan(x, seg, reverse):
    n = x.shape[0]
    row = lax.broadcasted_iota(jnp.int32, x.shape, 0) & (seg - 1)
    s = 1
    while s < seg:
        if reverse:
            shifted = pltpu.roll(x, n - s, 0)
            x = x + jnp.where(row < seg - s, shifted, 0.0)
        else:
            shifted = pltpu.roll(x, s, 0)
            x = x + jnp.where(row >= s, shifted, 0.0)
        s *= 2
    return x


def _pack_halves(x):
    n = x.shape[1] // 2
    bits = pltpu.bitcast(x.astype(BF16).astype(F32), jnp.uint32)
    return (bits[:, :n] & jnp.uint32(HI_MASK)) | (bits[:, n:] >> 16)


def _unpack_halves(w):
    hi = pltpu.bitcast(w & jnp.uint32(HI_MASK), F32)
    lo = pltpu.bitcast(w << 16, F32)
    return jnp.concatenate([hi, lo], axis=1)


def _mod_kernel(c_ref, w_ref, b_ref, o_ref):
    c = c_ref[...]
    s = c * _sigmoid(c)
    o_ref[...] = _dot3(s, w_ref[...], _dot) + b_ref[...]


def _modulation(cc, w_ada, b_ada):
    rows, d = cc.shape
    n = w_ada.shape[1]
    bn = 1536
    return pl.pallas_call(
        _mod_kernel,
        out_shape=jax.ShapeDtypeStruct((rows, n), F32),
        grid=(n // bn,),
        in_specs=[pl.BlockSpec((rows, d), lambda j: (0, 0)),
                  pl.BlockSpec((d, bn), lambda j: (0, j)),
                  pl.BlockSpec((1, bn), lambda j: (0, j))],
        out_specs=pl.BlockSpec((rows, bn), lambda j: (0, j)),
        compiler_params=_cparams("arbitrary"),
        name="modulation",
    )(cc, w_ada, b_ada)


def _ctx_kernel(x_ref, sh_ref, sc_ref, nw_ref, wkv_ref, wg_ref, afb_ref, bias_ref, sf_ref, sb_ref):
    x = x_ref[...]
    n = x.shape[0]
    hn = _rms(x) * nw_ref[...] * (1.0 + sc_ref[...]) + sh_ref[...]
    hb = hn.astype(BF16)
    kv = _dot(hb, wkv_ref[...])
    g = _dot(hb, wg_ref[...])
    z = _dot(g.astype(BF16), afb_ref[...]) + bias_ref[...]
    la = _log_sigmoid(z) * (1.0 / GATE_NORMALIZER)
    p = _segment_scan(la, n, reverse=False)
    p_f, p_b = p[:, :GLA_QK], p[:, GLA_QK:]
    w_f = jnp.exp(p_f[n - 1:n, :] - p_f)
    w_b = jnp.exp(p_b - la[:, GLA_QK:])
    k = kv[:, :GLA_QK]
    vb = kv[:, GLA_QK:].astype(BF16)
    ke_f = (k * w_f).astype(BF16)
    ke_b = (k * w_b).astype(BF16)
    for h in range(GLA_HEADS):
        vh = vb[:, h * GLA_DV:(h + 1) * GLA_DV]
        sf_ref[h] = _dot_tn(vh, ke_f[:, h * GLA_DK:(h + 1) * GLA_DK])
        sb_ref[h] = _dot_tn(vh, ke_b[:, h * GLA_DK:(h + 1) * GLA_DK])


def _context_states(ctx, csh1, csc1, nw1, wkv, wg, afb, bias_fb):
    bsz, n, d = ctx.shape
    full = lambda a: pl.BlockSpec(a.shape, lambda b: (0,) * a.ndim)
    st = jax.ShapeDtypeStruct((bsz, GLA_HEADS, GLA_DV, GLA_DK), F32)
    st_spec = pl.BlockSpec((None, GLA_HEADS, GLA_DV, GLA_DK), lambda b: (b, 0, 0, 0))
    return pl.pallas_call(
        _ctx_kernel,
        out_shape=(st, st),
        grid=(bsz,),
        in_specs=[pl.BlockSpec((None, n, d), lambda b: (b, 0, 0)),
                  full(csh1), full(csc1), full(nw1), full(wkv), full(wg), full(afb), full(bias_fb)],
        out_specs=(st_spec, st_spec),
        compiler_params=_cparams("arbitrary"),
        name="context_states",
    )(ctx, csh1, csc1, nw1, wkv, wg, afb, bias_fb)


def _proj_kernel(x_ref, sh_ref, sc_ref, nw_ref, wm_ref, wg_ref, afb_ref, bias_ref, cw_ref,
                 af_ref, ab_ref, v_ref, oi_ref, sr_ref, yc_ref, dec_ref):
    x = x_ref[...]
    tm = x.shape[0]
    nc = tm // CHUNK
    hn = _rms(x) * nw_ref[...] * (1.0 + sc_ref[...]) + sh_ref[...]
    hb = hn.astype(BF16)
    p = _dot(hb, wm_ref[...])
    g = _dot(hb, wg_ref[...])
    z = _dot(g.astype(BF16), afb_ref[...]) + bias_ref[...]
    la = _log_sigmoid(z) * (1.0 / GATE_NORMALIZER)

    q = p[:, 0:GLA_QK] * (GLA_DK ** -0.5)
    k = p[:, GLA_QK:2 * GLA_QK]
    vb = p[:, 2 * GLA_QK:2 * GLA_QK + GLA_WIDTH].astype(BF16)
    v_ref[...] = vb

    b_f = _segment_scan(la[:, :GLA_QK], CHUNK, reverse=False)
    b_f3 = b_f.reshape(nc, CHUNK, GLA_QK)
    tot_f = b_f3[:, CHUNK - 1:CHUNK, :]
    qd_f = (q * jnp.exp(b_f)).astype(BF16)
    ki_f = (k * jnp.exp(-b_f)).astype(BF16)
    ke_f = (k * jnp.exp(tot_f - b_f3).reshape(tm, GLA_QK)).astype(BF16)
    b_b = _segment_scan(la[:, GLA_QK:], CHUNK, reverse=True)
    b_b3 = b_b.reshape(nc, CHUNK, GLA_QK)
    tot_b = b_b3[:, 0:1, :]
    qd_b = (q * jnp.exp(b_b)).astype(BF16)
    ki_b = (k * jnp.exp(-b_b)).astype(BF16)
    ke_b = (k * jnp.exp(tot_b - b_b3).reshape(tm, GLA_QK)).astype(BF16)

    af_ref[:, :GLA_QK] = qd_f
    af_ref[:, GLA_QK:] = ke_f
    ab_ref[:, :GLA_QK] = qd_b
    ab_ref[:, GLA_QK:] = ke_b
    dec_ref[:, :GLA_QK] = jnp.exp(tot_f).reshape(nc, GLA_QK)
    dec_ref[:, GLA_QK:] = jnp.exp(tot_b).reshape(nc, GLA_QK)

    gi = INTRA_GROUP
    ri = lax.broadcasted_iota(jnp.int32, (gi, gi), 0)
    ci = lax.broadcasted_iota(jnp.int32, (gi, gi), 1)
    same = (ri // CHUNK) == (ci // CHUNK)
    lower = same & (ci <= ri)
    upper = same & (ci >= ri)
    for gidx in range(tm // gi):
        rows = slice(gidx * gi, (gidx + 1) * gi)
        for h in range(GLA_HEADS):
            lanes = slice(h * GLA_DK, (h + 1) * GLA_DK)
            s_f = _dot_nt(qd_f[rows, lanes], ki_f[rows, lanes])
            s_b = _dot_nt(qd_b[rows, lanes], ki_b[rows, lanes])
            s = jnp.where(lower, s_f, 0.0) + jnp.where(upper, s_b, 0.0)
            oi_ref[rows, h * GLA_DV:(h + 1) * GLA_DV] = _dot(
                s.astype(BF16), vb[rows, h * GLA_DV:(h + 1) * GLA_DV])

    r = p[:, 1024:1536]
    sr_ref[...] = (r * _sigmoid(r)).astype(BF16)

    u = p[:, 2048:2560] * p[:, 2560:3072]
    row = lax.broadcasted_iota(jnp.int32, u.shape, 0) & (GRID_W - 1)
    prev = jnp.where(row >= 1, pltpu.roll(u, 1, 0), 0.0)
    nxt = jnp.where(row < GRID_W - 1, pltpu.roll(u, tm - 1, 0), 0.0)
    cw = cw_ref[...]
    conv = prev * cw[0:1, :] + u * cw[1:2, :] + nxt * cw[2:3, :]
    yc_ref[...] = (p[:, 1536:2048] * conv).astype(BF16)


def _projection(x2, sh1, sc1, nw1, wm, wg, afb, bias_fb, conv_w, seq):
    n, d = x2.shape
    tm = TOKEN_TILE
    tpb = seq // tm
    full = lambda a: pl.BlockSpec(a.shape, lambda i: (0,) * a.ndim)
    per_batch = pl.BlockSpec((None, 1, d), lambda i: (i // tpb, 0, 0))
    tok = lambda w: pl.BlockSpec((tm, w), lambda i: (i, 0))
    out_shape = (
        jax.ShapeDtypeStruct((n, 2 * GLA_QK), BF16),
        jax.ShapeDtypeStruct((n, 2 * GLA_QK), BF16),
        jax.ShapeDtypeStruct((n, GLA_WIDTH), BF16),
        jax.ShapeDtypeStruct((n, GLA_WIDTH), F32),
        jax.ShapeDtypeStruct((n, GLA_WIDTH), BF16),
        jax.ShapeDtypeStruct((n, CONV_WIDTH), BF16),
        jax.ShapeDtypeStruct((n // CHUNK, 2 * GLA_QK), F32),
    )
    out_specs = (tok(2 * GLA_QK), tok(2 * GLA_QK), tok(GLA_WIDTH), tok(GLA_WIDTH), tok(GLA_WIDTH),
                 tok(CONV_WIDTH), pl.BlockSpec((tm // CHUNK, 2 * GLA_QK), lambda i: (i, 0)))
    return pl.pallas_call(
        _proj_kernel,
        out_shape=out_shape,
        grid=(n // tm,),
        in_specs=[tok(d), per_batch, per_batch, full(nw1), full(wm), full(wg), full(afb),
                  full(bias_fb), full(conv_w)],
        out_specs=out_specs,
        compiler_params=_cparams("arbitrary"),
        name="projection",
    )(x2, sh1, sc1, nw1, wm, wg, afb, bias_fb, conv_w)


def _scan_kernel(af_ref, vf_ref, df_ref, ab_ref, vb_ref, db_ref, s0f_ref, s0b_ref,
                 of_ref, ob_ref, st_ref):
    @pl.when(pl.program_id(1) == 0)
    def _():
        st_ref[0] = s0f_ref[...]
        st_ref[1] = s0b_ref[...]

    nc = af_ref.shape[0] // CHUNK

    def one_chunk(c, a_ref, v_ref, d_ref, o_ref, direction):
        r0 = pl.multiple_of(c * CHUNK, CHUNK)
        rows = pl.ds(r0, CHUNK)
        qd = a_ref[rows, 0:GLA_QK]
        ke = a_ref[rows, GLA_QK:2 * GLA_QK]
        v = v_ref[rows, :]
        d = d_ref[pl.ds(c, 1), direction * GLA_QK:(direction + 1) * GLA_QK]
        for h in range(GLA_HEADS):
            lanes = slice(h * GLA_DK, (h + 1) * GLA_DK)
            vl = slice(h * GLA_DV, (h + 1) * GLA_DV)
            s = st_ref[direction, h]
            o_ref[rows, vl] = _dot_nt(qd[:, lanes], s.astype(BF16))
            st_ref[direction, h] = s * d[:, lanes] + _dot_tn(v[:, vl], ke[:, lanes])

    def body(c, carry):
        one_chunk(c, af_ref, vf_ref, df_ref, of_ref, 0)
        one_chunk(nc - 1 - c, ab_ref, vb_ref, db_ref, ob_ref, 1)
        return carry

    lax.fori_loop(0, nc, body, 0)


def _gla_scan(a_f, a_b, vb, dec, s0f, s0b, bsz, seq):
    n = a_f.shape[0]
    tb = GLA_BLOCK
    nb = seq // tb
    fwd = lambda w: pl.BlockSpec((tb, w), lambda b, j: (b * nb + j, 0))
    bwd = lambda w: pl.BlockSpec((tb, w), lambda b, j: (b * nb + nb - 1 - j, 0))
    dfw = pl.BlockSpec((tb // CHUNK, 2 * GLA_QK), lambda b, j: (b * nb + j, 0))
    dbw = pl.BlockSpec((tb // CHUNK, 2 * GLA_QK), lambda b, j: (b * nb + nb - 1 - j, 0))
    st_spec = pl.BlockSpec((None, GLA_HEADS, GLA_DV, GLA_DK), lambda b, j: (b, 0, 0, 0))
    out = jax.ShapeDtypeStruct((n, GLA_WIDTH), F32)
    return pl.pallas_call(
        _scan_kernel,
        out_shape=(out, out),
        grid=(bsz, nb),
        in_specs=[fwd(2 * GLA_QK), fwd(GLA_WIDTH), dfw, bwd(2 * GLA_QK), bwd(GLA_WIDTH), dbw,
                  st_spec, st_spec],
        out_specs=(fwd(GLA_WIDTH), bwd(GLA_WIDTH)),
        scratch_shapes=[pltpu.VMEM((2, GLA_HEADS, GLA_DV, GLA_DK), F32)],
        compiler_params=_cparams("arbitrary", "arbitrary"),
        name="gla_scan",
    )(a_f, vb, dec, a_b, vb, dec, s0f, s0b)


def _post_kernel(x_ref, oi_ref, of_ref, ob_ref, sr_ref, yc_ref, gnw_ref, wo_ref, g1_ref, nw2_ref,
                 sh2_ref, sc2_ref, rwt_ref, rb_ref, tri_ref,
                 h_ref, xw_ref, idx_ref, rank_ref, wt_ref, cnt_ref, run_ref):
    @pl.when(pl.program_id(0) == 0)
    def _():
        run_ref[...] = jnp.zeros_like(run_ref)

    tm = x_ref.shape[0]
    o = oi_ref[...] + of_ref[...] + ob_ref[...]
    normed = jnp.concatenate(
        [_rms(o[:, h * GLA_DV:(h + 1) * GLA_DV]) for h in range(GLA_HEADS)], axis=1)
    y_gla = (normed * gnw_ref[...] * sr_ref[...].astype(F32)).astype(BF16)
    wo = wo_ref[...]
    y = _dot(y_gla, wo[:GLA_WIDTH, :]) + _dot(yc_ref[...], wo[GLA_WIDTH:, :])
    h = x_ref[...] + g1_ref[...] * y
    h_ref[...] = h
    hn = _rms(h) * nw2_ref[...] * (1.0 + sc2_ref[...]) + sh2_ref[...]
    xw_ref[...] = _pack_halves(hn)

    logits = _dot3(rwt_ref[...], hn, _dot_nt) + rb_ref[...]
    eid = lax.broadcasted_iota(jnp.int32, logits.shape, 0)
    vals, idxs = [], []
    work = logits
    for _ in range(TOP_K):
        m = jnp.max(work, axis=0, keepdims=True)
        i = jnp.min(jnp.where(work == m, eid, N_EXPERTS), axis=0, keepdims=True)
        vals.append(m)
        idxs.append(i)
        work = jnp.where(eid == i, -jnp.inf, work)
    exps = [jnp.exp(v - vals[0]) for v in vals]
    denom = exps[0] + exps[1] + exps[2] + exps[3]
    wts = [e / denom for e in exps]

    onehot = jnp.zeros(logits.shape, F32)
    for i in idxs:
        onehot = onehot + jnp.where(eid == i, 1.0, 0.0)
    before = _dot(onehot.astype(BF16), tri_ref[...]) + run_ref[:, 0:1]
    for kk in range(TOP_K):
        idx_ref[kk:kk + 1, :] = idxs[kk]
        rk = jnp.sum(jnp.where(eid == idxs[kk], before, 0.0), axis=0, keepdims=True)
        rank_ref[kk:kk + 1, :] = rk.astype(jnp.int32)
    run = run_ref[...] + jnp.sum(onehot, axis=1, keepdims=True)
    run_ref[...] = run
    cnt_ref[...] = run.astype(jnp.int32)

    sub = lax.broadcasted_iota(jnp.int32, (GATE_PAD, tm), 0)
    wpad = jnp.zeros((GATE_PAD, tm), F32)
    for kk in range(TOP_K):
        wpad = wpad + jnp.where(sub == kk, wts[kk], 0.0)
    wt_ref[...] = wpad.T


def _post_mixer(x2, o_i, o_f, o_b, sr, yc, gnw, wo, g1, nw2, sh2, sc2, rwt, rb, tri, seq):
    n, d = x2.shape
    tm = TOKEN_TILE
    tpb = seq // tm
    full = lambda a: pl.BlockSpec(a.shape, lambda i: (0,) * a.ndim)
    per_batch = pl.BlockSpec((None, 1, d), lambda i: (i // tpb, 0, 0))
    tok = lambda w: pl.BlockSpec((tm, w), lambda i: (i, 0))
    lane_tok = pl.BlockSpec((TOP_K, tm), lambda i: (0, i))
    out_shape = (
        jax.ShapeDtypeStruct((n, d), F32),
        jax.ShapeDtypeStruct((n, d // 2), jnp.uint32),
        jax.ShapeDtypeStruct((TOP_K, n), jnp.int32),
        jax.ShapeDtypeStruct((TOP_K, n), jnp.int32),
        jax.ShapeDtypeStruct((n, GATE_PAD), F32),
        jax.ShapeDtypeStruct((N_EXPERTS, GATE_PAD), jnp.int32),
    )
    out_specs = (tok(d), tok(d // 2), lane_tok, lane_tok, tok(GATE_PAD),
                 pl.BlockSpec((N_EXPERTS, GATE_PAD), lambda i: (0, 0)))
    return pl.pallas_call(
        _post_kernel,
        out_shape=out_shape,
        grid=(n // tm,),
        in_specs=[tok(d), tok(GLA_WIDTH), tok(GLA_WIDTH), tok(GLA_WIDTH), tok(GLA_WIDTH),
                  tok(CONV_WIDTH), full(gnw), full(wo), per_batch, full(nw2), per_batch, per_batch,
                  full(rwt), full(rb), full(tri)],
        out_specs=out_specs,
        scratch_shapes=[pltpu.VMEM((N_EXPERTS, GATE_PAD), F32)],
        compiler_params=_cparams("arbitrary"),
        name="post_mixer",
    )(x2, o_i, o_f, o_b, sr, yc, gnw, wo, g1, nw2, sh2, sc2, rwt, rb, tri)


def _slot_kernel(start_ref, idx_ref, rank_ref, slot_ref):
    idx = idx_ref[...]
    base = jnp.zeros(idx.shape, jnp.int32)
    for e in range(N_EXPERTS):
        base = jnp.where(idx == e, start_ref[e], base)
    slot_ref[...] = base + rank_ref[...]


def _slots(group_start, idx_t, rank_t):
    k, n = idx_t.shape
    bn = min(n, 8192)
    blk = pl.BlockSpec((k, bn), lambda i, s: (0, i))
    return pl.pallas_call(
        _slot_kernel,
        out_shape=jax.ShapeDtypeStruct((k, n), jnp.int32),
        grid_spec=pltpu.PrefetchScalarGridSpec(
            num_scalar_prefetch=1, grid=(n // bn,), in_specs=[blk, blk], out_specs=blk),
        compiler_params=_cparams("arbitrary"),
        name="slots",
    )(group_start, idx_t, rank_t)


SC_CORES = 2
SC_SUBCORES = 16
SC_WORKERS = SC_CORES * SC_SUBCORES
SCATTER_CHUNK = 128
GATHER_CHUNK = 64


def _sc_mesh():
    return plsc.VectorSubcoreMesh(core_axis_name="c", subcore_axis_name="s")


def _sc_worker():
    return lax.axis_index("s") * SC_CORES + lax.axis_index("c")


def _sc_scatter_rows(src, idx, n_out):
    n, w = src.shape
    m = idx.shape[0]
    c = SCATTER_CHUNK
    per = m // SC_WORKERS
    nch = per // c
    assert per * SC_WORKERS == m and nch * c == per and n % per == 0

    def body(src_hbm, idx_hbm, out_hbm, idx_v, rows_v):
        wid = _sc_worker()
        src_base = lax.rem(wid * per, n)
        pltpu.sync_copy(idx_hbm.at[wid], idx_v)

        @pl.loop(0, nch)
        def _(j):
            pltpu.sync_copy(src_hbm.at[pl.ds(src_base + j * c, c)], rows_v)
            pltpu.sync_copy(rows_v, out_hbm.at[idx_v.at[j]])

    return pl.kernel(
        body,
        out_type=jax.ShapeDtypeStruct((n_out, w), src.dtype),
        mesh=_sc_mesh(),
        scratch_types=[pltpu.VMEM((nch, c), jnp.int32), pltpu.VMEM((c, w), src.dtype)],
        name="sc_dispatch",
    )(src, idx.reshape(SC_WORKERS, nch, c))


def _sc_gather_rows(table, idx):
    w = table.shape[1]
    m = idx.shape[0]
    c = GATHER_CHUNK
    per = m // SC_WORKERS
    nch = per // c
    assert per * SC_WORKERS == m and nch * c == per and nch % 2 == 0

    def body(table_hbm, idx_hbm, out_hbm, idx_v, rows_v, sem0, sem1):
        wid = _sc_worker()
        base = wid * per
        sems = (sem0, sem1)
        pltpu.sync_copy(idx_hbm.at[wid], idx_v)

        def gather(j, b):
            return pltpu.make_async_copy(table_hbm.at[idx_v.at[j]], rows_v.at[b], sems[b])

        gather(0, 0).start()

        @pl.loop(0, nch, step=2)
        def _(g):
            for b in range(2):
                j = g + b
                gather(j, b).wait()

                @pl.when(j + 1 < nch)
                def _():
                    gather(j + 1, 1 - b).start()

                pltpu.sync_copy(rows_v.at[b], out_hbm.at[pl.ds(base + j * c, c)])

    return pl.kernel(
        body,
        out_type=jax.ShapeDtypeStruct((m, w), table.dtype),
        mesh=_sc_mesh(),
        scratch_types=[pltpu.VMEM((nch, c), jnp.int32), pltpu.VMEM((2, c, w), table.dtype),
                       pltpu.SemaphoreType.DMA, pltpu.SemaphoreType.DMA],
        name="sc_combine_gather",
    )(table, idx.reshape(SC_WORKERS, nch, c))


PERM_BLOCK = 256


def _gu_split_kernel(w_ref, perm_ref, wg_ref, wu_ref):
    half = PERM_BLOCK // 2
    perm = perm_ref[...]
    for b in range(w_ref.shape[1] // PERM_BLOCK):
        blk = w_ref[:, b * PERM_BLOCK:(b + 1) * PERM_BLOCK].astype(BF16)
        t = _dot(blk, perm)
        wg_ref[:, b * half:(b + 1) * half] = t[:, :half].astype(BF16)
        wu_ref[:, b * half:(b + 1) * half] = t[:, half:].astype(BF16)


def _gu_split(w_gu):
    ne, d, two_f = w_gu.shape
    rows = 256
    half = PERM_BLOCK // 2
    j = jnp.arange(PERM_BLOCK)
    perm = (jnp.arange(PERM_BLOCK)[None, :] == (j // 2 + (j % 2) * half)[:, None]).astype(BF16)
    out = jax.ShapeDtypeStruct((ne, d, two_f // 2), BF16)
    ospec = pl.BlockSpec((None, rows, two_f // 2), lambda e, r: (e, r, 0))
    return pl.pallas_call(
        _gu_split_kernel,
        out_shape=(out, out),
        grid=(ne, d // rows),
        in_specs=[pl.BlockSpec((None, rows, two_f), lambda e, r: (e, r, 0)),
                  pl.BlockSpec((PERM_BLOCK, PERM_BLOCK), lambda e, r: (0, 0))],
        out_specs=(ospec, ospec),
        compiler_params=_cparams("arbitrary", "arbitrary"),
        name="gu_split",
    )(w_gu, perm)


def _expert_kernel(te_ref, nu_ref, x_ref, wg_ref, wu_ref, wd_ref, bg_ref, bu_ref, bd_ref, y_ref):
    i = pl.program_id(0)

    @pl.when(i < nu_ref[0])
    def _():
        x = _unpack_halves(x_ref[...]).astype(BF16)
        g = _dot(x, wg_ref[...]) + bg_ref[...]
        u = _dot(x, wu_ref[...]) + bu_ref[...]
        gate = jnp.minimum(g, SWIGLU_LIMIT)
        up = jnp.clip(u, -SWIGLU_LIMIT, SWIGLU_LIMIT)
        act = (up + 1.0) * (gate * _sigmoid(SWIGLU_ALPHA * gate))
        y = _dot(act.astype(BF16), wd_ref[...]) + bd_ref[...]
        y_ref[...] = _pack_halves(y)

    @pl.when(i >= nu_ref[0])
    def _():
        y_ref[...] = jnp.zeros_like(y_ref)


def _experts(tile_expert, n_used, xs, wg, wu, wd, bg, bu, bd):
    n_slots, half = xs.shape
    tm = EXPERT_TILE
    n_tiles = n_slots // tm
    d = wg.shape[1]
    row_in = pl.BlockSpec((tm, half), lambda i, te, nu: (jnp.minimum(i, nu[0] - 1), 0))
    row_out = pl.BlockSpec((tm, half), lambda i, te, nu: (jnp.where(i < nu[0], i, n_tiles - 1), 0))
    wspec = lambda a: pl.BlockSpec((None,) + a.shape[1:], lambda i, te, nu: (te[i], 0, 0))
    return pl.pallas_call(
        _expert_kernel,
        out_shape=jax.ShapeDtypeStruct((n_slots, half), jnp.uint32),
        grid_spec=pltpu.PrefetchScalarGridSpec(
            num_scalar_prefetch=2, grid=(n_tiles,),
            in_specs=[row_in, wspec(wg), wspec(wu), wspec(wd), wspec(bg), wspec(bu), wspec(bd)],
            out_specs=row_out),
        compiler_params=_cparams("arbitrary"),
        name="experts",
    )(tile_expert, n_used, xs, wg, wu, wd, bg, bu, bd)


def _combine_kernel(y4_ref, h_ref, wt_ref, g2_ref, fnw_ref, o_ref):
    wt = wt_ref[...]
    acc = jnp.zeros(h_ref.shape, F32)
    for kk in range(TOP_K):
        acc = acc + wt[:, kk:kk + 1] * _unpack_halves(y4_ref[kk])
    h = h_ref[...] + g2_ref[...] * acc
    o_ref[...] = _rms(h) * fnw_ref[...]


def _combine(y4, h, wt, g2, fnw, seq):
    n, d = h.shape
    tt = COMBINE_TILE
    tpb = seq // tt
    return pl.pallas_call(
        _combine_kernel,
        out_shape=jax.ShapeDtypeStruct((n, d), F32),
        grid=(n // tt,),
        in_specs=[pl.BlockSpec((TOP_K, tt, d // 2), lambda i: (0, i, 0)),
                  pl.BlockSpec((tt, d), lambda i: (i, 0)),
                  pl.BlockSpec((tt, GATE_PAD), lambda i: (i, 0)),
                  pl.BlockSpec((None, 1, d), lambda i: (i // tpb, 0, 0)),
                  pl.BlockSpec((1, d), lambda i: (0, 0))],
        out_specs=pl.BlockSpec((tt, d), lambda i: (i, 0)),
        compiler_params=_cparams("arbitrary"),
        name="combine",
    )(y4, h, wt, g2, fnw)


def kernel(x, c, ctx, c_ctx, w_ada, b_ada, norm1_w, w_in, a_up_f, a_bias_f, a_up_b, a_bias_b,
           gla_norm_w, conv_w, w_out, norm2_w, router_w, router_b, w_gu, b_gu, w_dn, b_dn,
           final_norm_w):
    bsz, seq, d = x.shape
    n = bsz * seq
    assert w_ada.shape[0] == 1 and d == D_MODEL
    assert seq % TOKEN_TILE == 0 and seq % GLA_BLOCK == 0 and seq % GRID_W == 0
    assert ctx.shape[1] & (ctx.shape[1] - 1) == 0

    w_in0 = w_in[0]
    v_end = 2 * GLA_QK + GLA_WIDTH
    r_end = v_end + GLA_WIDTH
    g_end = r_end + 2 * GATE_RANK
    wm = jnp.concatenate([w_in0[:, :r_end], w_in0[:, g_end:]], axis=1).astype(BF16)
    wg_low = jnp.pad(w_in0[:, r_end:g_end], ((0, 0), (0, GATE_PAD - 2 * GATE_RANK))).astype(BF16)
    afb = jnp.zeros((GATE_PAD, 2 * GLA_QK), F32)
    afb = afb.at[:GATE_RANK, :GLA_QK].set(a_up_f[0]).at[GATE_RANK:2 * GATE_RANK, GLA_QK:].set(a_up_b[0])
    afb = afb.astype(BF16)
    bias_fb = jnp.concatenate([a_bias_f[0], a_bias_b[0]])[None]
    wkv = wm[:, GLA_QK:v_end]
    row = lambda a: a.reshape(1, -1)

    mod = _modulation(jnp.concatenate([c, c_ctx[None]], axis=0), w_ada[0], row(b_ada[0]))
    sh1, sc1, g1, sh2, sc2, g2 = [mod[:bsz, i * d:(i + 1) * d].reshape(bsz, 1, d) for i in range(6)]
    csh1, csc1 = mod[bsz:, 0:d], mod[bsz:, d:2 * d]

    s0f, s0b = _context_states(ctx, csh1, csc1, row(norm1_w[0]), wkv, wg_low, afb, bias_fb)

    x2 = x.reshape(n, d)
    a_f, a_b, vb, o_i, sr, yc, dec = _projection(
        x2, sh1, sc1, row(norm1_w[0]), wm, wg_low, afb, bias_fb, conv_w[0], seq)
    o_f, o_b = _gla_scan(a_f, a_b, vb, dec, s0f, s0b, bsz, seq)

    tri = jnp.triu(jnp.ones((TOKEN_TILE, TOKEN_TILE), BF16), k=1)
    h, xw, idx_t, rank_t, wt, cnt = _post_mixer(
        x2, o_i, o_f, o_b, sr, yc, jnp.tile(gla_norm_w[0], GLA_HEADS)[None], w_out[0].astype(BF16),
        g1, row(norm2_w[0]), sh2, sc2, router_w[0].T, router_b[0][:, None], tri, seq)

    counts = cnt[:, 0]
    padded = (counts + EXPERT_TILE - 1) // EXPERT_TILE * EXPERT_TILE
    group_end = jnp.cumsum(padded)
    group_start = (group_end - padded).astype(jnp.int32)
    n_tiles = n * TOP_K // EXPERT_TILE + N_EXPERTS
    n_used = (group_end[-1] // EXPERT_TILE).astype(jnp.int32)
    tile_ids = jnp.minimum(jnp.arange(n_tiles, dtype=jnp.int32), n_used - 1) * EXPERT_TILE
    tile_expert = jnp.minimum(jnp.sum(tile_ids[:, None] >= group_end[None, :], axis=1),
                              N_EXPERTS - 1).astype(jnp.int32)

    slot_t = _slots(group_start, idx_t, rank_t)
    slot_flat = slot_t.reshape(-1)
    xs = _sc_scatter_rows(xw, slot_flat, n_tiles * EXPERT_TILE)
    wg_e, wu_e = _gu_split(w_gu[0])
    ys = _experts(tile_expert, n_used.reshape(1), xs, wg_e, wu_e, w_dn[0].astype(BF16),
                  b_gu[0][:, None, 0::2], b_gu[0][:, None, 1::2], b_dn[0][:, None, :])
    y4 = _sc_gather_rows(ys, slot_flat).reshape(TOP_K, n, d // 2)
    out = _combine(y4, h, wt, g2, row(final_norm_w), seq)
    return out.reshape(bsz, seq, d)
```

```python
import functools

import jax
import jax.numpy as jnp
from jax import lax
from jax.experimental import pallas as pl
from jax.experimental.pallas import tpu as pltpu
from jax.experimental.pallas import tpu_sc as plsc

D_MODEL = 1024
GLA_HEADS = 4
GLA_DK = 64
GLA_DV = 128
GLA_QK = GLA_HEADS * GLA_DK
GLA_WIDTH = GLA_HEADS * GLA_DV
CONV_WIDTH = 512
GATE_RANK = 16
GATE_PAD = 128
GATE_NORMALIZER = 16.0
CHUNK = 64
GRID_W = 64
N_EXPERTS = 32
TOP_K = 4
SWIGLU_LIMIT = 7.0
SWIGLU_ALPHA = 1.702
EPS = 1e-6

TOKEN_TILE = 512
GLA_BLOCK = 512
INTRA_GROUP = 256
EXPERT_TILE = 512
COMBINE_TILE = 512
N_CHUNKS = 2
VMEM_LIMIT = 56 * 1024 * 1024

F32 = jnp.float32
BF16 = jnp.bfloat16
HI_MASK = 0xFFFF0000


def _cparams(*sem):
    return pltpu.CompilerParams(dimension_semantics=sem, vmem_limit_bytes=VMEM_LIMIT)


def _rms(x):
    return x * lax.rsqrt(jnp.mean(x * x, axis=-1, keepdims=True) + EPS)


def _sigmoid(x):
    return 1.0 / (1.0 + jnp.exp(-x))


def _log_sigmoid(z):
    return jnp.minimum(z, 0.0) - jnp.log(1.0 + jnp.exp(-jnp.abs(z)))


def _dot(a, b):
    return jnp.dot(a, b, preferred_element_type=F32)


def _dot_nt(a, b):
    return lax.dot_general(a, b, (((1,), (1,)), ((), ())), preferred_element_type=F32)


def _dot_tn(a, b):
    return lax.dot_general(a, b, (((0,), (0,)), ((), ())), preferred_element_type=F32)


def _split_bf16(a):
    hi = a.astype(BF16)
    lo = (a - hi.astype(F32)).astype(BF16)
    return hi, lo


def _dot3(a, b, dot):
    a_hi, a_lo = _split_bf16(a)
    b_hi, b_lo = _split_bf16(b)
    return dot(a_hi, b_hi) + (dot(a_hi, b_lo) + dot(a_lo, b_hi))


def _segment_scan(x, seg, reverse):
    n = x.shape[0]
    row = lax.broadcasted_iota(jnp.int32, x.shape, 0) & (seg - 1)
    s = 1
    while s < seg:
        if reverse:
            shifted = pltpu.roll(x, n - s, 0)
            x = x + jnp.where(row < seg - s, shifted, 0.0)
        else:
            shifted = pltpu.roll(x, s, 0)
            x = x + jnp.where(row >= s, shifted, 0.0)
        s *= 2
    return x


def _pack_halves(x):
    n = x.shape[1] // 2
    bits = pltpu.bitcast(x.astype(BF16).astype(F32), jnp.uint32)
    return (bits[:, :n] & jnp.uint32(HI_MASK)) | (bits[:, n:] >> 16)


def _unpack_halves(w):
    hi = pltpu.bitcast(w & jnp.uint32(HI_MASK), F32)
    lo = pltpu.bitcast(w << 16, F32)
    return jnp.concatenate([hi, lo], axis=1)


def _mod_kernel(c_ref, w_ref, b_ref, o_ref):
    c = c_ref[...]
    s = c * _sigmoid(c)
    o_ref[...] = _dot3(s, w_ref[...], _dot) + b_ref[...]


def _modulation(cc, w_ada, b_ada):
    rows, d = cc.shape
    n = w_ada.shape[1]
    bn = 1536
    return pl.pallas_call(
        _mod_kernel,
        out_shape=jax.ShapeDtypeStruct((rows, n), F32),
        grid=(n // bn,),
        in_specs=[pl.BlockSpec((rows, d), lambda j: (0, 0)),
                  pl.BlockSpec((d, bn), lambda j: (0, j)),
                  pl.BlockSpec((1, bn), lambda j: (0, j))],
        out_specs=pl.BlockSpec((rows, bn), lambda j: (0, j)),
        compiler_params=_cparams("arbitrary"),
        name="modulation",
    )(cc, w_ada, b_ada)


def _ctx_kernel(x_ref, sh_ref, sc_ref, nw_ref, wkv_ref, wg_ref, afb_ref, bias_ref, sf_ref, sb_ref):
    x = x_ref[...]
    n = x.shape[0]
    hn = _rms(x) * nw_ref[...] * (1.0 + sc_ref[...]) + sh_ref[...]
    hb = hn.astype(BF16)
    kv = _dot(hb, wkv_ref[...])
    g = _dot(hb, wg_ref[...])
    z = _dot(g.astype(BF16), afb_ref[...]) + bias_ref[...]
    la = _log_sigmoid(z) * (1.0 / GATE_NORMALIZER)
    p = _segment_scan(la, n, reverse=False)
    p_f, p_b = p[:, :GLA_QK], p[:, GLA_QK:]
    w_f = jnp.exp(p_f[n - 1:n, :] - p_f)
    w_b = jnp.exp(p_b - la[:, GLA_QK:])
    k = kv[:, :GLA_QK]
    vb = kv[:, GLA_QK:].astype(BF16)
    ke_f = (k * w_f).astype(BF16)
    ke_b = (k * w_b).astype(BF16)
    for h in range(GLA_HEADS):
        vh = vb[:, h * GLA_DV:(h + 1) * GLA_DV]
        sf_ref[h] = _dot_tn(vh, ke_f[:, h * GLA_DK:(h + 1) * GLA_DK])
        sb_ref[h] = _dot_tn(vh, ke_b[:, h * GLA_DK:(h + 1) * GLA_DK])


def _context_states(ctx, csh1, csc1, nw1, wkv, wg, afb, bias_fb):
    bsz, n, d = ctx.shape
    full = lambda a: pl.BlockSpec(a.shape, lambda b: (0,) * a.ndim)
    st = jax.ShapeDtypeStruct((bsz, GLA_HEADS, GLA_DV, GLA_DK), F32)
    st_spec = pl.BlockSpec((None, GLA_HEADS, GLA_DV, GLA_DK), lambda b: (b, 0, 0, 0))
    return pl.pallas_call(
        _ctx_kernel,
        out_shape=(st, st),
        grid=(bsz,),
        in_specs=[pl.BlockSpec((None, n, d), lambda b: (b, 0, 0)),
                  full(csh1), full(csc1), full(nw1), full(wkv), full(wg), full(afb), full(bias_fb)],
        out_specs=(st_spec, st_spec),
        compiler_params=_cparams("arbitrary"),
        name="context_states",
    )(ctx, csh1, csc1, nw1, wkv, wg, afb, bias_fb)


def _proj_kernel(x_ref, sh_ref, sc_ref, nw_ref, wm_ref, wg_ref, afb_ref, bias_ref, cw_ref,
                 af_ref, ab_ref, v_ref, oi_ref, sr_ref, yc_ref, dec_ref):
    x = x_ref[...]
    tm = x.shape[0]
    nc = tm // CHUNK
    hn = _rms(x) * nw_ref[...] * (1.0 + sc_ref[...]) + sh_ref[...]
    hb = hn.astype(BF16)
    p = _dot(hb, wm_ref[...])
    g = _dot(hb, wg_ref[...])
    z = _dot(g.astype(BF16), afb_ref[...]) + bias_ref[...]
    la = _log_sigmoid(z) * (1.0 / GATE_NORMALIZER)

    q = p[:, 0:GLA_QK] * (GLA_DK ** -0.5)
    k = p[:, GLA_QK:2 * GLA_QK]
    vb = p[:, 2 * GLA_QK:2 * GLA_QK + GLA_WIDTH].astype(BF16)
    v_ref[...] = vb

    b_f = _segment_scan(la[:, :GLA_QK], CHUNK, reverse=False)
    b_f3 = b_f.reshape(nc, CHUNK, GLA_QK)
    tot_f = b_f3[:, CHUNK - 1:CHUNK, :]
    qd_f = (q * jnp.exp(b_f)).astype(BF16)
    ki_f = (k * jnp.exp(-b_f)).astype(BF16)
    ke_f = (k * jnp.exp(tot_f - b_f3).reshape(tm, GLA_QK)).astype(BF16)
    b_b = _segment_scan(la[:, GLA_QK:], CHUNK, reverse=True)
    b_b3 = b_b.reshape(nc, CHUNK, GLA_QK)
    tot_b = b_b3[:, 0:1, :]
    qd_b = (q * jnp.exp(b_b)).astype(BF16)
    ki_b = (k * jnp.exp(-b_b)).astype(BF16)
    ke_b = (k * jnp.exp(tot_b - b_b3).reshape(tm, GLA_QK)).astype(BF16)

    af_ref[:, :GLA_QK] = qd_f
    af_ref[:, GLA_QK:] = ke_f
    ab_ref[:, :GLA_QK] = qd_b
    ab_ref[:, GLA_QK:] = ke_b
    dec_ref[:, :GLA_QK] = jnp.exp(tot_f).reshape(nc, GLA_QK)
    dec_ref[:, GLA_QK:] = jnp.exp(tot_b).reshape(nc, GLA_QK)

    gi = INTRA_GROUP
    ri = lax.broadcasted_iota(jnp.int32, (gi, gi), 0)
    ci = lax.broadcasted_iota(jnp.int32, (gi, gi), 1)
    same = (ri // CHUNK) == (ci // CHUNK)
    lower = same & (ci <= ri)
    upper = same & (ci >= ri)
    for gidx in range(tm // gi):
        rows = slice(gidx * gi, (gidx + 1) * gi)
        for h in range(GLA_HEADS):
            lanes = slice(h * GLA_DK, (h + 1) * GLA_DK)
            s_f = _dot_nt(qd_f[rows, lanes], ki_f[rows, lanes])
            s_b = _dot_nt(qd_b[rows, lanes], ki_b[rows, lanes])
            s = jnp.where(lower, s_f, 0.0) + jnp.where(upper, s_b, 0.0)
            oi_ref[rows, h * GLA_DV:(h + 1) * GLA_DV] = _dot(
                s.astype(BF16), vb[rows, h * GLA_DV:(h + 1) * GLA_DV])

    r = p[:, 1024:1536]
    sr_ref[...] = (r * _sigmoid(r)).astype(BF16)

    u = p[:, 2048:2560] * p[:, 2560:3072]
    row = lax.broadcasted_iota(jnp.int32, u.shape, 0) & (GRID_W - 1)
    prev = jnp.where(row >= 1, pltpu.roll(u, 1, 0), 0.0)
    nxt = jnp.where(row < GRID_W - 1, pltpu.roll(u, tm - 1, 0), 0.0)
    cw = cw_ref[...]
    conv = prev * cw[0:1, :] + u * cw[1:2, :] + nxt * cw[2:3, :]
    yc_ref[...] = (p[:, 1536:2048] * conv).astype(BF16)


def _projection(x2, row0, n, sh1, sc1, nw1, wm, wg, afb, bias_fb, conv_w, seq):
    d = x2.shape[1]
    tm = TOKEN_TILE
    tpb = seq // tm
    t0 = row0 // tm
    full = lambda a: pl.BlockSpec(a.shape, lambda i: (0,) * a.ndim)
    per_batch = pl.BlockSpec((None, 1, d), lambda i: (i // tpb, 0, 0))
    tok = lambda w: pl.BlockSpec((tm, w), lambda i: (i, 0))
    out_shape = (
        jax.ShapeDtypeStruct((n, 2 * GLA_QK), BF16),
        jax.ShapeDtypeStruct((n, 2 * GLA_QK), BF16),
        jax.ShapeDtypeStruct((n, GLA_WIDTH), BF16),
        jax.ShapeDtypeStruct((n, GLA_WIDTH), F32),
        jax.ShapeDtypeStruct((n, GLA_WIDTH), BF16),
        jax.ShapeDtypeStruct((n, CONV_WIDTH), BF16),
        jax.ShapeDtypeStruct((n // CHUNK, 2 * GLA_QK), F32),
    )
    out_specs = (tok(2 * GLA_QK), tok(2 * GLA_QK), tok(GLA_WIDTH), tok(GLA_WIDTH), tok(GLA_WIDTH),
                 tok(CONV_WIDTH), pl.BlockSpec((tm // CHUNK, 2 * GLA_QK), lambda i: (i, 0)))
    return pl.pallas_call(
        _proj_kernel,
        out_shape=out_shape,
        grid=(n // tm,),
        in_specs=[pl.BlockSpec((tm, d), lambda i: (i + t0, 0)), per_batch, per_batch, full(nw1),
                  full(wm), full(wg), full(afb), full(bias_fb), full(conv_w)],
        out_specs=out_specs,
        compiler_params=_cparams("arbitrary"),
        name="projection",
    )(x2, sh1, sc1, nw1, wm, wg, afb, bias_fb, conv_w)


def _scan_kernel(af_ref, vf_ref, df_ref, ab_ref, vb_ref, db_ref, s0f_ref, s0b_ref,
                 of_ref, ob_ref, st_ref):
    @pl.when(pl.program_id(1) == 0)
    def _():
        st_ref[0] = s0f_ref[...]
        st_ref[1] = s0b_ref[...]

    nc = af_ref.shape[0] // CHUNK

    def one_chunk(c, a_ref, v_ref, d_ref, o_ref, direction):
        r0 = pl.multiple_of(c * CHUNK, CHUNK)
        rows = pl.ds(r0, CHUNK)
        qd = a_ref[rows, 0:GLA_QK]
        ke = a_ref[rows, GLA_QK:2 * GLA_QK]
        v = v_ref[rows, :]
        d = d_ref[pl.ds(c, 1), direction * GLA_QK:(direction + 1) * GLA_QK]
        for h in range(GLA_HEADS):
            lanes = slice(h * GLA_DK, (h + 1) * GLA_DK)
            vl = slice(h * GLA_DV, (h + 1) * GLA_DV)
            s = st_ref[direction, h]
            o_ref[rows, vl] = _dot_nt(qd[:, lanes], s.astype(BF16))
            st_ref[direction, h] = s * d[:, lanes] + _dot_tn(v[:, vl], ke[:, lanes])

    def body(c, carry):
        one_chunk(c, af_ref, vf_ref, df_ref, of_ref, 0)
        one_chunk(nc - 1 - c, ab_ref, vb_ref, db_ref, ob_ref, 1)
        return carry

    lax.fori_loop(0, nc, body, 0)


def _gla_scan(a_f, a_b, vb, dec, s0f, s0b, bsz, seq):
    n = a_f.shape[0]
    tb = GLA_BLOCK
    nb = seq // tb
    fwd = lambda w: pl.BlockSpec((tb, w), lambda b, j: (b * nb + j, 0))
    bwd = lambda w: pl.BlockSpec((tb, w), lambda b, j: (b * nb + nb - 1 - j, 0))
    dfw = pl.BlockSpec((tb // CHUNK, 2 * GLA_QK), lambda b, j: (b * nb + j, 0))
    dbw = pl.BlockSpec((tb // CHUNK, 2 * GLA_QK), lambda b, j: (b * nb + nb - 1 - j, 0))
    st_spec = pl.BlockSpec((None, GLA_HEADS, GLA_DV, GLA_DK), lambda b, j: (b, 0, 0, 0))
    out = jax.ShapeDtypeStruct((n, GLA_WIDTH), F32)
    return pl.pallas_call(
        _scan_kernel,
        out_shape=(out, out),
        grid=(bsz, nb),
        in_specs=[fwd(2 * GLA_QK), fwd(GLA_WIDTH), dfw, bwd(2 * GLA_QK), bwd(GLA_WIDTH), dbw,
                  st_spec, st_spec],
        out_specs=(fwd(GLA_WIDTH), bwd(GLA_WIDTH)),
        scratch_shapes=[pltpu.VMEM((2, GLA_HEADS, GLA_DV, GLA_DK), F32)],
        compiler_params=_cparams("arbitrary", "arbitrary"),
        name="gla_scan",
    )(a_f, vb, dec, a_b, vb, dec, s0f, s0b)


def _post_kernel(x_ref, oi_ref, of_ref, ob_ref, sr_ref, yc_ref, gnw_ref, wo_ref, g1_ref, nw2_ref,
                 sh2_ref, sc2_ref, rwt_ref, rb_ref, tri_ref,
                 h_ref, xw_ref, idx_ref, rank_ref, wt_ref, cnt_ref, run_ref):
    @pl.when(pl.program_id(0) == 0)
    def _():
        run_ref[...] = jnp.zeros_like(run_ref)

    tm = x_ref.shape[0]
    o = oi_ref[...] + of_ref[...] + ob_ref[...]
    normed = jnp.concatenate(
        [_rms(o[:, h * GLA_DV:(h + 1) * GLA_DV]) for h in range(GLA_HEADS)], axis=1)
    y_gla = (normed * gnw_ref[...] * sr_ref[...].astype(F32)).astype(BF16)
    wo = wo_ref[...]
    y = _dot(y_gla, wo[:GLA_WIDTH, :]) + _dot(yc_ref[...], wo[GLA_WIDTH:, :])
    h = x_ref[...] + g1_ref[...] * y
    h_ref[...] = h
    hn = _rms(h) * nw2_ref[...] * (1.0 + sc2_ref[...]) + sh2_ref[...]
    xw_ref[...] = _pack_halves(hn)

    logits = _dot3(rwt_ref[...], hn, _dot_nt) + rb_ref[...]
    eid = lax.broadcasted_iota(jnp.int32, logits.shape, 0)
    vals, idxs = [], []
    work = logits
    for _ in range(TOP_K):
        m = jnp.max(work, axis=0, keepdims=True)
        i = jnp.min(jnp.where(work == m, eid, N_EXPERTS), axis=0, keepdims=True)
        vals.append(m)
        idxs.append(i)
        work = jnp.where(eid == i, -jnp.inf, work)
    exps = [jnp.exp(v - vals[0]) for v in vals]
    denom = exps[0] + exps[1] + exps[2] + exps[3]
    wts = [e / denom for e in exps]

    onehot = jnp.zeros(logits.shape, F32)
    for i in idxs:
        onehot = onehot + jnp.where(eid == i, 1.0, 0.0)
    before = _dot(onehot.astype(BF16), tri_ref[...]) + run_ref[:, 0:1]
    for kk in range(TOP_K):
        idx_ref[kk:kk + 1, :] = idxs[kk]
        rk = jnp.sum(jnp.where(eid == idxs[kk], before, 0.0), axis=0, keepdims=True)
        rank_ref[kk:kk + 1, :] = rk.astype(jnp.int32)
    run = run_ref[...] + jnp.sum(onehot, axis=1, keepdims=True)
    run_ref[...] = run
    cnt_ref[...] = run.astype(jnp.int32)

    sub = lax.broadcasted_iota(jnp.int32, (GATE_PAD, tm), 0)
    wpad = jnp.zeros((GATE_PAD, tm), F32)
    for kk in range(TOP_K):
        wpad = wpad + jnp.where(sub == kk, wts[kk], 0.0)
    wt_ref[...] = wpad.T


def _post_mixer(x2, row0, o_i, o_f, o_b, sr, yc, gnw, wo, g1, nw2, sh2, sc2, rwt, rb, tri, seq):
    n, d = o_i.shape[0], x2.shape[1]
    tm = TOKEN_TILE
    tpb = seq // tm
    t0 = row0 // tm
    full = lambda a: pl.BlockSpec(a.shape, lambda i: (0,) * a.ndim)
    per_batch = pl.BlockSpec((None, 1, d), lambda i: (i // tpb, 0, 0))
    tok = lambda w: pl.BlockSpec((tm, w), lambda i: (i, 0))
    lane_tok = pl.BlockSpec((TOP_K, tm), lambda i: (0, i))
    out_shape = (
        jax.ShapeDtypeStruct((n, d), F32),
        jax.ShapeDtypeStruct((n, d // 2), jnp.uint32),
        jax.ShapeDtypeStruct((TOP_K, n), jnp.int32),
        jax.ShapeDtypeStruct((TOP_K, n), jnp.int32),
        jax.ShapeDtypeStruct((n, GATE_PAD), F32),
        jax.ShapeDtypeStruct((N_EXPERTS, GATE_PAD), jnp.int32),
    )
    out_specs = (tok(d), tok(d // 2), lane_tok, lane_tok, tok(GATE_PAD),
                 pl.BlockSpec((N_EXPERTS, GATE_PAD), lambda i: (0, 0)))
    return pl.pallas_call(
        _post_kernel,
        out_shape=out_shape,
        grid=(n // tm,),
        in_specs=[pl.BlockSpec((tm, d), lambda i: (i + t0, 0)), tok(GLA_WIDTH), tok(GLA_WIDTH),
                  tok(GLA_WIDTH), tok(GLA_WIDTH),
                  tok(CONV_WIDTH), full(gnw), full(wo), per_batch, full(nw2), per_batch, per_batch,
                  full(rwt), full(rb), full(tri)],
        out_specs=out_specs,
        scratch_shapes=[pltpu.VMEM((N_EXPERTS, GATE_PAD), F32)],
        compiler_params=_cparams("arbitrary"),
        name="post_mixer",
    )(x2, o_i, o_f, o_b, sr, yc, gnw, wo, g1, nw2, sh2, sc2, rwt, rb, tri)


def _slot_kernel(start_ref, idx_ref, rank_ref, slot_ref):
    idx = idx_ref[...]
    base = jnp.zeros(idx.shape, jnp.int32)
    for e in range(N_EXPERTS):
        base = jnp.where(idx == e, start_ref[e], base)
    slot_ref[...] = base + rank_ref[...]


def _slots(group_start, idx_t, rank_t):
    k, n = idx_t.shape
    bn = min(n, 8192)
    blk = pl.BlockSpec((k, bn), lambda i, s: (0, i))
    return pl.pallas_call(
        _slot_kernel,
        out_shape=jax.ShapeDtypeStruct((k, n), jnp.int32),
        grid_spec=pltpu.PrefetchScalarGridSpec(
            num_scalar_prefetch=1, grid=(n // bn,), in_specs=[blk, blk], out_specs=blk),
        compiler_params=_cparams("arbitrary"),
        name="slots",
    )(group_start, idx_t, rank_t)


SC_CORES = 2
SC_SUBCORES = 16
SC_WORKERS = SC_CORES * SC_SUBCORES
SCATTER_CHUNK = 128
GATHER_CHUNK = 64


def _sc_mesh():
    return plsc.VectorSubcoreMesh(core_axis_name="c", subcore_axis_name="s")


def _sc_worker():
    return lax.axis_index("s") * SC_CORES + lax.axis_index("c")


def _sc_scatter_rows(src, idx, n_out):
    n, w = src.shape
    m = idx.shape[0]
    c = SCATTER_CHUNK
    per = m // SC_WORKERS
    nch = per // c
    assert per * SC_WORKERS == m and nch * c == per and n % per == 0

    def body(src_hbm, idx_hbm, out_hbm, idx_v, rows_v):
        wid = _sc_worker()
        src_base = lax.rem(wid * per, n)
        pltpu.sync_copy(idx_hbm.at[wid], idx_v)

        @pl.loop(0, nch)
        def _(j):
            pltpu.sync_copy(src_hbm.at[pl.ds(src_base + j * c, c)], rows_v)
            pltpu.sync_copy(rows_v, out_hbm.at[idx_v.at[j]])

    return pl.kernel(
        body,
        out_type=jax.ShapeDtypeStruct((n_out, w), src.dtype),
        mesh=_sc_mesh(),
        scratch_types=[pltpu.VMEM((nch, c), jnp.int32), pltpu.VMEM((c, w), src.dtype)],
        name="sc_dispatch",
    )(src, idx.reshape(SC_WORKERS, nch, c))


def _sc_gather_rows(table, idx):
    w = table.shape[1]
    m = idx.shape[0]
    c = GATHER_CHUNK
    per = m // SC_WORKERS
    nch = per // c
    assert per * SC_WORKERS == m and nch * c == per and nch % 2 == 0

    def body(table_hbm, idx_hbm, out_hbm, idx_v, rows_v, sem0, sem1):
        wid = _sc_worker()
        base = wid * per
        sems = (sem0, sem1)
        pltpu.sync_copy(idx_hbm.at[wid], idx_v)

        def gather(j, b):
            return pltpu.make_async_copy(table_hbm.at[idx_v.at[j]], rows_v.at[b], sems[b])

        gather(0, 0).start()

        @pl.loop(0, nch, step=2)
        def _(g):
            for b in range(2):
                j = g + b
                gather(j, b).wait()

                @pl.when(j + 1 < nch)
                def _():
                    gather(j + 1, 1 - b).start()

                pltpu.sync_copy(rows_v.at[b], out_hbm.at[pl.ds(base + j * c, c)])

    return pl.kernel(
        body,
        out_type=jax.ShapeDtypeStruct((m, w), table.dtype),
        mesh=_sc_mesh(),
        scratch_types=[pltpu.VMEM((nch, c), jnp.int32), pltpu.VMEM((2, c, w), table.dtype),
                       pltpu.SemaphoreType.DMA, pltpu.SemaphoreType.DMA],
        name="sc_combine_gather",
    )(table, idx.reshape(SC_WORKERS, nch, c))


PERM_BLOCK = 256


def _gu_split_kernel(w_ref, perm_ref, wg_ref, wu_ref):
    half = PERM_BLOCK // 2
    perm = perm_ref[...]
    for b in range(w_ref.shape[1] // PERM_BLOCK):
        blk = w_ref[:, b * PERM_BLOCK:(b + 1) * PERM_BLOCK].astype(BF16)
        t = _dot(blk, perm)
        wg_ref[:, b * half:(b + 1) * half] = t[:, :half].astype(BF16)
        wu_ref[:, b * half:(b + 1) * half] = t[:, half:].astype(BF16)


def _gu_split(w_gu):
    ne, d, two_f = w_gu.shape
    rows = 256
    half = PERM_BLOCK // 2
    j = jnp.arange(PERM_BLOCK)
    perm = (jnp.arange(PERM_BLOCK)[None, :] == (j // 2 + (j % 2) * half)[:, None]).astype(BF16)
    out = jax.ShapeDtypeStruct((ne, d, two_f // 2), BF16)
    ospec = pl.BlockSpec((None, rows, two_f // 2), lambda e, r: (e, r, 0))
    return pl.pallas_call(
        _gu_split_kernel,
        out_shape=(out, out),
        grid=(ne, d // rows),
        in_specs=[pl.BlockSpec((None, rows, two_f), lambda e, r: (e, r, 0)),
                  pl.BlockSpec((PERM_BLOCK, PERM_BLOCK), lambda e, r: (0, 0))],
        out_specs=(ospec, ospec),
        compiler_params=_cparams("arbitrary", "arbitrary"),
        name="gu_split",
    )(w_gu, perm)


def _expert_kernel(te_ref, nu_ref, x_ref, wg_ref, wu_ref, wd_ref, bg_ref, bu_ref, bd_ref, y_ref):
    i = pl.program_id(0)

    @pl.when(i < nu_ref[0])
    def _():
        x = _unpack_halves(x_ref[...]).astype(BF16)
        g = _dot(x, wg_ref[...]) + bg_ref[...]
        u = _dot(x, wu_ref[...]) + bu_ref[...]
        gate = jnp.minimum(g, SWIGLU_LIMIT)
        up = jnp.clip(u, -SWIGLU_LIMIT, SWIGLU_LIMIT)
        act = (up + 1.0) * (gate * _sigmoid(SWIGLU_ALPHA * gate))
        y = _dot(act.astype(BF16), wd_ref[...]) + bd_ref[...]
        y_ref[...] = _pack_halves(y)

    @pl.when(i >= nu_ref[0])
    def _():
        y_ref[...] = jnp.zeros_like(y_ref)


def _experts(tile_expert, n_used, xs, wg, wu, wd, bg, bu, bd):
    n_slots, half = xs.shape
    tm = EXPERT_TILE
    n_tiles = n_slots // tm
    d = wg.shape[1]
    row_in = pl.BlockSpec((tm, half), lambda i, te, nu: (jnp.minimum(i, nu[0] - 1), 0))
    row_out = pl.BlockSpec((tm, half), lambda i, te, nu: (jnp.where(i < nu[0], i, n_tiles - 1), 0))
    wspec = lambda a: pl.BlockSpec((None,) + a.shape[1:], lambda i, te, nu: (te[i], 0, 0))
    return pl.pallas_call(
        _expert_kernel,
        out_shape=jax.ShapeDtypeStruct((n_slots, half), jnp.uint32),
        grid_spec=pltpu.PrefetchScalarGridSpec(
            num_scalar_prefetch=2, grid=(n_tiles,),
            in_specs=[row_in, wspec(wg), wspec(wu), wspec(wd), wspec(bg), wspec(bu), wspec(bd)],
            out_specs=row_out),
        compiler_params=_cparams("arbitrary"),
        name="experts",
    )(tile_expert, n_used, xs, wg, wu, wd, bg, bu, bd)


def _combine_kernel(y4_ref, h_ref, wt_ref, g2_ref, fnw_ref, *rest):
    o_ref = rest[-1]
    wt = wt_ref[...]
    acc = jnp.zeros(h_ref.shape, F32)
    for kk in range(TOP_K):
        acc = acc + wt[:, kk:kk + 1] * _unpack_halves(y4_ref[kk])
    h = h_ref[...] + g2_ref[...] * acc
    o_ref[...] = _rms(h) * fnw_ref[...]


def _combine(y4, h, wt, g2, fnw, seq, row0, n_total, prev_out):
    n, d = h.shape
    tt = COMBINE_TILE
    tpb = seq // tt
    t0 = row0 // tt
    in_specs = [pl.BlockSpec((TOP_K, tt, d // 2), lambda i: (0, i, 0)),
                pl.BlockSpec((tt, d), lambda i: (i, 0)),
                pl.BlockSpec((tt, GATE_PAD), lambda i: (i, 0)),
                pl.BlockSpec((None, 1, d), lambda i: (i // tpb, 0, 0)),
                pl.BlockSpec((1, d), lambda i: (0, 0))]
    args = [y4, h, wt, g2, fnw]
    aliases = {}
    if prev_out is not None:
        in_specs.append(pl.BlockSpec(memory_space=pl.ANY))
        args.append(prev_out)
        aliases = {len(args) - 1: 0}
    return pl.pallas_call(
        _combine_kernel,
        out_shape=jax.ShapeDtypeStruct((n_total, d), F32),
        grid=(n // tt,),
        in_specs=in_specs,
        out_specs=pl.BlockSpec((tt, d), lambda i: (i + t0, 0)),
        input_output_aliases=aliases,
        compiler_params=_cparams("arbitrary"),
        name="combine",
    )(*args)


def kernel(x, c, ctx, c_ctx, w_ada, b_ada, norm1_w, w_in, a_up_f, a_bias_f, a_up_b, a_bias_b,
           gla_norm_w, conv_w, w_out, norm2_w, router_w, router_b, w_gu, b_gu, w_dn, b_dn,
           final_norm_w):
    bsz, seq, d = x.shape
    n = bsz * seq
    assert w_ada.shape[0] == 1 and d == D_MODEL
    assert seq % TOKEN_TILE == 0 and seq % GLA_BLOCK == 0 and seq % GRID_W == 0
    assert ctx.shape[1] & (ctx.shape[1] - 1) == 0

    w_in0 = w_in[0]
    v_end = 2 * GLA_QK + GLA_WIDTH
    r_end = v_end + GLA_WIDTH
    g_end = r_end + 2 * GATE_RANK
    wm = jnp.concatenate([w_in0[:, :r_end], w_in0[:, g_end:]], axis=1).astype(BF16)
    wg_low = jnp.pad(w_in0[:, r_end:g_end], ((0, 0), (0, GATE_PAD - 2 * GATE_RANK))).astype(BF16)
    afb = jnp.zeros((GATE_PAD, 2 * GLA_QK), F32)
    afb = afb.at[:GATE_RANK, :GLA_QK].set(a_up_f[0]).at[GATE_RANK:2 * GATE_RANK, GLA_QK:].set(a_up_b[0])
    afb = afb.astype(BF16)
    bias_fb = jnp.concatenate([a_bias_f[0], a_bias_b[0]])[None]
    wkv = wm[:, GLA_QK:v_end]
    row = lambda a: a.reshape(1, -1)

    mod = _modulation(jnp.concatenate([c, c_ctx[None]], axis=0), w_ada[0], row(b_ada[0]))
    sh1, sc1, g1, sh2, sc2, g2 = [mod[:bsz, i * d:(i + 1) * d].reshape(bsz, 1, d) for i in range(6)]
    csh1, csc1 = mod[bsz:, 0:d], mod[bsz:, d:2 * d]

    s0f, s0b = _context_states(ctx, csh1, csc1, row(norm1_w[0]), wkv, wg_low, afb, bias_fb)

    x2 = x.reshape(n, d)
    tri = jnp.triu(jnp.ones((TOKEN_TILE, TOKEN_TILE), BF16), k=1)
    gnw = jnp.tile(gla_norm_w[0], GLA_HEADS)[None]
    wo = w_out[0].astype(BF16)
    wg_e, wu_e = _gu_split(w_gu[0])
    wd_e = w_dn[0].astype(BF16)
    bg_e, bu_e, bd_e = b_gu[0][:, None, 0::2], b_gu[0][:, None, 1::2], b_dn[0][:, None, :]

    n_chunks = N_CHUNKS if bsz % N_CHUNKS == 0 else 1
    cb = bsz // n_chunks
    nc = cb * seq
    n_tiles = nc * TOP_K // EXPERT_TILE + N_EXPERTS
    routed = []
    for ci in range(n_chunks):
        bs = slice(ci * cb, (ci + 1) * cb)
        row0 = ci * nc
        a_f, a_b, vb, o_i, sr, yc, dec = _projection(
            x2, row0, nc, sh1[bs], sc1[bs], row(norm1_w[0]), wm, wg_low, afb, bias_fb, conv_w[0], seq)
        o_f, o_b = _gla_scan(a_f, a_b, vb, dec, s0f[bs], s0b[bs], cb, seq)
        h, xw, idx_t, rank_t, wt, cnt = _post_mixer(
            x2, row0, o_i, o_f, o_b, sr, yc, gnw, wo, g1[bs], row(norm2_w[0]), sh2[bs], sc2[bs],
            router_w[0].T, router_b[0][:, None], tri, seq)

        counts = cnt[:, 0]
        padded = (counts + EXPERT_TILE - 1) // EXPERT_TILE * EXPERT_TILE
        group_end = jnp.cumsum(padded)
        group_start = (group_end - padded).astype(jnp.int32)
        n_used = (group_end[-1] // EXPERT_TILE).astype(jnp.int32)
        tile_ids = jnp.minimum(jnp.arange(n_tiles, dtype=jnp.int32), n_used - 1) * EXPERT_TILE
        tile_expert = jnp.minimum(jnp.sum(tile_ids[:, None] >= group_end[None, :], axis=1),
                                  N_EXPERTS - 1).astype(jnp.int32)
        slot_flat = _slots(group_start, idx_t, rank_t).reshape(-1)
        xs = _sc_scatter_rows(xw, slot_flat, n_tiles * EXPERT_TILE)
        routed.append((h, wt, slot_flat, xs, tile_expert, n_used, bs, row0))

    gathered = []
    for h, wt, slot_flat, xs, tile_expert, n_used, bs, row0 in routed:
        ys = _experts(tile_expert, n_used.reshape(1), xs, wg_e, wu_e, wd_e, bg_e, bu_e, bd_e)
        y4 = _sc_gather_rows(ys, slot_flat).reshape(TOP_K, nc, d // 2)
        gathered.append((y4, h, wt, bs, row0))

    out = None
    for y4, h, wt, bs, row0 in gathered:
        out = _combine(y4, h, wt, g2[bs], row(final_norm_w), seq, row0, n, out)
    return out.reshape(bsz, seq, d)
```

```python
import functools

import jax
import jax.numpy as jnp
from jax import lax
from jax.experimental import pallas as pl
from jax.experimental.pallas import tpu as pltpu
from jax.experimental.pallas import tpu_sc as plsc

D_MODEL = 1024
GLA_HEADS = 4
GLA_DK = 64
GLA_DV = 128
GLA_QK = GLA_HEADS * GLA_DK
GLA_WIDTH = GLA_HEADS * GLA_DV
CONV_WIDTH = 512
GATE_RANK = 16
GATE_PAD = 128
GATE_NORMALIZER = 16.0
CHUNK = 64
GRID_W = 64
N_EXPERTS = 32
TOP_K = 4
SWIGLU_LIMIT = 7.0
SWIGLU_ALPHA = 1.702
EPS = 1e-6

TOKEN_TILE = 512
PROJ_SUB = 256
GLA_BLOCK = 512
INTRA_GROUP = 256
EXPERT_TILE = 512
COMBINE_TILE = 512
N_CHUNKS = 2
VMEM_LIMIT = 56 * 1024 * 1024

F32 = jnp.float32
BF16 = jnp.bfloat16
HI_MASK = 0xFFFF0000


def _cparams(*sem):
    return pltpu.CompilerParams(dimension_semantics=sem, vmem_limit_bytes=VMEM_LIMIT)


def _rms(x):
    return x * lax.rsqrt(jnp.mean(x * x, axis=-1, keepdims=True) + EPS)


def _sigmoid(x):
    return 1.0 / (1.0 + jnp.exp(-x))


def _log_sigmoid(z):
    return jnp.minimum(z, 0.0) - jnp.log(1.0 + jnp.exp(-jnp.abs(z)))


def _dot(a, b):
    return jnp.dot(a, b, preferred_element_type=F32)


def _dot_nt(a, b):
    return lax.dot_general(a, b, (((1,), (1,)), ((), ())), preferred_element_type=F32)


def _dot_tn(a, b):
    return lax.dot_general(a, b, (((0,), (0,)), ((), ())), preferred_element_type=F32)


def _split_bf16(a):
    hi = a.astype(BF16)
    lo = (a - hi.astype(F32)).astype(BF16)
    return hi, lo


def _dot3(a, b, dot):
    a_hi, a_lo = _split_bf16(a)
    b_hi, b_lo = _split_bf16(b)
    return dot(a_hi, b_hi) + (dot(a_hi, b_lo) + dot(a_lo, b_hi))


def _segment_scan(x, seg, reverse):
    n = x.shape[0]
    row = lax.broadcasted_iota(jnp.int32, x.shape, 0) & (seg - 1)
    s = 1
    while s < seg:
        if reverse:
            shifted = pltpu.roll(x, n - s, 0)
            x = x + jnp.where(row < seg - s, shifted, 0.0)
        else:
            shifted = pltpu.roll(x, s, 0)
            x = x + jnp.where(row >= s, shifted, 0.0)
        s *= 2
    return x


def _pack_halves(x):
    n = x.shape[1] // 2
    bits = pltpu.bitcast(x.astype(BF16).astype(F32), jnp.uint32)
    return (bits[:, :n] & jnp.uint32(HI_MASK)) | (bits[:, n:] >> 16)


def _unpack_halves(w):
    hi = pltpu.bitcast(w & jnp.uint32(HI_MASK), F32)
    lo = pltpu.bitcast(w << 16, F32)
    return jnp.concatenate([hi, lo], axis=1)


def _mod_kernel(c_ref, w_ref, b_ref, o_ref):
    c = c_ref[...]
    s = c * _sigmoid(c)
    o_ref[...] = _dot3(s, w_ref[...], _dot) + b_ref[...]


def _modulation(cc, w_ada, b_ada):
    rows, d = cc.shape
    n = w_ada.shape[1]
    bn = 1536
    return pl.pallas_call(
        _mod_kernel,
        out_shape=jax.ShapeDtypeStruct((rows, n), F32),
        grid=(n // bn,),
        in_specs=[pl.BlockSpec((rows, d), lambda j: (0, 0)),
                  pl.BlockSpec((d, bn), lambda j: (0, j)),
                  pl.BlockSpec((1, bn), lambda j: (0, j))],
        out_specs=pl.BlockSpec((rows, bn), lambda j: (0, j)),
        compiler_params=_cparams("arbitrary"),
        name="modulation",
    )(cc, w_ada, b_ada)


def _ctx_kernel(x_ref, sh_ref, sc_ref, nw_ref, wkv_ref, wg_ref, afb_ref, bias_ref, sf_ref, sb_ref):
    x = x_ref[...]
    n = x.shape[0]
    hn = _rms(x) * nw_ref[...] * (1.0 + sc_ref[...]) + sh_ref[...]
    hb = hn.astype(BF16)
    kv = _dot(hb, wkv_ref[...])
    g = _dot(hb, wg_ref[...])
    z = _dot(g.astype(BF16), afb_ref[...]) + bias_ref[...]
    la = _log_sigmoid(z) * (1.0 / GATE_NORMALIZER)
    p = _segment_scan(la, n, reverse=False)
    p_f, p_b = p[:, :GLA_QK], p[:, GLA_QK:]
    w_f = jnp.exp(p_f[n - 1:n, :] - p_f)
    w_b = jnp.exp(p_b - la[:, GLA_QK:])
    k = kv[:, :GLA_QK]
    vb = kv[:, GLA_QK:].astype(BF16)
    ke_f = (k * w_f).astype(BF16)
    ke_b = (k * w_b).astype(BF16)
    for h in range(GLA_HEADS):
        vh = vb[:, h * GLA_DV:(h + 1) * GLA_DV]
        sf_ref[h] = _dot_tn(vh, ke_f[:, h * GLA_DK:(h + 1) * GLA_DK])
        sb_ref[h] = _dot_tn(vh, ke_b[:, h * GLA_DK:(h + 1) * GLA_DK])


def _context_states(ctx, csh1, csc1, nw1, wkv, wg, afb, bias_fb):
    bsz, n, d = ctx.shape
    full = lambda a: pl.BlockSpec(a.shape, lambda b: (0,) * a.ndim)
    st = jax.ShapeDtypeStruct((bsz, GLA_HEADS, GLA_DV, GLA_DK), F32)
    st_spec = pl.BlockSpec((None, GLA_HEADS, GLA_DV, GLA_DK), lambda b: (b, 0, 0, 0))
    return pl.pallas_call(
        _ctx_kernel,
        out_shape=(st, st),
        grid=(bsz,),
        in_specs=[pl.BlockSpec((None, n, d), lambda b: (b, 0, 0)),
                  full(csh1), full(csc1), full(nw1), full(wkv), full(wg), full(afb), full(bias_fb)],
        out_specs=(st_spec, st_spec),
        compiler_params=_cparams("arbitrary"),
        name="context_states",
    )(ctx, csh1, csc1, nw1, wkv, wg, afb, bias_fb)


def _proj_kernel(x_ref, sh_ref, sc_ref, nw_ref, wm_ref, wg_ref, afb_ref, bias_ref, cw_ref,
                 af_ref, ab_ref, v_ref, oi_ref, sr_ref, yc_ref, dec_ref, p_scr):
    subs = range(x_ref.shape[0] // PROJ_SUB)
    gates = []
    for sub in subs:
        x = x_ref[sub * PROJ_SUB:(sub + 1) * PROJ_SUB, :]
        hn = _rms(x) * nw_ref[...] * (1.0 + sc_ref[...]) + sh_ref[...]
        hb = hn.astype(BF16)
        p_scr[sub] = _dot(hb, wm_ref[...])
        gates.append(_dot(hb, wg_ref[...]))
    for sub in subs:
        _proj_rows(sub, gates[sub], afb_ref, bias_ref, cw_ref,
                   af_ref, ab_ref, v_ref, oi_ref, sr_ref, yc_ref, dec_ref, p_scr.at[sub])


def _proj_rows(sub, g, afb_ref, bias_ref, cw_ref,
               af_ref, ab_ref, v_ref, oi_ref, sr_ref, yc_ref, dec_ref, p):
    tm = PROJ_SUB
    nc = tm // CHUNK
    rs = slice(sub * tm, (sub + 1) * tm)
    cs = slice(sub * nc, (sub + 1) * nc)
    z = _dot(g.astype(BF16), afb_ref[...]) + bias_ref[...]
    la = _log_sigmoid(z) * (1.0 / GATE_NORMALIZER)

    q = p[:, 0:GLA_QK] * (GLA_DK ** -0.5)
    k = p[:, GLA_QK:2 * GLA_QK]
    vb = p[:, 2 * GLA_QK:2 * GLA_QK + GLA_WIDTH].astype(BF16)
    v_ref[rs, :] = vb

    b_f = _segment_scan(la[:, :GLA_QK], CHUNK, reverse=False)
    b_f3 = b_f.reshape(nc, CHUNK, GLA_QK)
    tot_f = b_f3[:, CHUNK - 1:CHUNK, :]
    qd_f = (q * jnp.exp(b_f)).astype(BF16)
    ki_f = (k * jnp.exp(-b_f)).astype(BF16)
    ke_f = (k * jnp.exp(tot_f - b_f3).reshape(tm, GLA_QK)).astype(BF16)
    b_b = _segment_scan(la[:, GLA_QK:], CHUNK, reverse=True)
    b_b3 = b_b.reshape(nc, CHUNK, GLA_QK)
    tot_b = b_b3[:, 0:1, :]
    qd_b = (q * jnp.exp(b_b)).astype(BF16)
    ki_b = (k * jnp.exp(-b_b)).astype(BF16)
    ke_b = (k * jnp.exp(tot_b - b_b3).reshape(tm, GLA_QK)).astype(BF16)

    af_ref[rs, :GLA_QK] = qd_f
    af_ref[rs, GLA_QK:] = ke_f
    ab_ref[rs, :GLA_QK] = qd_b
    ab_ref[rs, GLA_QK:] = ke_b
    dec_ref[cs, :GLA_QK] = jnp.exp(tot_f).reshape(nc, GLA_QK)
    dec_ref[cs, GLA_QK:] = jnp.exp(tot_b).reshape(nc, GLA_QK)

    gi = INTRA_GROUP
    ri = lax.broadcasted_iota(jnp.int32, (gi, gi), 0)
    ci = lax.broadcasted_iota(jnp.int32, (gi, gi), 1)
    same = (ri // CHUNK) == (ci // CHUNK)
    lower = same & (ci <= ri)
    upper = same & (ci >= ri)
    for gidx in range(tm // gi):
        rows = slice(gidx * gi, (gidx + 1) * gi)
        orows = slice(sub * tm + gidx * gi, sub * tm + (gidx + 1) * gi)
        for h in range(GLA_HEADS):
            lanes = slice(h * GLA_DK, (h + 1) * GLA_DK)
            s_f = _dot_nt(qd_f[rows, lanes], ki_f[rows, lanes])
            s_b = _dot_nt(qd_b[rows, lanes], ki_b[rows, lanes])
            s = jnp.where(lower, s_f, 0.0) + jnp.where(upper, s_b, 0.0)
            oi_ref[orows, h * GLA_DV:(h + 1) * GLA_DV] = _dot(
                s.astype(BF16), vb[rows, h * GLA_DV:(h + 1) * GLA_DV])

    r = p[:, 1024:1536]
    sr_ref[rs, :] = (r * _sigmoid(r)).astype(BF16)

    u = p[:, 2048:2560] * p[:, 2560:3072]
    row = lax.broadcasted_iota(jnp.int32, u.shape, 0) & (GRID_W - 1)
    prev = jnp.where(row >= 1, pltpu.roll(u, 1, 0), 0.0)
    nxt = jnp.where(row < GRID_W - 1, pltpu.roll(u, tm - 1, 0), 0.0)
    cw = cw_ref[...]
    conv = prev * cw[0:1, :] + u * cw[1:2, :] + nxt * cw[2:3, :]
    yc_ref[rs, :] = (p[:, 1536:2048] * conv).astype(BF16)


def _projection(x2, row0, n, sh1, sc1, nw1, wm, wg, afb, bias_fb, conv_w, seq):
    d = x2.shape[1]
    tm = TOKEN_TILE
    tpb = seq // tm
    t0 = row0 // tm
    full = lambda a: pl.BlockSpec(a.shape, lambda i: (0,) * a.ndim)
    per_batch = pl.BlockSpec((None, 1, d), lambda i: (i // tpb, 0, 0))
    tok = lambda w: pl.BlockSpec((tm, w), lambda i: (i, 0))
    out_shape = (
        jax.ShapeDtypeStruct((n, 2 * GLA_QK), BF16),
        jax.ShapeDtypeStruct((n, 2 * GLA_QK), BF16),
        jax.ShapeDtypeStruct((n, GLA_WIDTH), BF16),
        jax.ShapeDtypeStruct((n, GLA_WIDTH), F32),
        jax.ShapeDtypeStruct((n, GLA_WIDTH), BF16),
        jax.ShapeDtypeStruct((n, CONV_WIDTH), BF16),
        jax.ShapeDtypeStruct((n // CHUNK, 2 * GLA_QK), F32),
    )
    out_specs = (tok(2 * GLA_QK), tok(2 * GLA_QK), tok(GLA_WIDTH), tok(GLA_WIDTH), tok(GLA_WIDTH),
                 tok(CONV_WIDTH), pl.BlockSpec((tm // CHUNK, 2 * GLA_QK), lambda i: (i, 0)))
    return pl.pallas_call(
        _proj_kernel,
        out_shape=out_shape,
        grid=(n // tm,),
        in_specs=[pl.BlockSpec((tm, d), lambda i: (i + t0, 0)), per_batch, per_batch, full(nw1),
                  full(wm), full(wg), full(afb), full(bias_fb), full(conv_w)],
        out_specs=out_specs,
        scratch_shapes=[pltpu.VMEM((tm // PROJ_SUB, PROJ_SUB, wm.shape[1]), F32)],
        compiler_params=_cparams("arbitrary"),
        name="projection",
    )(x2, sh1, sc1, nw1, wm, wg, afb, bias_fb, conv_w)


def _scan_kernel(af_ref, vf_ref, df_ref, ab_ref, vb_ref, db_ref, s0f_ref, s0b_ref,
                 of_ref, ob_ref, st_ref):
    @pl.when(pl.program_id(1) == 0)
    def _():
        st_ref[0] = s0f_ref[...]
        st_ref[1] = s0b_ref[...]

    nc = af_ref.shape[0] // CHUNK

    for direction, (a_ref, v_ref, d_ref, o_ref) in enumerate(
            ((af_ref, vf_ref, df_ref, of_ref), (ab_ref, vb_ref, db_ref, ob_ref))):
        order = range(nc) if direction == 0 else range(nc - 1, -1, -1)
        for h in range(GLA_HEADS):
            lanes = slice(h * GLA_DK, (h + 1) * GLA_DK)
            klanes = slice(GLA_QK + h * GLA_DK, GLA_QK + (h + 1) * GLA_DK)
            dlanes = slice(direction * GLA_QK + h * GLA_DK, direction * GLA_QK + (h + 1) * GLA_DK)
            vl = slice(h * GLA_DV, (h + 1) * GLA_DV)
            rows = [slice(c * CHUNK, (c + 1) * CHUNK) for c in range(nc)]
            kv = [_dot_tn(v_ref[rows[c], vl], a_ref[rows[c], klanes]) for c in range(nc)]
            s = st_ref[direction, h]
            start = [None] * nc
            for c in order:
                start[c] = s
                s = s * d_ref[c:c + 1, dlanes] + kv[c]
            st_ref[direction, h] = s
            for c in range(nc):
                o_ref[rows[c], vl] = _dot_nt(a_ref[rows[c], lanes], start[c].astype(BF16))


def _gla_scan(a_f, a_b, vb, dec, s0f, s0b, bsz, seq):
    n = a_f.shape[0]
    tb = GLA_BLOCK
    nb = seq // tb
    fwd = lambda w: pl.BlockSpec((tb, w), lambda b, j: (b * nb + j, 0))
    bwd = lambda w: pl.BlockSpec((tb, w), lambda b, j: (b * nb + nb - 1 - j, 0))
    dfw = pl.BlockSpec((tb // CHUNK, 2 * GLA_QK), lambda b, j: (b * nb + j, 0))
    dbw = pl.BlockSpec((tb // CHUNK, 2 * GLA_QK), lambda b, j: (b * nb + nb - 1 - j, 0))
    st_spec = pl.BlockSpec((None, GLA_HEADS, GLA_DV, GLA_DK), lambda b, j: (b, 0, 0, 0))
    out = jax.ShapeDtypeStruct((n, GLA_WIDTH), F32)
    return pl.pallas_call(
        _scan_kernel,
        out_shape=(out, out),
        grid=(bsz, nb),
        in_specs=[fwd(2 * GLA_QK), fwd(GLA_WIDTH), dfw, bwd(2 * GLA_QK), bwd(GLA_WIDTH), dbw,
                  st_spec, st_spec],
        out_specs=(fwd(GLA_WIDTH), bwd(GLA_WIDTH)),
        scratch_shapes=[pltpu.VMEM((2, GLA_HEADS, GLA_DV, GLA_DK), F32)],
        compiler_params=_cparams("arbitrary", "arbitrary"),
        name="gla_scan",
    )(a_f, vb, dec, a_b, vb, dec, s0f, s0b)


def _post_kernel(x_ref, oi_ref, of_ref, ob_ref, sr_ref, yc_ref, gnw_ref, wo_ref, g1_ref, nw2_ref,
                 sh2_ref, sc2_ref, rwt_ref, rb_ref, tri_ref,
                 h_ref, xw_ref, idx_ref, rank_ref, wt_ref, cnt_ref, run_ref):
    @pl.when(pl.program_id(0) == 0)
    def _():
        run_ref[...] = jnp.zeros_like(run_ref)

    tm = x_ref.shape[0]
    o = oi_ref[...] + of_ref[...] + ob_ref[...]
    normed = jnp.concatenate(
        [_rms(o[:, h * GLA_DV:(h + 1) * GLA_DV]) for h in range(GLA_HEADS)], axis=1)
    y_gla = (normed * gnw_ref[...] * sr_ref[...].astype(F32)).astype(BF16)
    wo = wo_ref[...]
    y = _dot(y_gla, wo[:GLA_WIDTH, :]) + _dot(yc_ref[...], wo[GLA_WIDTH:, :])
    h = x_ref[...] + g1_ref[...] * y
    h_ref[...] = h
    hn = _rms(h) * nw2_ref[...] * (1.0 + sc2_ref[...]) + sh2_ref[...]
    xw_ref[...] = _pack_halves(hn)

    logits = _dot3(rwt_ref[...], hn, _dot_nt) + rb_ref[...]
    eid = lax.broadcasted_iota(jnp.int32, logits.shape, 0)
    vals, idxs = [], []
    work = logits
    for _ in range(TOP_K):
        m = jnp.max(work, axis=0, keepdims=True)
        i = jnp.min(jnp.where(work == m, eid, N_EXPERTS), axis=0, keepdims=True)
        vals.append(m)
        idxs.append(i)
        work = jnp.where(eid == i, -jnp.inf, work)
    exps = [jnp.exp(v - vals[0]) for v in vals]
    denom = exps[0] + exps[1] + exps[2] + exps[3]
    wts = [e / denom for e in exps]

    onehot = jnp.zeros(logits.shape, F32)
    for i in idxs:
        onehot = onehot + jnp.where(eid == i, 1.0, 0.0)
    before = _dot(onehot.astype(BF16), tri_ref[...]) + run_ref[:, 0:1]
    for kk in range(TOP_K):
        idx_ref[kk:kk + 1, :] = idxs[kk]
        rk = jnp.sum(jnp.where(eid == idxs[kk], before, 0.0), axis=0, keepdims=True)
        rank_ref[kk:kk + 1, :] = rk.astype(jnp.int32)
    run = run_ref[...] + jnp.sum(onehot, axis=1, keepdims=True)
    run_ref[...] = run
    cnt_ref[...] = run.astype(jnp.int32)

    sub = lax.broadcasted_iota(jnp.int32, (GATE_PAD, tm), 0)
    wpad = jnp.zeros((GATE_PAD, tm), F32)
    for kk in range(TOP_K):
        wpad = wpad + jnp.where(sub == kk, wts[kk], 0.0)
    wt_ref[...] = wpad.T


def _post_mixer(x2, row0, o_i, o_f, o_b, sr, yc, gnw, wo, g1, nw2, sh2, sc2, rwt, rb, tri, seq):
    n, d = o_i.shape[0], x2.shape[1]
    tm = TOKEN_TILE
    tpb = seq // tm
    t0 = row0 // tm
    full = lambda a: pl.BlockSpec(a.shape, lambda i: (0,) * a.ndim)
    per_batch = pl.BlockSpec((None, 1, d), lambda i: (i // tpb, 0, 0))
    tok = lambda w: pl.BlockSpec((tm, w), lambda i: (i, 0))
    lane_tok = pl.BlockSpec((TOP_K, tm), lambda i: (0, i))
    out_shape = (
        jax.ShapeDtypeStruct((n, d), F32),
        jax.ShapeDtypeStruct((n, d // 2), jnp.uint32),
        jax.ShapeDtypeStruct((TOP_K, n), jnp.int32),
        jax.ShapeDtypeStruct((TOP_K, n), jnp.int32),
        jax.ShapeDtypeStruct((n, GATE_PAD), F32),
        jax.ShapeDtypeStruct((N_EXPERTS, GATE_PAD), jnp.int32),
    )
    out_specs = (tok(d), tok(d // 2), lane_tok, lane_tok, tok(GATE_PAD),
                 pl.BlockSpec((N_EXPERTS, GATE_PAD), lambda i: (0, 0)))
    return pl.pallas_call(
        _post_kernel,
        out_shape=out_shape,
        grid=(n // tm,),
        in_specs=[pl.BlockSpec((tm, d), lambda i: (i + t0, 0)), tok(GLA_WIDTH), tok(GLA_WIDTH),
                  tok(GLA_WIDTH), tok(GLA_WIDTH),
                  tok(CONV_WIDTH), full(gnw), full(wo), per_batch, full(nw2), per_batch, per_batch,
                  full(rwt), full(rb), full(tri)],
        out_specs=out_specs,
        scratch_shapes=[pltpu.VMEM((N_EXPERTS, GATE_PAD), F32)],
        compiler_params=_cparams("arbitrary"),
        name="post_mixer",
    )(x2, o_i, o_f, o_b, sr, yc, gnw, wo, g1, nw2, sh2, sc2, rwt, rb, tri)


def _slot_kernel(start_ref, idx_ref, rank_ref, slot_ref):
    idx = idx_ref[...]
    base = jnp.zeros(idx.shape, jnp.int32)
    for e in range(N_EXPERTS):
        base = jnp.where(idx == e, start_ref[e], base)
    slot_ref[...] = base + rank_ref[...]


def _slots(group_start, idx_t, rank_t):
    k, n = idx_t.shape
    bn = min(n, 8192)
    blk = pl.BlockSpec((k, bn), lambda i, s: (0, i))
    return pl.pallas_call(
        _slot_kernel,
        out_shape=jax.ShapeDtypeStruct((k, n), jnp.int32),
        grid_spec=pltpu.PrefetchScalarGridSpec(
            num_scalar_prefetch=1, grid=(n // bn,), in_specs=[blk, blk], out_specs=blk),
        compiler_params=_cparams("arbitrary"),
        name="slots",
    )(group_start, idx_t, rank_t)


SC_CORES = 2
SC_SUBCORES = 16
SC_WORKERS = SC_CORES * SC_SUBCORES
SCATTER_CHUNK = 128
GATHER_CHUNK = 64


def _sc_mesh():
    return plsc.VectorSubcoreMesh(core_axis_name="c", subcore_axis_name="s")


def _sc_worker():
    return lax.axis_index("s") * SC_CORES + lax.axis_index("c")


def _sc_scatter_rows(src, idx, n_out):
    n, w = src.shape
    m = idx.shape[0]
    c = SCATTER_CHUNK
    per = m // SC_WORKERS
    nch = per // c
    assert per * SC_WORKERS == m and nch * c == per and n % per == 0

    def body(src_hbm, idx_hbm, out_hbm, idx_v, rows_v):
        wid = _sc_worker()
        src_base = lax.rem(wid * per, n)
        pltpu.sync_copy(idx_hbm.at[wid], idx_v)

        @pl.loop(0, nch)
        def _(j):
            pltpu.sync_copy(src_hbm.at[pl.ds(src_base + j * c, c)], rows_v)
            pltpu.sync_copy(rows_v, out_hbm.at[idx_v.at[j]])

    return pl.kernel(
        body,
        out_type=jax.ShapeDtypeStruct((n_out, w), src.dtype),
        mesh=_sc_mesh(),
        scratch_types=[pltpu.VMEM((nch, c), jnp.int32), pltpu.VMEM((c, w), src.dtype)],
        name="sc_dispatch",
    )(src, idx.reshape(SC_WORKERS, nch, c))


def _sc_gather_rows(table, idx):
    w = table.shape[1]
    m = idx.shape[0]
    c = GATHER_CHUNK
    per = m // SC_WORKERS
    nch = per // c
    assert per * SC_WORKERS == m and nch * c == per and nch % 2 == 0

    def body(table_hbm, idx_hbm, out_hbm, idx_v, rows_v, sem0, sem1):
        wid = _sc_worker()
        base = wid * per
        sems = (sem0, sem1)
        pltpu.sync_copy(idx_hbm.at[wid], idx_v)

        def gather(j, b):
            return pltpu.make_async_copy(table_hbm.at[idx_v.at[j]], rows_v.at[b], sems[b])

        gather(0, 0).start()

        @pl.loop(0, nch, step=2)
        def _(g):
            for b in range(2):
                j = g + b
                gather(j, b).wait()

                @pl.when(j + 1 < nch)
                def _():
                    gather(j + 1, 1 - b).start()

                pltpu.sync_copy(rows_v.at[b], out_hbm.at[pl.ds(base + j * c, c)])

    return pl.kernel(
        body,
        out_type=jax.ShapeDtypeStruct((m, w), table.dtype),
        mesh=_sc_mesh(),
        scratch_types=[pltpu.VMEM((nch, c), jnp.int32), pltpu.VMEM((2, c, w), table.dtype),
                       pltpu.SemaphoreType.DMA, pltpu.SemaphoreType.DMA],
        name="sc_combine_gather",
    )(table, idx.reshape(SC_WORKERS, nch, c))


PERM_BLOCK = 256


def _gu_split_kernel(w_ref, perm_ref, wg_ref, wu_ref):
    half = PERM_BLOCK // 2
    perm = perm_ref[...]
    for b in range(w_ref.shape[1] // PERM_BLOCK):
        blk = w_ref[:, b * PERM_BLOCK:(b + 1) * PERM_BLOCK].astype(BF16)
        t = _dot(blk, perm)
        wg_ref[:, b * half:(b + 1) * half] = t[:, :half].astype(BF16)
        wu_ref[:, b * half:(b + 1) * half] = t[:, half:].astype(BF16)


def _gu_split(w_gu):
    ne, d, two_f = w_gu.shape
    rows = min(d, 1024)
    half = PERM_BLOCK // 2
    j = jnp.arange(PERM_BLOCK)
    perm = (jnp.arange(PERM_BLOCK)[None, :] == (j // 2 + (j % 2) * half)[:, None]).astype(BF16)
    out = jax.ShapeDtypeStruct((ne, d, two_f // 2), BF16)
    ospec = pl.BlockSpec((None, rows, two_f // 2), lambda e, r: (e, r, 0))
    return pl.pallas_call(
        _gu_split_kernel,
        out_shape=(out, out),
        grid=(ne, d // rows),
        in_specs=[pl.BlockSpec((None, rows, two_f), lambda e, r: (e, r, 0)),
                  pl.BlockSpec((PERM_BLOCK, PERM_BLOCK), lambda e, r: (0, 0))],
        out_specs=(ospec, ospec),
        compiler_params=_cparams("arbitrary", "arbitrary"),
        name="gu_split",
    )(w_gu, perm)


def _expert_kernel(te_ref, nu_ref, x_ref, wg_ref, wu_ref, wd_ref, bg_ref, bu_ref, bd_ref, y_ref):
    i = pl.program_id(0)

    @pl.when(i < nu_ref[0])
    def _():
        x = _unpack_halves(x_ref[...]).astype(BF16)
        g = _dot(x, wg_ref[...]) + bg_ref[...]
        u = _dot(x, wu_ref[...]) + bu_ref[...]
        gate = jnp.minimum(g, SWIGLU_LIMIT)
        up = jnp.clip(u, -SWIGLU_LIMIT, SWIGLU_LIMIT)
        act = (up + 1.0) * (gate * _sigmoid(SWIGLU_ALPHA * gate))
        y = _dot(act.astype(BF16), wd_ref[...]) + bd_ref[...]
        y_ref[...] = _pack_halves(y)

    @pl.when(i >= nu_ref[0])
    def _():
        y_ref[...] = jnp.zeros_like(y_ref)


def _experts(tile_expert, n_used, xs, wg, wu, wd, bg, bu, bd):
    n_slots, half = xs.shape
    tm = EXPERT_TILE
    n_tiles = n_slots // tm
    d = wg.shape[1]
    row_in = pl.BlockSpec((tm, half), lambda i, te, nu: (jnp.minimum(i, nu[0] - 1), 0))
    row_out = pl.BlockSpec((tm, half), lambda i, te, nu: (jnp.where(i < nu[0], i, n_tiles - 1), 0))
    wspec = lambda a: pl.BlockSpec((None,) + a.shape[1:], lambda i, te, nu: (te[i], 0, 0))
    return pl.pallas_call(
        _expert_kernel,
        out_shape=jax.ShapeDtypeStruct((n_slots, half), jnp.uint32),
        grid_spec=pltpu.PrefetchScalarGridSpec(
            num_scalar_prefetch=2, grid=(n_tiles,),
            in_specs=[row_in, wspec(wg), wspec(wu), wspec(wd), wspec(bg), wspec(bu), wspec(bd)],
            out_specs=row_out),
        compiler_params=_cparams("arbitrary"),
        name="experts",
    )(tile_expert, n_used, xs, wg, wu, wd, bg, bu, bd)


def _combine_kernel(y4_ref, h_ref, wt_ref, g2_ref, fnw_ref, *rest):
    o_ref = rest[-1]
    wt = wt_ref[...]
    acc = jnp.zeros(h_ref.shape, F32)
    for kk in range(TOP_K):
        acc = acc + wt[:, kk:kk + 1] * _unpack_halves(y4_ref[kk])
    h = h_ref[...] + g2_ref[...] * acc
    o_ref[...] = _rms(h) * fnw_ref[...]


def _combine(y4, h, wt, g2, fnw, seq, row0, n_total, prev_out):
    n, d = h.shape
    tt = COMBINE_TILE
    tpb = seq // tt
    t0 = row0 // tt
    in_specs = [pl.BlockSpec((TOP_K, tt, d // 2), lambda i: (0, i, 0)),
                pl.BlockSpec((tt, d), lambda i: (i, 0)),
                pl.BlockSpec((tt, GATE_PAD), lambda i: (i, 0)),
                pl.BlockSpec((None, 1, d), lambda i: (i // tpb, 0, 0)),
                pl.BlockSpec((1, d), lambda i: (0, 0))]
    args = [y4, h, wt, g2, fnw]
    aliases = {}
    if prev_out is not None:
        in_specs.append(pl.BlockSpec(memory_space=pl.ANY))
        args.append(prev_out)
        aliases = {len(args) - 1: 0}
    return pl.pallas_call(
        _combine_kernel,
        out_shape=jax.ShapeDtypeStruct((n_total, d), F32),
        grid=(n // tt,),
        in_specs=in_specs,
        out_specs=pl.BlockSpec((tt, d), lambda i: (i + t0, 0)),
        input_output_aliases=aliases,
        compiler_params=_cparams("arbitrary"),
        name="combine",
    )(*args)


def kernel(x, c, ctx, c_ctx, w_ada, b_ada, norm1_w, w_in, a_up_f, a_bias_f, a_up_b, a_bias_b,
           gla_norm_w, conv_w, w_out, norm2_w, router_w, router_b, w_gu, b_gu, w_dn, b_dn,
           final_norm_w):
    bsz, seq, d = x.shape
    n = bsz * seq
    assert w_ada.shape[0] == 1 and d == D_MODEL
    assert seq % TOKEN_TILE == 0 and seq % GLA_BLOCK == 0 and seq % GRID_W == 0
    assert ctx.shape[1] & (ctx.shape[1] - 1) == 0

    w_in0 = w_in[0]
    v_end = 2 * GLA_QK + GLA_WIDTH
    r_end = v_end + GLA_WIDTH
    g_end = r_end + 2 * GATE_RANK
    wm = jnp.concatenate([w_in0[:, :r_end], w_in0[:, g_end:]], axis=1).astype(BF16)
    wg_low = jnp.pad(w_in0[:, r_end:g_end], ((0, 0), (0, GATE_PAD - 2 * GATE_RANK))).astype(BF16)
    afb = jnp.zeros((GATE_PAD, 2 * GLA_QK), F32)
    afb = afb.at[:GATE_RANK, :GLA_QK].set(a_up_f[0]).at[GATE_RANK:2 * GATE_RANK, GLA_QK:].set(a_up_b[0])
    afb = afb.astype(BF16)
    bias_fb = jnp.concatenate([a_bias_f[0], a_bias_b[0]])[None]
    wkv = wm[:, GLA_QK:v_end]
    row = lambda a: a.reshape(1, -1)

    mod = _modulation(jnp.concatenate([c, c_ctx[None]], axis=0), w_ada[0], row(b_ada[0]))
    sh1, sc1, g1, sh2, sc2, g2 = [mod[:bsz, i * d:(i + 1) * d].reshape(bsz, 1, d) for i in range(6)]
    csh1, csc1 = mod[bsz:, 0:d], mod[bsz:, d:2 * d]

    s0f, s0b = _context_states(ctx, csh1, csc1, row(norm1_w[0]), wkv, wg_low, afb, bias_fb)

    x2 = x.reshape(n, d)
    tri = jnp.triu(jnp.ones((TOKEN_TILE, TOKEN_TILE), BF16), k=1)
    gnw = jnp.tile(gla_norm_w[0], GLA_HEADS)[None]
    wo = w_out[0].astype(BF16)
    wg_e, wu_e = _gu_split(w_gu[0])
    wd_e = w_dn[0].astype(BF16)
    bg_e, bu_e, bd_e = b_gu[0][:, None, 0::2], b_gu[0][:, None, 1::2], b_dn[0][:, None, :]

    n_chunks = N_CHUNKS if bsz % N_CHUNKS == 0 else 1
    cb = bsz // n_chunks
    nc = cb * seq
    n_tiles = nc * TOP_K // EXPERT_TILE + N_EXPERTS
    routed = []
    for ci in range(n_chunks):
        bs = slice(ci * cb, (ci + 1) * cb)
        row0 = ci * nc
        a_f, a_b, vb, o_i, sr, yc, dec = _projection(
            x2, row0, nc, sh1[bs], sc1[bs], row(norm1_w[0]), wm, wg_low, afb, bias_fb, conv_w[0], seq)
        o_f, o_b = _gla_scan(a_f, a_b, vb, dec, s0f[bs], s0b[bs], cb, seq)
        h, xw, idx_t, rank_t, wt, cnt = _post_mixer(
            x2, row0, o_i, o_f, o_b, sr, yc, gnw, wo, g1[bs], row(norm2_w[0]), sh2[bs], sc2[bs],
            router_w[0].T, router_b[0][:, None], tri, seq)

        counts = cnt[:, 0]
        padded = (counts + EXPERT_TILE - 1) // EXPERT_TILE * EXPERT_TILE
        group_end = jnp.cumsum(padded)
        group_start = (group_end - padded).astype(jnp.int32)
        n_used = (group_end[-1] // EXPERT_TILE).astype(jnp.int32)
        tile_ids = jnp.minimum(jnp.arange(n_tiles, dtype=jnp.int32), n_used - 1) * EXPERT_TILE
        tile_expert = jnp.minimum(jnp.sum(tile_ids[:, None] >= group_end[None, :], axis=1),
                                  N_EXPERTS - 1).astype(jnp.int32)
        slot_flat = _slots(group_start, idx_t, rank_t).reshape(-1)
        xs = _sc_scatter_rows(xw, slot_flat, n_tiles * EXPERT_TILE)
        routed.append((h, wt, slot_flat, xs, tile_expert, n_used, bs, row0))

    gathered = []
    for h, wt, slot_flat, xs, tile_expert, n_used, bs, row0 in routed:
        ys = _experts(tile_expert, n_used.reshape(1), xs, wg_e, wu_e, wd_e, bg_e, bu_e, bd_e)
        y4 = _sc_gather_rows(ys, slot_flat).reshape(TOP_K, nc, d // 2)
        gathered.append((y4, h, wt, bs, row0))

    out = None
    for y4, h, wt, bs, row0 in gathered:
        out = _combine(y4, h, wt, g2[bs], row(final_norm_w), seq, row0, n, out)
    return out.reshape(bsz, seq, d)
```

```python
import functools

import jax
import jax.numpy as jnp
from jax import lax
from jax.experimental import pallas as pl
from jax.experimental.pallas import tpu as pltpu
from jax.experimental.pallas import tpu_sc as plsc

D_MODEL = 1024
GLA_HEADS = 4
GLA_DK = 64
GLA_DV = 128
GLA_QK = GLA_HEADS * GLA_DK
GLA_WIDTH = GLA_HEADS * GLA_DV
CONV_WIDTH = 512
GATE_RANK = 16
GATE_PAD = 128
GATE_NORMALIZER = 16.0
CHUNK = 64
GRID_W = 64
N_EXPERTS = 32
TOP_K = 4
SWIGLU_LIMIT = 7.0
SWIGLU_ALPHA = 1.702
EPS = 1e-6

TOKEN_TILE = 512
PROJ_SUB = 256
PROJ_COLS = 256
GLA_BLOCK = 512
INTRA_GROUP = 256
EXPERT_TILE = 512
COMBINE_TILE = 512
N_CHUNKS = 2
VMEM_LIMIT = 56 * 1024 * 1024

F32 = jnp.float32
BF16 = jnp.bfloat16
HI_MASK = 0xFFFF0000


def _cparams(*sem):
    return pltpu.CompilerParams(dimension_semantics=sem, vmem_limit_bytes=VMEM_LIMIT)


def _rms(x):
    return x * lax.rsqrt(jnp.mean(x * x, axis=-1, keepdims=True) + EPS)


def _sigmoid(x):
    return 1.0 / (1.0 + jnp.exp(-x))


def _log_sigmoid(z):
    return jnp.minimum(z, 0.0) - jnp.log(1.0 + jnp.exp(-jnp.abs(z)))


def _dot(a, b):
    return jnp.dot(a, b, preferred_element_type=F32)


def _dot_nt(a, b):
    return lax.dot_general(a, b, (((1,), (1,)), ((), ())), preferred_element_type=F32)


def _dot_tn(a, b):
    return lax.dot_general(a, b, (((0,), (0,)), ((), ())), preferred_element_type=F32)


def _split_bf16(a):
    hi = a.astype(BF16)
    lo = (a - hi.astype(F32)).astype(BF16)
    return hi, lo


def _dot3(a, b, dot):
    a_hi, a_lo = _split_bf16(a)
    b_hi, b_lo = _split_bf16(b)
    return dot(a_hi, b_hi) + (dot(a_hi, b_lo) + dot(a_lo, b_hi))


def _segment_scan(x, seg, reverse, after_step=None):
    n = x.shape[0]
    row = lax.broadcasted_iota(jnp.int32, x.shape, 0) & (seg - 1)
    s = 1
    while s < seg:
        if reverse:
            shifted = pltpu.roll(x, n - s, 0)
            x = x + jnp.where(row < seg - s, shifted, 0.0)
        else:
            shifted = pltpu.roll(x, s, 0)
            x = x + jnp.where(row >= s, shifted, 0.0)
        if after_step is not None:
            after_step()
        s *= 2
    return x


def _pack_halves(x):
    n = x.shape[1] // 2
    bits = pltpu.bitcast(x.astype(BF16).astype(F32), jnp.uint32)
    return (bits[:, :n] & jnp.uint32(HI_MASK)) | (bits[:, n:] >> 16)


def _unpack_halves(w):
    hi = pltpu.bitcast(w & jnp.uint32(HI_MASK), F32)
    lo = pltpu.bitcast(w << 16, F32)
    return jnp.concatenate([hi, lo], axis=1)


def _mod_kernel(c_ref, w_ref, b_ref, o_ref):
    c = c_ref[...]
    s = c * _sigmoid(c)
    o_ref[...] = _dot3(s, w_ref[...], _dot) + b_ref[...]


def _modulation(cc, w_ada, b_ada):
    rows, d = cc.shape
    n = w_ada.shape[1]
    bn = 1536
    return pl.pallas_call(
        _mod_kernel,
        out_shape=jax.ShapeDtypeStruct((rows, n), F32),
        grid=(n // bn,),
        in_specs=[pl.BlockSpec((rows, d), lambda j: (0, 0)),
                  pl.BlockSpec((d, bn), lambda j: (0, j)),
                  pl.BlockSpec((1, bn), lambda j: (0, j))],
        out_specs=pl.BlockSpec((rows, bn), lambda j: (0, j)),
        compiler_params=_cparams("arbitrary"),
        name="modulation",
    )(cc, w_ada, b_ada)


def _ctx_kernel(x_ref, sh_ref, sc_ref, nw_ref, wkv_ref, wg_ref, afb_ref, bias_ref, sf_ref, sb_ref):
    x = x_ref[...]
    n = x.shape[0]
    hn = _rms(x) * nw_ref[...] * (1.0 + sc_ref[...]) + sh_ref[...]
    hb = hn.astype(BF16)
    kv = _dot(hb, wkv_ref[...])
    g = _dot(hb, wg_ref[...])
    z = _dot(g.astype(BF16), afb_ref[...]) + bias_ref[...]
    la = _log_sigmoid(z) * (1.0 / GATE_NORMALIZER)
    p = _segment_scan(la, n, False)
    p_f, p_b = p[:, :GLA_QK], p[:, GLA_QK:]
    w_f = jnp.exp(p_f[n - 1:n, :] - p_f)
    w_b = jnp.exp(p_b - la[:, GLA_QK:])
    k = kv[:, :GLA_QK]
    vb = kv[:, GLA_QK:].astype(BF16)
    ke_f = (k * w_f).astype(BF16)
    ke_b = (k * w_b).astype(BF16)
    for h in range(GLA_HEADS):
        vh = vb[:, h * GLA_DV:(h + 1) * GLA_DV]
        sf_ref[h] = _dot_tn(vh, ke_f[:, h * GLA_DK:(h + 1) * GLA_DK])
        sb_ref[h] = _dot_tn(vh, ke_b[:, h * GLA_DK:(h + 1) * GLA_DK])


def _context_states(ctx, csh1, csc1, nw1, wkv, wg, afb, bias_fb):
    bsz, n, d = ctx.shape
    full = lambda a: pl.BlockSpec(a.shape, lambda b: (0,) * a.ndim)
    st = jax.ShapeDtypeStruct((bsz, GLA_HEADS, GLA_DV, GLA_DK), F32)
    st_spec = pl.BlockSpec((None, GLA_HEADS, GLA_DV, GLA_DK), lambda b: (b, 0, 0, 0))
    return pl.pallas_call(
        _ctx_kernel,
        out_shape=(st, st),
        grid=(bsz,),
        in_specs=[pl.BlockSpec((None, n, d), lambda b: (b, 0, 0)),
                  full(csh1), full(csc1), full(nw1), full(wkv), full(wg), full(afb), full(bias_fb)],
        out_specs=(st_spec, st_spec),
        compiler_params=_cparams("arbitrary"),
        name="context_states",
    )(ctx, csh1, csc1, nw1, wkv, wg, afb, bias_fb)


def _proj_kernel(x_ref, sh_ref, sc_ref, nw_ref, wm_ref, wg_ref, afb_ref, bias_ref, cw_ref,
                 af_ref, ab_ref, v_ref, oi_ref, sr_ref, yc_ref, dec_ref, p_scr):
    subs = range(x_ref.shape[0] // PROJ_SUB)
    nc = PROJ_SUB // CHUNK
    nblk = wm_ref.shape[1] // PROJ_COLS
    factors = []
    for sub in subs:
        x = x_ref[sub * PROJ_SUB:(sub + 1) * PROJ_SUB, :]
        hn = _rms(x) * nw_ref[...] * (1.0 + sc_ref[...]) + sh_ref[...]
        hb = hn.astype(BF16)
        g = _dot(hb, wg_ref[...])
        z = _dot(g.astype(BF16), afb_ref[...]) + bias_ref[...]

        todo = iter(range(nblk))

        def wide_block():
            j = next(todo, None)
            if j is not None:
                cols = slice(j * PROJ_COLS, (j + 1) * PROJ_COLS)
                p_scr[sub, :, cols] = _dot(hb, wm_ref[:, cols])

        la = _log_sigmoid(z) * (1.0 / GATE_NORMALIZER)
        wide_block()
        b_f = _segment_scan(la[:, :GLA_QK], CHUNK, False, wide_block)
        b_b = _segment_scan(la[:, GLA_QK:], CHUNK, True, wide_block)
        b_f3 = b_f.reshape(nc, CHUNK, GLA_QK)
        tot_f = b_f3[:, CHUNK - 1:CHUNK, :]
        b_b3 = b_b.reshape(nc, CHUNK, GLA_QK)
        tot_b = b_b3[:, 0:1, :]
        fac = []
        for arg in (b_f, -b_f, (tot_f - b_f3).reshape(PROJ_SUB, GLA_QK),
                    b_b, -b_b, (tot_b - b_b3).reshape(PROJ_SUB, GLA_QK)):
            fac.append(jnp.exp(arg))
            wide_block()
        fac += [jnp.exp(tot_f).reshape(nc, GLA_QK), jnp.exp(tot_b).reshape(nc, GLA_QK)]
        factors.append(fac)
        for _ in range(nblk):
            wide_block()
    for sub in subs:
        _proj_rows(sub, factors[sub], cw_ref, af_ref, ab_ref, v_ref, oi_ref, sr_ref, yc_ref,
                   dec_ref, p_scr.at[sub])


def _proj_rows(sub, factors, cw_ref, af_ref, ab_ref, v_ref, oi_ref, sr_ref, yc_ref, dec_ref, p):
    tm = PROJ_SUB
    nc = tm // CHUNK
    rs = slice(sub * tm, (sub + 1) * tm)
    cs = slice(sub * nc, (sub + 1) * nc)
    e_f, ei_f, ee_f, e_b, ei_b, ee_b, dec_f, dec_b = factors

    q = p[:, 0:GLA_QK] * (GLA_DK ** -0.5)
    k = p[:, GLA_QK:2 * GLA_QK]
    vb = p[:, 2 * GLA_QK:2 * GLA_QK + GLA_WIDTH].astype(BF16)
    v_ref[rs, :] = vb
    qd_f = (q * e_f).astype(BF16)
    ki_f = (k * ei_f).astype(BF16)
    ke_f = (k * ee_f).astype(BF16)
    qd_b = (q * e_b).astype(BF16)
    ki_b = (k * ei_b).astype(BF16)
    ke_b = (k * ee_b).astype(BF16)

    af_ref[rs, :GLA_QK] = qd_f
    af_ref[rs, GLA_QK:] = ke_f
    ab_ref[rs, :GLA_QK] = qd_b
    ab_ref[rs, GLA_QK:] = ke_b
    dec_ref[cs, :GLA_QK] = dec_f
    dec_ref[cs, GLA_QK:] = dec_b

    gi = INTRA_GROUP
    ri = lax.broadcasted_iota(jnp.int32, (gi, gi), 0)
    ci = lax.broadcasted_iota(jnp.int32, (gi, gi), 1)
    same = (ri // CHUNK) == (ci // CHUNK)
    lower = same & (ci <= ri)
    upper = same & (ci >= ri)
    for gidx in range(tm // gi):
        rows = slice(gidx * gi, (gidx + 1) * gi)
        orows = slice(sub * tm + gidx * gi, sub * tm + (gidx + 1) * gi)
        for h in range(GLA_HEADS):
            lanes = slice(h * GLA_DK, (h + 1) * GLA_DK)
            s_f = _dot_nt(qd_f[rows, lanes], ki_f[rows, lanes])
            s_b = _dot_nt(qd_b[rows, lanes], ki_b[rows, lanes])
            s = jnp.where(lower, s_f, 0.0) + jnp.where(upper, s_b, 0.0)
            oi_ref[orows, h * GLA_DV:(h + 1) * GLA_DV] = _dot(
                s.astype(BF16), vb[rows, h * GLA_DV:(h + 1) * GLA_DV])

    r = p[:, 1024:1536]
    sr_ref[rs, :] = (r * _sigmoid(r)).astype(BF16)

    u = p[:, 2048:2560] * p[:, 2560:3072]
    row = lax.broadcasted_iota(jnp.int32, u.shape, 0) & (GRID_W - 1)
    prev = jnp.where(row >= 1, pltpu.roll(u, 1, 0), 0.0)
    nxt = jnp.where(row < GRID_W - 1, pltpu.roll(u, tm - 1, 0), 0.0)
    cw = cw_ref[...]
    conv = prev * cw[0:1, :] + u * cw[1:2, :] + nxt * cw[2:3, :]
    yc_ref[rs, :] = (p[:, 1536:2048] * conv).astype(BF16)


def _projection(x2, row0, n, sh1, sc1, nw1, wm, wg, afb, bias_fb, conv_w, seq):
    d = x2.shape[1]
    tm = TOKEN_TILE
    tpb = seq // tm
    t0 = row0 // tm
    full = lambda a: pl.BlockSpec(a.shape, lambda i: (0,) * a.ndim)
    per_batch = pl.BlockSpec((None, 1, d), lambda i: (i // tpb, 0, 0))
    tok = lambda w: pl.BlockSpec((tm, w), lambda i: (i, 0))
    out_shape = (
        jax.ShapeDtypeStruct((n, 2 * GLA_QK), BF16),
        jax.ShapeDtypeStruct((n, 2 * GLA_QK), BF16),
        jax.ShapeDtypeStruct((n, GLA_WIDTH), BF16),
        jax.ShapeDtypeStruct((n, GLA_WIDTH), F32),
        jax.ShapeDtypeStruct((n, GLA_WIDTH), BF16),
        jax.ShapeDtypeStruct((n, CONV_WIDTH), BF16),
        jax.ShapeDtypeStruct((n // CHUNK, 2 * GLA_QK), F32),
    )
    out_specs = (tok(2 * GLA_QK), tok(2 * GLA_QK), tok(GLA_WIDTH), tok(GLA_WIDTH), tok(GLA_WIDTH),
                 tok(CONV_WIDTH), pl.BlockSpec((tm // CHUNK, 2 * GLA_QK), lambda i: (i, 0)))
    return pl.pallas_call(
        _proj_kernel,
        out_shape=out_shape,
        grid=(n // tm,),
        in_specs=[pl.BlockSpec((tm, d), lambda i: (i + t0, 0)), per_batch, per_batch, full(nw1),
                  full(wm), full(wg), full(afb), full(bias_fb), full(conv_w)],
        out_specs=out_specs,
        scratch_shapes=[pltpu.VMEM((tm // PROJ_SUB, PROJ_SUB, wm.shape[1]), F32)],
        compiler_params=_cparams("arbitrary"),
        name="projection",
    )(x2, sh1, sc1, nw1, wm, wg, afb, bias_fb, conv_w)


def _scan_kernel(af_ref, vf_ref, df_ref, ab_ref, vb_ref, db_ref, s0f_ref, s0b_ref,
                 of_ref, ob_ref, st_ref):
    @pl.when(pl.program_id(1) == 0)
    def _():
        st_ref[0] = s0f_ref[...]
        st_ref[1] = s0b_ref[...]

    nc = af_ref.shape[0] // CHUNK

    for direction, (a_ref, v_ref, d_ref, o_ref) in enumerate(
            ((af_ref, vf_ref, df_ref, of_ref), (ab_ref, vb_ref, db_ref, ob_ref))):
        order = range(nc) if direction == 0 else range(nc - 1, -1, -1)
        for h in range(GLA_HEADS):
            lanes = slice(h * GLA_DK, (h + 1) * GLA_DK)
            klanes = slice(GLA_QK + h * GLA_DK, GLA_QK + (h + 1) * GLA_DK)
            dlanes = slice(direction * GLA_QK + h * GLA_DK, direction * GLA_QK + (h + 1) * GLA_DK)
            vl = slice(h * GLA_DV, (h + 1) * GLA_DV)
            rows = [slice(c * CHUNK, (c + 1) * CHUNK) for c in range(nc)]
            kv = [_dot_tn(v_ref[rows[c], vl], a_ref[rows[c], klanes]) for c in range(nc)]
            s = st_ref[direction, h]
            start = [None] * nc
            for c in order:
                start[c] = s
                s = s * d_ref[c:c + 1, dlanes] + kv[c]
            st_ref[direction, h] = s
            for c in range(nc):
                o_ref[rows[c], vl] = _dot_nt(a_ref[rows[c], lanes], start[c].astype(BF16))


def _gla_scan(a_f, a_b, vb, dec, s0f, s0b, bsz, seq):
    n = a_f.shape[0]
    tb = GLA_BLOCK
    nb = seq // tb
    fwd = lambda w: pl.BlockSpec((tb, w), lambda b, j: (b * nb + j, 0))
    bwd = lambda w: pl.BlockSpec((tb, w), lambda b, j: (b * nb + nb - 1 - j, 0))
    dfw = pl.BlockSpec((tb // CHUNK, 2 * GLA_QK), lambda b, j: (b * nb + j, 0))
    dbw = pl.BlockSpec((tb // CHUNK, 2 * GLA_QK), lambda b, j: (b * nb + nb - 1 - j, 0))
    st_spec = pl.BlockSpec((None, GLA_HEADS, GLA_DV, GLA_DK), lambda b, j: (b, 0, 0, 0))
    out = jax.ShapeDtypeStruct((n, GLA_WIDTH), F32)
    return pl.pallas_call(
        _scan_kernel,
        out_shape=(out, out),
        grid=(bsz, nb),
        in_specs=[fwd(2 * GLA_QK), fwd(GLA_WIDTH), dfw, bwd(2 * GLA_QK), bwd(GLA_WIDTH), dbw,
                  st_spec, st_spec],
        out_specs=(fwd(GLA_WIDTH), bwd(GLA_WIDTH)),
        scratch_shapes=[pltpu.VMEM((2, GLA_HEADS, GLA_DV, GLA_DK), F32)],
        compiler_params=_cparams("arbitrary", "arbitrary"),
        name="gla_scan",
    )(a_f, vb, dec, a_b, vb, dec, s0f, s0b)


def _post_kernel(x_ref, oi_ref, of_ref, ob_ref, sr_ref, yc_ref, gnw_ref, wo_ref, g1_ref, nw2_ref,
                 sh2_ref, sc2_ref, rwt_ref, rb_ref, tri_ref,
                 h_ref, xw_ref, idx_ref, rank_ref, wt_ref, cnt_ref, run_ref):
    @pl.when(pl.program_id(0) == 0)
    def _():
        run_ref[...] = jnp.zeros_like(run_ref)

    tm = x_ref.shape[0]
    o = oi_ref[...] + of_ref[...] + ob_ref[...]
    normed = jnp.concatenate(
        [_rms(o[:, h * GLA_DV:(h + 1) * GLA_DV]) for h in range(GLA_HEADS)], axis=1)
    y_gla = (normed * gnw_ref[...] * sr_ref[...].astype(F32)).astype(BF16)
    wo = wo_ref[...]
    y = _dot(y_gla, wo[:GLA_WIDTH, :]) + _dot(yc_ref[...], wo[GLA_WIDTH:, :])
    h = x_ref[...] + g1_ref[...] * y
    h_ref[...] = h
    hn = _rms(h) * nw2_ref[...] * (1.0 + sc2_ref[...]) + sh2_ref[...]
    xw_ref[...] = _pack_halves(hn)

    logits = _dot3(rwt_ref[...], hn, _dot_nt) + rb_ref[...]
    eid = lax.broadcasted_iota(jnp.int32, logits.shape, 0)
    vals, idxs = [], []
    work = logits
    for _ in range(TOP_K):
        m = jnp.max(work, axis=0, keepdims=True)
        i = jnp.min(jnp.where(work == m, eid, N_EXPERTS), axis=0, keepdims=True)
        vals.append(m)
        idxs.append(i)
        work = jnp.where(eid == i, -jnp.inf, work)
    exps = [jnp.exp(v - vals[0]) for v in vals]
    denom = exps[0] + exps[1] + exps[2] + exps[3]
    wts = [e / denom for e in exps]

    onehot = jnp.zeros(logits.shape, F32)
    for i in idxs:
        onehot = onehot + jnp.where(eid == i, 1.0, 0.0)
    before = _dot(onehot.astype(BF16), tri_ref[...]) + run_ref[:, 0:1]
    for kk in range(TOP_K):
        idx_ref[kk:kk + 1, :] = idxs[kk]
        rk = jnp.sum(jnp.where(eid == idxs[kk], before, 0.0), axis=0, keepdims=True)
        rank_ref[kk:kk + 1, :] = rk.astype(jnp.int32)
    run = run_ref[...] + jnp.sum(onehot, axis=1, keepdims=True)
    run_ref[...] = run
    cnt_ref[...] = run.astype(jnp.int32)

    sub = lax.broadcasted_iota(jnp.int32, (GATE_PAD, tm), 0)
    wpad = jnp.zeros((GATE_PAD, tm), F32)
    for kk in range(TOP_K):
        wpad = wpad + jnp.where(sub == kk, wts[kk], 0.0)
    wt_ref[...] = wpad.T


def _post_mixer(x2, row0, o_i, o_f, o_b, sr, yc, gnw, wo, g1, nw2, sh2, sc2, rwt, rb, tri, seq):
    n, d = o_i.shape[0], x2.shape[1]
    tm = TOKEN_TILE
    tpb = seq // tm
    t0 = row0 // tm
    full = lambda a: pl.BlockSpec(a.shape, lambda i: (0,) * a.ndim)
    per_batch = pl.BlockSpec((None, 1, d), lambda i: (i // tpb, 0, 0))
    tok = lambda w: pl.BlockSpec((tm, w), lambda i: (i, 0))
    lane_tok = pl.BlockSpec((TOP_K, tm), lambda i: (0, i))
    out_shape = (
        jax.ShapeDtypeStruct((n, d), F32),
        jax.ShapeDtypeStruct((n, d // 2), jnp.uint32),
        jax.ShapeDtypeStruct((TOP_K, n), jnp.int32),
        jax.ShapeDtypeStruct((TOP_K, n), jnp.int32),
        jax.ShapeDtypeStruct((n, GATE_PAD), F32),
        jax.ShapeDtypeStruct((N_EXPERTS, GATE_PAD), jnp.int32),
    )
    out_specs = (tok(d), tok(d // 2), lane_tok, lane_tok, tok(GATE_PAD),
                 pl.BlockSpec((N_EXPERTS, GATE_PAD), lambda i: (0, 0)))
    return pl.pallas_call(
        _post_kernel,
        out_shape=out_shape,
        grid=(n // tm,),
        in_specs=[pl.BlockSpec((tm, d), lambda i: (i + t0, 0)), tok(GLA_WIDTH), tok(GLA_WIDTH),
                  tok(GLA_WIDTH), tok(GLA_WIDTH),
                  tok(CONV_WIDTH), full(gnw), full(wo), per_batch, full(nw2), per_batch, per_batch,
                  full(rwt), full(rb), full(tri)],
        out_specs=out_specs,
        scratch_shapes=[pltpu.VMEM((N_EXPERTS, GATE_PAD), F32)],
        compiler_params=_cparams("arbitrary"),
        name="post_mixer",
    )(x2, o_i, o_f, o_b, sr, yc, gnw, wo, g1, nw2, sh2, sc2, rwt, rb, tri)


def _slot_kernel(start_ref, idx_ref, rank_ref, slot_ref):
    idx = idx_ref[...]
    base = jnp.zeros(idx.shape, jnp.int32)
    for e in range(N_EXPERTS):
        base = jnp.where(idx == e, start_ref[e], base)
    slot_ref[...] = base + rank_ref[...]


def _slots(group_start, idx_t, rank_t):
    k, n = idx_t.shape
    bn = min(n, 8192)
    blk = pl.BlockSpec((k, bn), lambda i, s: (0, i))
    return pl.pallas_call(
        _slot_kernel,
        out_shape=jax.ShapeDtypeStruct((k, n), jnp.int32),
        grid_spec=pltpu.PrefetchScalarGridSpec(
            num_scalar_prefetch=1, grid=(n // bn,), in_specs=[blk, blk], out_specs=blk),
        compiler_params=_cparams("arbitrary"),
        name="slots",
    )(group_start, idx_t, rank_t)


SC_CORES = 2
SC_SUBCORES = 16
SC_WORKERS = SC_CORES * SC_SUBCORES
SCATTER_CHUNK = 128
GATHER_CHUNK = 64


def _sc_mesh():
    return plsc.VectorSubcoreMesh(core_axis_name="c", subcore_axis_name="s")


def _sc_worker():
    return lax.axis_index("s") * SC_CORES + lax.axis_index("c")


def _sc_scatter_rows(src, idx, n_out):
    n, w = src.shape
    m = idx.shape[0]
    c = SCATTER_CHUNK
    per = m // SC_WORKERS
    nch = per // c
    assert per * SC_WORKERS == m and nch * c == per and n % per == 0

    def body(src_hbm, idx_hbm, out_hbm, idx_v, rows_v):
        wid = _sc_worker()
        src_base = lax.rem(wid * per, n)
        pltpu.sync_copy(idx_hbm.at[wid], idx_v)

        @pl.loop(0, nch)
        def _(j):
            pltpu.sync_copy(src_hbm.at[pl.ds(src_base + j * c, c)], rows_v)
            pltpu.sync_copy(rows_v, out_hbm.at[idx_v.at[j]])

    return pl.kernel(
        body,
        out_type=jax.ShapeDtypeStruct((n_out, w), src.dtype),
        mesh=_sc_mesh(),
        scratch_types=[pltpu.VMEM((nch, c), jnp.int32), pltpu.VMEM((c, w), src.dtype)],
        name="sc_dispatch",
    )(src, idx.reshape(SC_WORKERS, nch, c))


def _sc_gather_rows(table, idx):
    w = table.shape[1]
    m = idx.shape[0]
    c = GATHER_CHUNK
    per = m // SC_WORKERS
    nch = per // c
    assert per * SC_WORKERS == m and nch * c == per and nch % 2 == 0

    def body(table_hbm, idx_hbm, out_hbm, idx_v, rows_v, sem0, sem1):
        wid = _sc_worker()
        base = wid * per
        sems = (sem0, sem1)
        pltpu.sync_copy(idx_hbm.at[wid], idx_v)

        def gather(j, b):
            return pltpu.make_async_copy(table_hbm.at[idx_v.at[j]], rows_v.at[b], sems[b])

        gather(0, 0).start()

        @pl.loop(0, nch, step=2)
        def _(g):
            for b in range(2):
                j = g + b
                gather(j, b).wait()

                @pl.when(j + 1 < nch)
                def _():
                    gather(j + 1, 1 - b).start()

                pltpu.sync_copy(rows_v.at[b], out_hbm.at[pl.ds(base + j * c, c)])

    return pl.kernel(
        body,
        out_type=jax.ShapeDtypeStruct((m, w), table.dtype),
        mesh=_sc_mesh(),
        scratch_types=[pltpu.VMEM((nch, c), jnp.int32), pltpu.VMEM((2, c, w), table.dtype),
                       pltpu.SemaphoreType.DMA, pltpu.SemaphoreType.DMA],
        name="sc_combine_gather",
    )(table, idx.reshape(SC_WORKERS, nch, c))


PERM_BLOCK = 256


def _expert_kernel(te_ref, first_ref, nu_ref, x_ref, wgu_ref, wdn_ref, perm_ref, bg_ref, bu_ref,
                   bd_ref, y_ref, wg_s, wu_s, wd_s):
    i = pl.program_id(0)
    used = i < nu_ref[0]

    @pl.when(used & (first_ref[i] == 1))
    def _():
        half = PERM_BLOCK // 2
        perm = perm_ref[...]
        for b in range(wgu_ref.shape[1] // PERM_BLOCK):
            blk = wgu_ref[:, b * PERM_BLOCK:(b + 1) * PERM_BLOCK].astype(BF16)
            t = _dot(blk, perm)
            wg_s[:, b * half:(b + 1) * half] = t[:, :half].astype(BF16)
            wu_s[:, b * half:(b + 1) * half] = t[:, half:].astype(BF16)
        wd_s[...] = wdn_ref[...].astype(BF16)

    @pl.when(used)
    def _():
        x = _unpack_halves(x_ref[...]).astype(BF16)
        g = _dot(x, wg_s[...]) + bg_ref[...]
        u = _dot(x, wu_s[...]) + bu_ref[...]
        gate = jnp.minimum(g, SWIGLU_LIMIT)
        up = jnp.clip(u, -SWIGLU_LIMIT, SWIGLU_LIMIT)
        act = (up + 1.0) * (gate * _sigmoid(SWIGLU_ALPHA * gate))
        y = _dot(act.astype(BF16), wd_s[...]) + bd_ref[...]
        y_ref[...] = _pack_halves(y)

    @pl.when(jnp.logical_not(used))
    def _():
        y_ref[...] = jnp.zeros_like(y_ref)


def _experts(tile_expert, tile_first, n_used, xs, w_gu, w_dn, bg, bu, bd):
    n_slots, half = xs.shape
    tm = EXPERT_TILE
    n_tiles = n_slots // tm
    d, f = w_dn.shape[2], w_dn.shape[1]
    ph = PERM_BLOCK // 2
    j = jnp.arange(PERM_BLOCK)
    perm = (jnp.arange(PERM_BLOCK)[None, :] == (j // 2 + (j % 2) * ph)[:, None]).astype(BF16)
    row_in = pl.BlockSpec((tm, half), lambda i, te, tf, nu: (jnp.minimum(i, nu[0] - 1), 0))
    row_out = pl.BlockSpec((tm, half),
                           lambda i, te, tf, nu: (jnp.where(i < nu[0], i, n_tiles - 1), 0))
    wspec = lambda a: pl.BlockSpec((None,) + a.shape[1:], lambda i, te, tf, nu: (te[i], 0, 0))
    return pl.pallas_call(
        _expert_kernel,
        out_shape=jax.ShapeDtypeStruct((n_slots, half), jnp.uint32),
        grid_spec=pltpu.PrefetchScalarGridSpec(
            num_scalar_prefetch=3, grid=(n_tiles,),
            in_specs=[row_in, wspec(w_gu), wspec(w_dn),
                      pl.BlockSpec((PERM_BLOCK, PERM_BLOCK), lambda i, te, tf, nu: (0, 0)),
                      wspec(bg), wspec(bu), wspec(bd)],
            out_specs=row_out,
            scratch_shapes=[pltpu.VMEM((w_gu.shape[1], f), BF16), pltpu.VMEM((w_gu.shape[1], f), BF16),
                            pltpu.VMEM((f, d), BF16)]),
        compiler_params=_cparams("arbitrary"),
        name="experts",
    )(tile_expert, tile_first, n_used, xs, w_gu, w_dn, perm, bg, bu, bd)


def _combine_kernel(y4_ref, h_ref, wt_ref, g2_ref, fnw_ref, *rest):
    o_ref = rest[-1]
    wt = wt_ref[...]
    acc = jnp.zeros(h_ref.shape, F32)
    for kk in range(TOP_K):
        acc = acc + wt[:, kk:kk + 1] * _unpack_halves(y4_ref[kk])
    h = h_ref[...] + g2_ref[...] * acc
    o_ref[...] = _rms(h) * fnw_ref[...]


def _combine(y4, h, wt, g2, fnw, seq, row0, n_total, prev_out):
    n, d = h.shape
    tt = COMBINE_TILE
    tpb = seq // tt
    t0 = row0 // tt
    in_specs = [pl.BlockSpec((TOP_K, tt, d // 2), lambda i: (0, i, 0)),
                pl.BlockSpec((tt, d), lambda i: (i, 0)),
                pl.BlockSpec((tt, GATE_PAD), lambda i: (i, 0)),
                pl.BlockSpec((None, 1, d), lambda i: (i // tpb, 0, 0)),
                pl.BlockSpec((1, d), lambda i: (0, 0))]
    args = [y4, h, wt, g2, fnw]
    aliases = {}
    if prev_out is not None:
        in_specs.append(pl.BlockSpec(memory_space=pl.ANY))
        args.append(prev_out)
        aliases = {len(args) - 1: 0}
    return pl.pallas_call(
        _combine_kernel,
        out_shape=jax.ShapeDtypeStruct((n_total, d), F32),
        grid=(n // tt,),
        in_specs=in_specs,
        out_specs=pl.BlockSpec((tt, d), lambda i: (i + t0, 0)),
        input_output_aliases=aliases,
        compiler_params=_cparams("arbitrary"),
        name="combine",
    )(*args)


def kernel(x, c, ctx, c_ctx, w_ada, b_ada, norm1_w, w_in, a_up_f, a_bias_f, a_up_b, a_bias_b,
           gla_norm_w, conv_w, w_out, norm2_w, router_w, router_b, w_gu, b_gu, w_dn, b_dn,
           final_norm_w):
    bsz, seq, d = x.shape
    n = bsz * seq
    assert w_ada.shape[0] == 1 and d == D_MODEL
    assert seq % TOKEN_TILE == 0 and seq % GLA_BLOCK == 0 and seq % GRID_W == 0
    assert ctx.shape[1] & (ctx.shape[1] - 1) == 0

    w_in0 = w_in[0]
    v_end = 2 * GLA_QK + GLA_WIDTH
    r_end = v_end + GLA_WIDTH
    g_end = r_end + 2 * GATE_RANK
    wm = jnp.concatenate([w_in0[:, :r_end], w_in0[:, g_end:]], axis=1).astype(BF16)
    wg_low = jnp.pad(w_in0[:, r_end:g_end], ((0, 0), (0, GATE_PAD - 2 * GATE_RANK))).astype(BF16)
    afb = jnp.zeros((GATE_PAD, 2 * GLA_QK), F32)
    afb = afb.at[:GATE_RANK, :GLA_QK].set(a_up_f[0]).at[GATE_RANK:2 * GATE_RANK, GLA_QK:].set(a_up_b[0])
    afb = afb.astype(BF16)
    bias_fb = jnp.concatenate([a_bias_f[0], a_bias_b[0]])[None]
    wkv = wm[:, GLA_QK:v_end]
    row = lambda a: a.reshape(1, -1)

    mod = _modulation(jnp.concatenate([c, c_ctx[None]], axis=0), w_ada[0], row(b_ada[0]))
    sh1, sc1, g1, sh2, sc2, g2 = [mod[:bsz, i * d:(i + 1) * d].reshape(bsz, 1, d) for i in range(6)]
    csh1, csc1 = mod[bsz:, 0:d], mod[bsz:, d:2 * d]

    s0f, s0b = _context_states(ctx, csh1, csc1, row(norm1_w[0]), wkv, wg_low, afb, bias_fb)

    x2 = x.reshape(n, d)
    tri = jnp.triu(jnp.ones((TOKEN_TILE, TOKEN_TILE), BF16), k=1)
    gnw = jnp.tile(gla_norm_w[0], GLA_HEADS)[None]
    wo = w_out[0].astype(BF16)
    bg_e, bu_e, bd_e = b_gu[0][:, None, 0::2], b_gu[0][:, None, 1::2], b_dn[0][:, None, :]

    n_chunks = N_CHUNKS if bsz % N_CHUNKS == 0 else 1
    cb = bsz // n_chunks
    nc = cb * seq
    n_tiles = nc * TOP_K // EXPERT_TILE + N_EXPERTS
    routed = []
    for ci in range(n_chunks):
        bs = slice(ci * cb, (ci + 1) * cb)
        row0 = ci * nc
        a_f, a_b, vb, o_i, sr, yc, dec = _projection(
            x2, row0, nc, sh1[bs], sc1[bs], row(norm1_w[0]), wm, wg_low, afb, bias_fb, conv_w[0], seq)
        o_f, o_b = _gla_scan(a_f, a_b, vb, dec, s0f[bs], s0b[bs], cb, seq)
        h, xw, idx_t, rank_t, wt, cnt = _post_mixer(
            x2, row0, o_i, o_f, o_b, sr, yc, gnw, wo, g1[bs], row(norm2_w[0]), sh2[bs], sc2[bs],
            router_w[0].T, router_b[0][:, None], tri, seq)

        counts = cnt[:, 0]
        padded = (counts + EXPERT_TILE - 1) // EXPERT_TILE * EXPERT_TILE
        group_end = jnp.cumsum(padded)
        group_start = (group_end - padded).astype(jnp.int32)
        n_used = (group_end[-1] // EXPERT_TILE).astype(jnp.int32)
        tile_ids = jnp.minimum(jnp.arange(n_tiles, dtype=jnp.int32), n_used - 1) * EXPERT_TILE
        tile_expert = jnp.minimum(jnp.sum(tile_ids[:, None] >= group_end[None, :], axis=1),
                                  N_EXPERTS - 1).astype(jnp.int32)
        tile_first = jnp.concatenate([jnp.ones((1,), jnp.int32),
                                      (tile_expert[1:] != tile_expert[:-1]).astype(jnp.int32)])
        slot_flat = _slots(group_start, idx_t, rank_t).reshape(-1)
        xs = _sc_scatter_rows(xw, slot_flat, n_tiles * EXPERT_TILE)
        routed.append((h, wt, slot_flat, xs, tile_expert, tile_first, n_used, bs, row0))

    gathered = []
    for h, wt, slot_flat, xs, tile_expert, tile_first, n_used, bs, row0 in routed:
        ys = _experts(tile_expert, tile_first, n_used.reshape(1), xs, w_gu[0], w_dn[0],
                      bg_e, bu_e, bd_e)
        y4 = _sc_gather_rows(ys, slot_flat).reshape(TOP_K, nc, d // 2)
        gathered.append((y4, h, wt, bs, row0))

    out = None
    for y4, h, wt, bs, row0 in gathered:
        out = _combine(y4, h, wt, g2[bs], row(final_norm_w), seq, row0, n, out)
    return out.reshape(bsz, seq, d)
```

```python
import functools

import jax
import jax.numpy as jnp
from jax import lax
from jax.experimental import pallas as pl
from jax.experimental.pallas import tpu as pltpu
from jax.experimental.pallas import tpu_sc as plsc

D_MODEL = 1024
GLA_HEADS = 4
GLA_DK = 64
GLA_DV = 128
GLA_QK = GLA_HEADS * GLA_DK
GLA_WIDTH = GLA_HEADS * GLA_DV
CONV_WIDTH = 512
GATE_RANK = 16
GATE_PAD = 128
GATE_NORMALIZER = 16.0
CHUNK = 64
GRID_W = 64
N_EXPERTS = 32
TOP_K = 4
SWIGLU_LIMIT = 7.0
SWIGLU_ALPHA = 1.702
EPS = 1e-6

TOKEN_TILE = 512
PROJ_SUB = 256
PROJ_COLS = 256
GLA_BLOCK = 512
INTRA_GROUP = 256
EXPERT_TILE = 512
COMBINE_TILE = 512
N_CHUNKS = 2
VMEM_LIMIT = 56 * 1024 * 1024

F32 = jnp.float32
BF16 = jnp.bfloat16
HI_MASK = 0xFFFF0000


def _cparams(*sem):
    return pltpu.CompilerParams(dimension_semantics=sem, vmem_limit_bytes=VMEM_LIMIT)


def _rms(x):
    return x * lax.rsqrt(jnp.mean(x * x, axis=-1, keepdims=True) + EPS)


def _sigmoid(x):
    return 1.0 / (1.0 + jnp.exp(-x))


def _log_sigmoid(z):
    return jnp.minimum(z, 0.0) - jnp.log(1.0 + jnp.exp(-jnp.abs(z)))


def _dot(a, b):
    return jnp.dot(a, b, preferred_element_type=F32)


def _dot_nt(a, b):
    return lax.dot_general(a, b, (((1,), (1,)), ((), ())), preferred_element_type=F32)


def _dot_tn(a, b):
    return lax.dot_general(a, b, (((0,), (0,)), ((), ())), preferred_element_type=F32)


def _split_bf16(a):
    hi = a.astype(BF16)
    lo = (a - hi.astype(F32)).astype(BF16)
    return hi, lo


def _dot3(a, b, dot):
    a_hi, a_lo = _split_bf16(a)
    b_hi, b_lo = _split_bf16(b)
    return dot(a_hi, b_hi) + (dot(a_hi, b_lo) + dot(a_lo, b_hi))


def _segment_scan(x, seg, reverse, after_step=None):
    n = x.shape[0]
    row = lax.broadcasted_iota(jnp.int32, x.shape, 0) & (seg - 1)
    s = 1
    while s < seg:
        if reverse:
            shifted = pltpu.roll(x, n - s, 0)
            x = x + jnp.where(row < seg - s, shifted, 0.0)
        else:
            shifted = pltpu.roll(x, s, 0)
            x = x + jnp.where(row >= s, shifted, 0.0)
        if after_step is not None:
            after_step()
        s *= 2
    return x


def _pack_halves(x):
    n = x.shape[1] // 2
    bits = pltpu.bitcast(x.astype(BF16).astype(F32), jnp.uint32)
    return (bits[:, :n] & jnp.uint32(HI_MASK)) | (bits[:, n:] >> 16)


def _unpack_halves(w):
    hi = pltpu.bitcast(w & jnp.uint32(HI_MASK), F32)
    lo = pltpu.bitcast(w << 16, F32)
    return jnp.concatenate([hi, lo], axis=1)


def _mod_kernel(c_ref, w_ref, b_ref, o_ref):
    c = c_ref[...]
    s = c * _sigmoid(c)
    o_ref[...] = _dot3(s, w_ref[...], _dot) + b_ref[...]


def _modulation(cc, w_ada, b_ada):
    rows, d = cc.shape
    n = w_ada.shape[1]
    bn = 1536
    return pl.pallas_call(
        _mod_kernel,
        out_shape=jax.ShapeDtypeStruct((rows, n), F32),
        grid=(n // bn,),
        in_specs=[pl.BlockSpec((rows, d), lambda j: (0, 0)),
                  pl.BlockSpec((d, bn), lambda j: (0, j)),
                  pl.BlockSpec((1, bn), lambda j: (0, j))],
        out_specs=pl.BlockSpec((rows, bn), lambda j: (0, j)),
        compiler_params=_cparams("arbitrary"),
        name="modulation",
    )(cc, w_ada, b_ada)


def _ctx_kernel(x_ref, sh_ref, sc_ref, nw_ref, wkv_ref, wg_ref, afb_ref, bias_ref, sf_ref, sb_ref):
    x = x_ref[...]
    n = x.shape[0]
    hn = _rms(x) * nw_ref[...] * (1.0 + sc_ref[...]) + sh_ref[...]
    hb = hn.astype(BF16)
    kv = _dot(hb, wkv_ref[...])
    g = _dot(hb, wg_ref[...])
    z = _dot(g.astype(BF16), afb_ref[...]) + bias_ref[...]
    la = _log_sigmoid(z) * (1.0 / GATE_NORMALIZER)
    p = _segment_scan(la, n, False)
    p_f, p_b = p[:, :GLA_QK], p[:, GLA_QK:]
    w_f = jnp.exp(p_f[n - 1:n, :] - p_f)
    w_b = jnp.exp(p_b - la[:, GLA_QK:])
    k = kv[:, :GLA_QK]
    vb = kv[:, GLA_QK:].astype(BF16)
    ke_f = (k * w_f).astype(BF16)
    ke_b = (k * w_b).astype(BF16)
    for h in range(GLA_HEADS):
        vh = vb[:, h * GLA_DV:(h + 1) * GLA_DV]
        sf_ref[h] = _dot_tn(vh, ke_f[:, h * GLA_DK:(h + 1) * GLA_DK])
        sb_ref[h] = _dot_tn(vh, ke_b[:, h * GLA_DK:(h + 1) * GLA_DK])


def _context_states(ctx, csh1, csc1, nw1, wkv, wg, afb, bias_fb):
    bsz, n, d = ctx.shape
    full = lambda a: pl.BlockSpec(a.shape, lambda b: (0,) * a.ndim)
    st = jax.ShapeDtypeStruct((bsz, GLA_HEADS, GLA_DV, GLA_DK), F32)
    st_spec = pl.BlockSpec((None, GLA_HEADS, GLA_DV, GLA_DK), lambda b: (b, 0, 0, 0))
    return pl.pallas_call(
        _ctx_kernel,
        out_shape=(st, st),
        grid=(bsz,),
        in_specs=[pl.BlockSpec((None, n, d), lambda b: (b, 0, 0)),
                  full(csh1), full(csc1), full(nw1), full(wkv), full(wg), full(afb), full(bias_fb)],
        out_specs=(st_spec, st_spec),
        compiler_params=_cparams("arbitrary"),
        name="context_states",
    )(ctx, csh1, csc1, nw1, wkv, wg, afb, bias_fb)


def _proj_kernel(x_ref, sh_ref, sc_ref, nw_ref, wm_ref, wg_ref, afb_ref, bias_ref, cw_ref,
                 af_ref, ab_ref, v_ref, oi_ref, sr_ref, yc_ref, dec_ref, p_scr):
    subs = range(x_ref.shape[0] // PROJ_SUB)
    nc = PROJ_SUB // CHUNK
    nblk = wm_ref.shape[1] // PROJ_COLS
    factors = []
    for sub in subs:
        x = x_ref[sub * PROJ_SUB:(sub + 1) * PROJ_SUB, :]
        hn = _rms(x) * nw_ref[...] * (1.0 + sc_ref[...]) + sh_ref[...]
        hb = hn.astype(BF16)
        g = _dot(hb, wg_ref[...])
        z = _dot(g.astype(BF16), afb_ref[...]) + bias_ref[...]

        todo = iter(range(nblk))

        def wide_block():
            j = next(todo, None)
            if j is not None:
                cols = slice(j * PROJ_COLS, (j + 1) * PROJ_COLS)
                p_scr[sub, :, cols] = _dot(hb, wm_ref[:, cols])

        la = _log_sigmoid(z) * (1.0 / GATE_NORMALIZER)
        wide_block()
        b_f = _segment_scan(la[:, :GLA_QK], CHUNK, False, wide_block)
        b_b = _segment_scan(la[:, GLA_QK:], CHUNK, True, wide_block)
        b_f3 = b_f.reshape(nc, CHUNK, GLA_QK)
        tot_f = b_f3[:, CHUNK - 1:CHUNK, :]
        b_b3 = b_b.reshape(nc, CHUNK, GLA_QK)
        tot_b = b_b3[:, 0:1, :]
        fac = []
        for arg in (b_f, -b_f, (tot_f - b_f3).reshape(PROJ_SUB, GLA_QK),
                    b_b, -b_b, (tot_b - b_b3).reshape(PROJ_SUB, GLA_QK)):
            fac.append(jnp.exp(arg))
            wide_block()
        fac += [jnp.exp(tot_f).reshape(nc, GLA_QK), jnp.exp(tot_b).reshape(nc, GLA_QK)]
        factors.append(fac)
        for _ in range(nblk):
            wide_block()
    for sub in subs:
        _proj_rows(sub, factors[sub], cw_ref, af_ref, ab_ref, v_ref, oi_ref, sr_ref, yc_ref,
                   dec_ref, p_scr.at[sub])


def _proj_rows(sub, factors, cw_ref, af_ref, ab_ref, v_ref, oi_ref, sr_ref, yc_ref, dec_ref, p):
    tm = PROJ_SUB
    nc = tm // CHUNK
    rs = slice(sub * tm, (sub + 1) * tm)
    cs = slice(sub * nc, (sub + 1) * nc)
    e_f, ei_f, ee_f, e_b, ei_b, ee_b, dec_f, dec_b = factors

    q = p[:, 0:GLA_QK] * (GLA_DK ** -0.5)
    k = p[:, GLA_QK:2 * GLA_QK]
    vb = p[:, 2 * GLA_QK:2 * GLA_QK + GLA_WIDTH].astype(BF16)
    v_ref[rs, :] = vb
    qd_f = (q * e_f).astype(BF16)
    ki_f = (k * ei_f).astype(BF16)
    ke_f = (k * ee_f).astype(BF16)
    qd_b = (q * e_b).astype(BF16)
    ki_b = (k * ei_b).astype(BF16)
    ke_b = (k * ee_b).astype(BF16)

    af_ref[rs, :GLA_QK] = qd_f
    af_ref[rs, GLA_QK:] = ke_f
    ab_ref[rs, :GLA_QK] = qd_b
    ab_ref[rs, GLA_QK:] = ke_b
    dec_ref[cs, :GLA_QK] = dec_f
    dec_ref[cs, GLA_QK:] = dec_b

    gi = INTRA_GROUP
    ri = lax.broadcasted_iota(jnp.int32, (gi, gi), 0)
    ci = lax.broadcasted_iota(jnp.int32, (gi, gi), 1)
    same = (ri // CHUNK) == (ci // CHUNK)
    lower = same & (ci <= ri)
    upper = same & (ci >= ri)
    for gidx in range(tm // gi):
        rows = slice(gidx * gi, (gidx + 1) * gi)
        orows = slice(sub * tm + gidx * gi, sub * tm + (gidx + 1) * gi)
        for h in range(GLA_HEADS):
            lanes = slice(h * GLA_DK, (h + 1) * GLA_DK)
            s_f = _dot_nt(qd_f[rows, lanes], ki_f[rows, lanes])
            s_b = _dot_nt(qd_b[rows, lanes], ki_b[rows, lanes])
            s = jnp.where(lower, s_f, 0.0) + jnp.where(upper, s_b, 0.0)
            oi_ref[orows, h * GLA_DV:(h + 1) * GLA_DV] = _dot(
                s.astype(BF16), vb[rows, h * GLA_DV:(h + 1) * GLA_DV])

    r = p[:, 1024:1536]
    sr_ref[rs, :] = (r * _sigmoid(r)).astype(BF16)

    u = p[:, 2048:2560] * p[:, 2560:3072]
    row = lax.broadcasted_iota(jnp.int32, u.shape, 0) & (GRID_W - 1)
    prev = jnp.where(row >= 1, pltpu.roll(u, 1, 0), 0.0)
    nxt = jnp.where(row < GRID_W - 1, pltpu.roll(u, tm - 1, 0), 0.0)
    cw = cw_ref[...]
    conv = prev * cw[0:1, :] + u * cw[1:2, :] + nxt * cw[2:3, :]
    yc_ref[rs, :] = (p[:, 1536:2048] * conv).astype(BF16)


def _projection(x2, row0, n, sh1, sc1, nw1, wm, wg, afb, bias_fb, conv_w, seq):
    d = x2.shape[1]
    tm = TOKEN_TILE
    tpb = seq // tm
    t0 = row0 // tm
    full = lambda a: pl.BlockSpec(a.shape, lambda i: (0,) * a.ndim)
    per_batch = pl.BlockSpec((None, 1, d), lambda i: (i // tpb, 0, 0))
    tok = lambda w: pl.BlockSpec((tm, w), lambda i: (i, 0))
    out_shape = (
        jax.ShapeDtypeStruct((n, 2 * GLA_QK), BF16),
        jax.ShapeDtypeStruct((n, 2 * GLA_QK), BF16),
        jax.ShapeDtypeStruct((n, GLA_WIDTH), BF16),
        jax.ShapeDtypeStruct((n, GLA_WIDTH), F32),
        jax.ShapeDtypeStruct((n, GLA_WIDTH), BF16),
        jax.ShapeDtypeStruct((n, CONV_WIDTH), BF16),
        jax.ShapeDtypeStruct((n // CHUNK, 2 * GLA_QK), F32),
    )
    out_specs = (tok(2 * GLA_QK), tok(2 * GLA_QK), tok(GLA_WIDTH), tok(GLA_WIDTH), tok(GLA_WIDTH),
                 tok(CONV_WIDTH), pl.BlockSpec((tm // CHUNK, 2 * GLA_QK), lambda i: (i, 0)))
    return pl.pallas_call(
        _proj_kernel,
        out_shape=out_shape,
        grid=(n // tm,),
        in_specs=[pl.BlockSpec((tm, d), lambda i: (i + t0, 0)), per_batch, per_batch, full(nw1),
                  full(wm), full(wg), full(afb), full(bias_fb), full(conv_w)],
        out_specs=out_specs,
        scratch_shapes=[pltpu.VMEM((tm // PROJ_SUB, PROJ_SUB, wm.shape[1]), F32)],
        compiler_params=_cparams("arbitrary"),
        name="projection",
    )(x2, sh1, sc1, nw1, wm, wg, afb, bias_fb, conv_w)


def _scan_kernel(af_ref, vf_ref, df_ref, ab_ref, vb_ref, db_ref, s0f_ref, s0b_ref,
                 of_ref, ob_ref, st_ref):
    @pl.when(pl.program_id(1) == 0)
    def _():
        st_ref[0] = s0f_ref[...]
        st_ref[1] = s0b_ref[...]

    nc = af_ref.shape[0] // CHUNK

    for direction, (a_ref, v_ref, d_ref, o_ref) in enumerate(
            ((af_ref, vf_ref, df_ref, of_ref), (ab_ref, vb_ref, db_ref, ob_ref))):
        order = range(nc) if direction == 0 else range(nc - 1, -1, -1)
        for h in range(GLA_HEADS):
            lanes = slice(h * GLA_DK, (h + 1) * GLA_DK)
            klanes = slice(GLA_QK + h * GLA_DK, GLA_QK + (h + 1) * GLA_DK)
            dlanes = slice(direction * GLA_QK + h * GLA_DK, direction * GLA_QK + (h + 1) * GLA_DK)
            vl = slice(h * GLA_DV, (h + 1) * GLA_DV)
            rows = [slice(c * CHUNK, (c + 1) * CHUNK) for c in range(nc)]
            kv = [_dot_tn(v_ref[rows[c], vl], a_ref[rows[c], klanes]) for c in range(nc)]
            s = st_ref[direction, h]
            start = [None] * nc
            for c in order:
                start[c] = s
                s = s * d_ref[c:c + 1, dlanes] + kv[c]
            st_ref[direction, h] = s
            for c in range(nc):
                o_ref[rows[c], vl] = _dot_nt(a_ref[rows[c], lanes], start[c].astype(BF16))


def _gla_scan(a_f, a_b, vb, dec, s0f, s0b, bsz, seq):
    n = a_f.shape[0]
    tb = GLA_BLOCK
    nb = seq // tb
    fwd = lambda w: pl.BlockSpec((tb, w), lambda b, j: (b * nb + j, 0))
    bwd = lambda w: pl.BlockSpec((tb, w), lambda b, j: (b * nb + nb - 1 - j, 0))
    dfw = pl.BlockSpec((tb // CHUNK, 2 * GLA_QK), lambda b, j: (b * nb + j, 0))
    dbw = pl.BlockSpec((tb // CHUNK, 2 * GLA_QK), lambda b, j: (b * nb + nb - 1 - j, 0))
    st_spec = pl.BlockSpec((None, GLA_HEADS, GLA_DV, GLA_DK), lambda b, j: (b, 0, 0, 0))
    out = jax.ShapeDtypeStruct((n, GLA_WIDTH), F32)
    return pl.pallas_call(
        _scan_kernel,
        out_shape=(out, out),
        grid=(bsz, nb),
        in_specs=[fwd(2 * GLA_QK), fwd(GLA_WIDTH), dfw, bwd(2 * GLA_QK), bwd(GLA_WIDTH), dbw,
                  st_spec, st_spec],
        out_specs=(fwd(GLA_WIDTH), bwd(GLA_WIDTH)),
        scratch_shapes=[pltpu.VMEM((2, GLA_HEADS, GLA_DV, GLA_DK), F32)],
        compiler_params=_cparams("arbitrary", "arbitrary"),
        name="gla_scan",
    )(a_f, vb, dec, a_b, vb, dec, s0f, s0b)


def _post_kernel(x_ref, oi_ref, of_ref, ob_ref, sr_ref, yc_ref, gnw_ref, wo_ref, g1_ref, nw2_ref,
                 sh2_ref, sc2_ref, rwt_ref, rb_ref, tri_ref,
                 h_ref, xw_ref, idx_ref, rank_ref, wt_ref, cnt_ref, run_ref):
    @pl.when(pl.program_id(0) == 0)
    def _():
        run_ref[...] = jnp.zeros_like(run_ref)

    tm = x_ref.shape[0]
    o = oi_ref[...] + of_ref[...] + ob_ref[...]
    normed = jnp.concatenate(
        [_rms(o[:, h * GLA_DV:(h + 1) * GLA_DV]) for h in range(GLA_HEADS)], axis=1)
    y_gla = (normed * gnw_ref[...] * sr_ref[...].astype(F32)).astype(BF16)
    wo = wo_ref[...]
    y = _dot(y_gla, wo[:GLA_WIDTH, :]) + _dot(yc_ref[...], wo[GLA_WIDTH:, :])
    h = x_ref[...] + g1_ref[...] * y
    h_ref[...] = h
    hn = _rms(h) * nw2_ref[...] * (1.0 + sc2_ref[...]) + sh2_ref[...]
    xw_ref[...] = _pack_halves(hn)

    logits = _dot3(rwt_ref[...], hn, _dot_nt) + rb_ref[...]
    eid = lax.broadcasted_iota(jnp.int32, logits.shape, 0)
    vals, idxs = [], []
    work = logits
    for _ in range(TOP_K):
        m = jnp.max(work, axis=0, keepdims=True)
        i = jnp.min(jnp.where(work == m, eid, N_EXPERTS), axis=0, keepdims=True)
        vals.append(m)
        idxs.append(i)
        work = jnp.where(eid == i, -jnp.inf, work)
    exps = [jnp.exp(v - vals[0]) for v in vals]
    denom = exps[0] + exps[1] + exps[2] + exps[3]
    wts = [e / denom for e in exps]

    onehot = jnp.zeros(logits.shape, F32)
    for i in idxs:
        onehot = onehot + jnp.where(eid == i, 1.0, 0.0)
    before = _dot(onehot.astype(BF16), tri_ref[...]) + run_ref[:, 0:1]
    for kk in range(TOP_K):
        idx_ref[kk:kk + 1, :] = idxs[kk]
        rk = jnp.sum(jnp.where(eid == idxs[kk], before, 0.0), axis=0, keepdims=True)
        rank_ref[kk:kk + 1, :] = rk.astype(jnp.int32)
    run = run_ref[...] + jnp.sum(onehot, axis=1, keepdims=True)
    run_ref[...] = run
    cnt_ref[...] = run.astype(jnp.int32)

    sub = lax.broadcasted_iota(jnp.int32, (GATE_PAD, tm), 0)
    wpad = jnp.zeros((GATE_PAD, tm), F32)
    for kk in range(TOP_K):
        wpad = wpad + jnp.where(sub == kk, wts[kk], 0.0)
    wt_ref[...] = wpad.T


def _post_mixer(x2, row0, o_i, o_f, o_b, sr, yc, gnw, wo, g1, nw2, sh2, sc2, rwt, rb, tri, seq):
    n, d = o_i.shape[0], x2.shape[1]
    tm = TOKEN_TILE
    tpb = seq // tm
    t0 = row0 // tm
    full = lambda a: pl.BlockSpec(a.shape, lambda i: (0,) * a.ndim)
    per_batch = pl.BlockSpec((None, 1, d), lambda i: (i // tpb, 0, 0))
    tok = lambda w: pl.BlockSpec((tm, w), lambda i: (i, 0))
    lane_tok = pl.BlockSpec((TOP_K, tm), lambda i: (0, i))
    out_shape = (
        jax.ShapeDtypeStruct((n, d), F32),
        jax.ShapeDtypeStruct((n, d // 2), jnp.uint32),
        jax.ShapeDtypeStruct((TOP_K, n), jnp.int32),
        jax.ShapeDtypeStruct((TOP_K, n), jnp.int32),
        jax.ShapeDtypeStruct((n, GATE_PAD), F32),
        jax.ShapeDtypeStruct((N_EXPERTS, GATE_PAD), jnp.int32),
    )
    out_specs = (tok(d), tok(d // 2), lane_tok, lane_tok, tok(GATE_PAD),
                 pl.BlockSpec((N_EXPERTS, GATE_PAD), lambda i: (0, 0)))
    return pl.pallas_call(
        _post_kernel,
        out_shape=out_shape,
        grid=(n // tm,),
        in_specs=[pl.BlockSpec((tm, d), lambda i: (i + t0, 0)), tok(GLA_WIDTH), tok(GLA_WIDTH),
                  tok(GLA_WIDTH), tok(GLA_WIDTH),
                  tok(CONV_WIDTH), full(gnw), full(wo), per_batch, full(nw2), per_batch, per_batch,
                  full(rwt), full(rb), full(tri)],
        out_specs=out_specs,
        scratch_shapes=[pltpu.VMEM((N_EXPERTS, GATE_PAD), F32)],
        compiler_params=_cparams("arbitrary"),
        name="post_mixer",
    )(x2, o_i, o_f, o_b, sr, yc, gnw, wo, g1, nw2, sh2, sc2, rwt, rb, tri)


def _slot_kernel(start_ref, idx_ref, rank_ref, slot_ref):
    idx = idx_ref[...]
    base = jnp.zeros(idx.shape, jnp.int32)
    for e in range(N_EXPERTS):
        base = jnp.where(idx == e, start_ref[e], base)
    slot_ref[...] = base + rank_ref[...]


def _slots(group_start, idx_t, rank_t):
    k, n = idx_t.shape
    bn = min(n, 8192)
    blk = pl.BlockSpec((k, bn), lambda i, s: (0, i))
    return pl.pallas_call(
        _slot_kernel,
        out_shape=jax.ShapeDtypeStruct((k, n), jnp.int32),
        grid_spec=pltpu.PrefetchScalarGridSpec(
            num_scalar_prefetch=1, grid=(n // bn,), in_specs=[blk, blk], out_specs=blk),
        compiler_params=_cparams("arbitrary"),
        name="slots",
    )(group_start, idx_t, rank_t)


SC_CORES = 2
SC_SUBCORES = 16
SC_WORKERS = SC_CORES * SC_SUBCORES
SCATTER_CHUNK = 128
GATHER_CHUNK = 64


def _sc_mesh():
    return plsc.VectorSubcoreMesh(core_axis_name="c", subcore_axis_name="s")


def _sc_worker():
    return lax.axis_index("s") * SC_CORES + lax.axis_index("c")


def _sc_scatter_rows(src, idx, n_out):
    n, w = src.shape
    m = idx.shape[0]
    c = SCATTER_CHUNK
    per = m // SC_WORKERS
    nch = per // c
    assert per * SC_WORKERS == m and nch * c == per and n % per == 0

    def body(src_hbm, idx_hbm, out_hbm, idx_v, rows_v):
        wid = _sc_worker()
        src_base = lax.rem(wid * per, n)
        pltpu.sync_copy(idx_hbm.at[wid], idx_v)

        @pl.loop(0, nch)
        def _(j):
            pltpu.sync_copy(src_hbm.at[pl.ds(src_base + j * c, c)], rows_v)
            pltpu.sync_copy(rows_v, out_hbm.at[idx_v.at[j]])

    return pl.kernel(
        body,
        out_type=jax.ShapeDtypeStruct((n_out, w), src.dtype),
        mesh=_sc_mesh(),
        scratch_types=[pltpu.VMEM((nch, c), jnp.int32), pltpu.VMEM((c, w), src.dtype)],
        name="sc_dispatch",
    )(src, idx.reshape(SC_WORKERS, nch, c))


def _sc_gather_rows(table, idx):
    w = table.shape[1]
    m = idx.shape[0]
    c = GATHER_CHUNK
    per = m // SC_WORKERS
    nch = per // c
    assert per * SC_WORKERS == m and nch * c == per and nch % 2 == 0

    def body(table_hbm, idx_hbm, out_hbm, idx_v, rows_v, sem0, sem1):
        wid = _sc_worker()
        base = wid * per
        sems = (sem0, sem1)
        pltpu.sync_copy(idx_hbm.at[wid], idx_v)

        def gather(j, b):
            return pltpu.make_async_copy(table_hbm.at[idx_v.at[j]], rows_v.at[b], sems[b])

        gather(0, 0).start()

        @pl.loop(0, nch, step=2)
        def _(g):
            for b in range(2):
                j = g + b
                gather(j, b).wait()

                @pl.when(j + 1 < nch)
                def _():
                    gather(j + 1, 1 - b).start()

                pltpu.sync_copy(rows_v.at[b], out_hbm.at[pl.ds(base + j * c, c)])

    return pl.kernel(
        body,
        out_type=jax.ShapeDtypeStruct((m, w), table.dtype),
        mesh=_sc_mesh(),
        scratch_types=[pltpu.VMEM((nch, c), jnp.int32), pltpu.VMEM((2, c, w), table.dtype),
                       pltpu.SemaphoreType.DMA, pltpu.SemaphoreType.DMA],
        name="sc_combine_gather",
    )(table, idx.reshape(SC_WORKERS, nch, c))


PERM_BLOCK = 256


def _expert_kernel(te_ref, first_ref, next_ref, nu_ref, x_ref, wgu_hbm, wdn_hbm, perm_ref, bg_ref,
                   bu_ref, bd_ref, y_ref, wgu_f, wdn_f, wg_s, wu_s, wd_s, sem):
    i = pl.program_id(0)
    used = i < nu_ref[0]

    def fetch(e):
        return (pltpu.make_async_copy(wgu_hbm.at[e], wgu_f, sem.at[0]),
                pltpu.make_async_copy(wdn_hbm.at[e], wdn_f, sem.at[1]))

    @pl.when(i == 0)
    def _():
        for cp in fetch(te_ref[0]):
            cp.start()

    @pl.when(used & (first_ref[i] == 1))
    def _():
        for cp in fetch(te_ref[i]):
            cp.wait()
        half = PERM_BLOCK // 2
        perm = perm_ref[...]
        for b in range(wgu_f.shape[1] // PERM_BLOCK):
            blk = wgu_f[:, b * PERM_BLOCK:(b + 1) * PERM_BLOCK].astype(BF16)
            t = _dot(blk, perm)
            wg_s[:, b * half:(b + 1) * half] = t[:, :half].astype(BF16)
            wu_s[:, b * half:(b + 1) * half] = t[:, half:].astype(BF16)
        wd_s[...] = wdn_f[...].astype(BF16)

        @pl.when(next_ref[i] >= 0)
        def _():
            for cp in fetch(next_ref[i]):
                cp.start()

    @pl.when(used)
    def _():
        x = _unpack_halves(x_ref[...]).astype(BF16)
        g = _dot(x, wg_s[...]) + bg_ref[...]
        u = _dot(x, wu_s[...]) + bu_ref[...]
        gate = jnp.minimum(g, SWIGLU_LIMIT)
        up = jnp.clip(u, -SWIGLU_LIMIT, SWIGLU_LIMIT)
        act = (up + 1.0) * (gate * _sigmoid(SWIGLU_ALPHA * gate))
        y = _dot(act.astype(BF16), wd_s[...]) + bd_ref[...]
        y_ref[...] = _pack_halves(y)

    @pl.when(jnp.logical_not(used))
    def _():
        y_ref[...] = jnp.zeros_like(y_ref)


def _experts(tile_expert, tile_first, tile_next, n_used, xs, w_gu, w_dn, bg, bu, bd):
    n_slots, half = xs.shape
    tm = EXPERT_TILE
    n_tiles = n_slots // tm
    dm, f, d = w_gu.shape[1], w_dn.shape[1], w_dn.shape[2]
    ph = PERM_BLOCK // 2
    j = jnp.arange(PERM_BLOCK)
    perm = (jnp.arange(PERM_BLOCK)[None, :] == (j // 2 + (j % 2) * ph)[:, None]).astype(BF16)
    row_in = pl.BlockSpec((tm, half), lambda i, te, tf, tn, nu: (jnp.minimum(i, nu[0] - 1), 0))
    row_out = pl.BlockSpec((tm, half),
                           lambda i, te, tf, tn, nu: (jnp.where(i < nu[0], i, n_tiles - 1), 0))
    bspec = lambda a: pl.BlockSpec((None,) + a.shape[1:], lambda i, te, tf, tn, nu: (te[i], 0, 0))
    return pl.pallas_call(
        _expert_kernel,
        out_shape=jax.ShapeDtypeStruct((n_slots, half), jnp.uint32),
        grid_spec=pltpu.PrefetchScalarGridSpec(
            num_scalar_prefetch=4, grid=(n_tiles,),
            in_specs=[row_in, pl.BlockSpec(memory_space=pl.ANY), pl.BlockSpec(memory_space=pl.ANY),
                      pl.BlockSpec((PERM_BLOCK, PERM_BLOCK), lambda i, te, tf, tn, nu: (0, 0)),
                      bspec(bg), bspec(bu), bspec(bd)],
            out_specs=row_out,
            scratch_shapes=[pltpu.VMEM((dm, 2 * f), F32), pltpu.VMEM((f, d), F32),
                            pltpu.VMEM((dm, f), BF16), pltpu.VMEM((dm, f), BF16),
                            pltpu.VMEM((f, d), BF16), pltpu.SemaphoreType.DMA((2,))]),
        compiler_params=_cparams("arbitrary"),
        name="experts",
    )(tile_expert, tile_first, tile_next, n_used, xs, w_gu, w_dn, perm, bg, bu, bd)


def _combine_kernel(y4_ref, h_ref, wt_ref, g2_ref, fnw_ref, *rest):
    o_ref = rest[-1]
    wt = wt_ref[...]
    acc = jnp.zeros(h_ref.shape, F32)
    for kk in range(TOP_K):
        acc = acc + wt[:, kk:kk + 1] * _unpack_halves(y4_ref[kk])
    h = h_ref[...] + g2_ref[...] * acc
    o_ref[...] = _rms(h) * fnw_ref[...]


def _combine(y4, h, wt, g2, fnw, seq, row0, n_total, prev_out):
    n, d = h.shape
    tt = COMBINE_TILE
    tpb = seq // tt
    t0 = row0 // tt
    in_specs = [pl.BlockSpec((TOP_K, tt, d // 2), lambda i: (0, i, 0)),
                pl.BlockSpec((tt, d), lambda i: (i, 0)),
                pl.BlockSpec((tt, GATE_PAD), lambda i: (i, 0)),
                pl.BlockSpec((None, 1, d), lambda i: (i // tpb, 0, 0)),
                pl.BlockSpec((1, d), lambda i: (0, 0))]
    args = [y4, h, wt, g2, fnw]
    aliases = {}
    if prev_out is not None:
        in_specs.append(pl.BlockSpec(memory_space=pl.ANY))
        args.append(prev_out)
        aliases = {len(args) - 1: 0}
    return pl.pallas_call(
        _combine_kernel,
        out_shape=jax.ShapeDtypeStruct((n_total, d), F32),
        grid=(n // tt,),
        in_specs=in_specs,
        out_specs=pl.BlockSpec((tt, d), lambda i: (i + t0, 0)),
        input_output_aliases=aliases,
        compiler_params=_cparams("arbitrary"),
        name="combine",
    )(*args)


def kernel(x, c, ctx, c_ctx, w_ada, b_ada, norm1_w, w_in, a_up_f, a_bias_f, a_up_b, a_bias_b,
           gla_norm_w, conv_w, w_out, norm2_w, router_w, router_b, w_gu, b_gu, w_dn, b_dn,
           final_norm_w):
    bsz, seq, d = x.shape
    n = bsz * seq
    assert w_ada.shape[0] == 1 and d == D_MODEL
    assert seq % TOKEN_TILE == 0 and seq % GLA_BLOCK == 0 and seq % GRID_W == 0
    assert ctx.shape[1] & (ctx.shape[1] - 1) == 0

    w_in0 = w_in[0]
    v_end = 2 * GLA_QK + GLA_WIDTH
    r_end = v_end + GLA_WIDTH
    g_end = r_end + 2 * GATE_RANK
    wm = jnp.concatenate([w_in0[:, :r_end], w_in0[:, g_end:]], axis=1).astype(BF16)
    wg_low = jnp.pad(w_in0[:, r_end:g_end], ((0, 0), (0, GATE_PAD - 2 * GATE_RANK))).astype(BF16)
    afb = jnp.zeros((GATE_PAD, 2 * GLA_QK), F32)
    afb = afb.at[:GATE_RANK, :GLA_QK].set(a_up_f[0]).at[GATE_RANK:2 * GATE_RANK, GLA_QK:].set(a_up_b[0])
    afb = afb.astype(BF16)
    bias_fb = jnp.concatenate([a_bias_f[0], a_bias_b[0]])[None]
    wkv = wm[:, GLA_QK:v_end]
    row = lambda a: a.reshape(1, -1)

    mod = _modulation(jnp.concatenate([c, c_ctx[None]], axis=0), w_ada[0], row(b_ada[0]))
    sh1, sc1, g1, sh2, sc2, g2 = [mod[:bsz, i * d:(i + 1) * d].reshape(bsz, 1, d) for i in range(6)]
    csh1, csc1 = mod[bsz:, 0:d], mod[bsz:, d:2 * d]

    s0f, s0b = _context_states(ctx, csh1, csc1, row(norm1_w[0]), wkv, wg_low, afb, bias_fb)

    x2 = x.reshape(n, d)
    tri = jnp.triu(jnp.ones((TOKEN_TILE, TOKEN_TILE), BF16), k=1)
    gnw = jnp.tile(gla_norm_w[0], GLA_HEADS)[None]
    wo = w_out[0].astype(BF16)
    bg_e, bu_e, bd_e = b_gu[0][:, None, 0::2], b_gu[0][:, None, 1::2], b_dn[0][:, None, :]

    n_chunks = N_CHUNKS if bsz % N_CHUNKS == 0 else 1
    cb = bsz // n_chunks
    nc = cb * seq
    n_tiles = nc * TOP_K // EXPERT_TILE + N_EXPERTS
    routed = []
    for ci in range(n_chunks):
        bs = slice(ci * cb, (ci + 1) * cb)
        row0 = ci * nc
        a_f, a_b, vb, o_i, sr, yc, dec = _projection(
            x2, row0, nc, sh1[bs], sc1[bs], row(norm1_w[0]), wm, wg_low, afb, bias_fb, conv_w[0], seq)
        o_f, o_b = _gla_scan(a_f, a_b, vb, dec, s0f[bs], s0b[bs], cb, seq)
        h, xw, idx_t, rank_t, wt, cnt = _post_mixer(
            x2, row0, o_i, o_f, o_b, sr, yc, gnw, wo, g1[bs], row(norm2_w[0]), sh2[bs], sc2[bs],
            router_w[0].T, router_b[0][:, None], tri, seq)

        counts = cnt[:, 0]
        padded = (counts + EXPERT_TILE - 1) // EXPERT_TILE * EXPERT_TILE
        group_end = jnp.cumsum(padded)
        group_start = (group_end - padded).astype(jnp.int32)
        n_used = (group_end[-1] // EXPERT_TILE).astype(jnp.int32)
        tile_ids = jnp.minimum(jnp.arange(n_tiles, dtype=jnp.int32), n_used - 1) * EXPERT_TILE
        tile_expert = jnp.minimum(jnp.sum(tile_ids[:, None] >= group_end[None, :], axis=1),
                                  N_EXPERTS - 1).astype(jnp.int32)
        tile_first = jnp.concatenate([jnp.ones((1,), jnp.int32),
                                      (tile_expert[1:] != tile_expert[:-1]).astype(jnp.int32)])
        nonempty = counts > 0
        later = jnp.where(nonempty[None, :] & (jnp.arange(N_EXPERTS)[None, :] >
                                               jnp.arange(N_EXPERTS)[:, None]),
                          jnp.arange(N_EXPERTS)[None, :], N_EXPERTS)
        next_expert = jnp.min(later, axis=1)
        next_expert = jnp.where(next_expert == N_EXPERTS, -1, next_expert).astype(jnp.int32)
        tile_next = next_expert[tile_expert]
        slot_flat = _slots(group_start, idx_t, rank_t).reshape(-1)
        xs = _sc_scatter_rows(xw, slot_flat, n_tiles * EXPERT_TILE)
        routed.append((h, wt, slot_flat, xs, tile_expert, tile_first, tile_next, n_used, bs, row0))

    gathered = []
    for h, wt, slot_flat, xs, tile_expert, tile_first, tile_next, n_used, bs, row0 in routed:
        ys = _experts(tile_expert, tile_first, tile_next, n_used.reshape(1), xs, w_gu[0], w_dn[0],
                      bg_e, bu_e, bd_e)
        y4 = _sc_gather_rows(ys, slot_flat).reshape(TOP_K, nc, d // 2)
        gathered.append((y4, h, wt, bs, row0))

    out = None
    for y4, h, wt, bs, row0 in gathered:
        out = _combine(y4, h, wt, g2[bs], row(final_norm_w), seq, row0, n, out)
    return out.reshape(bsz, seq, d)
```

```python
import functools

import jax
import jax.numpy as jnp
from jax import lax
from jax.experimental import pallas as pl
from jax.experimental.pallas import tpu as pltpu
from jax.experimental.pallas import tpu_sc as plsc

D_MODEL = 1024
GLA_HEADS = 4
GLA_DK = 64
GLA_DV = 128
GLA_QK = GLA_HEADS * GLA_DK
GLA_WIDTH = GLA_HEADS * GLA_DV
CONV_WIDTH = 512
GATE_RANK = 16
GATE_PAD = 128
GATE_NORMALIZER = 16.0
CHUNK = 64
GRID_W = 64
N_EXPERTS = 32
TOP_K = 4
SWIGLU_LIMIT = 7.0
SWIGLU_ALPHA = 1.702
EPS = 1e-6

TOKEN_TILE = 512
PROJ_SUB = 256
PROJ_COLS = 256
INTRA_GROUP = 256
EXPERT_TILE = 512
COMBINE_TILE = 512
N_CHUNKS = 2
VMEM_LIMIT = 56 * 1024 * 1024

F32 = jnp.float32
BF16 = jnp.bfloat16
HI_MASK = 0xFFFF0000


def _cparams(*sem):
    return pltpu.CompilerParams(dimension_semantics=sem, vmem_limit_bytes=VMEM_LIMIT)


def _rms(x):
    return x * lax.rsqrt(jnp.mean(x * x, axis=-1, keepdims=True) + EPS)


def _sigmoid(x):
    return 1.0 / (1.0 + jnp.exp(-x))


def _log_sigmoid(z):
    return jnp.minimum(z, 0.0) - jnp.log(1.0 + jnp.exp(-jnp.abs(z)))


def _dot(a, b):
    return jnp.dot(a, b, preferred_element_type=F32)


def _dot_nt(a, b):
    return lax.dot_general(a, b, (((1,), (1,)), ((), ())), preferred_element_type=F32)


def _dot_tn(a, b):
    return lax.dot_general(a, b, (((0,), (0,)), ((), ())), preferred_element_type=F32)


def _split_bf16(a):
    hi = a.astype(BF16)
    lo = (a - hi.astype(F32)).astype(BF16)
    return hi, lo


def _dot3(a, b, dot):
    a_hi, a_lo = _split_bf16(a)
    b_hi, b_lo = _split_bf16(b)
    return dot(a_hi, b_hi) + (dot(a_hi, b_lo) + dot(a_lo, b_hi))


def _segment_scan(x, seg, reverse, after_step=None):
    n = x.shape[0]
    row = lax.broadcasted_iota(jnp.int32, x.shape, 0) & (seg - 1)
    s = 1
    while s < seg:
        if reverse:
            shifted = pltpu.roll(x, n - s, 0)
            x = x + jnp.where(row < seg - s, shifted, 0.0)
        else:
            shifted = pltpu.roll(x, s, 0)
            x = x + jnp.where(row >= s, shifted, 0.0)
        if after_step is not None:
            after_step()
        s *= 2
    return x


def _pack_halves(x):
    n = x.shape[1] // 2
    bits = pltpu.bitcast(x.astype(BF16).astype(F32), jnp.uint32)
    return (bits[:, :n] & jnp.uint32(HI_MASK)) | (bits[:, n:] >> 16)


def _unpack_halves(w):
    hi = pltpu.bitcast(w & jnp.uint32(HI_MASK), F32)
    lo = pltpu.bitcast(w << 16, F32)
    return jnp.concatenate([hi, lo], axis=1)


def _mod_kernel(c_ref, w_ref, b_ref, o_ref):
    c = c_ref[...]
    s = c * _sigmoid(c)
    o_ref[...] = _dot3(s, w_ref[...], _dot) + b_ref[...]


def _modulation(cc, w_ada, b_ada):
    rows, d = cc.shape
    n = w_ada.shape[1]
    bn = 1536
    return pl.pallas_call(
        _mod_kernel,
        out_shape=jax.ShapeDtypeStruct((rows, n), F32),
        grid=(n // bn,),
        in_specs=[pl.BlockSpec((rows, d), lambda j: (0, 0)),
                  pl.BlockSpec((d, bn), lambda j: (0, j)),
                  pl.BlockSpec((1, bn), lambda j: (0, j))],
        out_specs=pl.BlockSpec((rows, bn), lambda j: (0, j)),
        compiler_params=_cparams("arbitrary"),
        name="modulation",
    )(cc, w_ada, b_ada)


def _ctx_kernel(x_ref, sh_ref, sc_ref, nw_ref, wkv_ref, wg_ref, afb_ref, bias_ref, sf_ref, sb_ref):
    x = x_ref[...]
    n = x.shape[0]
    hn = _rms(x) * nw_ref[...] * (1.0 + sc_ref[...]) + sh_ref[...]
    hb = hn.astype(BF16)
    kv = _dot(hb, wkv_ref[...])
    g = _dot(hb, wg_ref[...])
    z = _dot(g.astype(BF16), afb_ref[...]) + bias_ref[...]
    la = _log_sigmoid(z) * (1.0 / GATE_NORMALIZER)
    p = _segment_scan(la, n, False)
    p_f, p_b = p[:, :GLA_QK], p[:, GLA_QK:]
    w_f = jnp.exp(p_f[n - 1:n, :] - p_f)
    w_b = jnp.exp(p_b - la[:, GLA_QK:])
    k = kv[:, :GLA_QK]
    vb = kv[:, GLA_QK:].astype(BF16)
    ke_f = (k * w_f).astype(BF16)
    ke_b = (k * w_b).astype(BF16)
    for h in range(GLA_HEADS):
        vh = vb[:, h * GLA_DV:(h + 1) * GLA_DV]
        sf_ref[h] = _dot_tn(vh, ke_f[:, h * GLA_DK:(h + 1) * GLA_DK])
        sb_ref[h] = _dot_tn(vh, ke_b[:, h * GLA_DK:(h + 1) * GLA_DK])


def _context_states(ctx, csh1, csc1, nw1, wkv, wg, afb, bias_fb):
    bsz, n, d = ctx.shape
    full = lambda a: pl.BlockSpec(a.shape, lambda b: (0,) * a.ndim)
    st = jax.ShapeDtypeStruct((bsz, GLA_HEADS, GLA_DV, GLA_DK), F32)
    st_spec = pl.BlockSpec((None, GLA_HEADS, GLA_DV, GLA_DK), lambda b: (b, 0, 0, 0))
    return pl.pallas_call(
        _ctx_kernel,
        out_shape=(st, st),
        grid=(bsz,),
        in_specs=[pl.BlockSpec((None, n, d), lambda b: (b, 0, 0)),
                  full(csh1), full(csc1), full(nw1), full(wkv), full(wg), full(afb), full(bias_fb)],
        out_specs=(st_spec, st_spec),
        compiler_params=_cparams("arbitrary"),
        name="context_states",
    )(ctx, csh1, csc1, nw1, wkv, wg, afb, bias_fb)


def _chunk_scan(st_ref, qd, ke, v, dec, descending):
    nc = qd.shape[0] // CHUNK
    order = range(nc - 1, -1, -1) if descending else range(nc)
    rows = [slice(c * CHUNK, (c + 1) * CHUNK) for c in range(nc)]
    outs = []
    for h in range(GLA_HEADS):
        lanes = slice(h * GLA_DK, (h + 1) * GLA_DK)
        vl = slice(h * GLA_DV, (h + 1) * GLA_DV)
        kv = [_dot_tn(v[rows[c], vl], ke[rows[c], lanes]) for c in range(nc)]
        s = st_ref[h]
        start = [None] * nc
        for c in order:
            start[c] = s
            s = s * dec[c:c + 1, lanes] + kv[c]
        st_ref[h] = s
        outs.append(jnp.concatenate(
            [_dot_nt(qd[rows[c], lanes], start[c].astype(BF16)) for c in range(nc)], axis=0))
    return jnp.concatenate(outs, axis=1)


def _proj_kernel(tpb, x_ref, sh_ref, sc_ref, nw_ref, wm_ref, wg_ref, afb_ref, bias_ref, cw_ref,
                 s0_ref, ab_ref, v_ref, oi_ref, sr_ref, yc_ref, dec_ref, p_scr, st_ref):
    @pl.when(pl.program_id(0) % tpb == 0)
    def _():
        st_ref[...] = s0_ref[...]

    subs = range(x_ref.shape[0] // PROJ_SUB)
    nc = PROJ_SUB // CHUNK
    nblk = wm_ref.shape[1] // PROJ_COLS
    factors = []
    for sub in subs:
        x = x_ref[sub * PROJ_SUB:(sub + 1) * PROJ_SUB, :]
        hn = _rms(x) * nw_ref[...] * (1.0 + sc_ref[...]) + sh_ref[...]
        hb = hn.astype(BF16)
        g = _dot(hb, wg_ref[...])
        z = _dot(g.astype(BF16), afb_ref[...]) + bias_ref[...]

        todo = iter(range(nblk))

        def wide_block():
            j = next(todo, None)
            if j is not None:
                cols = slice(j * PROJ_COLS, (j + 1) * PROJ_COLS)
                p_scr[sub, :, cols] = _dot(hb, wm_ref[:, cols])

        la = _log_sigmoid(z) * (1.0 / GATE_NORMALIZER)
        wide_block()
        b_f = _segment_scan(la[:, :GLA_QK], CHUNK, False, wide_block)
        b_b = _segment_scan(la[:, GLA_QK:], CHUNK, True, wide_block)
        b_f3 = b_f.reshape(nc, CHUNK, GLA_QK)
        tot_f = b_f3[:, CHUNK - 1:CHUNK, :]
        b_b3 = b_b.reshape(nc, CHUNK, GLA_QK)
        tot_b = b_b3[:, 0:1, :]
        fac = []
        for arg in (b_f, -b_f, (tot_f - b_f3).reshape(PROJ_SUB, GLA_QK),
                    b_b, -b_b, (tot_b - b_b3).reshape(PROJ_SUB, GLA_QK)):
            fac.append(jnp.exp(arg))
            wide_block()
        fac += [jnp.exp(tot_f).reshape(nc, GLA_QK), jnp.exp(tot_b).reshape(nc, GLA_QK)]
        factors.append(fac)
        for _ in range(nblk):
            wide_block()
    for sub in subs:
        _proj_rows(sub, factors[sub], cw_ref, ab_ref, v_ref, oi_ref, sr_ref, yc_ref,
                   dec_ref, p_scr.at[sub], st_ref)


def _proj_rows(sub, factors, cw_ref, ab_ref, v_ref, oi_ref, sr_ref, yc_ref, dec_ref, p, st_ref):
    tm = PROJ_SUB
    nc = tm // CHUNK
    rs = slice(sub * tm, (sub + 1) * tm)
    cs = slice(sub * nc, (sub + 1) * nc)
    e_f, ei_f, ee_f, e_b, ei_b, ee_b, dec_f, dec_b = factors

    q = p[:, 0:GLA_QK] * (GLA_DK ** -0.5)
    k = p[:, GLA_QK:2 * GLA_QK]
    vb = p[:, 2 * GLA_QK:2 * GLA_QK + GLA_WIDTH].astype(BF16)
    v_ref[rs, :] = vb
    qd_f = (q * e_f).astype(BF16)
    ki_f = (k * ei_f).astype(BF16)
    ke_f = (k * ee_f).astype(BF16)
    qd_b = (q * e_b).astype(BF16)
    ki_b = (k * ei_b).astype(BF16)
    ke_b = (k * ee_b).astype(BF16)

    ab_ref[rs, :GLA_QK] = qd_b
    ab_ref[rs, GLA_QK:] = ke_b
    dec_ref[cs, :GLA_QK] = dec_f
    dec_ref[cs, GLA_QK:] = dec_b

    inter_f = _chunk_scan(st_ref, qd_f, ke_f, vb, dec_f, False)

    gi = INTRA_GROUP
    ri = lax.broadcasted_iota(jnp.int32, (gi, gi), 0)
    ci = lax.broadcasted_iota(jnp.int32, (gi, gi), 1)
    same = (ri // CHUNK) == (ci // CHUNK)
    lower = same & (ci <= ri)
    upper = same & (ci >= ri)
    for gidx in range(tm // gi):
        rows = slice(gidx * gi, (gidx + 1) * gi)
        orows = slice(sub * tm + gidx * gi, sub * tm + (gidx + 1) * gi)
        for h in range(GLA_HEADS):
            lanes = slice(h * GLA_DK, (h + 1) * GLA_DK)
            s_f = _dot_nt(qd_f[rows, lanes], ki_f[rows, lanes])
            s_b = _dot_nt(qd_b[rows, lanes], ki_b[rows, lanes])
            s = jnp.where(lower, s_f, 0.0) + jnp.where(upper, s_b, 0.0)
            vl = slice(h * GLA_DV, (h + 1) * GLA_DV)
            oi_ref[orows, vl] = _dot(s.astype(BF16), vb[rows, vl]) + inter_f[rows, vl]

    r = p[:, 1024:1536]
    sr_ref[rs, :] = (r * _sigmoid(r)).astype(BF16)

    u = p[:, 2048:2560] * p[:, 2560:3072]
    row = lax.broadcasted_iota(jnp.int32, u.shape, 0) & (GRID_W - 1)
    prev = jnp.where(row >= 1, pltpu.roll(u, 1, 0), 0.0)
    nxt = jnp.where(row < GRID_W - 1, pltpu.roll(u, tm - 1, 0), 0.0)
    cw = cw_ref[...]
    conv = prev * cw[0:1, :] + u * cw[1:2, :] + nxt * cw[2:3, :]
    yc_ref[rs, :] = (p[:, 1536:2048] * conv).astype(BF16)


def _projection(x2, row0, n, sh1, sc1, nw1, wm, wg, afb, bias_fb, conv_w, s0f, seq):
    d = x2.shape[1]
    tm = TOKEN_TILE
    tpb = seq // tm
    t0 = row0 // tm
    full = lambda a: pl.BlockSpec(a.shape, lambda i: (0,) * a.ndim)
    per_batch = pl.BlockSpec((None, 1, d), lambda i: (i // tpb, 0, 0))
    tok = lambda w: pl.BlockSpec((tm, w), lambda i: (i, 0))
    st_spec = pl.BlockSpec((None, GLA_HEADS, GLA_DV, GLA_DK), lambda i: (i // tpb, 0, 0, 0))
    out_shape = (
        jax.ShapeDtypeStruct((n, 2 * GLA_QK), BF16),
        jax.ShapeDtypeStruct((n, GLA_WIDTH), BF16),
        jax.ShapeDtypeStruct((n, GLA_WIDTH), F32),
        jax.ShapeDtypeStruct((n, GLA_WIDTH), BF16),
        jax.ShapeDtypeStruct((n, CONV_WIDTH), BF16),
        jax.ShapeDtypeStruct((n // CHUNK, 2 * GLA_QK), F32),
    )
    out_specs = (tok(2 * GLA_QK), tok(GLA_WIDTH), tok(GLA_WIDTH), tok(GLA_WIDTH),
                 tok(CONV_WIDTH), pl.BlockSpec((tm // CHUNK, 2 * GLA_QK), lambda i: (i, 0)))
    return pl.pallas_call(
        functools.partial(_proj_kernel, tpb),
        out_shape=out_shape,
        grid=(n // tm,),
        in_specs=[pl.BlockSpec((tm, d), lambda i: (i + t0, 0)), per_batch, per_batch, full(nw1),
                  full(wm), full(wg), full(afb), full(bias_fb), full(conv_w), st_spec],
        out_specs=out_specs,
        scratch_shapes=[pltpu.VMEM((tm // PROJ_SUB, PROJ_SUB, wm.shape[1]), F32),
                        pltpu.VMEM((GLA_HEADS, GLA_DV, GLA_DK), F32)],
        compiler_params=_cparams("arbitrary"),
        name="projection",
    )(x2, sh1, sc1, nw1, wm, wg, afb, bias_fb, conv_w, s0f)


def _post_kernel(tpb, x_ref, oi_ref, ab_ref, v_ref, dec_ref, s0_ref, sr_ref, yc_ref, gnw_ref, wo_ref,
                 g1_ref, nw2_ref, sh2_ref, sc2_ref, rwt_ref, rb_ref, tri_ref,
                 h_ref, xw_ref, idx_ref, rank_ref, wt_ref, cnt_ref, run_ref, st_ref):
    @pl.when(pl.program_id(0) == 0)
    def _():
        run_ref[...] = jnp.zeros_like(run_ref)

    @pl.when(pl.program_id(0) % tpb == 0)
    def _():
        st_ref[...] = s0_ref[...]

    tm = x_ref.shape[0]
    inter_b = _chunk_scan(st_ref, ab_ref[:, :GLA_QK], ab_ref[:, GLA_QK:], v_ref[...],
                          dec_ref[:, GLA_QK:], True)
    o = oi_ref[...] + inter_b
    normed = jnp.concatenate(
        [_rms(o[:, h * GLA_DV:(h + 1) * GLA_DV]) for h in range(GLA_HEADS)], axis=1)
    y_gla = (normed * gnw_ref[...] * sr_ref[...].astype(F32)).astype(BF16)
    wo = wo_ref[...]
    y = _dot(y_gla, wo[:GLA_WIDTH, :]) + _dot(yc_ref[...], wo[GLA_WIDTH:, :])
    h = x_ref[...] + g1_ref[...] * y
    h_ref[...] = h
    hn = _rms(h) * nw2_ref[...] * (1.0 + sc2_ref[...]) + sh2_ref[...]
    xw_ref[...] = _pack_halves(hn)

    logits = _dot3(rwt_ref[...], hn, _dot_nt) + rb_ref[...]
    eid = lax.broadcasted_iota(jnp.int32, logits.shape, 0)
    vals, idxs = [], []
    work = logits
    for _ in range(TOP_K):
        m = jnp.max(work, axis=0, keepdims=True)
        i = jnp.min(jnp.where(work == m, eid, N_EXPERTS), axis=0, keepdims=True)
        vals.append(m)
        idxs.append(i)
        work = jnp.where(eid == i, -jnp.inf, work)
    exps = [jnp.exp(v - vals[0]) for v in vals]
    denom = exps[0] + exps[1] + exps[2] + exps[3]
    wts = [e / denom for e in exps]

    onehot = jnp.zeros(logits.shape, F32)
    for i in idxs:
        onehot = onehot + jnp.where(eid == i, 1.0, 0.0)
    before = _dot(onehot.astype(BF16), tri_ref[...]) + run_ref[:, 0:1]
    for kk in range(TOP_K):
        idx_ref[kk:kk + 1, :] = idxs[kk]
        rk = jnp.sum(jnp.where(eid == idxs[kk], before, 0.0), axis=0, keepdims=True)
        rank_ref[kk:kk + 1, :] = rk.astype(jnp.int32)
    run = run_ref[...] + jnp.sum(onehot, axis=1, keepdims=True)
    run_ref[...] = run
    cnt_ref[...] = run.astype(jnp.int32)

    sub = lax.broadcasted_iota(jnp.int32, (GATE_PAD, tm), 0)
    wpad = jnp.zeros((GATE_PAD, tm), F32)
    for kk in range(TOP_K):
        wpad = wpad + jnp.where(sub == kk, wts[kk], 0.0)
    wt_ref[...] = wpad.T


def _post_mixer(x2, row0, o_i, a_b, vb, dec, s0b, sr, yc, gnw, wo, g1, nw2, sh2, sc2, rwt, rb, tri,
                seq):
    n, d = o_i.shape[0], x2.shape[1]
    tm = TOKEN_TILE
    tpb = seq // tm
    t0 = row0 // tm
    rev = lambda i: (i // tpb) * tpb + (tpb - 1 - i % tpb)
    full = lambda a: pl.BlockSpec(a.shape, lambda i: (0,) * a.ndim)
    per_batch = pl.BlockSpec((None, 1, d), lambda i: (i // tpb, 0, 0))
    tok = lambda w: pl.BlockSpec((tm, w), lambda i: (rev(i), 0))
    lane_tok = pl.BlockSpec((TOP_K, tm), lambda i: (0, rev(i)))
    st_spec = pl.BlockSpec((None, GLA_HEADS, GLA_DV, GLA_DK), lambda i: (i // tpb, 0, 0, 0))
    dec_spec = pl.BlockSpec((tm // CHUNK, 2 * GLA_QK), lambda i: (rev(i), 0))
    out_shape = (
        jax.ShapeDtypeStruct((n, d), F32),
        jax.ShapeDtypeStruct((n, d // 2), jnp.uint32),
        jax.ShapeDtypeStruct((TOP_K, n), jnp.int32),
        jax.ShapeDtypeStruct((TOP_K, n), jnp.int32),
        jax.ShapeDtypeStruct((n, GATE_PAD), F32),
        jax.ShapeDtypeStruct((N_EXPERTS, GATE_PAD), jnp.int32),
    )
    out_specs = (tok(d), tok(d // 2), lane_tok, lane_tok, tok(GATE_PAD),
                 pl.BlockSpec((N_EXPERTS, GATE_PAD), lambda i: (0, 0)))
    return pl.pallas_call(
        functools.partial(_post_kernel, tpb),
        out_shape=out_shape,
        grid=(n // tm,),
        in_specs=[pl.BlockSpec((tm, d), lambda i: (rev(i) + t0, 0)), tok(GLA_WIDTH),
                  tok(2 * GLA_QK), tok(GLA_WIDTH), dec_spec, st_spec, tok(GLA_WIDTH),
                  tok(CONV_WIDTH), full(gnw), full(wo), per_batch, full(nw2), per_batch, per_batch,
                  full(rwt), full(rb), full(tri)],
        out_specs=out_specs,
        scratch_shapes=[pltpu.VMEM((N_EXPERTS, GATE_PAD), F32),
                        pltpu.VMEM((GLA_HEADS, GLA_DV, GLA_DK), F32)],
        compiler_params=_cparams("arbitrary"),
        name="post_mixer",
    )(x2, o_i, a_b, vb, dec, s0b, sr, yc, gnw, wo, g1, nw2, sh2, sc2, rwt, rb, tri)


def _slot_kernel(start_ref, idx_ref, rank_ref, slot_ref):
    idx = idx_ref[...]
    base = jnp.zeros(idx.shape, jnp.int32)
    for e in range(N_EXPERTS):
        base = jnp.where(idx == e, start_ref[e], base)
    slot_ref[...] = base + rank_ref[...]


def _slots(group_start, idx_t, rank_t):
    k, n = idx_t.shape
    bn = min(n, 8192)
    blk = pl.BlockSpec((k, bn), lambda i, s: (0, i))
    return pl.pallas_call(
        _slot_kernel,
        out_shape=jax.ShapeDtypeStruct((k, n), jnp.int32),
        grid_spec=pltpu.PrefetchScalarGridSpec(
            num_scalar_prefetch=1, grid=(n // bn,), in_specs=[blk, blk], out_specs=blk),
        compiler_params=_cparams("arbitrary"),
        name="slots",
    )(group_start, idx_t, rank_t)


SC_CORES = 2
SC_SUBCORES = 16
SC_WORKERS = SC_CORES * SC_SUBCORES
SCATTER_CHUNK = 128
GATHER_CHUNK = 64


def _sc_mesh():
    return plsc.VectorSubcoreMesh(core_axis_name="c", subcore_axis_name="s")


def _sc_worker():
    return lax.axis_index("s") * SC_CORES + lax.axis_index("c")


def _sc_scatter_rows(src, idx, n_out):
    n, w = src.shape
    m = idx.shape[0]
    c = SCATTER_CHUNK
    per = m // SC_WORKERS
    nch = per // c
    assert per * SC_WORKERS == m and nch * c == per and n % per == 0

    def body(src_hbm, idx_hbm, out_hbm, idx_v, rows_v):
        wid = _sc_worker()
        src_base = lax.rem(wid * per, n)
        pltpu.sync_copy(idx_hbm.at[wid], idx_v)

        @pl.loop(0, nch)
        def _(j):
            pltpu.sync_copy(src_hbm.at[pl.ds(src_base + j * c, c)], rows_v)
            pltpu.sync_copy(rows_v, out_hbm.at[idx_v.at[j]])

    return pl.kernel(
        body,
        out_type=jax.ShapeDtypeStruct((n_out, w), src.dtype),
        mesh=_sc_mesh(),
        scratch_types=[pltpu.VMEM((nch, c), jnp.int32), pltpu.VMEM((c, w), src.dtype)],
        name="sc_dispatch",
    )(src, idx.reshape(SC_WORKERS, nch, c))


def _sc_gather_rows(table, idx):
    w = table.shape[1]
    m = idx.shape[0]
    c = GATHER_CHUNK
    per = m // SC_WORKERS
    nch = per // c
    assert per * SC_WORKERS == m and nch * c == per and nch % 2 == 0

    def body(table_hbm, idx_hbm, out_hbm, idx_v, rows_v, sem0, sem1):
        wid = _sc_worker()
        base = wid * per
        sems = (sem0, sem1)
        pltpu.sync_copy(idx_hbm.at[wid], idx_v)

        def gather(j, b):
            return pltpu.make_async_copy(table_hbm.at[idx_v.at[j]], rows_v.at[b], sems[b])

        gather(0, 0).start()

        @pl.loop(0, nch, step=2)
        def _(g):
            for b in range(2):
                j = g + b
                gather(j, b).wait()

                @pl.when(j + 1 < nch)
                def _():
                    gather(j + 1, 1 - b).start()

                pltpu.sync_copy(rows_v.at[b], out_hbm.at[pl.ds(base + j * c, c)])

    return pl.kernel(
        body,
        out_type=jax.ShapeDtypeStruct((m, w), table.dtype),
        mesh=_sc_mesh(),
        scratch_types=[pltpu.VMEM((nch, c), jnp.int32), pltpu.VMEM((2, c, w), table.dtype),
                       pltpu.SemaphoreType.DMA, pltpu.SemaphoreType.DMA],
        name="sc_combine_gather",
    )(table, idx.reshape(SC_WORKERS, nch, c))


PERM_BLOCK = 256


def _expert_kernel(te_ref, first_ref, next_ref, nu_ref, x_ref, wgu_hbm, wdn_hbm, perm_ref, bg_ref,
                   bu_ref, bd_ref, y_ref, wgu_f, wdn_f, wg_s, wu_s, wd_s, sem):
    i = pl.program_id(0)
    used = i < nu_ref[0]

    def fetch(e):
        return (pltpu.make_async_copy(wgu_hbm.at[e], wgu_f, sem.at[0]),
                pltpu.make_async_copy(wdn_hbm.at[e], wdn_f, sem.at[1]))

    @pl.when(i == 0)
    def _():
        for cp in fetch(te_ref[0]):
            cp.start()

    @pl.when(used & (first_ref[i] == 1))
    def _():
        for cp in fetch(te_ref[i]):
            cp.wait()
        half = PERM_BLOCK // 2
        perm = perm_ref[...]
        for b in range(wgu_f.shape[1] // PERM_BLOCK):
            blk = wgu_f[:, b * PERM_BLOCK:(b + 1) * PERM_BLOCK].astype(BF16)
            t = _dot(blk, perm)
            wg_s[:, b * half:(b + 1) * half] = t[:, :half].astype(BF16)
            wu_s[:, b * half:(b + 1) * half] = t[:, half:].astype(BF16)
        wd_s[...] = wdn_f[...].astype(BF16)

        @pl.when(next_ref[i] >= 0)
        def _():
            for cp in fetch(next_ref[i]):
                cp.start()

    @pl.when(used)
    def _():
        x = _unpack_halves(x_ref[...]).astype(BF16)
        g = _dot(x, wg_s[...]) + bg_ref[...]
        u = _dot(x, wu_s[...]) + bu_ref[...]
        gate = jnp.minimum(g, SWIGLU_LIMIT)
        up = jnp.clip(u, -SWIGLU_LIMIT, SWIGLU_LIMIT)
        act = (up + 1.0) * (gate * _sigmoid(SWIGLU_ALPHA * gate))
        y = _dot(act.astype(BF16), wd_s[...]) + bd_ref[...]
        y_ref[...] = _pack_halves(y)

    @pl.when(jnp.logical_not(used))
    def _():
        y_ref[...] = jnp.zeros_like(y_ref)


def _experts(tile_expert, tile_first, tile_next, n_used, xs, w_gu, w_dn, bg, bu, bd):
    n_slots, half = xs.shape
    tm = EXPERT_TILE
    n_tiles = n_slots // tm
    dm, f, d = w_gu.shape[1], w_dn.shape[1], w_dn.shape[2]
    ph = PERM_BLOCK // 2
    j = jnp.arange(PERM_BLOCK)
    perm = (jnp.arange(PERM_BLOCK)[None, :] == (j // 2 + (j % 2) * ph)[:, None]).astype(BF16)
    row_in = pl.BlockSpec((tm, half), lambda i, te, tf, tn, nu: (jnp.minimum(i, nu[0] - 1), 0))
    row_out = pl.BlockSpec((tm, half),
                           lambda i, te, tf, tn, nu: (jnp.where(i < nu[0], i, n_tiles - 1), 0))
    bspec = lambda a: pl.BlockSpec((None,) + a.shape[1:], lambda i, te, tf, tn, nu: (te[i], 0, 0))
    return pl.pallas_call(
        _expert_kernel,
        out_shape=jax.ShapeDtypeStruct((n_slots, half), jnp.uint32),
        grid_spec=pltpu.PrefetchScalarGridSpec(
            num_scalar_prefetch=4, grid=(n_tiles,),
            in_specs=[row_in, pl.BlockSpec(memory_space=pl.ANY), pl.BlockSpec(memory_space=pl.ANY),
                      pl.BlockSpec((PERM_BLOCK, PERM_BLOCK), lambda i, te, tf, tn, nu: (0, 0)),
                      bspec(bg), bspec(bu), bspec(bd)],
            out_specs=row_out,
            scratch_shapes=[pltpu.VMEM((dm, 2 * f), F32), pltpu.VMEM((f, d), F32),
                            pltpu.VMEM((dm, f), BF16), pltpu.VMEM((dm, f), BF16),
                            pltpu.VMEM((f, d), BF16), pltpu.SemaphoreType.DMA((2,))]),
        compiler_params=_cparams("arbitrary"),
        name="experts",
    )(tile_expert, tile_first, tile_next, n_used, xs, w_gu, w_dn, perm, bg, bu, bd)


def _combine_kernel(y4_ref, h_ref, wt_ref, g2_ref, fnw_ref, *rest):
    o_ref = rest[-1]
    wt = wt_ref[...]
    acc = jnp.zeros(h_ref.shape, F32)
    for kk in range(TOP_K):
        acc = acc + wt[:, kk:kk + 1] * _unpack_halves(y4_ref[kk])
    h = h_ref[...] + g2_ref[...] * acc
    o_ref[...] = _rms(h) * fnw_ref[...]


def _combine(y4, h, wt, g2, fnw, seq, row0, n_total, prev_out):
    n, d = h.shape
    tt = COMBINE_TILE
    tpb = seq // tt
    t0 = row0 // tt
    in_specs = [pl.BlockSpec((TOP_K, tt, d // 2), lambda i: (0, i, 0)),
                pl.BlockSpec((tt, d), lambda i: (i, 0)),
                pl.BlockSpec((tt, GATE_PAD), lambda i: (i, 0)),
                pl.BlockSpec((None, 1, d), lambda i: (i // tpb, 0, 0)),
                pl.BlockSpec((1, d), lambda i: (0, 0))]
    args = [y4, h, wt, g2, fnw]
    aliases = {}
    if prev_out is not None:
        in_specs.append(pl.BlockSpec(memory_space=pl.ANY))
        args.append(prev_out)
        aliases = {len(args) - 1: 0}
    return pl.pallas_call(
        _combine_kernel,
        out_shape=jax.ShapeDtypeStruct((n_total, d), F32),
        grid=(n // tt,),
        in_specs=in_specs,
        out_specs=pl.BlockSpec((tt, d), lambda i: (i + t0, 0)),
        input_output_aliases=aliases,
        compiler_params=_cparams("arbitrary"),
        name="combine",
    )(*args)


def kernel(x, c, ctx, c_ctx, w_ada, b_ada, norm1_w, w_in, a_up_f, a_bias_f, a_up_b, a_bias_b,
           gla_norm_w, conv_w, w_out, norm2_w, router_w, router_b, w_gu, b_gu, w_dn, b_dn,
           final_norm_w):
    bsz, seq, d = x.shape
    n = bsz * seq
    assert w_ada.shape[0] == 1 and d == D_MODEL
    assert seq % TOKEN_TILE == 0 and seq % GRID_W == 0
    assert ctx.shape[1] & (ctx.shape[1] - 1) == 0

    w_in0 = w_in[0]
    v_end = 2 * GLA_QK + GLA_WIDTH
    r_end = v_end + GLA_WIDTH
    g_end = r_end + 2 * GATE_RANK
    wm = jnp.concatenate([w_in0[:, :r_end], w_in0[:, g_end:]], axis=1).astype(BF16)
    wg_low = jnp.pad(w_in0[:, r_end:g_end], ((0, 0), (0, GATE_PAD - 2 * GATE_RANK))).astype(BF16)
    afb = jnp.zeros((GATE_PAD, 2 * GLA_QK), F32)
    afb = afb.at[:GATE_RANK, :GLA_QK].set(a_up_f[0]).at[GATE_RANK:2 * GATE_RANK, GLA_QK:].set(a_up_b[0])
    afb = afb.astype(BF16)
    bias_fb = jnp.concatenate([a_bias_f[0], a_bias_b[0]])[None]
    wkv = wm[:, GLA_QK:v_end]
    row = lambda a: a.reshape(1, -1)

    mod = _modulation(jnp.concatenate([c, c_ctx[None]], axis=0), w_ada[0], row(b_ada[0]))
    sh1, sc1, g1, sh2, sc2, g2 = [mod[:bsz, i * d:(i + 1) * d].reshape(bsz, 1, d) for i in range(6)]
    csh1, csc1 = mod[bsz:, 0:d], mod[bsz:, d:2 * d]

    s0f, s0b = _context_states(ctx, csh1, csc1, row(norm1_w[0]), wkv, wg_low, afb, bias_fb)

    x2 = x.reshape(n, d)
    tri = jnp.triu(jnp.ones((TOKEN_TILE, TOKEN_TILE), BF16), k=1)
    gnw = jnp.tile(gla_norm_w[0], GLA_HEADS)[None]
    wo = w_out[0].astype(BF16)
    bg_e, bu_e, bd_e = b_gu[0][:, None, 0::2], b_gu[0][:, None, 1::2], b_dn[0][:, None, :]

    n_chunks = N_CHUNKS if bsz % N_CHUNKS == 0 else 1
    cb = bsz // n_chunks
    nc = cb * seq
    n_tiles = nc * TOP_K // EXPERT_TILE + N_EXPERTS
    routed = []
    for ci in range(n_chunks):
        bs = slice(ci * cb, (ci + 1) * cb)
        row0 = ci * nc
        a_b, vb, o_i, sr, yc, dec = _projection(
            x2, row0, nc, sh1[bs], sc1[bs], row(norm1_w[0]), wm, wg_low, afb, bias_fb, conv_w[0],
            s0f[bs], seq)
        h, xw, idx_t, rank_t, wt, cnt = _post_mixer(
            x2, row0, o_i, a_b, vb, dec, s0b[bs], sr, yc, gnw, wo, g1[bs], row(norm2_w[0]), sh2[bs],
            sc2[bs], router_w[0].T, router_b[0][:, None], tri, seq)

        counts = cnt[:, 0]
        padded = (counts + EXPERT_TILE - 1) // EXPERT_TILE * EXPERT_TILE
        group_end = jnp.cumsum(padded)
        group_start = (group_end - padded).astype(jnp.int32)
        n_used = (group_end[-1] // EXPERT_TILE).astype(jnp.int32)
        tile_ids = jnp.minimum(jnp.arange(n_tiles, dtype=jnp.int32), n_used - 1) * EXPERT_TILE
        tile_expert = jnp.minimum(jnp.sum(tile_ids[:, None] >= group_end[None, :], axis=1),
                                  N_EXPERTS - 1).astype(jnp.int32)
        tile_first = jnp.concatenate([jnp.ones((1,), jnp.int32),
                                      (tile_expert[1:] != tile_expert[:-1]).astype(jnp.int32)])
        nonempty = counts > 0
        later = jnp.where(nonempty[None, :] & (jnp.arange(N_EXPERTS)[None, :] >
                                               jnp.arange(N_EXPERTS)[:, None]),
                          jnp.arange(N_EXPERTS)[None, :], N_EXPERTS)
        next_expert = jnp.min(later, axis=1)
        next_expert = jnp.where(next_expert == N_EXPERTS, -1, next_expert).astype(jnp.int32)
        tile_next = next_expert[tile_expert]
        slot_flat = _slots(group_start, idx_t, rank_t).reshape(-1)
        xs = _sc_scatter_rows(xw, slot_flat, n_tiles * EXPERT_TILE)
        routed.append((h, wt, slot_flat, xs, tile_expert, tile_first, tile_next, n_used, bs, row0))

    gathered = []
    for h, wt, slot_flat, xs, tile_expert, tile_first, tile_next, n_used, bs, row0 in routed:
        ys = _experts(tile_expert, tile_first, tile_next, n_used.reshape(1), xs, w_gu[0], w_dn[0],
                      bg_e, bu_e, bd_e)
        y4 = _sc_gather_rows(ys, slot_flat).reshape(TOP_K, nc, d // 2)
        gathered.append((y4, h, wt, bs, row0))

    out = None
    for y4, h, wt, bs, row0 in gathered:
        out = _combine(y4, h, wt, g2[bs], row(final_norm_w), seq, row0, n, out)
    return out.reshape(bsz, seq, d)
```

```python
import functools

import jax
import jax.numpy as jnp
from jax import lax
from jax.experimental import pallas as pl
from jax.experimental.pallas import tpu as pltpu
from jax.experimental.pallas import tpu_sc as plsc

D_MODEL = 1024
GLA_HEADS = 4
GLA_DK = 64
GLA_DV = 128
GLA_QK = GLA_HEADS * GLA_DK
GLA_WIDTH = GLA_HEADS * GLA_DV
CONV_WIDTH = 512
GATE_RANK = 16
GATE_PAD = 128
GATE_NORMALIZER = 16.0
CHUNK = 64
GRID_W = 64
N_EXPERTS = 32
TOP_K = 4
SWIGLU_LIMIT = 7.0
SWIGLU_ALPHA = 1.702
EPS = 1e-6

TOKEN_TILE = 512
PROJ_SUB = 256
PROJ_COLS = 256
INTRA_GROUP = 256
EXPERT_TILE = 512
COMBINE_TILE = 512
N_CHUNKS = 2
VMEM_LIMIT = 56 * 1024 * 1024

F32 = jnp.float32
BF16 = jnp.bfloat16
HI_MASK = 0xFFFF0000
LOG2_E = 1.4426950408889634


def _cparams(*sem):
    return pltpu.CompilerParams(dimension_semantics=sem, vmem_limit_bytes=VMEM_LIMIT)


def _rms(x):
    return x * lax.rsqrt(jnp.mean(x * x, axis=-1, keepdims=True) + EPS)


def _sigmoid(x, scale=1.0):
    return 1.0 / (1.0 + jnp.exp2(x * (-scale * LOG2_E)))


def _log_sigmoid(z):
    return jnp.minimum(z, 0.0) - jnp.log(1.0 + jnp.exp(-jnp.abs(z)))


def _dot(a, b):
    return jnp.dot(a, b, preferred_element_type=F32)


def _dot_nt(a, b):
    return lax.dot_general(a, b, (((1,), (1,)), ((), ())), preferred_element_type=F32)


def _dot_tn(a, b):
    return lax.dot_general(a, b, (((0,), (0,)), ((), ())), preferred_element_type=F32)


def _split_bf16(a):
    hi = a.astype(BF16)
    lo = (a - hi.astype(F32)).astype(BF16)
    return hi, lo


def _dot3(a, b, dot):
    a_hi, a_lo = _split_bf16(a)
    b_hi, b_lo = _split_bf16(b)
    return dot(a_hi, b_hi) + (dot(a_hi, b_lo) + dot(a_lo, b_hi))


def _segment_scan(x, seg, reverse, after_step=None):
    n = x.shape[0]
    row = lax.broadcasted_iota(jnp.int32, x.shape, 0) & (seg - 1)
    s = 1
    while s < seg:
        if reverse:
            shifted = pltpu.roll(x, n - s, 0)
            x = x + jnp.where(row < seg - s, shifted, 0.0)
        else:
            shifted = pltpu.roll(x, s, 0)
            x = x + jnp.where(row >= s, shifted, 0.0)
        if after_step is not None:
            after_step()
        s *= 2
    return x


def _pack_halves(x):
    n = x.shape[1] // 2
    bits = pltpu.bitcast(x.astype(BF16).astype(F32), jnp.uint32)
    return (bits[:, :n] & jnp.uint32(HI_MASK)) | (bits[:, n:] >> 16)


def _unpack_halves(w):
    hi = pltpu.bitcast(w & jnp.uint32(HI_MASK), F32)
    lo = pltpu.bitcast(w << 16, F32)
    return jnp.concatenate([hi, lo], axis=1)


def _mod_kernel(c_ref, w_ref, b_ref, o_ref):
    c = c_ref[...]
    s = c * _sigmoid(c)
    o_ref[...] = _dot3(s, w_ref[...], _dot) + b_ref[...]


def _modulation(cc, w_ada, b_ada):
    rows, d = cc.shape
    n = w_ada.shape[1]
    bn = 1536
    return pl.pallas_call(
        _mod_kernel,
        out_shape=jax.ShapeDtypeStruct((rows, n), F32),
        grid=(n // bn,),
        in_specs=[pl.BlockSpec((rows, d), lambda j: (0, 0)),
                  pl.BlockSpec((d, bn), lambda j: (0, j)),
                  pl.BlockSpec((1, bn), lambda j: (0, j))],
        out_specs=pl.BlockSpec((rows, bn), lambda j: (0, j)),
        compiler_params=_cparams("arbitrary"),
        name="modulation",
    )(cc, w_ada, b_ada)


def _ctx_kernel(x_ref, sh_ref, sc_ref, nw_ref, wkv_ref, wg_ref, afb_ref, bias_ref, sf_ref, sb_ref):
    x = x_ref[...]
    n = x.shape[0]
    hn = _rms(x) * nw_ref[...] * (1.0 + sc_ref[...]) + sh_ref[...]
    hb = hn.astype(BF16)
    kv = _dot(hb, wkv_ref[...])
    g = _dot(hb, wg_ref[...])
    z = _dot(g.astype(BF16), afb_ref[...]) + bias_ref[...]
    la = _log_sigmoid(z) * (1.0 / GATE_NORMALIZER)
    p = _segment_scan(la, n, False)
    p_f, p_b = p[:, :GLA_QK], p[:, GLA_QK:]
    w_f = jnp.exp(p_f[n - 1:n, :] - p_f)
    w_b = jnp.exp(p_b - la[:, GLA_QK:])
    k = kv[:, :GLA_QK]
    vb = kv[:, GLA_QK:].astype(BF16)
    ke_f = (k * w_f).astype(BF16)
    ke_b = (k * w_b).astype(BF16)
    for h in range(GLA_HEADS):
        vh = vb[:, h * GLA_DV:(h + 1) * GLA_DV]
        sf_ref[h] = _dot_tn(vh, ke_f[:, h * GLA_DK:(h + 1) * GLA_DK])
        sb_ref[h] = _dot_tn(vh, ke_b[:, h * GLA_DK:(h + 1) * GLA_DK])


def _context_states(ctx, csh1, csc1, nw1, wkv, wg, afb, bias_fb):
    bsz, n, d = ctx.shape
    full = lambda a: pl.BlockSpec(a.shape, lambda b: (0,) * a.ndim)
    st = jax.ShapeDtypeStruct((bsz, GLA_HEADS, GLA_DV, GLA_DK), F32)
    st_spec = pl.BlockSpec((None, GLA_HEADS, GLA_DV, GLA_DK), lambda b: (b, 0, 0, 0))
    return pl.pallas_call(
        _ctx_kernel,
        out_shape=(st, st),
        grid=(bsz,),
        in_specs=[pl.BlockSpec((None, n, d), lambda b: (b, 0, 0)),
                  full(csh1), full(csc1), full(nw1), full(wkv), full(wg), full(afb), full(bias_fb)],
        out_specs=(st_spec, st_spec),
        compiler_params=_cparams("arbitrary"),
        name="context_states",
    )(ctx, csh1, csc1, nw1, wkv, wg, afb, bias_fb)


def _chunk_scan(st_ref, qd, ke, v, dec, descending):
    nc = qd.shape[0] // CHUNK
    order = range(nc - 1, -1, -1) if descending else range(nc)
    rows = [slice(c * CHUNK, (c + 1) * CHUNK) for c in range(nc)]
    outs = []
    for h in range(GLA_HEADS):
        lanes = slice(h * GLA_DK, (h + 1) * GLA_DK)
        vl = slice(h * GLA_DV, (h + 1) * GLA_DV)
        kv = [_dot_tn(v[rows[c], vl], ke[rows[c], lanes]) for c in range(nc)]
        s = st_ref[h]
        start = [None] * nc
        for c in order:
            start[c] = s
            s = s * dec[c:c + 1, lanes] + kv[c]
        st_ref[h] = s
        outs.append(jnp.concatenate(
            [_dot_nt(qd[rows[c], lanes], start[c].astype(BF16)) for c in range(nc)], axis=0))
    return jnp.concatenate(outs, axis=1)


def _proj_kernel(tpb, x_ref, sh_ref, sc_ref, nw_ref, wm_ref, wg_ref, afb_ref, bias_ref, cw_ref,
                 s0_ref, ab_ref, v_ref, oi_ref, sr_ref, yc_ref, dec_ref, p_scr, st_ref):
    @pl.when(pl.program_id(0) % tpb == 0)
    def _():
        st_ref[...] = s0_ref[...]

    subs = range(x_ref.shape[0] // PROJ_SUB)
    nc = PROJ_SUB // CHUNK
    nblk = wm_ref.shape[1] // PROJ_COLS
    factors = []
    for sub in subs:
        x = x_ref[sub * PROJ_SUB:(sub + 1) * PROJ_SUB, :]
        hn = _rms(x) * nw_ref[...] * (1.0 + sc_ref[...]) + sh_ref[...]
        hb = hn.astype(BF16)
        g = _dot(hb, wg_ref[...])
        z = _dot(g.astype(BF16), afb_ref[...]) + bias_ref[...]

        todo = iter(range(nblk))

        def wide_block():
            j = next(todo, None)
            if j is not None:
                cols = slice(j * PROJ_COLS, (j + 1) * PROJ_COLS)
                p_scr[sub, :, cols] = _dot(hb, wm_ref[:, cols])

        la = _log_sigmoid(z) * (1.0 / GATE_NORMALIZER)
        wide_block()
        b_f = _segment_scan(la[:, :GLA_QK], CHUNK, False, wide_block)
        b_b = _segment_scan(la[:, GLA_QK:], CHUNK, True, wide_block)
        b_f3 = b_f.reshape(nc, CHUNK, GLA_QK)
        tot_f = b_f3[:, CHUNK - 1:CHUNK, :]
        b_b3 = b_b.reshape(nc, CHUNK, GLA_QK)
        tot_b = b_b3[:, 0:1, :]
        fac = []
        for arg in (b_f, -b_f, (tot_f - b_f3).reshape(PROJ_SUB, GLA_QK),
                    b_b, -b_b, (tot_b - b_b3).reshape(PROJ_SUB, GLA_QK)):
            fac.append(jnp.exp(arg))
            wide_block()
        fac += [jnp.exp(tot_f).reshape(nc, GLA_QK), jnp.exp(tot_b).reshape(nc, GLA_QK)]
        factors.append(fac)
        for _ in range(nblk):
            wide_block()
    for sub in subs:
        _proj_rows(sub, factors[sub], cw_ref, ab_ref, v_ref, oi_ref, sr_ref, yc_ref,
                   dec_ref, p_scr.at[sub], st_ref)


def _proj_rows(sub, factors, cw_ref, ab_ref, v_ref, oi_ref, sr_ref, yc_ref, dec_ref, p, st_ref):
    tm = PROJ_SUB
    nc = tm // CHUNK
    rs = slice(sub * tm, (sub + 1) * tm)
    cs = slice(sub * nc, (sub + 1) * nc)
    e_f, ei_f, ee_f, e_b, ei_b, ee_b, dec_f, dec_b = factors

    q = p[:, 0:GLA_QK] * (GLA_DK ** -0.5)
    k = p[:, GLA_QK:2 * GLA_QK]
    vb = p[:, 2 * GLA_QK:2 * GLA_QK + GLA_WIDTH].astype(BF16)
    v_ref[rs, :] = vb
    qd_f = (q * e_f).astype(BF16)
    ki_f = (k * ei_f).astype(BF16)
    ke_f = (k * ee_f).astype(BF16)
    qd_b = (q * e_b).astype(BF16)
    ki_b = (k * ei_b).astype(BF16)
    ke_b = (k * ee_b).astype(BF16)

    ab_ref[rs, :GLA_QK] = qd_b
    ab_ref[rs, GLA_QK:] = ke_b
    dec_ref[cs, :GLA_QK] = dec_f
    dec_ref[cs, GLA_QK:] = dec_b

    inter_f = _chunk_scan(st_ref, qd_f, ke_f, vb, dec_f, False)

    gi = INTRA_GROUP
    ri = lax.broadcasted_iota(jnp.int32, (gi, gi), 0)
    ci = lax.broadcasted_iota(jnp.int32, (gi, gi), 1)
    same = (ri // CHUNK) == (ci // CHUNK)
    lower = same & (ci <= ri)
    upper = same & (ci >= ri)
    for gidx in range(tm // gi):
        rows = slice(gidx * gi, (gidx + 1) * gi)
        orows = slice(sub * tm + gidx * gi, sub * tm + (gidx + 1) * gi)
        for h in range(GLA_HEADS):
            lanes = slice(h * GLA_DK, (h + 1) * GLA_DK)
            s_f = _dot_nt(qd_f[rows, lanes], ki_f[rows, lanes])
            s_b = _dot_nt(qd_b[rows, lanes], ki_b[rows, lanes])
            s = jnp.where(lower, s_f, 0.0) + jnp.where(upper, s_b, 0.0)
            vl = slice(h * GLA_DV, (h + 1) * GLA_DV)
            oi_ref[orows, vl] = _dot(s.astype(BF16), vb[rows, vl]) + inter_f[rows, vl]

    r = p[:, 1024:1536]
    sr_ref[rs, :] = (r * _sigmoid(r)).astype(BF16)

    u = p[:, 2048:2560] * p[:, 2560:3072]
    row = lax.broadcasted_iota(jnp.int32, u.shape, 0) & (GRID_W - 1)
    prev = jnp.where(row >= 1, pltpu.roll(u, 1, 0), 0.0)
    nxt = jnp.where(row < GRID_W - 1, pltpu.roll(u, tm - 1, 0), 0.0)
    cw = cw_ref[...]
    conv = prev * cw[0:1, :] + u * cw[1:2, :] + nxt * cw[2:3, :]
    yc_ref[rs, :] = (p[:, 1536:2048] * conv).astype(BF16)


def _projection(x2, row0, n, sh1, sc1, nw1, wm, wg, afb, bias_fb, conv_w, s0f, seq):
    d = x2.shape[1]
    tm = TOKEN_TILE
    tpb = seq // tm
    t0 = row0 // tm
    full = lambda a: pl.BlockSpec(a.shape, lambda i: (0,) * a.ndim)
    per_batch = pl.BlockSpec((None, 1, d), lambda i: (i // tpb, 0, 0))
    tok = lambda w: pl.BlockSpec((tm, w), lambda i: (i, 0))
    st_spec = pl.BlockSpec((None, GLA_HEADS, GLA_DV, GLA_DK), lambda i: (i // tpb, 0, 0, 0))
    out_shape = (
        jax.ShapeDtypeStruct((n, 2 * GLA_QK), BF16),
        jax.ShapeDtypeStruct((n, GLA_WIDTH), BF16),
        jax.ShapeDtypeStruct((n, GLA_WIDTH), F32),
        jax.ShapeDtypeStruct((n, GLA_WIDTH), BF16),
        jax.ShapeDtypeStruct((n, CONV_WIDTH), BF16),
        jax.ShapeDtypeStruct((n // CHUNK, 2 * GLA_QK), F32),
    )
    out_specs = (tok(2 * GLA_QK), tok(GLA_WIDTH), tok(GLA_WIDTH), tok(GLA_WIDTH),
                 tok(CONV_WIDTH), pl.BlockSpec((tm // CHUNK, 2 * GLA_QK), lambda i: (i, 0)))
    return pl.pallas_call(
        functools.partial(_proj_kernel, tpb),
        out_shape=out_shape,
        grid=(n // tm,),
        in_specs=[pl.BlockSpec((tm, d), lambda i: (i + t0, 0)), per_batch, per_batch, full(nw1),
                  full(wm), full(wg), full(afb), full(bias_fb), full(conv_w), st_spec],
        out_specs=out_specs,
        scratch_shapes=[pltpu.VMEM((tm // PROJ_SUB, PROJ_SUB, wm.shape[1]), F32),
                        pltpu.VMEM((GLA_HEADS, GLA_DV, GLA_DK), F32)],
        compiler_params=_cparams("arbitrary"),
        name="projection",
    )(x2, sh1, sc1, nw1, wm, wg, afb, bias_fb, conv_w, s0f)


def _post_kernel(tpb, x_ref, oi_ref, ab_ref, v_ref, dec_ref, s0_ref, sr_ref, yc_ref, gnw_ref, wo_ref,
                 g1_ref, nw2_ref, sh2_ref, sc2_ref, rwt_ref, rb_ref, tri_ref,
                 h_ref, xw_ref, idx_ref, rank_ref, wt_ref, cnt_ref, run_ref, st_ref):
    @pl.when(pl.program_id(0) == 0)
    def _():
        run_ref[...] = jnp.zeros_like(run_ref)

    @pl.when(pl.program_id(0) % tpb == 0)
    def _():
        st_ref[...] = s0_ref[...]

    tm = x_ref.shape[0]
    inter_b = _chunk_scan(st_ref, ab_ref[:, :GLA_QK], ab_ref[:, GLA_QK:], v_ref[...],
                          dec_ref[:, GLA_QK:], True)
    o = oi_ref[...] + inter_b
    normed = jnp.concatenate(
        [_rms(o[:, h * GLA_DV:(h + 1) * GLA_DV]) for h in range(GLA_HEADS)], axis=1)
    y_gla = (normed * gnw_ref[...] * sr_ref[...].astype(F32)).astype(BF16)
    wo = wo_ref[...]
    y = _dot(y_gla, wo[:GLA_WIDTH, :]) + _dot(yc_ref[...], wo[GLA_WIDTH:, :])
    h = x_ref[...] + g1_ref[...] * y
    h_ref[...] = h
    hn = _rms(h) * nw2_ref[...] * (1.0 + sc2_ref[...]) + sh2_ref[...]
    xw_ref[...] = _pack_halves(hn)

    logits = _dot3(rwt_ref[...], hn, _dot_nt) + rb_ref[...]
    eid = lax.broadcasted_iota(jnp.int32, logits.shape, 0)
    vals, idxs = [], []
    work = logits
    for _ in range(TOP_K):
        m = jnp.max(work, axis=0, keepdims=True)
        i = jnp.min(jnp.where(work == m, eid, N_EXPERTS), axis=0, keepdims=True)
        vals.append(m)
        idxs.append(i)
        work = jnp.where(eid == i, -jnp.inf, work)
    exps = [jnp.exp(v - vals[0]) for v in vals]
    denom = exps[0] + exps[1] + exps[2] + exps[3]
    wts = [e / denom for e in exps]

    onehot = jnp.zeros(logits.shape, F32)
    for i in idxs:
        onehot = onehot + jnp.where(eid == i, 1.0, 0.0)
    before = _dot(onehot.astype(BF16), tri_ref[...]) + run_ref[:, 0:1]
    for kk in range(TOP_K):
        idx_ref[kk:kk + 1, :] = idxs[kk]
        rk = jnp.sum(jnp.where(eid == idxs[kk], before, 0.0), axis=0, keepdims=True)
        rank_ref[kk:kk + 1, :] = rk.astype(jnp.int32)
    run = run_ref[...] + jnp.sum(onehot, axis=1, keepdims=True)
    run_ref[...] = run
    cnt_ref[...] = run.astype(jnp.int32)

    sub = lax.broadcasted_iota(jnp.int32, (GATE_PAD, tm), 0)
    wpad = jnp.zeros((GATE_PAD, tm), F32)
    for kk in range(TOP_K):
        wpad = wpad + jnp.where(sub == kk, wts[kk], 0.0)
    wt_ref[...] = wpad.T


def _post_mixer(x2, row0, o_i, a_b, vb, dec, s0b, sr, yc, gnw, wo, g1, nw2, sh2, sc2, rwt, rb, tri,
                seq):
    n, d = o_i.shape[0], x2.shape[1]
    tm = TOKEN_TILE
    tpb = seq // tm
    t0 = row0 // tm
    rev = lambda i: (i // tpb) * tpb + (tpb - 1 - i % tpb)
    full = lambda a: pl.BlockSpec(a.shape, lambda i: (0,) * a.ndim)
    per_batch = pl.BlockSpec((None, 1, d), lambda i: (i // tpb, 0, 0))
    tok = lambda w: pl.BlockSpec((tm, w), lambda i: (rev(i), 0))
    lane_tok = pl.BlockSpec((TOP_K, tm), lambda i: (0, rev(i)))
    st_spec = pl.BlockSpec((None, GLA_HEADS, GLA_DV, GLA_DK), lambda i: (i // tpb, 0, 0, 0))
    dec_spec = pl.BlockSpec((tm // CHUNK, 2 * GLA_QK), lambda i: (rev(i), 0))
    out_shape = (
        jax.ShapeDtypeStruct((n, d), F32),
        jax.ShapeDtypeStruct((n, d // 2), jnp.uint32),
        jax.ShapeDtypeStruct((TOP_K, n), jnp.int32),
        jax.ShapeDtypeStruct((TOP_K, n), jnp.int32),
        jax.ShapeDtypeStruct((n, GATE_PAD), F32),
        jax.ShapeDtypeStruct((N_EXPERTS, GATE_PAD), jnp.int32),
    )
    out_specs = (tok(d), tok(d // 2), lane_tok, lane_tok, tok(GATE_PAD),
                 pl.BlockSpec((N_EXPERTS, GATE_PAD), lambda i: (0, 0)))
    return pl.pallas_call(
        functools.partial(_post_kernel, tpb),
        out_shape=out_shape,
        grid=(n // tm,),
        in_specs=[pl.BlockSpec((tm, d), lambda i: (rev(i) + t0, 0)), tok(GLA_WIDTH),
                  tok(2 * GLA_QK), tok(GLA_WIDTH), dec_spec, st_spec, tok(GLA_WIDTH),
                  tok(CONV_WIDTH), full(gnw), full(wo), per_batch, full(nw2), per_batch, per_batch,
                  full(rwt), full(rb), full(tri)],
        out_specs=out_specs,
        scratch_shapes=[pltpu.VMEM((N_EXPERTS, GATE_PAD), F32),
                        pltpu.VMEM((GLA_HEADS, GLA_DV, GLA_DK), F32)],
        compiler_params=_cparams("arbitrary"),
        name="post_mixer",
    )(x2, o_i, a_b, vb, dec, s0b, sr, yc, gnw, wo, g1, nw2, sh2, sc2, rwt, rb, tri)


def _plan_kernel(cnt_ref, start_ref, te_ref, first_ref, next_ref, nu_ref):
    n_tiles = te_ref.shape[0]
    shift = EXPERT_TILE.bit_length() - 1

    def clear(t, carry):
        first_ref[t] = 0
        next_ref[t] = -1
        return carry

    lax.fori_loop(0, n_tiles, clear, 0)

    def group(e, tile0):
        nt = lax.shift_right_logical(cnt_ref[e] + (EXPERT_TILE - 1), shift)
        start_ref[e] = tile0 * EXPERT_TILE

        def fill(t, carry):
            te_ref[tile0 + t] = e
            return carry

        lax.fori_loop(0, nt, fill, 0)

        @pl.when(nt > 0)
        def _():
            first_ref[tile0] = 1

        return tile0 + nt

    n_used = lax.fori_loop(0, N_EXPERTS, group, 0)
    nu_ref[0] = n_used
    last = te_ref[n_used - 1]

    def tail(t, carry):
        te_ref[t] = last
        return carry

    lax.fori_loop(n_used, n_tiles, tail, 0)

    def link(k, nxt):
        e = N_EXPERTS - 1 - k
        has = cnt_ref[e] > 0

        @pl.when(has)
        def _():
            next_ref[lax.shift_right_logical(start_ref[e], shift)] = nxt

        return jnp.where(has, e, nxt)

    lax.fori_loop(0, N_EXPERTS, link, -1)


def _plan(counts, n_tiles):
    assert EXPERT_TILE & (EXPERT_TILE - 1) == 0
    smem = pl.BlockSpec(memory_space=pltpu.SMEM)
    vec = lambda m: jax.ShapeDtypeStruct((m,), jnp.int32)
    return pl.pallas_call(
        _plan_kernel,
        out_shape=(vec(N_EXPERTS), vec(n_tiles), vec(n_tiles), vec(n_tiles), vec(1)),
        in_specs=[smem],
        out_specs=(smem, smem, smem, smem, smem),
        name="plan",
    )(counts)


def _slot_kernel(start_ref, idx_ref, rank_ref, slot_ref):
    idx = idx_ref[...]
    base = jnp.zeros(idx.shape, jnp.int32)
    for e in range(N_EXPERTS):
        base = jnp.where(idx == e, start_ref[e], base)
    slot_ref[...] = base + rank_ref[...]


def _slots(group_start, idx_t, rank_t):
    k, n = idx_t.shape
    bn = min(n, 8192)
    blk = pl.BlockSpec((k, bn), lambda i, s: (0, i))
    return pl.pallas_call(
        _slot_kernel,
        out_shape=jax.ShapeDtypeStruct((k, n), jnp.int32),
        grid_spec=pltpu.PrefetchScalarGridSpec(
            num_scalar_prefetch=1, grid=(n // bn,), in_specs=[blk, blk], out_specs=blk),
        compiler_params=_cparams("arbitrary"),
        name="slots",
    )(group_start, idx_t, rank_t)


SC_CORES = 2
SC_SUBCORES = 16
SC_WORKERS = SC_CORES * SC_SUBCORES
SCATTER_CHUNK = 128
GATHER_CHUNK = 64


def _sc_mesh():
    return plsc.VectorSubcoreMesh(core_axis_name="c", subcore_axis_name="s")


def _sc_worker():
    return lax.axis_index("s") * SC_CORES + lax.axis_index("c")


def _sc_scatter_rows(src, idx, n_out):
    n, w = src.shape
    m = idx.shape[0]
    c = SCATTER_CHUNK
    per = m // SC_WORKERS
    nch = per // c
    assert per * SC_WORKERS == m and nch * c == per and n % per == 0

    def body(src_hbm, idx_hbm, out_hbm, idx_v, rows_v):
        wid = _sc_worker()
        src_base = lax.rem(wid * per, n)
        pltpu.sync_copy(idx_hbm.at[wid], idx_v)

        @pl.loop(0, nch)
        def _(j):
            pltpu.sync_copy(src_hbm.at[pl.ds(src_base + j * c, c)], rows_v)
            pltpu.sync_copy(rows_v, out_hbm.at[idx_v.at[j]])

    return pl.kernel(
        body,
        out_type=jax.ShapeDtypeStruct((n_out, w), src.dtype),
        mesh=_sc_mesh(),
        scratch_types=[pltpu.VMEM((nch, c), jnp.int32), pltpu.VMEM((c, w), src.dtype)],
        name="sc_dispatch",
    )(src, idx.reshape(SC_WORKERS, nch, c))


def _sc_gather_rows(table, idx):
    w = table.shape[1]
    m = idx.shape[0]
    c = GATHER_CHUNK
    per = m // SC_WORKERS
    nch = per // c
    assert per * SC_WORKERS == m and nch * c == per and nch % 2 == 0

    def body(table_hbm, idx_hbm, out_hbm, idx_v, rows_v, sem0, sem1):
        wid = _sc_worker()
        base = wid * per
        sems = (sem0, sem1)
        pltpu.sync_copy(idx_hbm.at[wid], idx_v)

        def gather(j, b):
            return pltpu.make_async_copy(table_hbm.at[idx_v.at[j]], rows_v.at[b], sems[b])

        gather(0, 0).start()

        @pl.loop(0, nch, step=2)
        def _(g):
            for b in range(2):
                j = g + b
                gather(j, b).wait()

                @pl.when(j + 1 < nch)
                def _():
                    gather(j + 1, 1 - b).start()

                pltpu.sync_copy(rows_v.at[b], out_hbm.at[pl.ds(base + j * c, c)])

    return pl.kernel(
        body,
        out_type=jax.ShapeDtypeStruct((m, w), table.dtype),
        mesh=_sc_mesh(),
        scratch_types=[pltpu.VMEM((nch, c), jnp.int32), pltpu.VMEM((2, c, w), table.dtype),
                       pltpu.SemaphoreType.DMA, pltpu.SemaphoreType.DMA],
        name="sc_combine_gather",
    )(table, idx.reshape(SC_WORKERS, nch, c))


PERM_BLOCK = 256


def _expert_kernel(te_ref, first_ref, next_ref, nu_ref, x_ref, wgu_hbm, wdn_hbm, perm_ref, bg_ref,
                   bu_ref, bd_ref, y_ref, wgu_f, wdn_f, wg_s, wu_s, wd_s, sem):
    i = pl.program_id(0)
    used = i < nu_ref[0]

    def fetch(e):
        return (pltpu.make_async_copy(wgu_hbm.at[e], wgu_f, sem.at[0]),
                pltpu.make_async_copy(wdn_hbm.at[e], wdn_f, sem.at[1]))

    @pl.when(i == 0)
    def _():
        for cp in fetch(te_ref[0]):
            cp.start()

    @pl.when(used & (first_ref[i] == 1))
    def _():
        for cp in fetch(te_ref[i]):
            cp.wait()
        half = PERM_BLOCK // 2
        perm = perm_ref[...]
        for b in range(wgu_f.shape[1] // PERM_BLOCK):
            blk = wgu_f[:, b * PERM_BLOCK:(b + 1) * PERM_BLOCK].astype(BF16)
            t = _dot(blk, perm)
            wg_s[:, b * half:(b + 1) * half] = t[:, :half].astype(BF16)
            wu_s[:, b * half:(b + 1) * half] = t[:, half:].astype(BF16)
        wd_s[...] = wdn_f[...].astype(BF16)

        @pl.when(next_ref[i] >= 0)
        def _():
            for cp in fetch(next_ref[i]):
                cp.start()

    @pl.when(used)
    def _():
        x = _unpack_halves(x_ref[...]).astype(BF16)
        g = _dot(x, wg_s[...]) + bg_ref[...]
        u = _dot(x, wu_s[...]) + bu_ref[...]
        gate = jnp.minimum(g, SWIGLU_LIMIT)
        up = jnp.clip(u, -SWIGLU_LIMIT, SWIGLU_LIMIT)
        act = (up + 1.0) * (gate * _sigmoid(gate, SWIGLU_ALPHA))
        y = _dot(act.astype(BF16), wd_s[...]) + bd_ref[...]
        y_ref[...] = _pack_halves(y)

    @pl.when(jnp.logical_not(used))
    def _():
        y_ref[...] = jnp.zeros_like(y_ref)


def _experts(tile_expert, tile_first, tile_next, n_used, xs, w_gu, w_dn, bg, bu, bd):
    n_slots, half = xs.shape
    tm = EXPERT_TILE
    n_tiles = n_slots // tm
    dm, f, d = w_gu.shape[1], w_dn.shape[1], w_dn.shape[2]
    ph = PERM_BLOCK // 2
    j = jnp.arange(PERM_BLOCK)
    perm = (jnp.arange(PERM_BLOCK)[None, :] == (j // 2 + (j % 2) * ph)[:, None]).astype(BF16)
    row_in = pl.BlockSpec((tm, half), lambda i, te, tf, tn, nu: (jnp.minimum(i, nu[0] - 1), 0))
    row_out = pl.BlockSpec((tm, half),
                           lambda i, te, tf, tn, nu: (jnp.where(i < nu[0], i, n_tiles - 1), 0))
    bspec = lambda a: pl.BlockSpec((None,) + a.shape[1:], lambda i, te, tf, tn, nu: (te[i], 0, 0))
    return pl.pallas_call(
        _expert_kernel,
        out_shape=jax.ShapeDtypeStruct((n_slots, half), jnp.uint32),
        grid_spec=pltpu.PrefetchScalarGridSpec(
            num_scalar_prefetch=4, grid=(n_tiles,),
            in_specs=[row_in, pl.BlockSpec(memory_space=pl.ANY), pl.BlockSpec(memory_space=pl.ANY),
                      pl.BlockSpec((PERM_BLOCK, PERM_BLOCK), lambda i, te, tf, tn, nu: (0, 0)),
                      bspec(bg), bspec(bu), bspec(bd)],
            out_specs=row_out,
            scratch_shapes=[pltpu.VMEM((dm, 2 * f), F32), pltpu.VMEM((f, d), F32),
                            pltpu.VMEM((dm, f), BF16), pltpu.VMEM((dm, f), BF16),
                            pltpu.VMEM((f, d), BF16), pltpu.SemaphoreType.DMA((2,))]),
        compiler_params=_cparams("arbitrary"),
        name="experts",
    )(tile_expert, tile_first, tile_next, n_used, xs, w_gu, w_dn, perm, bg, bu, bd)


def _combine_kernel(y4_ref, h_ref, wt_ref, g2_ref, fnw_ref, *rest):
    o_ref = rest[-1]
    wt = wt_ref[...]
    acc = jnp.zeros(h_ref.shape, F32)
    for kk in range(TOP_K):
        acc = acc + wt[:, kk:kk + 1] * _unpack_halves(y4_ref[kk])
    h = h_ref[...] + g2_ref[...] * acc
    o_ref[...] = _rms(h) * fnw_ref[...]


def _combine(y4, h, wt, g2, fnw, seq, row0, n_total, prev_out):
    n, d = h.shape
    tt = COMBINE_TILE
    tpb = seq // tt
    t0 = row0 // tt
    in_specs = [pl.BlockSpec((TOP_K, tt, d // 2), lambda i: (0, i, 0)),
                pl.BlockSpec((tt, d), lambda i: (i, 0)),
                pl.BlockSpec((tt, GATE_PAD), lambda i: (i, 0)),
                pl.BlockSpec((None, 1, d), lambda i: (i // tpb, 0, 0)),
                pl.BlockSpec((1, d), lambda i: (0, 0))]
    args = [y4, h, wt, g2, fnw]
    aliases = {}
    if prev_out is not None:
        in_specs.append(pl.BlockSpec(memory_space=pl.ANY))
        args.append(prev_out)
        aliases = {len(args) - 1: 0}
    return pl.pallas_call(
        _combine_kernel,
        out_shape=jax.ShapeDtypeStruct((n_total, d), F32),
        grid=(n // tt,),
        in_specs=in_specs,
        out_specs=pl.BlockSpec((tt, d), lambda i: (i + t0, 0)),
        input_output_aliases=aliases,
        compiler_params=_cparams("arbitrary"),
        name="combine",
    )(*args)


def kernel(x, c, ctx, c_ctx, w_ada, b_ada, norm1_w, w_in, a_up_f, a_bias_f, a_up_b, a_bias_b,
           gla_norm_w, conv_w, w_out, norm2_w, router_w, router_b, w_gu, b_gu, w_dn, b_dn,
           final_norm_w):
    bsz, seq, d = x.shape
    n = bsz * seq
    assert w_ada.shape[0] == 1 and d == D_MODEL
    assert seq % TOKEN_TILE == 0 and seq % GRID_W == 0
    assert ctx.shape[1] & (ctx.shape[1] - 1) == 0

    w_in0 = w_in[0]
    v_end = 2 * GLA_QK + GLA_WIDTH
    r_end = v_end + GLA_WIDTH
    g_end = r_end + 2 * GATE_RANK
    wm = jnp.concatenate([w_in0[:, :r_end], w_in0[:, g_end:]], axis=1).astype(BF16)
    wg_low = jnp.pad(w_in0[:, r_end:g_end], ((0, 0), (0, GATE_PAD - 2 * GATE_RANK))).astype(BF16)
    afb = jnp.zeros((GATE_PAD, 2 * GLA_QK), F32)
    afb = afb.at[:GATE_RANK, :GLA_QK].set(a_up_f[0]).at[GATE_RANK:2 * GATE_RANK, GLA_QK:].set(a_up_b[0])
    afb = afb.astype(BF16)
    bias_fb = jnp.concatenate([a_bias_f[0], a_bias_b[0]])[None]
    wkv = wm[:, GLA_QK:v_end]
    row = lambda a: a.reshape(1, -1)

    mod = _modulation(jnp.concatenate([c, c_ctx[None]], axis=0), w_ada[0], row(b_ada[0]))
    sh1, sc1, g1, sh2, sc2, g2 = [mod[:bsz, i * d:(i + 1) * d].reshape(bsz, 1, d) for i in range(6)]
    csh1, csc1 = mod[bsz:, 0:d], mod[bsz:, d:2 * d]

    s0f, s0b = _context_states(ctx, csh1, csc1, row(norm1_w[0]), wkv, wg_low, afb, bias_fb)

    x2 = x.reshape(n, d)
    tri = jnp.triu(jnp.ones((TOKEN_TILE, TOKEN_TILE), BF16), k=1)
    gnw = jnp.tile(gla_norm_w[0], GLA_HEADS)[None]
    wo = w_out[0].astype(BF16)
    bg_e, bu_e, bd_e = b_gu[0][:, None, 0::2], b_gu[0][:, None, 1::2], b_dn[0][:, None, :]

    n_chunks = N_CHUNKS if bsz % N_CHUNKS == 0 else 1
    cb = bsz // n_chunks
    nc = cb * seq
    n_tiles = nc * TOP_K // EXPERT_TILE + N_EXPERTS
    routed = []
    for ci in range(n_chunks):
        bs = slice(ci * cb, (ci + 1) * cb)
        row0 = ci * nc
        a_b, vb, o_i, sr, yc, dec = _projection(
            x2, row0, nc, sh1[bs], sc1[bs], row(norm1_w[0]), wm, wg_low, afb, bias_fb, conv_w[0],
            s0f[bs], seq)
        h, xw, idx_t, rank_t, wt, cnt = _post_mixer(
            x2, row0, o_i, a_b, vb, dec, s0b[bs], sr, yc, gnw, wo, g1[bs], row(norm2_w[0]), sh2[bs],
            sc2[bs], router_w[0].T, router_b[0][:, None], tri, seq)

        group_start, tile_expert, tile_first, tile_next, n_used = _plan(cnt[:, 0], n_tiles)
        slot_flat = _slots(group_start, idx_t, rank_t).reshape(-1)
        xs = _sc_scatter_rows(xw, slot_flat, n_tiles * EXPERT_TILE)
        routed.append((h, wt, slot_flat, xs, tile_expert, tile_first, tile_next, n_used, bs, row0))

    gathered = []
    for h, wt, slot_flat, xs, tile_expert, tile_first, tile_next, n_used, bs, row0 in routed:
        ys = _experts(tile_expert, tile_first, tile_next, n_used, xs, w_gu[0], w_dn[0],
                      bg_e, bu_e, bd_e)
        y4 = _sc_gather_rows(ys, slot_flat).reshape(TOP_K, nc, d // 2)
        gathered.append((y4, h, wt, bs, row0))

    out = None
    for y4, h, wt, bs, row0 in gathered:
        out = _combine(y4, h, wt, g2[bs], row(final_norm_w), seq, row0, n, out)
    return out.reshape(bsz, seq, d)
```

```python
import functools

import jax
import jax.numpy as jnp
from jax import lax
from jax.experimental import pallas as pl
from jax.experimental.pallas import tpu as pltpu
from jax.experimental.pallas import tpu_sc as plsc

D_MODEL = 1024
GLA_HEADS = 4
GLA_DK = 64
GLA_DV = 128
GLA_QK = GLA_HEADS * GLA_DK
GLA_WIDTH = GLA_HEADS * GLA_DV
CONV_WIDTH = 512
GATE_RANK = 16
GATE_PAD = 128
GATE_NORMALIZER = 16.0
CHUNK = 64
GRID_W = 64
N_EXPERTS = 32
TOP_K = 4
SWIGLU_LIMIT = 7.0
SWIGLU_ALPHA = 1.702
EPS = 1e-6

TOKEN_TILE = 1024
POST_TILE = 1024
PROJ_SUB = 512
PROJ_COLS = 1024
INTRA_GROUP = 256
EXPERT_TILE = 512
COMBINE_TILE = 1024
N_CHUNKS = 2
VMEM_LIMIT = 56 * 1024 * 1024

F32 = jnp.float32
BF16 = jnp.bfloat16
HI_MASK = 0xFFFF0000
LOG2_E = 1.4426950408889634


def _cparams(*sem):
    return pltpu.CompilerParams(dimension_semantics=sem, vmem_limit_bytes=VMEM_LIMIT)


def _rms(x):
    return x * lax.rsqrt(jnp.mean(x * x, axis=-1, keepdims=True) + EPS)


def _sigmoid(x, scale=1.0):
    return 1.0 / (1.0 + jnp.exp2(x * (-scale * LOG2_E)))


def _log_sigmoid(z):
    return jnp.minimum(z, 0.0) - jnp.log(1.0 + jnp.exp(-jnp.abs(z)))


def _dot(a, b):
    return jnp.dot(a, b, preferred_element_type=F32)


def _dot_nt(a, b):
    return lax.dot_general(a, b, (((1,), (1,)), ((), ())), preferred_element_type=F32)


def _dot_tn(a, b):
    return lax.dot_general(a, b, (((0,), (0,)), ((), ())), preferred_element_type=F32)


def _split_bf16(a):
    hi = a.astype(BF16)
    lo = (a - hi.astype(F32)).astype(BF16)
    return hi, lo


def _dot3(a, b, dot):
    a_hi, a_lo = _split_bf16(a)
    b_hi, b_lo = _split_bf16(b)
    return dot(a_hi, b_hi) + (dot(a_hi, b_lo) + dot(a_lo, b_hi))


def _segment_scan(x, seg, reverse, after_step=None):
    n = x.shape[0]
    row = lax.broadcasted_iota(jnp.int32, x.shape, 0) & (seg - 1)
    s = 1
    while s < seg:
        if reverse:
            shifted = pltpu.roll(x, n - s, 0)
            x = x + jnp.where(row < seg - s, shifted, 0.0)
        else:
            shifted = pltpu.roll(x, s, 0)
            x = x + jnp.where(row >= s, shifted, 0.0)
        if after_step is not None:
            after_step()
        s *= 2
    return x


def _pack_halves(x):
    n = x.shape[1] // 2
    bits = pltpu.bitcast(x.astype(BF16).astype(F32), jnp.uint32)
    return (bits[:, :n] & jnp.uint32(HI_MASK)) | (bits[:, n:] >> 16)


def _unpack_halves(w):
    hi = pltpu.bitcast(w & jnp.uint32(HI_MASK), F32)
    lo = pltpu.bitcast(w << 16, F32)
    return jnp.concatenate([hi, lo], axis=1)


def _mod_kernel(c_ref, w_ref, b_ref, o_ref):
    c = c_ref[...]
    s = c * _sigmoid(c)
    o_ref[...] = _dot3(s, w_ref[...], _dot) + b_ref[...]


def _modulation(cc, w_ada, b_ada):
    rows, d = cc.shape
    n = w_ada.shape[1]
    bn = 1536
    return pl.pallas_call(
        _mod_kernel,
        out_shape=jax.ShapeDtypeStruct((rows, n), F32),
        grid=(n // bn,),
        in_specs=[pl.BlockSpec((rows, d), lambda j: (0, 0)),
                  pl.BlockSpec((d, bn), lambda j: (0, j)),
                  pl.BlockSpec((1, bn), lambda j: (0, j))],
        out_specs=pl.BlockSpec((rows, bn), lambda j: (0, j)),
        compiler_params=_cparams("arbitrary"),
        name="modulation",
    )(cc, w_ada, b_ada)


def _ctx_kernel(x_ref, sh_ref, sc_ref, nw_ref, wkv_ref, wg_ref, afb_ref, bias_ref, sf_ref, sb_ref):
    x = x_ref[...]
    n = x.shape[0]
    hn = _rms(x) * nw_ref[...] * (1.0 + sc_ref[...]) + sh_ref[...]
    hb = hn.astype(BF16)
    kv = _dot(hb, wkv_ref[...])
    g = _dot(hb, wg_ref[...])
    z = _dot(g.astype(BF16), afb_ref[...]) + bias_ref[...]
    la = _log_sigmoid(z) * (1.0 / GATE_NORMALIZER)
    p = _segment_scan(la, n, False)
    p_f, p_b = p[:, :GLA_QK], p[:, GLA_QK:]
    w_f = jnp.exp(p_f[n - 1:n, :] - p_f)
    w_b = jnp.exp(p_b - la[:, GLA_QK:])
    k = kv[:, :GLA_QK]
    vb = kv[:, GLA_QK:].astype(BF16)
    ke_f = (k * w_f).astype(BF16)
    ke_b = (k * w_b).astype(BF16)
    for h in range(GLA_HEADS):
        vh = vb[:, h * GLA_DV:(h + 1) * GLA_DV]
        sf_ref[h] = _dot_tn(vh, ke_f[:, h * GLA_DK:(h + 1) * GLA_DK])
        sb_ref[h] = _dot_tn(vh, ke_b[:, h * GLA_DK:(h + 1) * GLA_DK])


def _context_states(ctx, csh1, csc1, nw1, wkv, wg, afb, bias_fb):
    bsz, n, d = ctx.shape
    full = lambda a: pl.BlockSpec(a.shape, lambda b: (0,) * a.ndim)
    st = jax.ShapeDtypeStruct((bsz, GLA_HEADS, GLA_DV, GLA_DK), F32)
    st_spec = pl.BlockSpec((None, GLA_HEADS, GLA_DV, GLA_DK), lambda b: (b, 0, 0, 0))
    return pl.pallas_call(
        _ctx_kernel,
        out_shape=(st, st),
        grid=(bsz,),
        in_specs=[pl.BlockSpec((None, n, d), lambda b: (b, 0, 0)),
                  full(csh1), full(csc1), full(nw1), full(wkv), full(wg), full(afb), full(bias_fb)],
        out_specs=(st_spec, st_spec),
        compiler_params=_cparams("arbitrary"),
        name="context_states",
    )(ctx, csh1, csc1, nw1, wkv, wg, afb, bias_fb)


def _chunk_scan(st_ref, qd, ke, v, dec, descending):
    nc = qd.shape[0] // CHUNK
    order = range(nc - 1, -1, -1) if descending else range(nc)
    rows = [slice(c * CHUNK, (c + 1) * CHUNK) for c in range(nc)]
    outs = []
    for h in range(GLA_HEADS):
        lanes = slice(h * GLA_DK, (h + 1) * GLA_DK)
        vl = slice(h * GLA_DV, (h + 1) * GLA_DV)
        kv = [_dot_tn(v[rows[c], vl], ke[rows[c], lanes]) for c in range(nc)]
        s = st_ref[h]
        start = [None] * nc
        for c in order:
            start[c] = s
            s = s * dec[c:c + 1, lanes] + kv[c]
        st_ref[h] = s
        outs.append(jnp.concatenate(
            [_dot_nt(qd[rows[c], lanes], start[c].astype(BF16)) for c in range(nc)], axis=0))
    return jnp.concatenate(outs, axis=1)


def _proj_kernel(tpb, x_ref, sh_ref, sc_ref, nw_ref, wm_ref, wg_ref, afb_ref, bias_ref, cw_ref,
                 s0_ref, ab_ref, v_ref, oi_ref, sr_ref, yc_ref, dec_ref, p_scr, st_ref):
    @pl.when(pl.program_id(0) % tpb == 0)
    def _():
        st_ref[...] = s0_ref[...]

    subs = range(x_ref.shape[0] // PROJ_SUB)
    nc = PROJ_SUB // CHUNK
    nblk = wm_ref.shape[1] // PROJ_COLS
    factors = []
    for sub in subs:
        x = x_ref[sub * PROJ_SUB:(sub + 1) * PROJ_SUB, :]
        hn = _rms(x) * nw_ref[...] * (1.0 + sc_ref[...]) + sh_ref[...]
        hb = hn.astype(BF16)
        g = _dot(hb, wg_ref[...])
        z = _dot(g.astype(BF16), afb_ref[...]) + bias_ref[...]

        todo = iter(range(nblk))

        def wide_block():
            j = next(todo, None)
            if j is not None:
                cols = slice(j * PROJ_COLS, (j + 1) * PROJ_COLS)
                p_scr[sub, :, cols] = _dot(hb, wm_ref[:, cols])

        la = _log_sigmoid(z) * (1.0 / GATE_NORMALIZER)
        wide_block()
        b_f = _segment_scan(la[:, :GLA_QK], CHUNK, False, wide_block)
        b_b = _segment_scan(la[:, GLA_QK:], CHUNK, True, wide_block)
        b_f3 = b_f.reshape(nc, CHUNK, GLA_QK)
        tot_f = b_f3[:, CHUNK - 1:CHUNK, :]
        b_b3 = b_b.reshape(nc, CHUNK, GLA_QK)
        tot_b = b_b3[:, 0:1, :]
        fac = []
        for arg in (b_f, -b_f, (tot_f - b_f3).reshape(PROJ_SUB, GLA_QK),
                    b_b, -b_b, (tot_b - b_b3).reshape(PROJ_SUB, GLA_QK)):
            fac.append(jnp.exp(arg))
            wide_block()
        fac += [jnp.exp(tot_f).reshape(nc, GLA_QK), jnp.exp(tot_b).reshape(nc, GLA_QK)]
        factors.append(fac)
        for _ in range(nblk):
            wide_block()
    for sub in subs:
        _proj_rows(sub, factors[sub], cw_ref, ab_ref, v_ref, oi_ref, sr_ref, yc_ref,
                   dec_ref, p_scr.at[sub], st_ref)


def _proj_rows(sub, factors, cw_ref, ab_ref, v_ref, oi_ref, sr_ref, yc_ref, dec_ref, p, st_ref):
    tm = PROJ_SUB
    nc = tm // CHUNK
    rs = slice(sub * tm, (sub + 1) * tm)
    cs = slice(sub * nc, (sub + 1) * nc)
    e_f, ei_f, ee_f, e_b, ei_b, ee_b, dec_f, dec_b = factors

    q = p[:, 0:GLA_QK] * (GLA_DK ** -0.5)
    k = p[:, GLA_QK:2 * GLA_QK]
    vb = p[:, 2 * GLA_QK:2 * GLA_QK + GLA_WIDTH].astype(BF16)
    v_ref[rs, :] = vb
    qd_f = (q * e_f).astype(BF16)
    ki_f = (k * ei_f).astype(BF16)
    ke_f = (k * ee_f).astype(BF16)
    qd_b = (q * e_b).astype(BF16)
    ki_b = (k * ei_b).astype(BF16)
    ke_b = (k * ee_b).astype(BF16)

    ab_ref[rs, :GLA_QK] = qd_b
    ab_ref[rs, GLA_QK:] = ke_b
    dec_ref[cs, :GLA_QK] = dec_f
    dec_ref[cs, GLA_QK:] = dec_b

    inter_f = _chunk_scan(st_ref, qd_f, ke_f, vb, dec_f, False)

    gi = INTRA_GROUP
    ri = lax.broadcasted_iota(jnp.int32, (gi, gi), 0)
    ci = lax.broadcasted_iota(jnp.int32, (gi, gi), 1)
    same = (ri // CHUNK) == (ci // CHUNK)
    lower = same & (ci <= ri)
    upper = same & (ci >= ri)
    for gidx in range(tm // gi):
        rows = slice(gidx * gi, (gidx + 1) * gi)
        orows = slice(sub * tm + gidx * gi, sub * tm + (gidx + 1) * gi)
        for h in range(GLA_HEADS):
            lanes = slice(h * GLA_DK, (h + 1) * GLA_DK)
            s_f = _dot_nt(qd_f[rows, lanes], ki_f[rows, lanes])
            s_b = _dot_nt(qd_b[rows, lanes], ki_b[rows, lanes])
            s = jnp.where(lower, s_f, 0.0) + jnp.where(upper, s_b, 0.0)
            vl = slice(h * GLA_DV, (h + 1) * GLA_DV)
            oi_ref[orows, vl] = _dot(s.astype(BF16), vb[rows, vl]) + inter_f[rows, vl]

    r = p[:, 1024:1536]
    sr_ref[rs, :] = (r * _sigmoid(r)).astype(BF16)

    u = p[:, 2048:2560] * p[:, 2560:3072]
    row = lax.broadcasted_iota(jnp.int32, u.shape, 0) & (GRID_W - 1)
    prev = jnp.where(row >= 1, pltpu.roll(u, 1, 0), 0.0)
    nxt = jnp.where(row < GRID_W - 1, pltpu.roll(u, tm - 1, 0), 0.0)
    cw = cw_ref[...]
    conv = prev * cw[0:1, :] + u * cw[1:2, :] + nxt * cw[2:3, :]
    yc_ref[rs, :] = (p[:, 1536:2048] * conv).astype(BF16)


def _projection(x2, row0, n, sh1, sc1, nw1, wm, wg, afb, bias_fb, conv_w, s0f, seq):
    d = x2.shape[1]
    tm = TOKEN_TILE
    tpb = seq // tm
    t0 = row0 // tm
    full = lambda a: pl.BlockSpec(a.shape, lambda i: (0,) * a.ndim)
    per_batch = pl.BlockSpec((None, 1, d), lambda i: (i // tpb, 0, 0))
    tok = lambda w: pl.BlockSpec((tm, w), lambda i: (i, 0))
    st_spec = pl.BlockSpec((None, GLA_HEADS, GLA_DV, GLA_DK), lambda i: (i // tpb, 0, 0, 0))
    out_shape = (
        jax.ShapeDtypeStruct((n, 2 * GLA_QK), BF16),
        jax.ShapeDtypeStruct((n, GLA_WIDTH), BF16),
        jax.ShapeDtypeStruct((n, GLA_WIDTH), F32),
        jax.ShapeDtypeStruct((n, GLA_WIDTH), BF16),
        jax.ShapeDtypeStruct((n, CONV_WIDTH), BF16),
        jax.ShapeDtypeStruct((n // CHUNK, 2 * GLA_QK), F32),
    )
    out_specs = (tok(2 * GLA_QK), tok(GLA_WIDTH), tok(GLA_WIDTH), tok(GLA_WIDTH),
                 tok(CONV_WIDTH), pl.BlockSpec((tm // CHUNK, 2 * GLA_QK), lambda i: (i, 0)))
    return pl.pallas_call(
        functools.partial(_proj_kernel, tpb),
        out_shape=out_shape,
        grid=(n // tm,),
        in_specs=[pl.BlockSpec((tm, d), lambda i: (i + t0, 0)), per_batch, per_batch, full(nw1),
                  full(wm), full(wg), full(afb), full(bias_fb), full(conv_w), st_spec],
        out_specs=out_specs,
        scratch_shapes=[pltpu.VMEM((tm // PROJ_SUB, PROJ_SUB, wm.shape[1]), F32),
                        pltpu.VMEM((GLA_HEADS, GLA_DV, GLA_DK), F32)],
        compiler_params=_cparams("arbitrary"),
        name="projection",
    )(x2, sh1, sc1, nw1, wm, wg, afb, bias_fb, conv_w, s0f)


def _post_kernel(tpb, x_ref, oi_ref, ab_ref, v_ref, dec_ref, s0_ref, sr_ref, yc_ref, gnw_ref, wo_ref,
                 g1_ref, nw2_ref, sh2_ref, sc2_ref, rwt_ref, rb_ref, tri_ref,
                 h_ref, xw_ref, idx_ref, rank_ref, wt_ref, cnt_ref, run_ref, st_ref):
    @pl.when(pl.program_id(0) == 0)
    def _():
        run_ref[...] = jnp.zeros_like(run_ref)

    @pl.when(pl.program_id(0) % tpb == 0)
    def _():
        st_ref[...] = s0_ref[...]

    tm = x_ref.shape[0]
    inter_b = _chunk_scan(st_ref, ab_ref[:, :GLA_QK], ab_ref[:, GLA_QK:], v_ref[...],
                          dec_ref[:, GLA_QK:], True)
    o = oi_ref[...] + inter_b
    normed = jnp.concatenate(
        [_rms(o[:, h * GLA_DV:(h + 1) * GLA_DV]) for h in range(GLA_HEADS)], axis=1)
    y_gla = (normed * gnw_ref[...] * sr_ref[...].astype(F32)).astype(BF16)
    wo = wo_ref[...]
    y = _dot(y_gla, wo[:GLA_WIDTH, :]) + _dot(yc_ref[...], wo[GLA_WIDTH:, :])
    h = x_ref[...] + g1_ref[...] * y
    h_ref[...] = h
    hn = _rms(h) * nw2_ref[...] * (1.0 + sc2_ref[...]) + sh2_ref[...]
    xw_ref[...] = _pack_halves(hn)

    logits = _dot3(rwt_ref[...], hn, _dot_nt) + rb_ref[...]
    eid = lax.broadcasted_iota(jnp.int32, logits.shape, 0)
    vals, idxs = [], []
    work = logits
    for _ in range(TOP_K):
        m = jnp.max(work, axis=0, keepdims=True)
        i = jnp.min(jnp.where(work == m, eid, N_EXPERTS), axis=0, keepdims=True)
        vals.append(m)
        idxs.append(i)
        work = jnp.where(eid == i, -jnp.inf, work)
    exps = [jnp.exp(v - vals[0]) for v in vals]
    denom = exps[0] + exps[1] + exps[2] + exps[3]
    wts = [e / denom for e in exps]

    onehot = jnp.zeros(logits.shape, F32)
    for i in idxs:
        onehot = onehot + jnp.where(eid == i, 1.0, 0.0)
    before = _dot(onehot.astype(BF16), tri_ref[...]) + run_ref[:, 0:1]
    for kk in range(TOP_K):
        idx_ref[kk:kk + 1, :] = idxs[kk]
        rk = jnp.sum(jnp.where(eid == idxs[kk], before, 0.0), axis=0, keepdims=True)
        rank_ref[kk:kk + 1, :] = rk.astype(jnp.int32)
    run = run_ref[...] + jnp.sum(onehot, axis=1, keepdims=True)
    run_ref[...] = run
    cnt_ref[...] = run.astype(jnp.int32)

    sub = lax.broadcasted_iota(jnp.int32, (GATE_PAD, tm), 0)
    wpad = jnp.zeros((GATE_PAD, tm), F32)
    for kk in range(TOP_K):
        wpad = wpad + jnp.where(sub == kk, wts[kk], 0.0)
    wt_ref[...] = wpad.T


def _post_mixer(x2, row0, o_i, a_b, vb, dec, s0b, sr, yc, gnw, wo, g1, nw2, sh2, sc2, rwt, rb, tri,
                seq):
    n, d = o_i.shape[0], x2.shape[1]
    tm = POST_TILE
    tpb = seq // tm
    t0 = row0 // tm
    rev =lambda i: (i // tpb) * tpb + (tpb - 1 - i % tpb)
    full = lambda a: pl.BlockSpec(a.shape, lambda i: (0,) * a.ndim)
    per_batch = pl.BlockSpec((None, 1, d), lambda i: (i // tpb, 0, 0))
    tok = lambda w: pl.BlockSpec((tm, w), lambda i: (rev(i), 0))
    lane_tok = pl.BlockSpec((TOP_K, tm), lambda i: (0, rev(i)))
    st_spec = pl.BlockSpec((None, GLA_HEADS, GLA_DV, GLA_DK), lambda i: (i // tpb, 0, 0, 0))
    dec_spec = pl.BlockSpec((tm // CHUNK, 2 * GLA_QK), lambda i: (rev(i), 0))
    out_shape = (
        jax.ShapeDtypeStruct((n, d), F32),
        jax.ShapeDtypeStruct((n, d // 2), jnp.uint32),
        jax.ShapeDtypeStruct((TOP_K, n), jnp.int32),
        jax.ShapeDtypeStruct((TOP_K, n), jnp.int32),
        jax.ShapeDtypeStruct((n, GATE_PAD), F32),
        jax.ShapeDtypeStruct((N_EXPERTS, GATE_PAD), jnp.int32),
    )
    out_specs = (tok(d), tok(d // 2), lane_tok, lane_tok, tok(GATE_PAD),
                 pl.BlockSpec((N_EXPERTS, GATE_PAD), lambda i: (0, 0)))
    return pl.pallas_call(
        functools.partial(_post_kernel, tpb),
        out_shape=out_shape,
        grid=(n // tm,),
        in_specs=[pl.BlockSpec((tm, d), lambda i: (rev(i) + t0, 0)), tok(GLA_WIDTH),
                  tok(2 * GLA_QK), tok(GLA_WIDTH), dec_spec, st_spec, tok(GLA_WIDTH),
                  tok(CONV_WIDTH), full(gnw), full(wo), per_batch, full(nw2), per_batch, per_batch,
                  full(rwt), full(rb), full(tri)],
        out_specs=out_specs,
        scratch_shapes=[pltpu.VMEM((N_EXPERTS, GATE_PAD), F32),
                        pltpu.VMEM((GLA_HEADS, GLA_DV, GLA_DK), F32)],
        compiler_params=_cparams("arbitrary"),
        name="post_mixer",
    )(x2, o_i, a_b, vb, dec, s0b, sr, yc, gnw, wo, g1, nw2, sh2, sc2, rwt, rb, tri)


def _plan_kernel(cnt_ref, start_ref, te_ref, first_ref, next_ref, nu_ref):
    n_tiles = te_ref.shape[0]
    shift = EXPERT_TILE.bit_length() - 1

    def clear(t, carry):
        first_ref[t] = 0
        next_ref[t] = -1
        return carry

    lax.fori_loop(0, n_tiles, clear, 0)

    def group(e, tile0):
        nt = lax.shift_right_logical(cnt_ref[e] + (EXPERT_TILE - 1), shift)
        start_ref[e] = tile0 * EXPERT_TILE

        def fill(t, carry):
            te_ref[tile0 + t] = e
            return carry

        lax.fori_loop(0, nt, fill, 0)

        @pl.when(nt > 0)
        def _():
            first_ref[tile0] = 1

        return tile0 + nt

    n_used = lax.fori_loop(0, N_EXPERTS, group, 0)
    nu_ref[0] = n_used
    last = te_ref[n_used - 1]

    def tail(t, carry):
        te_ref[t] = last
        return carry

    lax.fori_loop(n_used, n_tiles, tail, 0)

    def link(k, nxt):
        e = N_EXPERTS - 1 - k
        has = cnt_ref[e] > 0

        @pl.when(has)
        def _():
            next_ref[lax.shift_right_logical(start_ref[e], shift)] = nxt

        return jnp.where(has, e, nxt)

    lax.fori_loop(0, N_EXPERTS, link, -1)


def _plan(counts, n_tiles):
    assert EXPERT_TILE & (EXPERT_TILE - 1) == 0
    smem = pl.BlockSpec(memory_space=pltpu.SMEM)
    vec = lambda m: jax.ShapeDtypeStruct((m,), jnp.int32)
    return pl.pallas_call(
        _plan_kernel,
        out_shape=(vec(N_EXPERTS), vec(n_tiles), vec(n_tiles), vec(n_tiles), vec(1)),
        in_specs=[smem],
        out_specs=(smem, smem, smem, smem, smem),
        name="plan",
    )(counts)


def _slot_kernel(start_ref, idx_ref, rank_ref, slot_ref):
    idx = idx_ref[...]
    base = jnp.zeros(idx.shape, jnp.int32)
    for e in range(N_EXPERTS):
        base = jnp.where(idx == e, start_ref[e], base)
    slot_ref[...] = base + rank_ref[...]


def _slots(group_start, idx_t, rank_t):
    k, n = idx_t.shape
    bn = min(n, 8192)
    blk = pl.BlockSpec((k, bn), lambda i, s: (0, i))
    return pl.pallas_call(
        _slot_kernel,
        out_shape=jax.ShapeDtypeStruct((k, n), jnp.int32),
        grid_spec=pltpu.PrefetchScalarGridSpec(
            num_scalar_prefetch=1, grid=(n // bn,), in_specs=[blk, blk], out_specs=blk),
        compiler_params=_cparams("arbitrary"),
        name="slots",
    )(group_start, idx_t, rank_t)


SC_CORES = 2
SC_SUBCORES = 16
SC_WORKERS = SC_CORES * SC_SUBCORES
SCATTER_CHUNK = 128
GATHER_CHUNK = 64


def _sc_mesh():
    return plsc.VectorSubcoreMesh(core_axis_name="c", subcore_axis_name="s")


def _sc_worker():
    return lax.axis_index("s") * SC_CORES + lax.axis_index("c")


def _sc_scatter_rows(src, idx, n_out):
    n, w = src.shape
    m = idx.shape[0]
    c = SCATTER_CHUNK
    per = m // SC_WORKERS
    nch = per // c
    assert per * SC_WORKERS == m and nch * c == per and n % per == 0

    def body(src_hbm, idx_hbm, out_hbm, idx_v, rows_v):
        wid = _sc_worker()
        src_base = lax.rem(wid * per, n)
        pltpu.sync_copy(idx_hbm.at[wid], idx_v)

        @pl.loop(0, nch)
        def _(j):
            pltpu.sync_copy(src_hbm.at[pl.ds(src_base + j * c, c)], rows_v)
            pltpu.sync_copy(rows_v, out_hbm.at[idx_v.at[j]])

    return pl.kernel(
        body,
        out_type=jax.ShapeDtypeStruct((n_out, w), src.dtype),
        mesh=_sc_mesh(),
        scratch_types=[pltpu.VMEM((nch, c), jnp.int32), pltpu.VMEM((c, w), src.dtype)],
        name="sc_dispatch",
    )(src, idx.reshape(SC_WORKERS, nch, c))


def _sc_gather_rows(table, idx):
    w = table.shape[1]
    m = idx.shape[0]
    c = GATHER_CHUNK
    per = m // SC_WORKERS
    nch = per // c
    assert per * SC_WORKERS == m and nch * c == per and nch % 2 == 0

    def body(table_hbm, idx_hbm, out_hbm, idx_v, rows_v, sem0, sem1):
        wid = _sc_worker()
        base = wid * per
        sems = (sem0, sem1)
        pltpu.sync_copy(idx_hbm.at[wid], idx_v)

        def gather(j, b):
            return pltpu.make_async_copy(table_hbm.at[idx_v.at[j]], rows_v.at[b], sems[b])

        gather(0, 0).start()

        @pl.loop(0, nch, step=2)
        def _(g):
            for b in range(2):
                j = g + b
                gather(j, b).wait()

                @pl.when(j + 1 < nch)
                def _():
                    gather(j + 1, 1 - b).start()

                pltpu.sync_copy(rows_v.at[b], out_hbm.at[pl.ds(base + j * c, c)])

    return pl.kernel(
        body,
        out_type=jax.ShapeDtypeStruct((m, w), table.dtype),
        mesh=_sc_mesh(),
        scratch_types=[pltpu.VMEM((nch, c), jnp.int32), pltpu.VMEM((2, c, w), table.dtype),
                       pltpu.SemaphoreType.DMA, pltpu.SemaphoreType.DMA],
        name="sc_combine_gather",
    )(table, idx.reshape(SC_WORKERS, nch, c))


PERM_BLOCK = 256


def _expert_kernel(te_ref, first_ref, next_ref, nu_ref, x_ref, wgu_hbm, wdn_hbm, perm_ref, bg_ref,
                   bu_ref, bd_ref, y_ref, wgu_f, wdn_f, wg_s, wu_s, wd_s, sem):
    i = pl.program_id(0)
    used = i < nu_ref[0]

    def fetch(e):
        return (pltpu.make_async_copy(wgu_hbm.at[e], wgu_f, sem.at[0]),
                pltpu.make_async_copy(wdn_hbm.at[e], wdn_f, sem.at[1]))

    @pl.when(i == 0)
    def _():
        for cp in fetch(te_ref[0]):
            cp.start()

    @pl.when(used & (first_ref[i] == 1))
    def _():
        for cp in fetch(te_ref[i]):
            cp.wait()
        half = PERM_BLOCK // 2
        perm = perm_ref[...]
        for b in range(wgu_f.shape[1] // PERM_BLOCK):
            blk = wgu_f[:, b * PERM_BLOCK:(b + 1) * PERM_BLOCK].astype(BF16)
            t = _dot(blk, perm)
            wg_s[:, b * half:(b + 1) * half] = t[:, :half].astype(BF16)
            wu_s[:, b * half:(b + 1) * half] = t[:, half:].astype(BF16)
        wd_s[...] = wdn_f[...].astype(BF16)

        @pl.when(next_ref[i] >= 0)
        def _():
            for cp in fetch(next_ref[i]):
                cp.start()

    @pl.when(used)
    def _():
        x = _unpack_halves(x_ref[...]).astype(BF16)
        g = _dot(x, wg_s[...]) + bg_ref[...]
        u = _dot(x, wu_s[...]) + bu_ref[...]
        gate = jnp.minimum(g, SWIGLU_LIMIT)
        up = jnp.clip(u, -SWIGLU_LIMIT, SWIGLU_LIMIT)
        act = (up + 1.0) * (gate * _sigmoid(gate, SWIGLU_ALPHA))
        y = _dot(act.astype(BF16), wd_s[...]) + bd_ref[...]
        y_ref[...] = _pack_halves(y)

    @pl.when(jnp.logical_not(used))
    def _():
        y_ref[...] = jnp.zeros_like(y_ref)


def _experts(tile_expert, tile_first, tile_next, n_used, xs, w_gu, w_dn, bg, bu, bd):
    n_slots, half = xs.shape
    tm = EXPERT_TILE
    n_tiles = n_slots // tm
    dm, f, d = w_gu.shape[1], w_dn.shape[1], w_dn.shape[2]
    ph = PERM_BLOCK // 2
    j = jnp.arange(PERM_BLOCK)
    perm = (jnp.arange(PERM_BLOCK)[None, :] == (j // 2 + (j % 2) * ph)[:, None]).astype(BF16)
    row_in = pl.BlockSpec((tm, half), lambda i, te, tf, tn, nu: (jnp.minimum(i, nu[0] - 1), 0))
    row_out = pl.BlockSpec((tm, half),
                           lambda i, te, tf, tn, nu: (jnp.where(i < nu[0], i, n_tiles - 1), 0))
    bspec = lambda a: pl.BlockSpec((None,) + a.shape[1:], lambda i, te, tf, tn, nu: (te[i], 0, 0))
    return pl.pallas_call(
        _expert_kernel,
        out_shape=jax.ShapeDtypeStruct((n_slots, half), jnp.uint32),
        grid_spec=pltpu.PrefetchScalarGridSpec(
            num_scalar_prefetch=4, grid=(n_tiles,),
            in_specs=[row_in, pl.BlockSpec(memory_space=pl.ANY), pl.BlockSpec(memory_space=pl.ANY),
                      pl.BlockSpec((PERM_BLOCK, PERM_BLOCK), lambda i, te, tf, tn, nu: (0, 0)),
                      bspec(bg), bspec(bu), bspec(bd)],
            out_specs=row_out,
            scratch_shapes=[pltpu.VMEM((dm, 2 * f), F32), pltpu.VMEM((f, d), F32),
                            pltpu.VMEM((dm, f), BF16), pltpu.VMEM((dm, f), BF16),
                            pltpu.VMEM((f, d), BF16), pltpu.SemaphoreType.DMA((2,))]),
        compiler_params=_cparams("arbitrary"),
        name="experts",
    )(tile_expert, tile_first, tile_next, n_used, xs, w_gu, w_dn, perm, bg, bu, bd)


def _combine_kernel(y4_ref, h_ref, wt_ref, g2_ref, fnw_ref, *rest):
    o_ref = rest[-1]
    wt = wt_ref[...]
    acc = jnp.zeros(h_ref.shape, F32)
    for kk in range(TOP_K):
        acc = acc + wt[:, kk:kk + 1] * _unpack_halves(y4_ref[kk])
    h = h_ref[...] + g2_ref[...] * acc
    o_ref[...] = _rms(h) * fnw_ref[...]


def _combine(y4, h, wt, g2, fnw, seq, row0, n_total, prev_out):
    n, d = h.shape
    tt = COMBINE_TILE
    tpb = seq // tt
    t0 = row0 // tt
    in_specs = [pl.BlockSpec((TOP_K, tt, d // 2), lambda i: (0, i, 0)),
                pl.BlockSpec((tt, d), lambda i: (i, 0)),
                pl.BlockSpec((tt, GATE_PAD), lambda i: (i, 0)),
                pl.BlockSpec((None, 1, d), lambda i: (i // tpb, 0, 0)),
                pl.BlockSpec((1, d), lambda i: (0, 0))]
    args = [y4, h, wt, g2, fnw]
    aliases = {}
    if prev_out is not None:
        in_specs.append(pl.BlockSpec(memory_space=pl.ANY))
        args.append(prev_out)
        aliases = {len(args) - 1: 0}
    return pl.pallas_call(
        _combine_kernel,
        out_shape=jax.ShapeDtypeStruct((n_total, d), F32),
        grid=(n // tt,),
        in_specs=in_specs,
        out_specs=pl.BlockSpec((tt, d), lambda i: (i + t0, 0)),
        input_output_aliases=aliases,
        compiler_params=_cparams("arbitrary"),
        name="combine",
    )(*args)


def kernel(x, c, ctx, c_ctx, w_ada, b_ada, norm1_w, w_in, a_up_f, a_bias_f, a_up_b, a_bias_b,
           gla_norm_w, conv_w, w_out, norm2_w, router_w, router_b, w_gu, b_gu, w_dn, b_dn,
           final_norm_w):
    bsz, seq, d = x.shape
    n = bsz * seq
    assert w_ada.shape[0] == 1 and d == D_MODEL
    assert seq % TOKEN_TILE == 0 and seq % POST_TILE == 0 and seq % GRID_W == 0
    assert ctx.shape[1] & (ctx.shape[1] - 1) == 0

    w_in0 = w_in[0]
    v_end = 2 * GLA_QK + GLA_WIDTH
    r_end = v_end + GLA_WIDTH
    g_end = r_end + 2 * GATE_RANK
    wm = jnp.concatenate([w_in0[:, :r_end], w_in0[:, g_end:]], axis=1).astype(BF16)
    wg_low = jnp.pad(w_in0[:, r_end:g_end], ((0, 0), (0, GATE_PAD - 2 * GATE_RANK))).astype(BF16)
    afb = jnp.zeros((GATE_PAD, 2 * GLA_QK), F32)
    afb = afb.at[:GATE_RANK, :GLA_QK].set(a_up_f[0]).at[GATE_RANK:2 * GATE_RANK, GLA_QK:].set(a_up_b[0])
    afb = afb.astype(BF16)
    bias_fb = jnp.concatenate([a_bias_f[0], a_bias_b[0]])[None]
    wkv = wm[:, GLA_QK:v_end]
    row = lambda a: a.reshape(1, -1)

    mod = _modulation(jnp.concatenate([c, c_ctx[None]], axis=0), w_ada[0], row(b_ada[0]))
    sh1, sc1, g1, sh2, sc2, g2 = [mod[:bsz, i * d:(i + 1) * d].reshape(bsz, 1, d) for i in range(6)]
    csh1, csc1 = mod[bsz:, 0:d], mod[bsz:, d:2 * d]

    s0f, s0b = _context_states(ctx, csh1, csc1, row(norm1_w[0]), wkv, wg_low, afb, bias_fb)

    x2 = x.reshape(n, d)
    tri = jnp.triu(jnp.ones((POST_TILE, POST_TILE), BF16), k=1)
    gnw = jnp.tile(gla_norm_w[0], GLA_HEADS)[None]
    wo = w_out[0].astype(BF16)
    bg_e, bu_e, bd_e = b_gu[0][:, None, 0::2], b_gu[0][:, None, 1::2], b_dn[0][:, None, :]

    n_chunks = N_CHUNKS if bsz % N_CHUNKS == 0 else 1
    cb = bsz // n_chunks
    nc = cb * seq
    n_tiles = nc * TOP_K // EXPERT_TILE + N_EXPERTS
    routed = []
    for ci in range(n_chunks):
        bs = slice(ci * cb, (ci + 1) * cb)
        row0 = ci * nc
        a_b, vb, o_i, sr, yc, dec = _projection(
            x2, row0, nc, sh1[bs], sc1[bs], row(norm1_w[0]), wm, wg_low, afb, bias_fb, conv_w[0],
            s0f[bs], seq)
        h, xw, idx_t, rank_t, wt, cnt = _post_mixer(
            x2, row0, o_i, a_b, vb, dec, s0b[bs], sr, yc, gnw, wo, g1[bs], row(norm2_w[0]), sh2[bs],
            sc2[bs], router_w[0].T, router_b[0][:, None], tri, seq)

        group_start, tile_expert, tile_first, tile_next, n_used = _plan(cnt[:, 0], n_tiles)
        slot_flat = _slots(group_start, idx_t, rank_t).reshape(-1)
        xs = _sc_scatter_rows(xw, slot_flat, n_tiles * EXPERT_TILE)
        routed.append((h, wt, slot_flat, xs, tile_expert, tile_first, tile_next, n_used, bs, row0))

    gathered = []
    for h, wt, slot_flat, xs, tile_expert, tile_first, tile_next, n_used, bs, row0 in routed:
        ys = _experts(tile_expert, tile_first, tile_next, n_used, xs, w_gu[0], w_dn[0],
                      bg_e, bu_e, bd_e)
        y4 = _sc_gather_rows(ys, slot_flat).reshape(TOP_K, nc, d // 2)
        gathered.append((y4, h, wt, bs, row0))

    out = None
    for y4, h, wt, bs, row0 in gathered:
        out = _combine(y4, h, wt, g2[bs], row(final_norm_w), seq, row0, n, out)
    return out.reshape(bsz, seq, d)
```

```python
import functools

import jax
import jax.numpy as jnp
from jax import lax
from jax.experimental import pallas as pl
from jax.experimental.pallas import tpu as pltpu
from jax.experimental.pallas import tpu_sc as plsc

D_MODEL = 1024
GLA_HEADS = 4
GLA_DK = 64
GLA_DV = 128
GLA_QK = GLA_HEADS * GLA_DK
GLA_WIDTH = GLA_HEADS * GLA_DV
CONV_WIDTH = 512
GATE_RANK = 16
GATE_PAD = 128
GATE_NORMALIZER = 16.0
CHUNK = 64
GRID_W = 64
N_EXPERTS = 32
TOP_K = 4
SWIGLU_LIMIT = 7.0
SWIGLU_ALPHA = 1.702
EPS = 1e-6

TOKEN_TILE = 1024
POST_TILE = 1024
PROJ_SUB = 1024
PROJ_COLS = 1024
INTRA_GROUP = 256
EXPERT_TILE = 512
COMBINE_TILE = 1024
VMEM_LIMIT = 56 * 1024 * 1024

F32 = jnp.float32
BF16 = jnp.bfloat16
HI_MASK = 0xFFFF0000
LOG2_E = 1.4426950408889634


def _cparams(*sem):
    return pltpu.CompilerParams(dimension_semantics=sem, vmem_limit_bytes=VMEM_LIMIT)


def _rms(x):
    return x * lax.rsqrt(jnp.mean(x * x, axis=-1, keepdims=True) + EPS)


def _sigmoid(x, scale=1.0):
    return 1.0 / (1.0 + jnp.exp2(x * (-scale * LOG2_E)))


def _log_sigmoid(z):
    return jnp.minimum(z, 0.0) - jnp.log(1.0 + jnp.exp(-jnp.abs(z)))


def _dot(a, b):
    return jnp.dot(a, b, preferred_element_type=F32)


def _dot_nt(a, b):
    return lax.dot_general(a, b, (((1,), (1,)), ((), ())), preferred_element_type=F32)


def _dot_tn(a, b):
    return lax.dot_general(a, b, (((0,), (0,)), ((), ())), preferred_element_type=F32)


def _split_bf16(a):
    hi = a.astype(BF16)
    lo = (a - hi.astype(F32)).astype(BF16)
    return hi, lo


def _dot3(a, b, dot):
    a_hi, a_lo = _split_bf16(a)
    b_hi, b_lo = _split_bf16(b)
    return dot(a_hi, b_hi) + (dot(a_hi, b_lo) + dot(a_lo, b_hi))


def _segment_scan(x, seg, reverse, after_step=None):
    n = x.shape[0]
    row = lax.broadcasted_iota(jnp.int32, x.shape, 0) & (seg - 1)
    s = 1
    while s < seg:
        if reverse:
            shifted = pltpu.roll(x, n - s, 0)
            x = x + jnp.where(row < seg - s, shifted, 0.0)
        else:
            shifted = pltpu.roll(x, s, 0)
            x = x + jnp.where(row >= s, shifted, 0.0)
        if after_step is not None:
            after_step()
        s *= 2
    return x


def _pack_halves(x):
    n = x.shape[1] // 2
    bits = pltpu.bitcast(x.astype(BF16).astype(F32), jnp.uint32)
    return (bits[:, :n] & jnp.uint32(HI_MASK)) | (bits[:, n:] >> 16)


def _unpack_halves(w):
    hi = pltpu.bitcast(w & jnp.uint32(HI_MASK), F32)
    lo = pltpu.bitcast(w << 16, F32)
    return jnp.concatenate([hi, lo], axis=1)


def _mod_kernel(c_ref, w_ref, b_ref, o_ref):
    c = c_ref[...]
    s = c * _sigmoid(c)
    o_ref[...] = _dot3(s, w_ref[...], _dot) + b_ref[...]


def _modulation(cc, w_ada, b_ada):
    rows, d = cc.shape
    n = w_ada.shape[1]
    bn = 1536
    return pl.pallas_call(
        _mod_kernel,
        out_shape=jax.ShapeDtypeStruct((rows, n), F32),
        grid=(n // bn,),
        in_specs=[pl.BlockSpec((rows, d), lambda j: (0, 0)),
                  pl.BlockSpec((d, bn), lambda j: (0, j)),
                  pl.BlockSpec((1, bn), lambda j: (0, j))],
        out_specs=pl.BlockSpec((rows, bn), lambda j: (0, j)),
        compiler_params=_cparams("arbitrary"),
        name="modulation",
    )(cc, w_ada, b_ada)


def _ctx_kernel(x_ref, sh_ref, sc_ref, nw_ref, wkv_ref, wg_ref, afb_ref, bias_ref, sf_ref, sb_ref):
    x = x_ref[...]
    n = x.shape[0]
    hn = _rms(x) * nw_ref[...] * (1.0 + sc_ref[...]) + sh_ref[...]
    hb = hn.astype(BF16)
    kv = _dot(hb, wkv_ref[...])
    g = _dot(hb, wg_ref[...])
    z = _dot(g.astype(BF16), afb_ref[...]) + bias_ref[...]
    la = _log_sigmoid(z) * (1.0 / GATE_NORMALIZER)
    p = _segment_scan(la, n, False)
    p_f, p_b = p[:, :GLA_QK], p[:, GLA_QK:]
    w_f = jnp.exp(p_f[n - 1:n, :] - p_f)
    w_b = jnp.exp(p_b - la[:, GLA_QK:])
    k = kv[:, :GLA_QK]
    vb = kv[:, GLA_QK:].astype(BF16)
    ke_f = (k * w_f).astype(BF16)
    ke_b = (k * w_b).astype(BF16)
    for h in range(GLA_HEADS):
        vh = vb[:, h * GLA_DV:(h + 1) * GLA_DV]
        sf_ref[h] = _dot_tn(vh, ke_f[:, h * GLA_DK:(h + 1) * GLA_DK])
        sb_ref[h] = _dot_tn(vh, ke_b[:, h * GLA_DK:(h + 1) * GLA_DK])


def _context_states(ctx, csh1, csc1, nw1, wkv, wg, afb, bias_fb):
    bsz, n, d = ctx.shape
    full = lambda a: pl.BlockSpec(a.shape, lambda b: (0,) * a.ndim)
    st = jax.ShapeDtypeStruct((bsz, GLA_HEADS, GLA_DV, GLA_DK), F32)
    st_spec = pl.BlockSpec((None, GLA_HEADS, GLA_DV, GLA_DK), lambda b: (b, 0, 0, 0))
    return pl.pallas_call(
        _ctx_kernel,
        out_shape=(st, st),
        grid=(bsz,),
        in_specs=[pl.BlockSpec((None, n, d), lambda b: (b, 0, 0)),
                  full(csh1), full(csc1), full(nw1), full(wkv), full(wg), full(afb), full(bias_fb)],
        out_specs=(st_spec, st_spec),
        compiler_params=_cparams("arbitrary"),
        name="context_states",
    )(ctx, csh1, csc1, nw1, wkv, wg, afb, bias_fb)


def _chunk_scan(st_ref, qd, ke, v, dec, descending):
    nc = qd.shape[0] // CHUNK
    order = range(nc - 1, -1, -1) if descending else range(nc)
    rows = [slice(c * CHUNK, (c + 1) * CHUNK) for c in range(nc)]
    outs = []
    for h in range(GLA_HEADS):
        lanes = slice(h * GLA_DK, (h + 1) * GLA_DK)
        vl = slice(h * GLA_DV, (h + 1) * GLA_DV)
        kv = [_dot_tn(v[rows[c], vl], ke[rows[c], lanes]) for c in range(nc)]
        s = st_ref[h]
        start = [None] * nc
        for c in order:
            start[c] = s
            s = s * dec[c:c + 1, lanes] + kv[c]
        st_ref[h] = s
        outs.append(jnp.concatenate(
            [_dot_nt(qd[rows[c], lanes], start[c].astype(BF16)) for c in range(nc)], axis=0))
    return jnp.concatenate(outs, axis=1)


def _proj_kernel(tpb, x_ref, sh_ref, sc_ref, nw_ref, wm_ref, wg_ref, afb_ref, bias_ref, cw_ref,
                 s0_ref, ab_ref, v_ref, oi_ref, sr_ref, yc_ref, dec_ref, p_scr, st_ref):
    @pl.when(pl.program_id(0) % tpb == 0)
    def _():
        st_ref[...] = s0_ref[...]

    subs = range(x_ref.shape[0] // PROJ_SUB)
    nc = PROJ_SUB // CHUNK
    nblk = wm_ref.shape[1] // PROJ_COLS
    factors = []
    for sub in subs:
        x = x_ref[sub * PROJ_SUB:(sub + 1) * PROJ_SUB, :]
        hn = _rms(x) * nw_ref[...] * (1.0 + sc_ref[...]) + sh_ref[...]
        hb = hn.astype(BF16)
        g = _dot(hb, wg_ref[...])
        z = _dot(g.astype(BF16), afb_ref[...]) + bias_ref[...]

        todo = iter(range(nblk))

        def wide_block():
            j = next(todo, None)
            if j is not None:
                cols = slice(j * PROJ_COLS, (j + 1) * PROJ_COLS)
                p_scr[sub, :, cols] = _dot(hb, wm_ref[:, cols])

        la = _log_sigmoid(z) * (1.0 / GATE_NORMALIZER)
        wide_block()
        b_f = _segment_scan(la[:, :GLA_QK], CHUNK, False, wide_block)
        b_b = _segment_scan(la[:, GLA_QK:], CHUNK, True, wide_block)
        b_f3 = b_f.reshape(nc, CHUNK, GLA_QK)
        tot_f = b_f3[:, CHUNK - 1:CHUNK, :]
        b_b3 = b_b.reshape(nc, CHUNK, GLA_QK)
        tot_b = b_b3[:, 0:1, :]
        fac = []
        for arg in (b_f, -b_f, (tot_f - b_f3).reshape(PROJ_SUB, GLA_QK),
                    b_b, -b_b, (tot_b - b_b3).reshape(PROJ_SUB, GLA_QK)):
            fac.append(jnp.exp(arg))
            wide_block()
        fac += [jnp.exp(tot_f).reshape(nc, GLA_QK), jnp.exp(tot_b).reshape(nc, GLA_QK)]
        factors.append(fac)
        for _ in range(nblk):
            wide_block()
    for sub in subs:
        _proj_rows(sub, factors[sub], cw_ref, ab_ref, v_ref, oi_ref, sr_ref, yc_ref,
                   dec_ref, p_scr.at[sub], st_ref)


def _proj_rows(sub, factors, cw_ref, ab_ref, v_ref, oi_ref, sr_ref, yc_ref, dec_ref, p, st_ref):
    tm = PROJ_SUB
    nc = tm // CHUNK
    rs = slice(sub * tm, (sub + 1) * tm)
    cs = slice(sub * nc, (sub + 1) * nc)
    e_f, ei_f, ee_f, e_b, ei_b, ee_b, dec_f, dec_b = factors

    q = p[:, 0:GLA_QK] * (GLA_DK ** -0.5)
    k = p[:, GLA_QK:2 * GLA_QK]
    vb = p[:, 2 * GLA_QK:2 * GLA_QK + GLA_WIDTH].astype(BF16)
    v_ref[rs, :] = vb
    qd_f = (q * e_f).astype(BF16)
    ki_f = (k * ei_f).astype(BF16)
    ke_f = (k * ee_f).astype(BF16)
    qd_b = (q * e_b).astype(BF16)
    ki_b = (k * ei_b).astype(BF16)
    ke_b = (k * ee_b).astype(BF16)

    ab_ref[rs, :GLA_QK] = qd_b
    ab_ref[rs, GLA_QK:] = ke_b
    dec_ref[cs, :GLA_QK] = dec_f
    dec_ref[cs, GLA_QK:] = dec_b

    inter_f = _chunk_scan(st_ref, qd_f, ke_f, vb, dec_f, False)

    gi = INTRA_GROUP
    ri = lax.broadcasted_iota(jnp.int32, (gi, gi), 0)
    ci = lax.broadcasted_iota(jnp.int32, (gi, gi), 1)
    same = (ri // CHUNK) == (ci // CHUNK)
    lower = same & (ci <= ri)
    upper = same & (ci >= ri)
    for gidx in range(tm // gi):
        rows = slice(gidx * gi, (gidx + 1) * gi)
        orows = slice(sub * tm + gidx * gi, sub * tm + (gidx + 1) * gi)
        for h in range(GLA_HEADS):
            lanes = slice(h * GLA_DK, (h + 1) * GLA_DK)
            s_f = _dot_nt(qd_f[rows, lanes], ki_f[rows, lanes])
            s_b = _dot_nt(qd_b[rows, lanes], ki_b[rows, lanes])
            s = jnp.where(lower, s_f, 0.0) + jnp.where(upper, s_b, 0.0)
            vl = slice(h * GLA_DV, (h + 1) * GLA_DV)
            oi_ref[orows, vl] = _dot(s.astype(BF16), vb[rows, vl]) + inter_f[rows, vl]

    r = p[:, 1024:1536]
    sr_ref[rs, :] = (r * _sigmoid(r)).astype(BF16)

    u = p[:, 2048:2560] * p[:, 2560:3072]
    row = lax.broadcasted_iota(jnp.int32, u.shape, 0) & (GRID_W - 1)
    prev = jnp.where(row >= 1, pltpu.roll(u, 1, 0), 0.0)
    nxt = jnp.where(row < GRID_W - 1, pltpu.roll(u, tm - 1, 0), 0.0)
    cw = cw_ref[...]
    conv = prev * cw[0:1, :] + u * cw[1:2, :] + nxt * cw[2:3, :]
    yc_ref[rs, :] = (p[:, 1536:2048] * conv).astype(BF16)


def _projection(x2, row0, n, sh1, sc1, nw1, wm, wg, afb, bias_fb, conv_w, s0f, seq):
    d = x2.shape[1]
    tm = TOKEN_TILE
    tpb = seq // tm
    t0 = row0 // tm
    full = lambda a: pl.BlockSpec(a.shape, lambda i: (0,) * a.ndim)
    per_batch = pl.BlockSpec((None, 1, d), lambda i: (i // tpb, 0, 0))
    tok = lambda w: pl.BlockSpec((tm, w), lambda i: (i, 0))
    st_spec = pl.BlockSpec((None, GLA_HEADS, GLA_DV, GLA_DK), lambda i: (i // tpb, 0, 0, 0))
    out_shape = (
        jax.ShapeDtypeStruct((n, 2 * GLA_QK), BF16),
        jax.ShapeDtypeStruct((n, GLA_WIDTH), BF16),
        jax.ShapeDtypeStruct((n, GLA_WIDTH), F32),
        jax.ShapeDtypeStruct((n, GLA_WIDTH), BF16),
        jax.ShapeDtypeStruct((n, CONV_WIDTH), BF16),
        jax.ShapeDtypeStruct((n // CHUNK, 2 * GLA_QK), F32),
    )
    out_specs = (tok(2 * GLA_QK), tok(GLA_WIDTH), tok(GLA_WIDTH), tok(GLA_WIDTH),
                 tok(CONV_WIDTH), pl.BlockSpec((tm // CHUNK, 2 * GLA_QK), lambda i: (i, 0)))
    return pl.pallas_call(
        functools.partial(_proj_kernel, tpb),
        out_shape=out_shape,
        grid=(n // tm,),
        in_specs=[pl.BlockSpec((tm, d), lambda i: (i + t0, 0)), per_batch, per_batch, full(nw1),
                  full(wm), full(wg), full(afb), full(bias_fb), full(conv_w), st_spec],
        out_specs=out_specs,
        scratch_shapes=[pltpu.VMEM((tm // PROJ_SUB, PROJ_SUB, wm.shape[1]), F32),
                        pltpu.VMEM((GLA_HEADS, GLA_DV, GLA_DK), F32)],
        compiler_params=_cparams("arbitrary"),
        name="projection",
    )(x2, sh1, sc1, nw1, wm, wg, afb, bias_fb, conv_w, s0f)


def _post_kernel(tpb, x_ref, oi_ref, ab_ref, v_ref, dec_ref, s0_ref, sr_ref, yc_ref, gnw_ref, wo_ref,
                 g1_ref, nw2_ref, sh2_ref, sc2_ref, rwt_ref, rb_ref, tri_ref,
                 h_ref, xw_ref, idx_ref, rank_ref, wt_ref, cnt_ref, run_ref, st_ref):
    @pl.when(pl.program_id(0) == 0)
    def _():
        run_ref[...] = jnp.zeros_like(run_ref)

    @pl.when(pl.program_id(0) % tpb == 0)
    def _():
        st_ref[...] = s0_ref[...]

    tm = x_ref.shape[0]
    inter_b = _chunk_scan(st_ref, ab_ref[:, :GLA_QK], ab_ref[:, GLA_QK:], v_ref[...],
                          dec_ref[:, GLA_QK:], True)
    o = oi_ref[...] + inter_b
    normed = jnp.concatenate(
        [_rms(o[:, h * GLA_DV:(h + 1) * GLA_DV]) for h in range(GLA_HEADS)], axis=1)
    y_gla = (normed * gnw_ref[...] * sr_ref[...].astype(F32)).astype(BF16)
    wo = wo_ref[...]
    y = _dot(y_gla, wo[:GLA_WIDTH, :]) + _dot(yc_ref[...], wo[GLA_WIDTH:, :])
    h = x_ref[...] + g1_ref[...] * y
    h_ref[...] = h
    hn = _rms(h) * nw2_ref[...] * (1.0 + sc2_ref[...]) + sh2_ref[...]
    xw_ref[...] = _pack_halves(hn)

    logits = _dot3(rwt_ref[...], hn, _dot_nt) + rb_ref[...]
    eid = lax.broadcasted_iota(jnp.int32, logits.shape, 0)
    vals, idxs = [], []
    work = logits
    for _ in range(TOP_K):
        m = jnp.max(work, axis=0, keepdims=True)
        i = jnp.min(jnp.where(work == m, eid, N_EXPERTS), axis=0, keepdims=True)
        vals.append(m)
        idxs.append(i)
        work = jnp.where(eid == i, -jnp.inf, work)
    exps = [jnp.exp(v - vals[0]) for v in vals]
    denom = exps[0] + exps[1] + exps[2] + exps[3]
    wts = [e / denom for e in exps]

    onehot = jnp.zeros(logits.shape, F32)
    for i in idxs:
        onehot = onehot + jnp.where(eid == i, 1.0, 0.0)
    before = _dot(onehot.astype(BF16), tri_ref[...]) + run_ref[:, 0:1]
    for kk in range(TOP_K):
        idx_ref[kk:kk + 1, :] = idxs[kk]
        rk = jnp.sum(jnp.where(eid == idxs[kk], before, 0.0), axis=0, keepdims=True)
        rank_ref[kk:kk + 1, :] = rk.astype(jnp.int32)
    run = run_ref[...] + jnp.sum(onehot, axis=1, keepdims=True)
    run_ref[...] = run
    cnt_ref[...] = run.astype(jnp.int32)

    sub = lax.broadcasted_iota(jnp.int32, (GATE_PAD, tm), 0)
    wpad = jnp.zeros((GATE_PAD, tm), F32)
    for kk in range(TOP_K):
        wpad = wpad + jnp.where(sub == kk, wts[kk], 0.0)
    wt_ref[...] = wpad.T


def _post_mixer(x2, row0, o_i, a_b, vb, dec, s0b, sr, yc, gnw, wo, g1, nw2, sh2, sc2, rwt, rb, tri,
                seq):
    n, d = o_i.shape[0], x2.shape[1]
    tm = POST_TILE
    tpb = seq // tm
    t0 = row0 // tm
    rev =lambda i: (i // tpb) * tpb + (tpb - 1 - i % tpb)
    full = lambda a: pl.BlockSpec(a.shape, lambda i: (0,) * a.ndim)
    per_batch = pl.BlockSpec((None, 1, d), lambda i: (i // tpb, 0, 0))
    tok = lambda w: pl.BlockSpec((tm, w), lambda i: (rev(i), 0))
    lane_tok = pl.BlockSpec((TOP_K, tm), lambda i: (0, rev(i)))
    st_spec = pl.BlockSpec((None, GLA_HEADS, GLA_DV, GLA_DK), lambda i: (i // tpb, 0, 0, 0))
    dec_spec = pl.BlockSpec((tm // CHUNK, 2 * GLA_QK), lambda i: (rev(i), 0))
    out_shape = (
        jax.ShapeDtypeStruct((n, d), F32),
        jax.ShapeDtypeStruct((n, d // 2), jnp.uint32),
        jax.ShapeDtypeStruct((TOP_K, n), jnp.int32),
        jax.ShapeDtypeStruct((TOP_K, n), jnp.int32),
        jax.ShapeDtypeStruct((n, GATE_PAD), F32),
        jax.ShapeDtypeStruct((N_EXPERTS, GATE_PAD), jnp.int32),
    )
    out_specs = (tok(d), tok(d // 2), lane_tok, lane_tok, tok(GATE_PAD),
                 pl.BlockSpec((N_EXPERTS, GATE_PAD), lambda i: (0, 0)))
    return pl.pallas_call(
        functools.partial(_post_kernel, tpb),
        out_shape=out_shape,
        grid=(n // tm,),
        in_specs=[pl.BlockSpec((tm, d), lambda i: (rev(i) + t0, 0)), tok(GLA_WIDTH),
                  tok(2 * GLA_QK), tok(GLA_WIDTH), dec_spec, st_spec, tok(GLA_WIDTH),
                  tok(CONV_WIDTH), full(gnw), full(wo), per_batch, full(nw2), per_batch, per_batch,
                  full(rwt), full(rb), full(tri)],
        out_specs=out_specs,
        scratch_shapes=[pltpu.VMEM((N_EXPERTS, GATE_PAD), F32),
                        pltpu.VMEM((GLA_HEADS, GLA_DV, GLA_DK), F32)],
        compiler_params=_cparams("arbitrary"),
        name="post_mixer",
    )(x2, o_i, a_b, vb, dec, s0b, sr, yc, gnw, wo, g1, nw2, sh2, sc2, rwt, rb, tri)


def _plan_kernel(cnt_ref, start_ref, te_ref, first_ref, next_ref, nu_ref):
    n_tiles = te_ref.shape[0]
    shift = EXPERT_TILE.bit_length() - 1

    def clear(t, carry):
        first_ref[t] = 0
        next_ref[t] = -1
        return carry

    lax.fori_loop(0, n_tiles, clear, 0)

    def group(e, tile0):
        nt = lax.shift_right_logical(cnt_ref[e] + (EXPERT_TILE - 1), shift)
        start_ref[e] = tile0 * EXPERT_TILE

        def fill(t, carry):
            te_ref[tile0 + t] = e
            return carry

        lax.fori_loop(0, nt, fill, 0)

        @pl.when(nt > 0)
        def _():
            first_ref[tile0] = 1

        return tile0 + nt

    n_used = lax.fori_loop(0, N_EXPERTS, group, 0)
    nu_ref[0] = n_used
    last = te_ref[n_used - 1]

    def tail(t, carry):
        te_ref[t] = last
        return carry

    lax.fori_loop(n_used, n_tiles, tail, 0)

    def link(k, nxt):
        e = N_EXPERTS - 1 - k
        has = cnt_ref[e] > 0

        @pl.when(has)
        def _():
            next_ref[lax.shift_right_logical(start_ref[e], shift)] = nxt

        return jnp.where(has, e, nxt)

    lax.fori_loop(0, N_EXPERTS, link, -1)


def _plan(counts, n_tiles):
    assert EXPERT_TILE & (EXPERT_TILE - 1) == 0
    smem = pl.BlockSpec(memory_space=pltpu.SMEM)
    vec = lambda m: jax.ShapeDtypeStruct((m,), jnp.int32)
    return pl.pallas_call(
        _plan_kernel,
        out_shape=(vec(N_EXPERTS), vec(n_tiles), vec(n_tiles), vec(n_tiles), vec(1)),
        in_specs=[smem],
        out_specs=(smem, smem, smem, smem, smem),
        name="plan",
    )(counts)


def _slot_kernel(start_ref, idx_ref, rank_ref, slot_ref):
    idx = idx_ref[...]
    base = jnp.zeros(idx.shape, jnp.int32)
    for e in range(N_EXPERTS):
        base = jnp.where(idx == e, start_ref[e], base)
    slot_ref[...] = base + rank_ref[...]


def _slots(group_start, idx_t, rank_t):
    k, n = idx_t.shape
    bn = min(n, 8192)
    blk = pl.BlockSpec((k, bn), lambda i, s: (0, i))
    return pl.pallas_call(
        _slot_kernel,
        out_shape=jax.ShapeDtypeStruct((k, n), jnp.int32),
        grid_spec=pltpu.PrefetchScalarGridSpec(
            num_scalar_prefetch=1, grid=(n // bn,), in_specs=[blk, blk], out_specs=blk),
        compiler_params=_cparams("arbitrary"),
        name="slots",
    )(group_start, idx_t, rank_t)


SC_CORES = 2
SC_SUBCORES = 16
SC_WORKERS = SC_CORES * SC_SUBCORES
SCATTER_CHUNK = 128
GATHER_CHUNK = 64


def _sc_mesh():
    return plsc.VectorSubcoreMesh(core_axis_name="c", subcore_axis_name="s")


def _sc_worker():
    return lax.axis_index("s") * SC_CORES + lax.axis_index("c")


def _sc_scatter_rows(src, idx, n_out):
    n, w = src.shape
    m = idx.shape[0]
    c = SCATTER_CHUNK
    per = m // SC_WORKERS
    nch = per // c
    assert per * SC_WORKERS == m and nch * c == per and n % per == 0

    def body(src_hbm, idx_hbm, out_hbm, idx_v, rows_v):
        wid = _sc_worker()
        src_base = lax.rem(wid * per, n)
        pltpu.sync_copy(idx_hbm.at[wid], idx_v)

        @pl.loop(0, nch)
        def _(j):
            pltpu.sync_copy(src_hbm.at[pl.ds(src_base + j * c, c)], rows_v)
            pltpu.sync_copy(rows_v, out_hbm.at[idx_v.at[j]])

    return pl.kernel(
        body,
        out_type=jax.ShapeDtypeStruct((n_out, w), src.dtype),
        mesh=_sc_mesh(),
        scratch_types=[pltpu.VMEM((nch, c), jnp.int32), pltpu.VMEM((c, w), src.dtype)],
        name="sc_dispatch",
    )(src, idx.reshape(SC_WORKERS, nch, c))


def _sc_gather_rows(table, idx):
    w = table.shape[1]
    m = idx.shape[0]
    c = GATHER_CHUNK
    per = m // SC_WORKERS
    nch = per // c
    assert per * SC_WORKERS == m and nch * c == per and nch % 2 == 0

    def body(table_hbm, idx_hbm, out_hbm, idx_v, rows_v, sem0, sem1):
        wid = _sc_worker()
        base = wid * per
        sems = (sem0, sem1)
        pltpu.sync_copy(idx_hbm.at[wid], idx_v)

        def gather(j, b):
            return pltpu.make_async_copy(table_hbm.at[idx_v.at[j]], rows_v.at[b], sems[b])

        gather(0, 0).start()

        @pl.loop(0, nch, step=2)
        def _(g):
            for b in range(2):
                j = g + b
                gather(j, b).wait()

                @pl.when(j + 1 < nch)
                def _():
                    gather(j + 1, 1 - b).start()

                pltpu.sync_copy(rows_v.at[b], out_hbm.at[pl.ds(base + j * c, c)])

    return pl.kernel(
        body,
        out_type=jax.ShapeDtypeStruct((m, w), table.dtype),
        mesh=_sc_mesh(),
        scratch_types=[pltpu.VMEM((nch, c), jnp.int32), pltpu.VMEM((2, c, w), table.dtype),
                       pltpu.SemaphoreType.DMA, pltpu.SemaphoreType.DMA],
        name="sc_combine_gather",
    )(table, idx.reshape(SC_WORKERS, nch, c))


PERM_BLOCK = 256


def _expert_kernel(te_ref, first_ref, next_ref, nu_ref, x_ref, wgu_hbm, wdn_hbm, perm_ref, bg_ref,
                   bu_ref, bd_ref, y_ref, wgu_f, wdn_f, wg_s, wu_s, wd_s, sem):
    i = pl.program_id(0)
    used = i < nu_ref[0]

    def fetch(e):
        return (pltpu.make_async_copy(wgu_hbm.at[e], wgu_f, sem.at[0]),
                pltpu.make_async_copy(wdn_hbm.at[e], wdn_f, sem.at[1]))

    @pl.when(i == 0)
    def _():
        for cp in fetch(te_ref[0]):
            cp.start()

    @pl.when(used & (first_ref[i] == 1))
    def _():
        for cp in fetch(te_ref[i]):
            cp.wait()
        half = PERM_BLOCK // 2
        perm = perm_ref[...]
        for b in range(wgu_f.shape[1] // PERM_BLOCK):
            blk = wgu_f[:, b * PERM_BLOCK:(b + 1) * PERM_BLOCK].astype(BF16)
            t = _dot(blk, perm)
            wg_s[:, b * half:(b + 1) * half] = t[:, :half].astype(BF16)
            wu_s[:, b * half:(b + 1) * half] = t[:, half:].astype(BF16)
        wd_s[...] = wdn_f[...].astype(BF16)

        @pl.when(next_ref[i] >= 0)
        def _():
            for cp in fetch(next_ref[i]):
                cp.start()

    @pl.when(used)
    def _():
        x = _unpack_halves(x_ref[...]).astype(BF16)
        g = _dot(x, wg_s[...]) + bg_ref[...]
        u = _dot(x, wu_s[...]) + bu_ref[...]
        gate = jnp.minimum(g, SWIGLU_LIMIT)
        up = jnp.clip(u, -SWIGLU_LIMIT, SWIGLU_LIMIT)
        act = (up + 1.0) * (gate * _sigmoid(gate, SWIGLU_ALPHA))
        y = _dot(act.astype(BF16), wd_s[...]) + bd_ref[...]
        y_ref[...] = _pack_halves(y)

    @pl.when(jnp.logical_not(used))
    def _():
        y_ref[...] = jnp.zeros_like(y_ref)


def _experts(tile_expert, tile_first, tile_next, n_used, xs, w_gu, w_dn, bg, bu, bd):
    n_slots, half = xs.shape
    tm = EXPERT_TILE
    n_tiles = n_slots // tm
    dm, f, d = w_gu.shape[1], w_dn.shape[1], w_dn.shape[2]
    ph = PERM_BLOCK // 2
    j = jnp.arange(PERM_BLOCK)
    perm = (jnp.arange(PERM_BLOCK)[None, :] == (j // 2 + (j % 2) * ph)[:, None]).astype(BF16)
    row_in = pl.BlockSpec((tm, half), lambda i, te, tf, tn, nu: (jnp.minimum(i, nu[0] - 1), 0))
    row_out = pl.BlockSpec((tm, half),
                           lambda i, te, tf, tn, nu: (jnp.where(i < nu[0], i, n_tiles - 1), 0))
    bspec = lambda a: pl.BlockSpec((None,) + a.shape[1:], lambda i, te, tf, tn, nu: (te[i], 0, 0))
    return pl.pallas_call(
        _expert_kernel,
        out_shape=jax.ShapeDtypeStruct((n_slots, half), jnp.uint32),
        grid_spec=pltpu.PrefetchScalarGridSpec(
            num_scalar_prefetch=4, grid=(n_tiles,),
            in_specs=[row_in, pl.BlockSpec(memory_space=pl.ANY), pl.BlockSpec(memory_space=pl.ANY),
                      pl.BlockSpec((PERM_BLOCK, PERM_BLOCK), lambda i, te, tf, tn, nu: (0, 0)),
                      bspec(bg), bspec(bu), bspec(bd)],
            out_specs=row_out,
            scratch_shapes=[pltpu.VMEM((dm, 2 * f), F32), pltpu.VMEM((f, d), F32),
                            pltpu.VMEM((dm, f), BF16), pltpu.VMEM((dm, f), BF16),
                            pltpu.VMEM((f, d), BF16), pltpu.SemaphoreType.DMA((2,))]),
        compiler_params=_cparams("arbitrary"),
        name="experts",
    )(tile_expert, tile_first, tile_next, n_used, xs, w_gu, w_dn, perm, bg, bu, bd)


def _combine_kernel(y4_ref, h_ref, wt_ref, g2_ref, fnw_ref, *rest):
    o_ref = rest[-1]
    wt = wt_ref[...]
    acc = jnp.zeros(h_ref.shape, F32)
    for kk in range(TOP_K):
        acc = acc + wt[:, kk:kk + 1] * _unpack_halves(y4_ref[kk])
    h = h_ref[...] + g2_ref[...] * acc
    o_ref[...] = _rms(h) * fnw_ref[...]


def _combine(y4, h, wt, g2, fnw, seq, row0, n_total, prev_out):
    n, d = h.shape
    tt = COMBINE_TILE
    tpb = seq // tt
    t0 = row0 // tt
    in_specs = [pl.BlockSpec((TOP_K, tt, d // 2), lambda i: (0, i, 0)),
                pl.BlockSpec((tt, d), lambda i: (i, 0)),
                pl.BlockSpec((tt, GATE_PAD), lambda i: (i, 0)),
                pl.BlockSpec((None, 1, d), lambda i: (i // tpb, 0, 0)),
                pl.BlockSpec((1, d), lambda i: (0, 0))]
    args = [y4, h, wt, g2, fnw]
    aliases = {}
    if prev_out is not None:
        in_specs.append(pl.BlockSpec(memory_space=pl.ANY))
        args.append(prev_out)
        aliases = {len(args) - 1: 0}
    return pl.pallas_call(
        _combine_kernel,
        out_shape=jax.ShapeDtypeStruct((n_total, d), F32),
        grid=(n // tt,),
        in_specs=in_specs,
        out_specs=pl.BlockSpec((tt, d), lambda i: (i + t0, 0)),
        input_output_aliases=aliases,
        compiler_params=_cparams("arbitrary"),
        name="combine",
    )(*args)


def kernel(x, c, ctx, c_ctx, w_ada, b_ada, norm1_w, w_in, a_up_f, a_bias_f, a_up_b, a_bias_b,
           gla_norm_w, conv_w, w_out, norm2_w, router_w, router_b, w_gu, b_gu, w_dn, b_dn,
           final_norm_w):
    bsz, seq, d = x.shape
    n = bsz * seq
    assert w_ada.shape[0] == 1 and d == D_MODEL
    assert seq % TOKEN_TILE == 0 and seq % POST_TILE == 0 and seq % GRID_W == 0
    assert ctx.shape[1] & (ctx.shape[1] - 1) == 0

    w_in0 = w_in[0]
    v_end = 2 * GLA_QK + GLA_WIDTH
    r_end = v_end + GLA_WIDTH
    g_end = r_end + 2 * GATE_RANK
    wm = jnp.concatenate([w_in0[:, :r_end], w_in0[:, g_end:]], axis=1).astype(BF16)
    wg_low = jnp.pad(w_in0[:, r_end:g_end], ((0, 0), (0, GATE_PAD - 2 * GATE_RANK))).astype(BF16)
    afb = jnp.zeros((GATE_PAD, 2 * GLA_QK), F32)
    afb = afb.at[:GATE_RANK, :GLA_QK].set(a_up_f[0]).at[GATE_RANK:2 * GATE_RANK, GLA_QK:].set(a_up_b[0])
    afb = afb.astype(BF16)
    bias_fb = jnp.concatenate([a_bias_f[0], a_bias_b[0]])[None]
    wkv = wm[:, GLA_QK:v_end]
    row = lambda a: a.reshape(1, -1)

    mod = _modulation(jnp.concatenate([c, c_ctx[None]], axis=0), w_ada[0], row(b_ada[0]))
    sh1, sc1, g1, sh2, sc2, g2 = [mod[:bsz, i * d:(i + 1) * d].reshape(bsz, 1, d) for i in range(6)]
    csh1, csc1 = mod[bsz:, 0:d], mod[bsz:, d:2 * d]

    s0f, s0b = _context_states(ctx, csh1, csc1, row(norm1_w[0]), wkv, wg_low, afb, bias_fb)

    x2 = x.reshape(n, d)
    tri = jnp.triu(jnp.ones((POST_TILE, POST_TILE), BF16), k=1)
    gnw = jnp.tile(gla_norm_w[0], GLA_HEADS)[None]
    wo = w_out[0].astype(BF16)
    bg_e, bu_e, bd_e = b_gu[0][:, None, 0::2], b_gu[0][:, None, 1::2], b_dn[0][:, None, :]

    if bsz % 4 == 0:
        chunk_batches = (3 * bsz // 4, bsz // 4)
    elif bsz % 2 == 0:
        chunk_batches = (bsz // 2, bsz // 2)
    else:
        chunk_batches = (bsz,)
    routed = []
    b0 = 0
    for cb in chunk_batches:
        bs = slice(b0, b0 + cb)
        row0 = b0 * seq
        b0 += cb
        nc = cb * seq
        n_tiles = nc * TOP_K // EXPERT_TILE + N_EXPERTS
        a_b, vb, o_i, sr, yc, dec = _projection(
            x2, row0, nc, sh1[bs], sc1[bs], row(norm1_w[0]), wm, wg_low, afb, bias_fb, conv_w[0],
            s0f[bs], seq)
        h, xw, idx_t, rank_t, wt, cnt = _post_mixer(
            x2, row0, o_i, a_b, vb, dec, s0b[bs], sr, yc, gnw, wo, g1[bs], row(norm2_w[0]), sh2[bs],
            sc2[bs], router_w[0].T, router_b[0][:, None], tri, seq)

        group_start, tile_expert, tile_first, tile_next, n_used = _plan(cnt[:, 0], n_tiles)
        slot_flat = _slots(group_start, idx_t, rank_t).reshape(-1)
        xs = _sc_scatter_rows(xw, slot_flat, n_tiles * EXPERT_TILE)
        routed.append((h, wt, slot_flat, xs, tile_expert, tile_first, tile_next, n_used, bs, row0))

    gathered = []
    for h, wt, slot_flat, xs, tile_expert, tile_first, tile_next, n_used, bs, row0 in routed:
        ys = _experts(tile_expert, tile_first, tile_next, n_used, xs, w_gu[0], w_dn[0],
                      bg_e, bu_e, bd_e)
        y4 = _sc_gather_rows(ys, slot_flat).reshape(TOP_K, h.shape[0], d // 2)
        gathered.append((y4, h, wt, bs, row0))

    out = None
    for y4, h, wt, bs, row0 in gathered:
        out = _combine(y4, h, wt, g2[bs], row(final_norm_w), seq, row0, n, out)
    return out.reshape(bsz, seq, d)
```

```python
import functools

import jax
import jax.numpy as jnp
from jax import lax
from jax.experimental import pallas as pl
from jax.experimental.pallas import tpu as pltpu
from jax.experimental.pallas import tpu_sc as plsc

D_MODEL = 1024
GLA_HEADS = 4
GLA_DK = 64
GLA_DV = 128
GLA_QK = GLA_HEADS * GLA_DK
GLA_WIDTH = GLA_HEADS * GLA_DV
CONV_WIDTH = 512
GATE_RANK = 16
GATE_PAD = 128
GATE_NORMALIZER = 16.0
CHUNK = 64
GRID_W = 64
N_EXPERTS = 32
TOP_K = 4
SWIGLU_LIMIT = 7.0
SWIGLU_ALPHA = 1.702
EPS = 1e-6

TOKEN_TILE = 1024
POST_TILE = 1024
PROJ_SUB = 1024
PROJ_COLS = 1024
INTRA_GROUP = 256
EXPERT_TILE = 512
COMBINE_TILE = 1024
VMEM_LIMIT = 56 * 1024 * 1024

F32 = jnp.float32
BF16 = jnp.bfloat16
HI_MASK = 0xFFFF0000
LOG2_E = 1.4426950408889634


def _cparams(*sem):
    return pltpu.CompilerParams(dimension_semantics=sem, vmem_limit_bytes=VMEM_LIMIT)


def _mod_spec(d, component, batch_of_step):
    return pl.BlockSpec((None, None, 1, d), lambda i: (batch_of_step(i), component, 0, 0))


def _rms(x):
    return x * lax.rsqrt(jnp.mean(x * x, axis=-1, keepdims=True) + EPS)


def _sigmoid(x, scale=1.0):
    return 1.0 / (1.0 + jnp.exp2(x * (-scale * LOG2_E)))


def _log_sigmoid(z):
    return jnp.minimum(z, 0.0) - jnp.log(1.0 + jnp.exp(-jnp.abs(z)))


def _dot(a, b):
    return jnp.dot(a, b, preferred_element_type=F32)


def _dot_nt(a, b):
    return lax.dot_general(a, b, (((1,), (1,)), ((), ())), preferred_element_type=F32)


def _dot_tn(a, b):
    return lax.dot_general(a, b, (((0,), (0,)), ((), ())), preferred_element_type=F32)


def _split_bf16(a):
    hi = a.astype(BF16)
    lo = (a - hi.astype(F32)).astype(BF16)
    return hi, lo


def _dot3(a, b, dot):
    a_hi, a_lo = _split_bf16(a)
    b_hi, b_lo = _split_bf16(b)
    return dot(a_hi, b_hi) + (dot(a_hi, b_lo) + dot(a_lo, b_hi))


def _segment_scan(x, seg, reverse, after_step=None):
    n = x.shape[0]
    row = lax.broadcasted_iota(jnp.int32, x.shape, 0) & (seg - 1)
    s = 1
    while s < seg:
        if reverse:
            shifted = pltpu.roll(x, n - s, 0)
            x = x + jnp.where(row < seg - s, shifted, 0.0)
        else:
            shifted = pltpu.roll(x, s, 0)
            x = x + jnp.where(row >= s, shifted, 0.0)
        if after_step is not None:
            after_step()
        s *= 2
    return x


def _pack_halves(x):
    n = x.shape[1] // 2
    bits = pltpu.bitcast(x.astype(BF16).astype(F32), jnp.uint32)
    return (bits[:, :n] & jnp.uint32(HI_MASK)) | (bits[:, n:] >> 16)


def _unpack_halves(w):
    hi = pltpu.bitcast(w & jnp.uint32(HI_MASK), F32)
    lo = pltpu.bitcast(w << 16, F32)
    return jnp.concatenate([hi, lo], axis=1)


def _mod_kernel(c_ref, w_ref, b_ref, o_ref):
    c = c_ref[...]
    s = c * _sigmoid(c)
    o_ref[...] = _dot3(s, w_ref[...], _dot) + b_ref[...]


def _modulation(cc, w_ada, b_ada):
    rows, d = cc.shape
    n = w_ada.shape[1]
    bn = 1536
    return pl.pallas_call(
        _mod_kernel,
        out_shape=jax.ShapeDtypeStruct((rows, n), F32),
        grid=(n // bn,),
        in_specs=[pl.BlockSpec((rows, d), lambda j: (0, 0)),
                  pl.BlockSpec((d, bn), lambda j: (0, j)),
                  pl.BlockSpec((1, bn), lambda j: (0, j))],
        out_specs=pl.BlockSpec((rows, bn), lambda j: (0, j)),
        compiler_params=_cparams("arbitrary"),
        name="modulation",
    )(cc, w_ada, b_ada)


def _ctx_kernel(x_ref, sh_ref, sc_ref, nw_ref, wkv_ref, wg_ref, afb_ref, bias_ref, sf_ref, sb_ref):
    x = x_ref[...]
    n = x.shape[0]
    hn = _rms(x) * nw_ref[...] * (1.0 + sc_ref[...]) + sh_ref[...]
    hb = hn.astype(BF16)
    kv = _dot(hb, wkv_ref[...])
    g = _dot(hb, wg_ref[...])
    z = _dot(g.astype(BF16), afb_ref[...]) + bias_ref[...]
    la = _log_sigmoid(z) * (1.0 / GATE_NORMALIZER)
    p = _segment_scan(la, n, False)
    p_f, p_b = p[:, :GLA_QK], p[:, GLA_QK:]
    w_f = jnp.exp(p_f[n - 1:n, :] - p_f)
    w_b = jnp.exp(p_b - la[:, GLA_QK:])
    k = kv[:, :GLA_QK]
    vb = kv[:, GLA_QK:].astype(BF16)
    ke_f = (k * w_f).astype(BF16)
    ke_b = (k * w_b).astype(BF16)
    for h in range(GLA_HEADS):
        vh = vb[:, h * GLA_DV:(h + 1) * GLA_DV]
        sf_ref[h] = _dot_tn(vh, ke_f[:, h * GLA_DK:(h + 1) * GLA_DK])
        sb_ref[h] = _dot_tn(vh, ke_b[:, h * GLA_DK:(h + 1) * GLA_DK])


def _context_states(ctx, mod4, nw1, wkv, wg, afb, bias_fb):
    bsz, n, d = ctx.shape
    ctx_row = lambda b: bsz
    full = lambda a: pl.BlockSpec(a.shape, lambda b: (0,) * a.ndim)
    st = jax.ShapeDtypeStruct((bsz, GLA_HEADS, GLA_DV, GLA_DK), F32)
    st_spec = pl.BlockSpec((None, GLA_HEADS, GLA_DV, GLA_DK), lambda b: (b, 0, 0, 0))
    return pl.pallas_call(
        _ctx_kernel,
        out_shape=(st, st),
        grid=(bsz,),
        in_specs=[pl.BlockSpec((None, n, d), lambda b: (b, 0, 0)),
                  _mod_spec(d, 0, ctx_row), _mod_spec(d, 1, ctx_row), full(nw1), full(wkv), full(wg),
                  full(afb), full(bias_fb)],
        out_specs=(st_spec, st_spec),
        compiler_params=_cparams("arbitrary"),
        name="context_states",
    )(ctx, mod4, mod4, nw1, wkv, wg, afb, bias_fb)


def _chunk_scan(st_ref, qd, ke, v, dec, descending):
    nc = qd.shape[0] // CHUNK
    order = range(nc - 1, -1, -1) if descending else range(nc)
    rows = [slice(c * CHUNK, (c + 1) * CHUNK) for c in range(nc)]
    outs = []
    for h in range(GLA_HEADS):
        lanes = slice(h * GLA_DK, (h + 1) * GLA_DK)
        vl = slice(h * GLA_DV, (h + 1) * GLA_DV)
        kv = [_dot_tn(v[rows[c], vl], ke[rows[c], lanes]) for c in range(nc)]
        s = st_ref[h]
        start = [None] * nc
        for c in order:
            start[c] = s
            s = s * dec[c:c + 1, lanes] + kv[c]
        st_ref[h] = s
        outs.append(jnp.concatenate(
            [_dot_nt(qd[rows[c], lanes], start[c].astype(BF16)) for c in range(nc)], axis=0))
    return jnp.concatenate(outs, axis=1)


def _proj_kernel(tpb, x_ref, sh_ref, sc_ref, nw_ref, wm_ref, wg_ref, afb_ref, bias_ref, cw_ref,
                 s0_ref, ab_ref, v_ref, oi_ref, sr_ref, yc_ref, dec_ref, p_scr, st_ref):
    @pl.when(pl.program_id(0) % tpb == 0)
    def _():
        st_ref[...] = s0_ref[...]

    subs = range(x_ref.shape[0] // PROJ_SUB)
    nc = PROJ_SUB // CHUNK
    nblk = wm_ref.shape[1] // PROJ_COLS
    factors = []
    for sub in subs:
        x = x_ref[sub * PROJ_SUB:(sub + 1) * PROJ_SUB, :]
        hn = _rms(x) * nw_ref[...] * (1.0 + sc_ref[...]) + sh_ref[...]
        hb = hn.astype(BF16)
        g = _dot(hb, wg_ref[...])
        z = _dot(g.astype(BF16), afb_ref[...]) + bias_ref[...]

        todo = iter(range(nblk))

        def wide_block():
            j = next(todo, None)
            if j is not None:
                cols = slice(j * PROJ_COLS, (j + 1) * PROJ_COLS)
                p_scr[sub, :, cols] = _dot(hb, wm_ref[:, cols])

        la = _log_sigmoid(z) * (1.0 / GATE_NORMALIZER)
        wide_block()
        b_f = _segment_scan(la[:, :GLA_QK], CHUNK, False, wide_block)
        b_b = _segment_scan(la[:, GLA_QK:], CHUNK, True, wide_block)
        b_f3 = b_f.reshape(nc, CHUNK, GLA_QK)
        tot_f = b_f3[:, CHUNK - 1:CHUNK, :]
        b_b3 = b_b.reshape(nc, CHUNK, GLA_QK)
        tot_b = b_b3[:, 0:1, :]
        fac = []
        for arg in (b_f, -b_f, (tot_f - b_f3).reshape(PROJ_SUB, GLA_QK),
                    b_b, -b_b, (tot_b - b_b3).reshape(PROJ_SUB, GLA_QK)):
            fac.append(jnp.exp(arg))
            wide_block()
        fac += [jnp.exp(tot_f).reshape(nc, GLA_QK), jnp.exp(tot_b).reshape(nc, GLA_QK)]
        factors.append(fac)
        for _ in range(nblk):
            wide_block()
    for sub in subs:
        _proj_rows(sub, factors[sub], cw_ref, ab_ref, v_ref, oi_ref, sr_ref, yc_ref,
                   dec_ref, p_scr.at[sub], st_ref)


def _proj_rows(sub, factors, cw_ref, ab_ref, v_ref, oi_ref, sr_ref, yc_ref, dec_ref, p, st_ref):
    tm = PROJ_SUB
    nc = tm // CHUNK
    rs = slice(sub * tm, (sub + 1) * tm)
    cs = slice(sub * nc, (sub + 1) * nc)
    e_f, ei_f, ee_f, e_b, ei_b, ee_b, dec_f, dec_b = factors

    q = p[:, 0:GLA_QK] * (GLA_DK ** -0.5)
    k = p[:, GLA_QK:2 * GLA_QK]
    vb = p[:, 2 * GLA_QK:2 * GLA_QK + GLA_WIDTH].astype(BF16)
    v_ref[rs, :] = vb
    qd_f = (q * e_f).astype(BF16)
    ki_f = (k * ei_f).astype(BF16)
    ke_f = (k * ee_f).astype(BF16)
    qd_b = (q * e_b).astype(BF16)
    ki_b = (k * ei_b).astype(BF16)
    ke_b = (k * ee_b).astype(BF16)

    ab_ref[rs, :GLA_QK] = qd_b
    ab_ref[rs, GLA_QK:] = ke_b
    dec_ref[cs, :GLA_QK] = dec_f
    dec_ref[cs, GLA_QK:] = dec_b

    inter_f = _chunk_scan(st_ref, qd_f, ke_f, vb, dec_f, False)

    gi = INTRA_GROUP
    ri = lax.broadcasted_iota(jnp.int32, (gi, gi), 0)
    ci = lax.broadcasted_iota(jnp.int32, (gi, gi), 1)
    same = (ri // CHUNK) == (ci // CHUNK)
    lower = same & (ci <= ri)
    upper = same & (ci >= ri)
    for gidx in range(tm // gi):
        rows = slice(gidx * gi, (gidx + 1) * gi)
        orows = slice(sub * tm + gidx * gi, sub * tm + (gidx + 1) * gi)
        for h in range(GLA_HEADS):
            lanes = slice(h * GLA_DK, (h + 1) * GLA_DK)
            s_f = _dot_nt(qd_f[rows, lanes], ki_f[rows, lanes])
            s_b = _dot_nt(qd_b[rows, lanes], ki_b[rows, lanes])
            s = jnp.where(lower, s_f, 0.0) + jnp.where(upper, s_b, 0.0)
            vl = slice(h * GLA_DV, (h + 1) * GLA_DV)
            oi_ref[orows, vl] = _dot(s.astype(BF16), vb[rows, vl]) + inter_f[rows, vl]

    r = p[:, 1024:1536]
    sr_ref[rs, :] = (r * _sigmoid(r)).astype(BF16)

    u = p[:, 2048:2560] * p[:, 2560:3072]
    row = lax.broadcasted_iota(jnp.int32, u.shape, 0) & (GRID_W - 1)
    prev = jnp.where(row >= 1, pltpu.roll(u, 1, 0), 0.0)
    nxt = jnp.where(row < GRID_W - 1, pltpu.roll(u, tm - 1, 0), 0.0)
    cw = cw_ref[...]
    conv = prev * cw[0:1, :] + u * cw[1:2, :] + nxt * cw[2:3, :]
    yc_ref[rs, :] = (p[:, 1536:2048] * conv).astype(BF16)


def _projection(x2, row0, n, mod4, nw1, wm, wg, afb, bias_fb, conv_w, s0f, seq):
    d = x2.shape[1]
    tm = TOKEN_TILE
    tpb = seq // tm
    t0 = row0 // tm
    batch = lambda i: i // tpb + row0 // seq
    full = lambda a: pl.BlockSpec(a.shape, lambda i: (0,) * a.ndim)
    tok = lambda w: pl.BlockSpec((tm, w), lambda i: (i, 0))
    st_spec = pl.BlockSpec((None, GLA_HEADS, GLA_DV, GLA_DK), lambda i: (batch(i), 0, 0, 0))
    out_shape = (
        jax.ShapeDtypeStruct((n, 2 * GLA_QK), BF16),
        jax.ShapeDtypeStruct((n, GLA_WIDTH), BF16),
        jax.ShapeDtypeStruct((n, GLA_WIDTH), F32),
        jax.ShapeDtypeStruct((n, GLA_WIDTH), BF16),
        jax.ShapeDtypeStruct((n, CONV_WIDTH), BF16),
        jax.ShapeDtypeStruct((n // CHUNK, 2 * GLA_QK), F32),
    )
    out_specs = (tok(2 * GLA_QK), tok(GLA_WIDTH), tok(GLA_WIDTH), tok(GLA_WIDTH),
                 tok(CONV_WIDTH), pl.BlockSpec((tm // CHUNK, 2 * GLA_QK), lambda i: (i, 0)))
    return pl.pallas_call(
        functools.partial(_proj_kernel, tpb),
        out_shape=out_shape,
        grid=(n // tm,),
        in_specs=[pl.BlockSpec((tm, d), lambda i: (i + t0, 0)), _mod_spec(d, 0, batch),
                  _mod_spec(d, 1, batch), full(nw1),
                  full(wm), full(wg), full(afb), full(bias_fb), full(conv_w), st_spec],
        out_specs=out_specs,
        scratch_shapes=[pltpu.VMEM((tm // PROJ_SUB, PROJ_SUB, wm.shape[1]), F32),
                        pltpu.VMEM((GLA_HEADS, GLA_DV, GLA_DK), F32)],
        compiler_params=_cparams("arbitrary"),
        name="projection",
    )(x2, mod4, mod4, nw1, wm, wg, afb, bias_fb, conv_w, s0f)


def _post_kernel(tpb, x_ref, oi_ref, ab_ref, v_ref, dec_ref, s0_ref, sr_ref, yc_ref, gnw_ref, wo_ref,
                 g1_ref, nw2_ref, sh2_ref, sc2_ref, rwt_ref, rb_ref, tri_ref,
                 h_ref, xw_ref, idx_ref, rank_ref, wt_ref, cnt_ref, run_ref, st_ref):
    @pl.when(pl.program_id(0) == 0)
    def _():
        run_ref[...] = jnp.zeros_like(run_ref)

    @pl.when(pl.program_id(0) % tpb == 0)
    def _():
        st_ref[...] = s0_ref[...]

    tm = x_ref.shape[0]
    inter_b = _chunk_scan(st_ref, ab_ref[:, :GLA_QK], ab_ref[:, GLA_QK:], v_ref[...],
                          dec_ref[:, GLA_QK:], True)
    o = oi_ref[...] + inter_b
    normed = jnp.concatenate(
        [_rms(o[:, h * GLA_DV:(h + 1) * GLA_DV]) for h in range(GLA_HEADS)], axis=1)
    y_gla = (normed * gnw_ref[...] * sr_ref[...].astype(F32)).astype(BF16)
    wo = wo_ref[...]
    y = _dot(y_gla, wo[:GLA_WIDTH, :]) + _dot(yc_ref[...], wo[GLA_WIDTH:, :])
    h = x_ref[...] + g1_ref[...] * y
    h_ref[...] = h
    hn = _rms(h) * nw2_ref[...] * (1.0 + sc2_ref[...]) + sh2_ref[...]
    xw_ref[...] = _pack_halves(hn)

    logits = _dot3(rwt_ref[...], hn, _dot_nt) + rb_ref[...]
    eid = lax.broadcasted_iota(jnp.int32, logits.shape, 0)
    vals, idxs = [], []
    work = logits
    for _ in range(TOP_K):
        m = jnp.max(work, axis=0, keepdims=True)
        i = jnp.min(jnp.where(work == m, eid, N_EXPERTS), axis=0, keepdims=True)
        vals.append(m)
        idxs.append(i)
        work = jnp.where(eid == i, -jnp.inf, work)
    exps = [jnp.exp(v - vals[0]) for v in vals]
    denom = exps[0] + exps[1] + exps[2] + exps[3]
    wts = [e / denom for e in exps]

    onehot = jnp.zeros(logits.shape, F32)
    for i in idxs:
        onehot = onehot + jnp.where(eid == i, 1.0, 0.0)
    before = _dot(onehot.astype(BF16), tri_ref[...]) + run_ref[:, 0:1]
    for kk in range(TOP_K):
        idx_ref[kk:kk + 1, :] = idxs[kk]
        rk = jnp.sum(jnp.where(eid == idxs[kk], before, 0.0), axis=0, keepdims=True)
        rank_ref[kk:kk + 1, :] = rk.astype(jnp.int32)
    run = run_ref[...] + jnp.sum(onehot, axis=1, keepdims=True)
    run_ref[...] = run
    cnt_ref[...] = run.astype(jnp.int32)

    sub = lax.broadcasted_iota(jnp.int32, (GATE_PAD, tm), 0)
    wpad = jnp.zeros((GATE_PAD, tm), F32)
    for kk in range(TOP_K):
        wpad = wpad + jnp.where(sub == kk, wts[kk], 0.0)
    wt_ref[...] = wpad.T


def _post_mixer(x2, row0, o_i, a_b, vb, dec, s0b, sr, yc, gnw, wo, mod4, nw2, rwt, rb, tri, seq):
    n, d = o_i.shape[0], x2.shape[1]
    tm = POST_TILE
    tpb = seq // tm
    t0 = row0 // tm
    rev = lambda i: (i // tpb) * tpb + (tpb - 1 - i % tpb)
    batch = lambda i: i // tpb + row0 // seq
    full = lambda a: pl.BlockSpec(a.shape, lambda i: (0,) * a.ndim)
    tok = lambda w: pl.BlockSpec((tm, w), lambda i: (rev(i), 0))
    lane_tok = pl.BlockSpec((TOP_K, tm), lambda i: (0, rev(i)))
    st_spec = pl.BlockSpec((None, GLA_HEADS, GLA_DV, GLA_DK), lambda i: (batch(i), 0, 0, 0))
    dec_spec = pl.BlockSpec((tm // CHUNK, 2 * GLA_QK), lambda i: (rev(i), 0))
    out_shape = (
        jax.ShapeDtypeStruct((n, d), F32),
        jax.ShapeDtypeStruct((n, d // 2), jnp.uint32),
        jax.ShapeDtypeStruct((TOP_K, n), jnp.int32),
        jax.ShapeDtypeStruct((TOP_K, n), jnp.int32),
        jax.ShapeDtypeStruct((n, GATE_PAD), F32),
        jax.ShapeDtypeStruct((N_EXPERTS, GATE_PAD), jnp.int32),
    )
    out_specs = (tok(d), tok(d // 2), lane_tok, lane_tok, tok(GATE_PAD),
                 pl.BlockSpec((N_EXPERTS, GATE_PAD), lambda i: (0, 0)))
    return pl.pallas_call(
        functools.partial(_post_kernel, tpb),
        out_shape=out_shape,
        grid=(n // tm,),
        in_specs=[pl.BlockSpec((tm, d), lambda i: (rev(i) + t0, 0)), tok(GLA_WIDTH),
                  tok(2 * GLA_QK), tok(GLA_WIDTH), dec_spec, st_spec, tok(GLA_WIDTH),
                  tok(CONV_WIDTH), full(gnw), full(wo), _mod_spec(d, 2, batch), full(nw2),
                  _mod_spec(d, 3, batch), _mod_spec(d, 4, batch),
                  full(rwt), full(rb), full(tri)],
        out_specs=out_specs,
        scratch_shapes=[pltpu.VMEM((N_EXPERTS, GATE_PAD), F32),
                        pltpu.VMEM((GLA_HEADS, GLA_DV, GLA_DK), F32)],
        compiler_params=_cparams("arbitrary"),
        name="post_mixer",
    )(x2, o_i, a_b, vb, dec, s0b, sr, yc, gnw, wo, mod4, nw2, mod4, mod4, rwt, rb, tri)


def _plan_kernel(cnt_ref, start_ref, te_ref, first_ref, next_ref, nu_ref):
    n_tiles = te_ref.shape[0]
    shift = EXPERT_TILE.bit_length() - 1

    def clear(t, carry):
        first_ref[t] = 0
        next_ref[t] = -1
        return carry

    lax.fori_loop(0, n_tiles, clear, 0)

    def group(e, tile0):
        nt = lax.shift_right_logical(cnt_ref[e] + (EXPERT_TILE - 1), shift)
        start_ref[e] = tile0 * EXPERT_TILE

        def fill(t, carry):
            te_ref[tile0 + t] = e
            return carry

        lax.fori_loop(0, nt, fill, 0)

        @pl.when(nt > 0)
        def _():
            first_ref[tile0] = 1

        return tile0 + nt

    n_used = lax.fori_loop(0, N_EXPERTS, group, 0)
    nu_ref[0] = n_used
    last = te_ref[n_used - 1]

    def tail(t, carry):
        te_ref[t] = last
        return carry

    lax.fori_loop(n_used, n_tiles, tail, 0)

    def link(k, nxt):
        e = N_EXPERTS - 1 - k
        has = cnt_ref[e] > 0

        @pl.when(has)
        def _():
            next_ref[lax.shift_right_logical(start_ref[e], shift)] = nxt

        return jnp.where(has, e, nxt)

    lax.fori_loop(0, N_EXPERTS, link, -1)


def _plan(counts, n_tiles):
    assert EXPERT_TILE & (EXPERT_TILE - 1) == 0
    smem = pl.BlockSpec(memory_space=pltpu.SMEM)
    vec = lambda m: jax.ShapeDtypeStruct((m,), jnp.int32)
    return pl.pallas_call(
        _plan_kernel,
        out_shape=(vec(N_EXPERTS), vec(n_tiles), vec(n_tiles), vec(n_tiles), vec(1)),
        in_specs=[smem],
        out_specs=(smem, smem, smem, smem, smem),
        name="plan",
    )(counts)


def _slot_kernel(start_ref, idx_ref, rank_ref, slot_ref):
    idx = idx_ref[...]
    base = jnp.zeros(idx.shape, jnp.int32)
    for e in range(N_EXPERTS):
        base = jnp.where(idx == e, start_ref[e], base)
    slot_ref[...] = base + rank_ref[...]


def _slots(group_start, idx_t, rank_t):
    k, n = idx_t.shape
    bn = min(n, 8192)
    blk = pl.BlockSpec((k, bn), lambda i, s: (0, i))
    return pl.pallas_call(
        _slot_kernel,
        out_shape=jax.ShapeDtypeStruct((k, n), jnp.int32),
        grid_spec=pltpu.PrefetchScalarGridSpec(
            num_scalar_prefetch=1, grid=(n // bn,), in_specs=[blk, blk], out_specs=blk),
        compiler_params=_cparams("arbitrary"),
        name="slots",
    )(group_start, idx_t, rank_t)


SC_CORES = 2
SC_SUBCORES = 16
SC_WORKERS = SC_CORES * SC_SUBCORES
SCATTER_CHUNK = 128
GATHER_CHUNK = 64


def _sc_mesh():
    return plsc.VectorSubcoreMesh(core_axis_name="c", subcore_axis_name="s")


def _sc_worker():
    return lax.axis_index("s") * SC_CORES + lax.axis_index("c")


def _sc_scatter_rows(src, idx_t, n_out):
    n, w = src.shape
    k = idx_t.shape[0]
    c = SCATTER_CHUNK
    per = n // SC_WORKERS
    nch = per // c
    assert per * SC_WORKERS == n and nch * c == per

    def body(src_hbm, idx_hbm, out_hbm, idx_v, rows_v):
        wid = _sc_worker()
        for kk in range(k):
            pltpu.sync_copy(idx_hbm.at[kk, wid], idx_v.at[kk])

        @pl.loop(0, nch)
        def _(j):
            pltpu.sync_copy(src_hbm.at[pl.ds(wid * per + j * c, c)], rows_v)
            for kk in range(k):
                pltpu.sync_copy(rows_v, out_hbm.at[idx_v.at[kk, j]])

    return pl.kernel(
        body,
        out_type=jax.ShapeDtypeStruct((n_out, w), src.dtype),
        mesh=_sc_mesh(),
        scratch_types=[pltpu.VMEM((k, nch, c), jnp.int32), pltpu.VMEM((c, w), src.dtype)],
        name="sc_dispatch",
    )(src, idx_t.reshape(k, SC_WORKERS, nch, c))


def _sc_gather_rows(table, idx):
    w = table.shape[1]
    m = idx.shape[0]
    c = GATHER_CHUNK
    per = m // SC_WORKERS
    nch = per // c
    assert per * SC_WORKERS == m and nch * c == per and nch % 2 == 0

    def body(table_hbm, idx_hbm, out_hbm, idx_v, rows_v, sem0, sem1):
        wid = _sc_worker()
        base = wid * per
        sems = (sem0, sem1)
        pltpu.sync_copy(idx_hbm.at[wid], idx_v)

        def gather(j, b):
            return pltpu.make_async_copy(table_hbm.at[idx_v.at[j]], rows_v.at[b], sems[b])

        gather(0, 0).start()

        @pl.loop(0, nch, step=2)
        def _(g):
            for b in range(2):
                j = g + b
                gather(j, b).wait()

                @pl.when(j + 1 < nch)
                def _():
                    gather(j + 1, 1 - b).start()

                pltpu.sync_copy(rows_v.at[b], out_hbm.at[pl.ds(base + j * c, c)])

    return pl.kernel(
        body,
        out_type=jax.ShapeDtypeStruct((m, w), table.dtype),
        mesh=_sc_mesh(),
        scratch_types=[pltpu.VMEM((nch, c), jnp.int32), pltpu.VMEM((2, c, w), table.dtype),
                       pltpu.SemaphoreType.DMA, pltpu.SemaphoreType.DMA],
        name="sc_combine_gather",
    )(table, idx.reshape(SC_WORKERS, nch, c))


PERM_BLOCK = 256


def _expert_kernel(te_ref, first_ref, next_ref, nu_ref, x_ref, wgu_hbm, wdn_hbm, perm_ref, bg_ref,
                   bu_ref, bd_ref, y_ref, wgu_f, wdn_f, wg_s, wu_s, wd_s, sem):
    i = pl.program_id(0)
    used = i < nu_ref[0]

    def fetch(e):
        return (pltpu.make_async_copy(wgu_hbm.at[e], wgu_f, sem.at[0]),
                pltpu.make_async_copy(wdn_hbm.at[e], wdn_f, sem.at[1]))

    @pl.when(i == 0)
    def _():
        for cp in fetch(te_ref[0]):
            cp.start()

    @pl.when(used & (first_ref[i] == 1))
    def _():
        for cp in fetch(te_ref[i]):
            cp.wait()
        half = PERM_BLOCK // 2
        perm = perm_ref[...]
        for b in range(wgu_f.shape[1] // PERM_BLOCK):
            blk = wgu_f[:, b * PERM_BLOCK:(b + 1) * PERM_BLOCK].astype(BF16)
            t = _dot(blk, perm)
            wg_s[:, b * half:(b + 1) * half] = t[:, :half].astype(BF16)
            wu_s[:, b * half:(b + 1) * half] = t[:, half:].astype(BF16)
        wd_s[...] = wdn_f[...].astype(BF16)

        @pl.when(next_ref[i] >= 0)
        def _():
            for cp in fetch(next_ref[i]):
                cp.start()

    @pl.when(used)
    def _():
        x = _unpack_halves(x_ref[...]).astype(BF16)
        g = _dot(x, wg_s[...]) + bg_ref[...]
        u = _dot(x, wu_s[...]) + bu_ref[...]
        gate = jnp.minimum(g, SWIGLU_LIMIT)
        up = jnp.clip(u, -SWIGLU_LIMIT, SWIGLU_LIMIT)
        act = (up + 1.0) * (gate * _sigmoid(gate, SWIGLU_ALPHA))
        y = _dot(act.astype(BF16), wd_s[...]) + bd_ref[...]
        y_ref[...] = _pack_halves(y)

    @pl.when(jnp.logical_not(used))
    def _():
        y_ref[...] = jnp.zeros_like(y_ref)


def _experts(tile_expert, tile_first, tile_next, n_used, xs, w_gu, w_dn, bg, bu, bd):
    n_slots, half = xs.shape
    tm = EXPERT_TILE
    n_tiles = n_slots // tm
    dm, f, d = w_gu.shape[1], w_dn.shape[1], w_dn.shape[2]
    ph = PERM_BLOCK // 2
    j = jnp.arange(PERM_BLOCK)
    perm = (jnp.arange(PERM_BLOCK)[None, :] == (j // 2 + (j % 2) * ph)[:, None]).astype(BF16)
    row_in = pl.BlockSpec((tm, half), lambda i, te, tf, tn, nu: (jnp.minimum(i, nu[0] - 1), 0))
    row_out = pl.BlockSpec((tm, half),
                           lambda i, te, tf, tn, nu: (jnp.where(i < nu[0], i, n_tiles - 1), 0))
    bspec = lambda a: pl.BlockSpec((None,) + a.shape[1:], lambda i, te, tf, tn, nu: (te[i], 0, 0))
    return pl.pallas_call(
        _expert_kernel,
        out_shape=jax.ShapeDtypeStruct((n_slots, half), jnp.uint32),
        grid_spec=pltpu.PrefetchScalarGridSpec(
            num_scalar_prefetch=4, grid=(n_tiles,),
            in_specs=[row_in, pl.BlockSpec(memory_space=pl.ANY), pl.BlockSpec(memory_space=pl.ANY),
                      pl.BlockSpec((PERM_BLOCK, PERM_BLOCK), lambda i, te, tf, tn, nu: (0, 0)),
                      bspec(bg), bspec(bu), bspec(bd)],
            out_specs=row_out,
            scratch_shapes=[pltpu.VMEM((dm, 2 * f), F32), pltpu.VMEM((f, d), F32),
                            pltpu.VMEM((dm, f), BF16), pltpu.VMEM((dm, f), BF16),
                            pltpu.VMEM((f, d), BF16), pltpu.SemaphoreType.DMA((2,))]),
        compiler_params=_cparams("arbitrary"),
        name="experts",
    )(tile_expert, tile_first, tile_next, n_used, xs, w_gu, w_dn, perm, bg, bu, bd)


def _combine_kernel(y4_ref, h_ref, wt_ref, g2_ref, fnw_ref, *rest):
    o_ref = rest[-1]
    wt = wt_ref[...]
    acc = jnp.zeros(h_ref.shape, F32)
    for kk in range(TOP_K):
        acc = acc + wt[:, kk:kk + 1] * _unpack_halves(y4_ref[kk])
    h = h_ref[...] + g2_ref[...] * acc
    o_ref[...] = _rms(h) * fnw_ref[...]


def _combine(y4, h, wt, mod4, fnw, seq, row0, n_total, prev_out):
    n, d = h.shape
    tt = COMBINE_TILE
    tpb = seq // tt
    t0 = row0 // tt
    in_specs = [pl.BlockSpec((TOP_K, tt, d // 2), lambda i: (0, i, 0)),
                pl.BlockSpec((tt, d), lambda i: (i, 0)),
                pl.BlockSpec((tt, GATE_PAD), lambda i: (i, 0)),
                _mod_spec(d, 5, lambda i: i // tpb + row0 // seq),
                pl.BlockSpec((1, d), lambda i: (0, 0))]
    args = [y4, h, wt, mod4, fnw]
    aliases = {}
    if prev_out is not None:
        in_specs.append(pl.BlockSpec(memory_space=pl.ANY))
        args.append(prev_out)
        aliases = {len(args) - 1: 0}
    return pl.pallas_call(
        _combine_kernel,
        out_shape=jax.ShapeDtypeStruct((n_total, d), F32),
        grid=(n // tt,),
        in_specs=in_specs,
        out_specs=pl.BlockSpec((tt, d), lambda i: (i + t0, 0)),
        input_output_aliases=aliases,
        compiler_params=_cparams("arbitrary"),
        name="combine",
    )(*args)


def kernel(x, c, ctx, c_ctx, w_ada, b_ada, norm1_w, w_in, a_up_f, a_bias_f, a_up_b, a_bias_b,
           gla_norm_w, conv_w, w_out, norm2_w, router_w, router_b, w_gu, b_gu, w_dn, b_dn,
           final_norm_w):
    bsz, seq, d = x.shape
    n = bsz * seq
    assert w_ada.shape[0] == 1 and d == D_MODEL
    assert seq % TOKEN_TILE == 0 and seq % POST_TILE == 0 and seq % GRID_W == 0
    assert ctx.shape[1] & (ctx.shape[1] - 1) == 0

    w_in0 = w_in[0]
    v_end = 2 * GLA_QK + GLA_WIDTH
    r_end = v_end + GLA_WIDTH
    g_end = r_end + 2 * GATE_RANK
    wm = jnp.concatenate([w_in0[:, :r_end], w_in0[:, g_end:]], axis=1).astype(BF16)
    wg_low = jnp.pad(w_in0[:, r_end:g_end], ((0, 0), (0, GATE_PAD - 2 * GATE_RANK))).astype(BF16)
    afb = jnp.zeros((GATE_PAD, 2 * GLA_QK), F32)
    afb = afb.at[:GATE_RANK, :GLA_QK].set(a_up_f[0]).at[GATE_RANK:2 * GATE_RANK, GLA_QK:].set(a_up_b[0])
    afb = afb.astype(BF16)
    bias_fb = jnp.concatenate([a_bias_f[0], a_bias_b[0]])[None]
    wkv = wm[:, GLA_QK:v_end]
    row = lambda a: a.reshape(1, -1)

    mod = _modulation(jnp.concatenate([c, c_ctx[None]], axis=0), w_ada[0], row(b_ada[0]))
    mod4 = mod.reshape(bsz + 1, 6, 1, d)

    s0f, s0b = _context_states(ctx, mod4, row(norm1_w[0]), wkv, wg_low, afb, bias_fb)

    x2 = x.reshape(n, d)
    tri = jnp.triu(jnp.ones((POST_TILE, POST_TILE), BF16), k=1)
    gnw = jnp.tile(gla_norm_w[0], GLA_HEADS)[None]
    wo = w_out[0].astype(BF16)
    bg_e, bu_e, bd_e = b_gu[0][:, None, 0::2], b_gu[0][:, None, 1::2], b_dn[0][:, None, :]

    if bsz % 4 == 0:
        chunk_batches = (3 * bsz // 4, bsz // 4)
    elif bsz % 2 == 0:
        chunk_batches = (bsz // 2, bsz // 2)
    else:
        chunk_batches = (bsz,)
    routed = []
    b0 = 0
    for cb in chunk_batches:
        row0 = b0 * seq
        b0 += cb
        nc = cb * seq
        n_tiles = nc * TOP_K // EXPERT_TILE + N_EXPERTS
        a_b, vb, o_i, sr, yc, dec = _projection(
            x2, row0, nc, mod4, row(norm1_w[0]), wm, wg_low, afb, bias_fb, conv_w[0], s0f, seq)
        h, xw, idx_t, rank_t, wt, cnt = _post_mixer(
            x2, row0, o_i, a_b, vb, dec, s0b, sr, yc, gnw, wo, mod4, row(norm2_w[0]),
            router_w[0].T, router_b[0][:, None], tri, seq)

        group_start, tile_expert, tile_first, tile_next, n_used = _plan(cnt[:, 0], n_tiles)
        slot_t = _slots(group_start, idx_t, rank_t)
        xs = _sc_scatter_rows(xw, slot_t, n_tiles * EXPERT_TILE)
        routed.append((h, wt, slot_t.reshape(-1), xs, tile_expert, tile_first, tile_next, n_used, row0))

    gathered = []
    for h, wt, slot_flat, xs, tile_expert, tile_first, tile_next, n_used, row0 in routed:
        ys = _experts(tile_expert, tile_first, tile_next, n_used, xs, w_gu[0], w_dn[0],
                      bg_e, bu_e, bd_e)
        y4 = _sc_gather_rows(ys, slot_flat).reshape(TOP_K, h.shape[0], d // 2)
        gathered.append((y4, h, wt, row0))

    out = None
    for y4, h, wt, row0 in gathered:
        out = _combine(y4, h, wt, mod4, row(final_norm_w), seq, row0, n, out)
    return out.reshape(bsz, seq, d)
```

```python
import functools

import jax
import jax.numpy as jnp
from jax import lax
from jax.experimental import pallas as pl
from jax.experimental.pallas import tpu as pltpu
from jax.experimental.pallas import tpu_sc as plsc

D_MODEL = 1024
GLA_HEADS = 4
GLA_DK = 64
GLA_DV = 128
GLA_QK = GLA_HEADS * GLA_DK
GLA_WIDTH = GLA_HEADS * GLA_DV
CONV_WIDTH = 512
GATE_RANK = 16
GATE_PAD = 128
GATE_NORMALIZER = 16.0
CHUNK = 64
GRID_W = 64
N_EXPERTS = 32
TOP_K = 4
SWIGLU_LIMIT = 7.0
SWIGLU_ALPHA = 1.702
EPS = 1e-6

TOKEN_TILE = 1024
POST_TILE = 1024
PROJ_SUB = 1024
PROJ_COLS = 1024
INTRA_GROUP = 256
EXPERT_TILE = 512
COMBINE_TILE = 1024
VMEM_LIMIT = 56 * 1024 * 1024

F32 = jnp.float32
BF16 = jnp.bfloat16
HI_MASK = 0xFFFF0000
LOG2_E = 1.4426950408889634


def _cparams(*sem):
    return pltpu.CompilerParams(dimension_semantics=sem, vmem_limit_bytes=VMEM_LIMIT)


def _mod_spec(d, component, batch_of_step):
    return pl.BlockSpec((None, None, 1, d), lambda i: (batch_of_step(i), component, 0, 0))


def _rms(x):
    return x * lax.rsqrt(jnp.mean(x * x, axis=-1, keepdims=True) + EPS)


def _sigmoid(x, scale=1.0):
    return 1.0 / (1.0 + jnp.exp2(x * (-scale * LOG2_E)))


def _log_sigmoid(z):
    return jnp.minimum(z, 0.0) - jnp.log(1.0 + jnp.exp(-jnp.abs(z)))


def _dot(a, b):
    return jnp.dot(a, b, preferred_element_type=F32)


def _dot_nt(a, b):
    return lax.dot_general(a, b, (((1,), (1,)), ((), ())), preferred_element_type=F32)


def _dot_tn(a, b):
    return lax.dot_general(a, b, (((0,), (0,)), ((), ())), preferred_element_type=F32)


def _split_bf16(a):
    hi = a.astype(BF16)
    lo = (a - hi.astype(F32)).astype(BF16)
    return hi, lo


def _dot3(a, b, dot):
    a_hi, a_lo = _split_bf16(a)
    b_hi, b_lo = _split_bf16(b)
    return dot(a_hi, b_hi) + (dot(a_hi, b_lo) + dot(a_lo, b_hi))


def _segment_scan(x, seg, reverse, after_step=None):
    n = x.shape[0]
    row = lax.broadcasted_iota(jnp.int32, x.shape, 0) & (seg - 1)
    s = 1
    while s < seg:
        if reverse:
            shifted = pltpu.roll(x, n - s, 0)
            x = x + jnp.where(row < seg - s, shifted, 0.0)
        else:
            shifted = pltpu.roll(x, s, 0)
            x = x + jnp.where(row >= s, shifted, 0.0)
        if after_step is not None:
            after_step()
        s *= 2
    return x


def _pack_halves(x):
    n = x.shape[1] // 2
    bits = pltpu.bitcast(x.astype(BF16).astype(F32), jnp.uint32)
    return (bits[:, :n] & jnp.uint32(HI_MASK)) | (bits[:, n:] >> 16)


def _unpack_halves(w):
    hi = pltpu.bitcast(w & jnp.uint32(HI_MASK), F32)
    lo = pltpu.bitcast(w << 16, F32)
    return jnp.concatenate([hi, lo], axis=1)


def _mod_kernel(c_ref, w_ref, b_ref, o_ref):
    c = c_ref[...]
    s = c * _sigmoid(c)
    o_ref[...] = _dot3(s, w_ref[...], _dot) + b_ref[...]


def _modulation(cc, w_ada, b_ada):
    rows, d = cc.shape
    n = w_ada.shape[1]
    bn = 1536
    return pl.pallas_call(
        _mod_kernel,
        out_shape=jax.ShapeDtypeStruct((rows, n), F32),
        grid=(n // bn,),
        in_specs=[pl.BlockSpec((rows, d), lambda j: (0, 0)),
                  pl.BlockSpec((d, bn), lambda j: (0, j)),
                  pl.BlockSpec((1, bn), lambda j: (0, j))],
        out_specs=pl.BlockSpec((rows, bn), lambda j: (0, j)),
        compiler_params=_cparams("arbitrary"),
        name="modulation",
    )(cc, w_ada, b_ada)


def _ctx_kernel(x_ref, sh_ref, sc_ref, nw_ref, wkv_ref, wg_ref, afb_ref, bias_ref, sf_ref, sb_ref):
    x = x_ref[...]
    n = x.shape[0]
    hn = _rms(x) * nw_ref[...] * (1.0 + sc_ref[...]) + sh_ref[...]
    hb = hn.astype(BF16)
    kv = _dot(hb, wkv_ref[...])
    g = _dot(hb, wg_ref[...])
    z = _dot(g.astype(BF16), afb_ref[...]) + bias_ref[...]
    la = _log_sigmoid(z) * (1.0 / GATE_NORMALIZER)
    p = _segment_scan(la, n, False)
    p_f, p_b = p[:, :GLA_QK], p[:, GLA_QK:]
    w_f = jnp.exp(p_f[n - 1:n, :] - p_f)
    w_b = jnp.exp(p_b - la[:, GLA_QK:])
    k = kv[:, :GLA_QK]
    vb = kv[:, GLA_QK:].astype(BF16)
    ke_f = (k * w_f).astype(BF16)
    ke_b = (k * w_b).astype(BF16)
    for h in range(GLA_HEADS):
        vh = vb[:, h * GLA_DV:(h + 1) * GLA_DV]
        sf_ref[h] = _dot_tn(vh, ke_f[:, h * GLA_DK:(h + 1) * GLA_DK])
        sb_ref[h] = _dot_tn(vh, ke_b[:, h * GLA_DK:(h + 1) * GLA_DK])


def _context_states(ctx, mod4, nw1, wkv, wg, afb, bias_fb):
    bsz, n, d = ctx.shape
    ctx_row = lambda b: bsz
    full = lambda a: pl.BlockSpec(a.shape, lambda b: (0,) * a.ndim)
    st = jax.ShapeDtypeStruct((bsz, GLA_HEADS, GLA_DV, GLA_DK), F32)
    st_spec = pl.BlockSpec((None, GLA_HEADS, GLA_DV, GLA_DK), lambda b: (b, 0, 0, 0))
    return pl.pallas_call(
        _ctx_kernel,
        out_shape=(st, st),
        grid=(bsz,),
        in_specs=[pl.BlockSpec((None, n, d), lambda b: (b, 0, 0)),
                  _mod_spec(d, 0, ctx_row), _mod_spec(d, 1, ctx_row), full(nw1), full(wkv), full(wg),
                  full(afb), full(bias_fb)],
        out_specs=(st_spec, st_spec),
        compiler_params=_cparams("arbitrary"),
        name="context_states",
    )(ctx, mod4, mod4, nw1, wkv, wg, afb, bias_fb)


def _chunk_scan(st_ref, qd, ke, v, dec, descending):
    nc = qd.shape[0] // CHUNK
    order = range(nc - 1, -1, -1) if descending else range(nc)
    rows = [slice(c * CHUNK, (c + 1) * CHUNK) for c in range(nc)]
    outs = []
    for h in range(GLA_HEADS):
        lanes = slice(h * GLA_DK, (h + 1) * GLA_DK)
        vl = slice(h * GLA_DV, (h + 1) * GLA_DV)
        kv = [_dot_tn(v[rows[c], vl], ke[rows[c], lanes]) for c in range(nc)]
        s = st_ref[h]
        start = [None] * nc
        for c in order:
            start[c] = s
            s = s * dec[c:c + 1, lanes] + kv[c]
        st_ref[h] = s
        outs.append(jnp.concatenate(
            [_dot_nt(qd[rows[c], lanes], start[c].astype(BF16)) for c in range(nc)], axis=0))
    return jnp.concatenate(outs, axis=1)


def _proj_kernel(tpb, x_ref, sh_ref, sc_ref, nw_ref, wm_ref, wg_ref, afb_ref, bias_ref, cw_ref,
                 s0_ref, ab_ref, v_ref, oi_ref, sr_ref, yc_ref, dec_ref, p_scr, st_ref):
    @pl.when(pl.program_id(0) % tpb == 0)
    def _():
        st_ref[...] = s0_ref[...]

    subs = range(x_ref.shape[0] // PROJ_SUB)
    nc = PROJ_SUB // CHUNK
    nblk = wm_ref.shape[1] // PROJ_COLS
    factors = []
    for sub in subs:
        x = x_ref[sub * PROJ_SUB:(sub + 1) * PROJ_SUB, :]
        hn = _rms(x) * nw_ref[...] * (1.0 + sc_ref[...]) + sh_ref[...]
        hb = hn.astype(BF16)
        g = _dot(hb, wg_ref[...])
        z = _dot(g.astype(BF16), afb_ref[...]) + bias_ref[...]

        todo = iter(range(nblk))

        def wide_block():
            j = next(todo, None)
            if j is not None:
                cols = slice(j * PROJ_COLS, (j + 1) * PROJ_COLS)
                p_scr[sub, :, cols] = _dot(hb, wm_ref[:, cols])

        la = _log_sigmoid(z) * (1.0 / GATE_NORMALIZER)
        wide_block()
        b_f = _segment_scan(la[:, :GLA_QK], CHUNK, False, wide_block)
        b_b = _segment_scan(la[:, GLA_QK:], CHUNK, True, wide_block)
        b_f3 = b_f.reshape(nc, CHUNK, GLA_QK)
        tot_f = b_f3[:, CHUNK - 1:CHUNK, :]
        b_b3 = b_b.reshape(nc, CHUNK, GLA_QK)
        tot_b = b_b3[:, 0:1, :]
        fac = []
        for arg in (b_f, -b_f, (tot_f - b_f3).reshape(PROJ_SUB, GLA_QK),
                    b_b, -b_b, (tot_b - b_b3).reshape(PROJ_SUB, GLA_QK)):
            fac.append(jnp.exp(arg))
            wide_block()
        fac += [jnp.exp(tot_f).reshape(nc, GLA_QK), jnp.exp(tot_b).reshape(nc, GLA_QK)]
        factors.append(fac)
        for _ in range(nblk):
            wide_block()
    for sub in subs:
        _proj_rows(sub, factors[sub], cw_ref, ab_ref, v_ref, oi_ref, sr_ref, yc_ref,
                   dec_ref, p_scr.at[sub], st_ref)


def _proj_rows(sub, factors, cw_ref, ab_ref, v_ref, oi_ref, sr_ref, yc_ref, dec_ref, p, st_ref):
    tm = PROJ_SUB
    nc = tm // CHUNK
    rs = slice(sub * tm, (sub + 1) * tm)
    cs = slice(sub * nc, (sub + 1) * nc)
    e_f, ei_f, ee_f, e_b, ei_b, ee_b, dec_f, dec_b = factors

    q = p[:, 0:GLA_QK] * (GLA_DK ** -0.5)
    k = p[:, GLA_QK:2 * GLA_QK]
    vb = p[:, 2 * GLA_QK:2 * GLA_QK + GLA_WIDTH].astype(BF16)
    v_ref[rs, :] = vb
    qd_f = (q * e_f).astype(BF16)
    ki_f = (k * ei_f).astype(BF16)
    ke_f = (k * ee_f).astype(BF16)
    qd_b = (q * e_b).astype(BF16)
    ki_b = (k * ei_b).astype(BF16)
    ke_b = (k * ee_b).astype(BF16)

    ab_ref[rs, :GLA_QK] = qd_b
    ab_ref[rs, GLA_QK:] = ke_b
    dec_ref[cs, :GLA_QK] = dec_f
    dec_ref[cs, GLA_QK:] = dec_b

    inter_f = _chunk_scan(st_ref, qd_f, ke_f, vb, dec_f, False)

    gi = INTRA_GROUP
    ri = lax.broadcasted_iota(jnp.int32, (gi, gi), 0)
    ci = lax.broadcasted_iota(jnp.int32, (gi, gi), 1)
    same = (ri // CHUNK) == (ci // CHUNK)
    lower = same & (ci <= ri)
    upper = same & (ci >= ri)
    for gidx in range(tm // gi):
        rows = slice(gidx * gi, (gidx + 1) * gi)
        orows = slice(sub * tm + gidx * gi, sub * tm + (gidx + 1) * gi)
        for h in range(GLA_HEADS):
            lanes = slice(h * GLA_DK, (h + 1) * GLA_DK)
            s_f = _dot_nt(qd_f[rows, lanes], ki_f[rows, lanes])
            s_b = _dot_nt(qd_b[rows, lanes], ki_b[rows, lanes])
            s = jnp.where(lower, s_f, 0.0) + jnp.where(upper, s_b, 0.0)
            vl = slice(h * GLA_DV, (h + 1) * GLA_DV)
            oi_ref[orows, vl] = _dot(s.astype(BF16), vb[rows, vl]) + inter_f[rows, vl]

    r = p[:, 1024:1536]
    sr_ref[rs, :] = (r * _sigmoid(r)).astype(BF16)

    u = p[:, 2048:2560] * p[:, 2560:3072]
    row = lax.broadcasted_iota(jnp.int32, u.shape, 0) & (GRID_W - 1)
    prev = jnp.where(row >= 1, pltpu.roll(u, 1, 0), 0.0)
    nxt = jnp.where(row < GRID_W - 1, pltpu.roll(u, tm - 1, 0), 0.0)
    cw = cw_ref[...]
    conv = prev * cw[0:1, :] + u * cw[1:2, :] + nxt * cw[2:3, :]
    yc_ref[rs, :] = (p[:, 1536:2048] * conv).astype(BF16)


def _projection(x2, row0, n, mod4, nw1, wm, wg, afb, bias_fb, conv_w, s0f, seq):
    d = x2.shape[1]
    tm = TOKEN_TILE
    tpb = seq // tm
    t0 = row0 // tm
    batch = lambda i: i // tpb + row0 // seq
    full = lambda a: pl.BlockSpec(a.shape, lambda i: (0,) * a.ndim)
    tok = lambda w: pl.BlockSpec((tm, w), lambda i: (i, 0))
    st_spec = pl.BlockSpec((None, GLA_HEADS, GLA_DV, GLA_DK), lambda i: (batch(i), 0, 0, 0))
    out_shape = (
        jax.ShapeDtypeStruct((n, 2 * GLA_QK), BF16),
        jax.ShapeDtypeStruct((n, GLA_WIDTH), BF16),
        jax.ShapeDtypeStruct((n, GLA_WIDTH), F32),
        jax.ShapeDtypeStruct((n, GLA_WIDTH), BF16),
        jax.ShapeDtypeStruct((n, CONV_WIDTH), BF16),
        jax.ShapeDtypeStruct((n // CHUNK, 2 * GLA_QK), F32),
    )
    out_specs = (tok(2 * GLA_QK), tok(GLA_WIDTH), tok(GLA_WIDTH), tok(GLA_WIDTH),
                 tok(CONV_WIDTH), pl.BlockSpec((tm // CHUNK, 2 * GLA_QK), lambda i: (i, 0)))
    return pl.pallas_call(
        functools.partial(_proj_kernel, tpb),
        out_shape=out_shape,
        grid=(n // tm,),
        in_specs=[pl.BlockSpec((tm, d), lambda i: (i + t0, 0)), _mod_spec(d, 0, batch),
                  _mod_spec(d, 1, batch), full(nw1),
                  full(wm), full(wg), full(afb), full(bias_fb), full(conv_w), st_spec],
        out_specs=out_specs,
        scratch_shapes=[pltpu.VMEM((tm // PROJ_SUB, PROJ_SUB, wm.shape[1]), F32),
                        pltpu.VMEM((GLA_HEADS, GLA_DV, GLA_DK), F32)],
        compiler_params=_cparams("arbitrary"),
        name="projection",
    )(x2, mod4, mod4, nw1, wm, wg, afb, bias_fb, conv_w, s0f)


def _post_kernel(tpb, x_ref, oi_ref, ab_ref, v_ref, dec_ref, s0_ref, sr_ref, yc_ref, gnw_ref, wo_ref,
                 g1_ref, nw2_ref, sh2_ref, sc2_ref, rwt_ref, rb_ref, tri_ref,
                 h_ref, xw_ref, idx_ref, rank_ref, wt_ref, cnt_ref, run_ref, st_ref):
    @pl.when(pl.program_id(0) == 0)
    def _():
        run_ref[...] = jnp.zeros_like(run_ref)

    @pl.when(pl.program_id(0) % tpb == 0)
    def _():
        st_ref[...] = s0_ref[...]

    tm = x_ref.shape[0]
    inter_b = _chunk_scan(st_ref, ab_ref[:, :GLA_QK], ab_ref[:, GLA_QK:], v_ref[...],
                          dec_ref[:, GLA_QK:], True)
    o = oi_ref[...] + inter_b
    normed = jnp.concatenate(
        [_rms(o[:, h * GLA_DV:(h + 1) * GLA_DV]) for h in range(GLA_HEADS)], axis=1)
    y_gla = (normed * gnw_ref[...] * sr_ref[...].astype(F32)).astype(BF16)
    wo = wo_ref[...]
    y = _dot(y_gla, wo[:GLA_WIDTH, :]) + _dot(yc_ref[...], wo[GLA_WIDTH:, :])
    h = x_ref[...] + g1_ref[...] * y
    h_ref[...] = h
    hn = _rms(h) * nw2_ref[...] * (1.0 + sc2_ref[...]) + sh2_ref[...]
    xw_ref[...] = _pack_halves(hn)

    logits = _dot3(rwt_ref[...], hn, _dot_nt) + rb_ref[...]
    eid = lax.broadcasted_iota(jnp.int32, logits.shape, 0)
    vals, idxs = [], []
    work = logits
    for _ in range(TOP_K):
        m = jnp.max(work, axis=0, keepdims=True)
        i = jnp.min(jnp.where(work == m, eid, N_EXPERTS), axis=0, keepdims=True)
        vals.append(m)
        idxs.append(i)
        work = jnp.where(eid == i, -jnp.inf, work)
    exps = [jnp.exp(v - vals[0]) for v in vals]
    denom = exps[0] + exps[1] + exps[2] + exps[3]
    wts = [e / denom for e in exps]

    onehot = jnp.zeros(logits.shape, F32)
    for i in idxs:
        onehot = onehot + jnp.where(eid == i, 1.0, 0.0)
    before = _dot(onehot.astype(BF16), tri_ref[...]) + run_ref[:, 0:1]
    for kk in range(TOP_K):
        idx_ref[kk:kk + 1, :] = idxs[kk]
        rk = jnp.sum(jnp.where(eid == idxs[kk], before, 0.0), axis=0, keepdims=True)
        rank_ref[kk:kk + 1, :] = rk.astype(jnp.int32)
    run = run_ref[...] + jnp.sum(onehot, axis=1, keepdims=True)
    run_ref[...] = run
    cnt_ref[...] = run.astype(jnp.int32)

    sub = lax.broadcasted_iota(jnp.int32, (GATE_PAD, tm), 0)
    wpad = jnp.zeros((GATE_PAD, tm), F32)
    for kk in range(TOP_K):
        wpad = wpad + jnp.where(sub == kk, wts[kk], 0.0)
    wt_ref[...] = wpad.T


def _post_mixer(x2, row0, o_i, a_b, vb, dec, s0b, sr, yc, gnw, wo, mod4, nw2, rwt, rb, tri, seq):
    n, d = o_i.shape[0], x2.shape[1]
    tm = POST_TILE
    tpb = seq // tm
    t0 = row0 // tm
    rev = lambda i: (i // tpb) * tpb + (tpb - 1 - i % tpb)
    batch = lambda i: i // tpb + row0 // seq
    full = lambda a: pl.BlockSpec(a.shape, lambda i: (0,) * a.ndim)
    tok = lambda w: pl.BlockSpec((tm, w), lambda i: (rev(i), 0))
    lane_tok = pl.BlockSpec((TOP_K, tm), lambda i: (0, rev(i)))
    st_spec = pl.BlockSpec((None, GLA_HEADS, GLA_DV, GLA_DK), lambda i: (batch(i), 0, 0, 0))
    dec_spec = pl.BlockSpec((tm // CHUNK, 2 * GLA_QK), lambda i: (rev(i), 0))
    out_shape = (
        jax.ShapeDtypeStruct((n, d), F32),
        jax.ShapeDtypeStruct((n, d // 2), jnp.uint32),
        jax.ShapeDtypeStruct((TOP_K, n), jnp.int32),
        jax.ShapeDtypeStruct((TOP_K, n), jnp.int32),
        jax.ShapeDtypeStruct((n, GATE_PAD), F32),
        jax.ShapeDtypeStruct((N_EXPERTS, GATE_PAD), jnp.int32),
    )
    out_specs = (tok(d), tok(d // 2), lane_tok, lane_tok, tok(GATE_PAD),
                 pl.BlockSpec((N_EXPERTS, GATE_PAD), lambda i: (0, 0)))
    return pl.pallas_call(
        functools.partial(_post_kernel, tpb),
        out_shape=out_shape,
        grid=(n // tm,),
        in_specs=[pl.BlockSpec((tm, d), lambda i: (rev(i) + t0, 0)), tok(GLA_WIDTH),
                  tok(2 * GLA_QK), tok(GLA_WIDTH), dec_spec, st_spec, tok(GLA_WIDTH),
                  tok(CONV_WIDTH), full(gnw), full(wo), _mod_spec(d, 2, batch), full(nw2),
                  _mod_spec(d, 3, batch), _mod_spec(d, 4, batch),
                  full(rwt), full(rb), full(tri)],
        out_specs=out_specs,
        scratch_shapes=[pltpu.VMEM((N_EXPERTS, GATE_PAD), F32),
                        pltpu.VMEM((GLA_HEADS, GLA_DV, GLA_DK), F32)],
        compiler_params=_cparams("arbitrary"),
        name="post_mixer",
    )(x2, o_i, a_b, vb, dec, s0b, sr, yc, gnw, wo, mod4, nw2, mod4, mod4, rwt, rb, tri)


def _plan_kernel(cnt_ref, start_ref, te_ref, first_ref, next_ref, par_ref, nu_ref, second_ref):
    n_tiles = te_ref.shape[0]
    shift = EXPERT_TILE.bit_length() - 1

    def clear(t, carry):
        first_ref[t] = 0
        next_ref[t] = -1
        par_ref[t] = 0
        return carry

    lax.fori_loop(0, n_tiles, clear, 0)

    def group(e, carry):
        tile0, ordinal = carry
        nt = lax.shift_right_logical(cnt_ref[e] + (EXPERT_TILE - 1), shift)
        start_ref[e] = tile0 * EXPERT_TILE

        def fill(t, c):
            te_ref[tile0 + t] = e
            return c

        lax.fori_loop(0, nt, fill, 0)

        @pl.when(nt > 0)
        def _():
            first_ref[tile0] = 1
            par_ref[tile0] = ordinal & 1

        return tile0 + nt, ordinal + jnp.where(nt > 0, 1, 0)

    n_used, _ = lax.fori_loop(0, N_EXPERTS, group, (0, 0))
    nu_ref[0] = n_used
    last = te_ref[n_used - 1]

    def tail(t, carry):
        te_ref[t] = last
        return carry

    lax.fori_loop(n_used, n_tiles, tail, 0)

    def link(k, carry):
        nxt, nxt2 = carry
        e = N_EXPERTS - 1 - k
        has = cnt_ref[e] > 0

        @pl.when(has)
        def _():
            next_ref[lax.shift_right_logical(start_ref[e], shift)] = nxt2

        return jnp.where(has, e, nxt), jnp.where(has, nxt, nxt2)

    _, second = lax.fori_loop(0, N_EXPERTS, link, (-1, -1))
    second_ref[0] = second


def _plan(counts, n_tiles):
    assert EXPERT_TILE & (EXPERT_TILE - 1) == 0
    smem = pl.BlockSpec(memory_space=pltpu.SMEM)
    vec = lambda m: jax.ShapeDtypeStruct((m,), jnp.int32)
    return pl.pallas_call(
        _plan_kernel,
        out_shape=(vec(N_EXPERTS), vec(n_tiles), vec(n_tiles), vec(n_tiles), vec(n_tiles), vec(1),
                   vec(1)),
        in_specs=[smem],
        out_specs=(smem,) * 7,
        name="plan",
    )(counts)


def _slot_kernel(start_ref, idx_ref, rank_ref, slot_ref):
    idx = idx_ref[...]
    base = jnp.zeros(idx.shape, jnp.int32)
    for e in range(N_EXPERTS):
        base = jnp.where(idx == e, start_ref[e], base)
    slot_ref[...] = base + rank_ref[...]


def _slots(group_start, idx_t, rank_t):
    k, n = idx_t.shape
    bn = min(n, 8192)
    blk = pl.BlockSpec((k, bn), lambda i, s: (0, i))
    return pl.pallas_call(
        _slot_kernel,
        out_shape=jax.ShapeDtypeStruct((k, n), jnp.int32),
        grid_spec=pltpu.PrefetchScalarGridSpec(
            num_scalar_prefetch=1, grid=(n // bn,), in_specs=[blk, blk], out_specs=blk),
        compiler_params=_cparams("arbitrary"),
        name="slots",
    )(group_start, idx_t, rank_t)


SC_CORES = 2
SC_SUBCORES = 16
SC_WORKERS = SC_CORES * SC_SUBCORES
SCATTER_CHUNK = 128
GATHER_CHUNK = 64


def _sc_mesh():
    return plsc.VectorSubcoreMesh(core_axis_name="c", subcore_axis_name="s")


def _sc_worker():
    return lax.axis_index("s") * SC_CORES + lax.axis_index("c")


def _sc_scatter_rows(src, idx_t, n_out):
    n, w = src.shape
    k = idx_t.shape[0]
    c = SCATTER_CHUNK
    per = n // SC_WORKERS
    nch = per // c
    assert per * SC_WORKERS == n and nch * c == per

    def body(src_hbm, idx_hbm, out_hbm, idx_v, rows_v):
        wid = _sc_worker()
        for kk in range(k):
            pltpu.sync_copy(idx_hbm.at[kk, wid], idx_v.at[kk])

        @pl.loop(0, nch)
        def _(j):
            pltpu.sync_copy(src_hbm.at[pl.ds(wid * per + j * c, c)], rows_v)
            for kk in range(k):
                pltpu.sync_copy(rows_v, out_hbm.at[idx_v.at[kk, j]])

    return pl.kernel(
        body,
        out_type=jax.ShapeDtypeStruct((n_out, w), src.dtype),
        mesh=_sc_mesh(),
        scratch_types=[pltpu.VMEM((k, nch, c), jnp.int32), pltpu.VMEM((c, w), src.dtype)],
        name="sc_dispatch",
    )(src, idx_t.reshape(k, SC_WORKERS, nch, c))


def _sc_gather_rows(table, idx):
    w = table.shape[1]
    m = idx.shape[0]
    c = GATHER_CHUNK
    per = m // SC_WORKERS
    nch = per // c
    assert per * SC_WORKERS == m and nch * c == per and nch % 2 == 0

    def body(table_hbm, idx_hbm, out_hbm, idx_v, rows_v, sem0, sem1):
        wid = _sc_worker()
        base = wid * per
        sems = (sem0, sem1)
        pltpu.sync_copy(idx_hbm.at[wid], idx_v)

        def gather(j, b):
            return pltpu.make_async_copy(table_hbm.at[idx_v.at[j]], rows_v.at[b], sems[b])

        gather(0, 0).start()

        @pl.loop(0, nch, step=2)
        def _(g):
            for b in range(2):
                j = g + b
                gather(j, b).wait()

                @pl.when(j + 1 < nch)
                def _():
                    gather(j + 1, 1 - b).start()

                pltpu.sync_copy(rows_v.at[b], out_hbm.at[pl.ds(base + j * c, c)])

    return pl.kernel(
        body,
        out_type=jax.ShapeDtypeStruct((m, w), table.dtype),
        mesh=_sc_mesh(),
        scratch_types=[pltpu.VMEM((nch, c), jnp.int32), pltpu.VMEM((2, c, w), table.dtype),
                       pltpu.SemaphoreType.DMA, pltpu.SemaphoreType.DMA],
        name="sc_combine_gather",
    )(table, idx.reshape(SC_WORKERS, nch, c))


PERM_BLOCK = 256


def _expert_kernel(te_ref, first_ref, next_ref, par_ref, nu_ref, second_ref, x_ref, wgu_hbm, wdn_hbm,
                   perm_ref, bg_ref, bu_ref, bd_ref, y_ref, wgu_f, wdn_f, wg_s, wu_s, wd_s, sem):
    i = pl.program_id(0)
    used = i < nu_ref[0]

    def fetch(e, slot):
        return (pltpu.make_async_copy(wgu_hbm.at[e], wgu_f.at[slot], sem.at[0, slot]),
                pltpu.make_async_copy(wdn_hbm.at[e], wdn_f.at[slot], sem.at[1, slot]))

    @pl.when(i == 0)
    def _():
        for cp in fetch(te_ref[0], 0):
            cp.start()

        @pl.when(second_ref[0] >= 0)
        def _():
            for cp in fetch(second_ref[0], 1):
                cp.start()

    @pl.when(used & (first_ref[i] == 1))
    def _():
        slot = par_ref[i]
        for cp in fetch(te_ref[i], slot):
            cp.wait()
        half = PERM_BLOCK // 2
        perm = perm_ref[...]
        for b in range(wgu_f.shape[2] // PERM_BLOCK):
            blk = wgu_f[slot, :, b * PERM_BLOCK:(b + 1) * PERM_BLOCK].astype(BF16)
            t = _dot(blk, perm)
            wg_s[:, b * half:(b + 1) * half] = t[:, :half].astype(BF16)
            wu_s[:, b * half:(b + 1) * half] = t[:, half:].astype(BF16)
        wd_s[...] = wdn_f[slot].astype(BF16)

        @pl.when(next_ref[i] >= 0)
        def _():
            for cp in fetch(next_ref[i], slot):
                cp.start()

    @pl.when(used)
    def _():
        x = _unpack_halves(x_ref[...]).astype(BF16)
        g = _dot(x, wg_s[...]) + bg_ref[...]
        u = _dot(x, wu_s[...]) + bu_ref[...]
        gate = jnp.minimum(g, SWIGLU_LIMIT)
        up = jnp.clip(u, -SWIGLU_LIMIT, SWIGLU_LIMIT)
        act = (up + 1.0) * (gate * _sigmoid(gate, SWIGLU_ALPHA))
        y = _dot(act.astype(BF16), wd_s[...]) + bd_ref[...]
        y_ref[...] = _pack_halves(y)

    @pl.when(jnp.logical_not(used))
    def _():
        y_ref[...] = jnp.zeros_like(y_ref)


def _experts(tile_expert, tile_first, tile_next, tile_parity, n_used, second, xs, w_gu, w_dn, bg, bu,
             bd):
    n_slots, half = xs.shape
    tm = EXPERT_TILE
    n_tiles = n_slots // tm
    dm, f, d = w_gu.shape[1], w_dn.shape[1], w_dn.shape[2]
    ph = PERM_BLOCK // 2
    j = jnp.arange(PERM_BLOCK)
    perm = (jnp.arange(PERM_BLOCK)[None, :] == (j // 2 + (j % 2) * ph)[:, None]).astype(BF16)
    row_in = pl.BlockSpec((tm, half), lambda i, te, tf, tn, tp, nu, sec: (jnp.minimum(i, nu[0] - 1), 0))
    row_out = pl.BlockSpec(
        (tm, half), lambda i, te, tf, tn, tp, nu, sec: (jnp.where(i < nu[0], i, n_tiles - 1), 0))
    bspec = lambda a: pl.BlockSpec((None,) + a.shape[1:],
                                   lambda i, te, tf, tn, tp, nu, sec: (te[i], 0, 0))
    return pl.pallas_call(
        _expert_kernel,
        out_shape=jax.ShapeDtypeStruct((n_slots, half), jnp.uint32),
        grid_spec=pltpu.PrefetchScalarGridSpec(
            num_scalar_prefetch=6, grid=(n_tiles,),
            in_specs=[row_in, pl.BlockSpec(memory_space=pl.ANY), pl.BlockSpec(memory_space=pl.ANY),
                      pl.BlockSpec((PERM_BLOCK, PERM_BLOCK),
                                   lambda i, te, tf, tn, tp, nu, sec: (0, 0)),
                      bspec(bg), bspec(bu), bspec(bd)],
            out_specs=row_out,
            scratch_shapes=[pltpu.VMEM((2, dm, 2 * f), F32), pltpu.VMEM((2, f, d), F32),
                            pltpu.VMEM((dm, f), BF16), pltpu.VMEM((dm, f), BF16),
                            pltpu.VMEM((f, d), BF16), pltpu.SemaphoreType.DMA((2, 2))]),
        compiler_params=_cparams("arbitrary"),
        name="experts",
    )(tile_expert, tile_first, tile_next, tile_parity, n_used, second, xs, w_gu, w_dn, perm, bg, bu,
      bd)


def _combine_kernel(y4_ref, h_ref, wt_ref, g2_ref, fnw_ref, *rest):
    o_ref = rest[-1]
    wt = wt_ref[...]
    acc = jnp.zeros(h_ref.shape, F32)
    for kk in range(TOP_K):
        acc = acc + wt[:, kk:kk + 1] * _unpack_halves(y4_ref[kk])
    h = h_ref[...] + g2_ref[...] * acc
    o_ref[...] = _rms(h) * fnw_ref[...]


def _combine(y4, h, wt, mod4, fnw, seq, row0, n_total, prev_out):
    n, d = h.shape
    tt = COMBINE_TILE
    tpb = seq // tt
    t0 = row0 // tt
    in_specs = [pl.BlockSpec((TOP_K, tt, d // 2), lambda i: (0, i, 0)),
                pl.BlockSpec((tt, d), lambda i: (i, 0)),
                pl.BlockSpec((tt, GATE_PAD), lambda i: (i, 0)),
                _mod_spec(d, 5, lambda i: i // tpb + row0 // seq),
                pl.BlockSpec((1, d), lambda i: (0, 0))]
    args = [y4, h, wt, mod4, fnw]
    aliases = {}
    if prev_out is not None:
        in_specs.append(pl.BlockSpec(memory_space=pl.ANY))
        args.append(prev_out)
        aliases = {len(args) - 1: 0}
    return pl.pallas_call(
        _combine_kernel,
        out_shape=jax.ShapeDtypeStruct((n_total, d), F32),
        grid=(n // tt,),
        in_specs=in_specs,
        out_specs=pl.BlockSpec((tt, d), lambda i: (i + t0, 0)),
        input_output_aliases=aliases,
        compiler_params=_cparams("arbitrary"),
        name="combine",
    )(*args)


def kernel(x, c, ctx, c_ctx, w_ada, b_ada, norm1_w, w_in, a_up_f, a_bias_f, a_up_b, a_bias_b,
           gla_norm_w, conv_w, w_out, norm2_w, router_w, router_b, w_gu, b_gu, w_dn, b_dn,
           final_norm_w):
    bsz, seq, d = x.shape
    n = bsz * seq
    assert w_ada.shape[0] == 1 and d == D_MODEL
    assert seq % TOKEN_TILE == 0 and seq % POST_TILE == 0 and seq % GRID_W == 0
    assert ctx.shape[1] & (ctx.shape[1] - 1) == 0

    w_in0 = w_in[0]
    v_end = 2 * GLA_QK + GLA_WIDTH
    r_end = v_end + GLA_WIDTH
    g_end = r_end + 2 * GATE_RANK
    wm = jnp.concatenate([w_in0[:, :r_end], w_in0[:, g_end:]], axis=1).astype(BF16)
    wg_low = jnp.pad(w_in0[:, r_end:g_end], ((0, 0), (0, GATE_PAD - 2 * GATE_RANK))).astype(BF16)
    afb = jnp.zeros((GATE_PAD, 2 * GLA_QK), F32)
    afb = afb.at[:GATE_RANK, :GLA_QK].set(a_up_f[0]).at[GATE_RANK:2 * GATE_RANK, GLA_QK:].set(a_up_b[0])
    afb = afb.astype(BF16)
    bias_fb = jnp.concatenate([a_bias_f[0], a_bias_b[0]])[None]
    wkv = wm[:, GLA_QK:v_end]
    row = lambda a: a.reshape(1, -1)

    mod = _modulation(jnp.concatenate([c, c_ctx[None]], axis=0), w_ada[0], row(b_ada[0]))
    mod4 = mod.reshape(bsz + 1, 6, 1, d)

    s0f, s0b = _context_states(ctx, mod4, row(norm1_w[0]), wkv, wg_low, afb, bias_fb)

    x2 = x.reshape(n, d)
    tri = jnp.triu(jnp.ones((POST_TILE, POST_TILE), BF16), k=1)
    gnw = jnp.tile(gla_norm_w[0], GLA_HEADS)[None]
    wo = w_out[0].astype(BF16)
    bg_e, bu_e, bd_e = b_gu[0][:, None, 0::2], b_gu[0][:, None, 1::2], b_dn[0][:, None, :]

    if bsz % 4 == 0:
        chunk_batches = (3 * bsz // 4, bsz // 4)
    elif bsz % 2 == 0:
        chunk_batches = (bsz // 2, bsz // 2)
    else:
        chunk_batches = (bsz,)
    routed = []
    b0 = 0
    for cb in chunk_batches:
        row0 = b0 * seq
        b0 += cb
        nc = cb * seq
        n_tiles = nc * TOP_K // EXPERT_TILE + N_EXPERTS
        a_b, vb, o_i, sr, yc, dec = _projection(
            x2, row0, nc, mod4, row(norm1_w[0]), wm, wg_low, afb, bias_fb, conv_w[0], s0f, seq)
        h, xw, idx_t, rank_t, wt, cnt = _post_mixer(
            x2, row0, o_i, a_b, vb, dec, s0b, sr, yc, gnw, wo, mod4, row(norm2_w[0]),
            router_w[0].T, router_b[0][:, None], tri, seq)

        group_start, tile_expert, tile_first, tile_next, tile_parity, n_used, second = _plan(
            cnt[:, 0], n_tiles)
        slot_t = _slots(group_start, idx_t, rank_t)
        xs = _sc_scatter_rows(xw, slot_t, n_tiles * EXPERT_TILE)
        routed.append((h, wt, slot_t.reshape(-1), xs,
                       (tile_expert, tile_first, tile_next, tile_parity, n_used, second), row0))

    gathered = []
    for h, wt, slot_flat, xs, tiles, row0 in routed:
        ys = _experts(*tiles, xs, w_gu[0], w_dn[0], bg_e, bu_e, bd_e)
        y4 = _sc_gather_rows(ys, slot_flat).reshape(TOP_K, h.shape[0], d // 2)
        gathered.append((y4, h, wt, row0))

    out = None
    for y4, h, wt, row0 in gathered:
        out = _combine(y4, h, wt, mod4, row(final_norm_w), seq, row0, n, out)
    return out.reshape(bsz, seq, d)
```

```python
import functools

import jax
import jax.numpy as jnp
from jax import lax
from jax.experimental import pallas as pl
from jax.experimental.pallas import tpu as pltpu
from jax.experimental.pallas import tpu_sc as plsc

D_MODEL = 1024
GLA_HEADS = 4
GLA_DK = 64
GLA_DV = 128
GLA_QK = GLA_HEADS * GLA_DK
GLA_WIDTH = GLA_HEADS * GLA_DV
CONV_WIDTH = 512
GATE_RANK = 16
GATE_PAD = 128
GATE_NORMALIZER = 16.0
CHUNK = 64
GRID_W = 64
N_EXPERTS = 32
TOP_K = 4
SWIGLU_LIMIT = 7.0
SWIGLU_ALPHA = 1.702
EPS = 1e-6

TOKEN_TILE = 1024
POST_TILE = 1024
PROJ_SUB = 1024
PROJ_COLS = 1024
INTRA_GROUP = 256
EXPERT_TILE = 512
COMBINE_TILE = 1024
VMEM_LIMIT = 56 * 1024 * 1024

F32 = jnp.float32
BF16 = jnp.bfloat16
HI_MASK = 0xFFFF0000
LOG2_E = 1.4426950408889634


def _cparams(*sem):
    return pltpu.CompilerParams(dimension_semantics=sem, vmem_limit_bytes=VMEM_LIMIT)


def _mod_spec(d, component, batch_of_step):
    return pl.BlockSpec((None, None, 1, d), lambda i: (batch_of_step(i), component, 0, 0))


def _rms(x):
    return x * lax.rsqrt(jnp.mean(x * x, axis=-1, keepdims=True) + EPS)


def _sigmoid(x, scale=1.0):
    return 1.0 / (1.0 + jnp.exp2(x * (-scale * LOG2_E)))


def _log_sigmoid(z):
    return jnp.minimum(z, 0.0) - jnp.log(1.0 + jnp.exp(-jnp.abs(z)))


def _dot(a, b):
    return jnp.dot(a, b, preferred_element_type=F32)


def _dot_nt(a, b):
    return lax.dot_general(a, b, (((1,), (1,)), ((), ())), preferred_element_type=F32)


def _dot_tn(a, b):
    return lax.dot_general(a, b, (((0,), (0,)), ((), ())), preferred_element_type=F32)


def _split_bf16(a):
    hi = a.astype(BF16)
    lo = (a - hi.astype(F32)).astype(BF16)
    return hi, lo


def _dot3(a, b, dot):
    a_hi, a_lo = _split_bf16(a)
    b_hi, b_lo = _split_bf16(b)
    return dot(a_hi, b_hi) + (dot(a_hi, b_lo) + dot(a_lo, b_hi))


def _segment_scan(x, seg, reverse, after_step=None):
    n = x.shape[0]
    row = lax.broadcasted_iota(jnp.int32, x.shape, 0) & (seg - 1)
    s = 1
    while s < seg:
        if reverse:
            shifted = pltpu.roll(x, n - s, 0)
            x = x + jnp.where(row < seg - s, shifted, 0.0)
        else:
            shifted = pltpu.roll(x, s, 0)
            x = x + jnp.where(row >= s, shifted, 0.0)
        if after_step is not None:
            after_step()
        s *= 2
    return x


def _pack_halves(x):
    n = x.shape[1] // 2
    bits = pltpu.bitcast(x.astype(BF16).astype(F32), jnp.uint32)
    return (bits[:, :n] & jnp.uint32(HI_MASK)) | (bits[:, n:] >> 16)


def _unpack_halves(w):
    hi = pltpu.bitcast(w & jnp.uint32(HI_MASK), F32)
    lo = pltpu.bitcast(w << 16, F32)
    return jnp.concatenate([hi, lo], axis=1)


def _mod_kernel(c_ref, w_ref, b_ref, o_ref):
    c = c_ref[...]
    s = c * _sigmoid(c)
    o_ref[...] = _dot3(s, w_ref[...], _dot) + b_ref[...]


def _modulation(cc, w_ada, b_ada):
    rows, d = cc.shape
    n = w_ada.shape[1]
    bn = 1536
    return pl.pallas_call(
        _mod_kernel,
        out_shape=jax.ShapeDtypeStruct((rows, n), F32),
        grid=(n // bn,),
        in_specs=[pl.BlockSpec((rows, d), lambda j: (0, 0)),
                  pl.BlockSpec((d, bn), lambda j: (0, j)),
                  pl.BlockSpec((1, bn), lambda j: (0, j))],
        out_specs=pl.BlockSpec((rows, bn), lambda j: (0, j)),
        compiler_params=_cparams("arbitrary"),
        name="modulation",
    )(cc, w_ada, b_ada)


def _ctx_kernel(x_ref, sh_ref, sc_ref, nw_ref, wkv_ref, wg_ref, afb_ref, bias_ref, sf_ref, sb_ref):
    x = x_ref[...]
    n = x.shape[0]
    hn = _rms(x) * nw_ref[...] * (1.0 + sc_ref[...]) + sh_ref[...]
    hb = hn.astype(BF16)
    kv = _dot(hb, wkv_ref[...])
    g = _dot(hb, wg_ref[...])
    z = _dot(g.astype(BF16), afb_ref[...]) + bias_ref[...]
    la = _log_sigmoid(z) * (1.0 / GATE_NORMALIZER)
    p = _segment_scan(la, n, False)
    p_f, p_b = p[:, :GLA_QK], p[:, GLA_QK:]
    w_f = jnp.exp(p_f[n - 1:n, :] - p_f)
    w_b = jnp.exp(p_b - la[:, GLA_QK:])
    k = kv[:, :GLA_QK]
    vb = kv[:, GLA_QK:].astype(BF16)
    ke_f = (k * w_f).astype(BF16)
    ke_b = (k * w_b).astype(BF16)
    for h in range(GLA_HEADS):
        vh = vb[:, h * GLA_DV:(h + 1) * GLA_DV]
        sf_ref[h] = _dot_tn(vh, ke_f[:, h * GLA_DK:(h + 1) * GLA_DK])
        sb_ref[h] = _dot_tn(vh, ke_b[:, h * GLA_DK:(h + 1) * GLA_DK])


def _context_states(ctx, mod4, nw1, wkv, wg, afb, bias_fb):
    bsz, n, d = ctx.shape
    ctx_row = lambda b: bsz
    full = lambda a: pl.BlockSpec(a.shape, lambda b: (0,) * a.ndim)
    st = jax.ShapeDtypeStruct((bsz, GLA_HEADS, GLA_DV, GLA_DK), F32)
    st_spec = pl.BlockSpec((None, GLA_HEADS, GLA_DV, GLA_DK), lambda b: (b, 0, 0, 0))
    return pl.pallas_call(
        _ctx_kernel,
        out_shape=(st, st),
        grid=(bsz,),
        in_specs=[pl.BlockSpec((None, n, d), lambda b: (b, 0, 0)),
                  _mod_spec(d, 0, ctx_row), _mod_spec(d, 1, ctx_row), full(nw1), full(wkv), full(wg),
                  full(afb), full(bias_fb)],
        out_specs=(st_spec, st_spec),
        compiler_params=_cparams("arbitrary"),
        name="context_states",
    )(ctx, mod4, mod4, nw1, wkv, wg, afb, bias_fb)


def _chunk_scan(st_ref, qd, ke, v, dec, descending):
    nc = qd.shape[0] // CHUNK
    order = range(nc - 1, -1, -1) if descending else range(nc)
    rows = [slice(c * CHUNK, (c + 1) * CHUNK) for c in range(nc)]
    outs = []
    for h in range(GLA_HEADS):
        lanes = slice(h * GLA_DK, (h + 1) * GLA_DK)
        vl = slice(h * GLA_DV, (h + 1) * GLA_DV)
        kv = [_dot_tn(v[rows[c], vl], ke[rows[c], lanes]) for c in range(nc)]
        s = st_ref[h]
        start = [None] * nc
        for c in order:
            start[c] = s
            s = s * dec[c:c + 1, lanes] + kv[c]
        st_ref[h] = s
        outs.append(jnp.concatenate(
            [_dot_nt(qd[rows[c], lanes], start[c].astype(BF16)) for c in range(nc)], axis=0))
    return jnp.concatenate(outs, axis=1)


def _proj_kernel(tpb, x_ref, sh_ref, sc_ref, nw_ref, wm_ref, wg_ref, afb_ref, bias_ref, cw_ref,
                 s0_ref, ab_ref, v_ref, oi_ref, sr_ref, yc_ref, dec_ref, p_scr, st_ref):
    @pl.when(pl.program_id(0) % tpb == 0)
    def _():
        st_ref[...] = s0_ref[...]

    subs = range(x_ref.shape[0] // PROJ_SUB)
    nc = PROJ_SUB // CHUNK
    nblk = wm_ref.shape[1] // PROJ_COLS
    factors = []
    for sub in subs:
        x = x_ref[sub * PROJ_SUB:(sub + 1) * PROJ_SUB, :]
        hn = _rms(x) * nw_ref[...] * (1.0 + sc_ref[...]) + sh_ref[...]
        hb = hn.astype(BF16)
        g = _dot(hb, wg_ref[...])
        z = _dot(g.astype(BF16), afb_ref[...]) + bias_ref[...]

        todo = iter(range(nblk))

        def wide_block():
            j = next(todo, None)
            if j is not None:
                cols = slice(j * PROJ_COLS, (j + 1) * PROJ_COLS)
                p_scr[sub, :, cols] = _dot(hb, wm_ref[:, cols])

        la = _log_sigmoid(z) * (1.0 / GATE_NORMALIZER)
        wide_block()
        b_f = _segment_scan(la[:, :GLA_QK], CHUNK, False, wide_block)
        b_b = _segment_scan(la[:, GLA_QK:], CHUNK, True, wide_block)
        b_f3 = b_f.reshape(nc, CHUNK, GLA_QK)
        tot_f = b_f3[:, CHUNK - 1:CHUNK, :]
        b_b3 = b_b.reshape(nc, CHUNK, GLA_QK)
        tot_b = b_b3[:, 0:1, :]
        fac = []
        for arg in (b_f, -b_f, (tot_f - b_f3).reshape(PROJ_SUB, GLA_QK),
                    b_b, -b_b, (tot_b - b_b3).reshape(PROJ_SUB, GLA_QK)):
            fac.append(jnp.exp(arg))
            wide_block()
        fac += [jnp.exp(tot_f).reshape(nc, GLA_QK), jnp.exp(tot_b).reshape(nc, GLA_QK)]
        factors.append(fac)
        for _ in range(nblk):
            wide_block()
    for sub in subs:
        _proj_rows(sub, factors[sub], cw_ref, ab_ref, v_ref, oi_ref, sr_ref, yc_ref,
                   dec_ref, p_scr.at[sub], st_ref)


def _proj_rows(sub, factors, cw_ref, ab_ref, v_ref, oi_ref, sr_ref, yc_ref, dec_ref, p, st_ref):
    tm = PROJ_SUB
    nc = tm // CHUNK
    rs = slice(sub * tm, (sub + 1) * tm)
    cs = slice(sub * nc, (sub + 1) * nc)
    e_f, ei_f, ee_f, e_b, ei_b, ee_b, dec_f, dec_b = factors

    q = p[:, 0:GLA_QK] * (GLA_DK ** -0.5)
    k = p[:, GLA_QK:2 * GLA_QK]
    vb = p[:, 2 * GLA_QK:2 * GLA_QK + GLA_WIDTH].astype(BF16)
    v_ref[rs, :] = vb
    qd_f = (q * e_f).astype(BF16)
    ki_f = (k * ei_f).astype(BF16)
    ke_f = (k * ee_f).astype(BF16)
    qd_b = (q * e_b).astype(BF16)
    ki_b = (k * ei_b).astype(BF16)
    ke_b = (k * ee_b).astype(BF16)

    ab_ref[rs, :GLA_QK] = qd_b
    ab_ref[rs, GLA_QK:] = ke_b
    dec_ref[cs, :GLA_QK] = dec_f
    dec_ref[cs, GLA_QK:] = dec_b

    inter_f = _chunk_scan(st_ref, qd_f, ke_f, vb, dec_f, False)

    gi = INTRA_GROUP
    ri = lax.broadcasted_iota(jnp.int32, (gi, gi), 0)
    ci = lax.broadcasted_iota(jnp.int32, (gi, gi), 1)
    same = (ri // CHUNK) == (ci // CHUNK)
    lower = same & (ci <= ri)
    upper = same & (ci >= ri)
    for gidx in range(tm // gi):
        rows = slice(gidx * gi, (gidx + 1) * gi)
        orows = slice(sub * tm + gidx * gi, sub * tm + (gidx + 1) * gi)
        for h in range(GLA_HEADS):
            lanes = slice(h * GLA_DK, (h + 1) * GLA_DK)
            s_f = _dot_nt(qd_f[rows, lanes], ki_f[rows, lanes])
            s_b = _dot_nt(qd_b[rows, lanes], ki_b[rows, lanes])
            s = jnp.where(lower, s_f, 0.0) + jnp.where(upper, s_b, 0.0)
            vl = slice(h * GLA_DV, (h + 1) * GLA_DV)
            oi_ref[orows, vl] = _dot(s.astype(BF16), vb[rows, vl]) + inter_f[rows, vl]

    r = p[:, 1024:1536]
    sr_ref[rs, :] = (r * _sigmoid(r)).astype(BF16)

    u = p[:, 2048:2560] * p[:, 2560:3072]
    row = lax.broadcasted_iota(jnp.int32, u.shape, 0) & (GRID_W - 1)
    prev = jnp.where(row >= 1, pltpu.roll(u, 1, 0), 0.0)
    nxt = jnp.where(row < GRID_W - 1, pltpu.roll(u, tm - 1, 0), 0.0)
    cw = cw_ref[...]
    conv = prev * cw[0:1, :] + u * cw[1:2, :] + nxt * cw[2:3, :]
    yc_ref[rs, :] = (p[:, 1536:2048] * conv).astype(BF16)


def _projection(x2, row0, n, mod4, nw1, wm, wg, afb, bias_fb, conv_w, s0f, seq):
    d = x2.shape[1]
    tm = TOKEN_TILE
    tpb = seq // tm
    t0 = row0 // tm
    batch = lambda i: i // tpb + row0 // seq
    full = lambda a: pl.BlockSpec(a.shape, lambda i: (0,) * a.ndim)
    tok = lambda w: pl.BlockSpec((tm, w), lambda i: (i, 0))
    st_spec = pl.BlockSpec((None, GLA_HEADS, GLA_DV, GLA_DK), lambda i: (batch(i), 0, 0, 0))
    out_shape = (
        jax.ShapeDtypeStruct((n, 2 * GLA_QK), BF16),
        jax.ShapeDtypeStruct((n, GLA_WIDTH), BF16),
        jax.ShapeDtypeStruct((n, GLA_WIDTH), F32),
        jax.ShapeDtypeStruct((n, GLA_WIDTH), BF16),
        jax.ShapeDtypeStruct((n, CONV_WIDTH), BF16),
        jax.ShapeDtypeStruct((n // CHUNK, 2 * GLA_QK), F32),
    )
    out_specs = (tok(2 * GLA_QK), tok(GLA_WIDTH), tok(GLA_WIDTH), tok(GLA_WIDTH),
                 tok(CONV_WIDTH), pl.BlockSpec((tm // CHUNK, 2 * GLA_QK), lambda i: (i, 0)))
    return pl.pallas_call(
        functools.partial(_proj_kernel, tpb),
        out_shape=out_shape,
        grid=(n // tm,),
        in_specs=[pl.BlockSpec((tm, d), lambda i: (i + t0, 0)), _mod_spec(d, 0, batch),
                  _mod_spec(d, 1, batch), full(nw1),
                  full(wm), full(wg), full(afb), full(bias_fb), full(conv_w), st_spec],
        out_specs=out_specs,
        scratch_shapes=[pltpu.VMEM((tm // PROJ_SUB, PROJ_SUB, wm.shape[1]), F32),
                        pltpu.VMEM((GLA_HEADS, GLA_DV, GLA_DK), F32)],
        compiler_params=_cparams("arbitrary"),
        name="projection",
    )(x2, mod4, mod4, nw1, wm, wg, afb, bias_fb, conv_w, s0f)


def _post_kernel(tpb, x_ref, oi_ref, ab_ref, v_ref, dec_ref, s0_ref, sr_ref, yc_ref, gnw_ref, wo_ref,
                 g1_ref, nw2_ref, sh2_ref, sc2_ref, rwt_ref, rb_ref, tri_ref,
                 h_ref, xw_ref, idx_ref, rank_ref, wt_ref, cnt_ref, run_ref, st_ref):
    @pl.when(pl.program_id(0) == 0)
    def _():
        run_ref[...] = jnp.zeros_like(run_ref)

    @pl.when(pl.program_id(0) % tpb == 0)
    def _():
        st_ref[...] = s0_ref[...]

    tm = x_ref.shape[0]
    inter_b = _chunk_scan(st_ref, ab_ref[:, :GLA_QK], ab_ref[:, GLA_QK:], v_ref[...],
                          dec_ref[:, GLA_QK:], True)
    o = oi_ref[...] + inter_b
    normed = jnp.concatenate(
        [_rms(o[:, h * GLA_DV:(h + 1) * GLA_DV]) for h in range(GLA_HEADS)], axis=1)
    y_gla = (normed * gnw_ref[...] * sr_ref[...].astype(F32)).astype(BF16)
    wo = wo_ref[...]
    y = _dot(y_gla, wo[:GLA_WIDTH, :]) + _dot(yc_ref[...], wo[GLA_WIDTH:, :])
    h = x_ref[...] + g1_ref[...] * y
    h_ref[...] = h
    hn = _rms(h) * nw2_ref[...] * (1.0 + sc2_ref[...]) + sh2_ref[...]
    xw_ref[...] = _pack_halves(hn)

    logits = _dot3(rwt_ref[...], hn, _dot_nt) + rb_ref[...]
    eid = lax.broadcasted_iota(jnp.int32, logits.shape, 0)
    vals, idxs = [], []
    work = logits
    for _ in range(TOP_K):
        m = jnp.max(work, axis=0, keepdims=True)
        i = jnp.min(jnp.where(work == m, eid, N_EXPERTS), axis=0, keepdims=True)
        vals.append(m)
        idxs.append(i)
        work = jnp.where(eid == i, -jnp.inf, work)
    exps = [jnp.exp(v - vals[0]) for v in vals]
    denom = exps[0] + exps[1] + exps[2] + exps[3]
    wts = [e / denom for e in exps]

    onehot = jnp.zeros(logits.shape, F32)
    for i in idxs:
        onehot = onehot + jnp.where(eid == i, 1.0, 0.0)
    before = _dot(onehot.astype(BF16), tri_ref[...]) + run_ref[:, 0:1]
    for kk in range(TOP_K):
        idx_ref[kk:kk + 1, :] = idxs[kk]
        rk = jnp.sum(jnp.where(eid == idxs[kk], before, 0.0), axis=0, keepdims=True)
        rank_ref[kk:kk + 1, :] = rk.astype(jnp.int32)
    run = run_ref[...] + jnp.sum(onehot, axis=1, keepdims=True)
    run_ref[...] = run
    cnt_ref[...] = run.astype(jnp.int32)

    sub = lax.broadcasted_iota(jnp.int32, (GATE_PAD, tm), 0)
    wpad = jnp.zeros((GATE_PAD, tm), F32)
    for kk in range(TOP_K):
        wpad = wpad + jnp.where(sub == kk, wts[kk], 0.0)
    wt_ref[...] = wpad.T


def _post_mixer(x2, row0, o_i, a_b, vb, dec, s0b, sr, yc, gnw, wo, mod4, nw2, rwt, rb, tri, seq):
    n, d = o_i.shape[0], x2.shape[1]
    tm = POST_TILE
    tpb = seq // tm
    t0 = row0 // tm
    rev = lambda i: (i // tpb) * tpb + (tpb - 1 - i % tpb)
    batch = lambda i: i // tpb + row0 // seq
    full = lambda a: pl.BlockSpec(a.shape, lambda i: (0,) * a.ndim)
    tok = lambda w: pl.BlockSpec((tm, w), lambda i: (rev(i), 0))
    lane_tok = pl.BlockSpec((TOP_K, tm), lambda i: (0, rev(i)))
    st_spec = pl.BlockSpec((None, GLA_HEADS, GLA_DV, GLA_DK), lambda i: (batch(i), 0, 0, 0))
    dec_spec = pl.BlockSpec((tm // CHUNK, 2 * GLA_QK), lambda i: (rev(i), 0))
    out_shape = (
        jax.ShapeDtypeStruct((n, d), F32),
        jax.ShapeDtypeStruct((n, d // 2), jnp.uint32),
        jax.ShapeDtypeStruct((TOP_K, n), jnp.int32),
        jax.ShapeDtypeStruct((TOP_K, n), jnp.int32),
        jax.ShapeDtypeStruct((n, GATE_PAD), F32),
        jax.ShapeDtypeStruct((N_EXPERTS, GATE_PAD), jnp.int32),
    )
    out_specs = (tok(d), tok(d // 2), lane_tok, lane_tok, tok(GATE_PAD),
                 pl.BlockSpec((N_EXPERTS, GATE_PAD), lambda i: (0, 0)))
    return pl.pallas_call(
        functools.partial(_post_kernel, tpb),
        out_shape=out_shape,
        grid=(n // tm,),
        in_specs=[pl.BlockSpec((tm, d), lambda i: (rev(i) + t0, 0)), tok(GLA_WIDTH),
                  tok(2 * GLA_QK), tok(GLA_WIDTH), dec_spec, st_spec, tok(GLA_WIDTH),
                  tok(CONV_WIDTH), full(gnw), full(wo), _mod_spec(d, 2, batch), full(nw2),
                  _mod_spec(d, 3, batch), _mod_spec(d, 4, batch),
                  full(rwt), full(rb), full(tri)],
        out_specs=out_specs,
        scratch_shapes=[pltpu.VMEM((N_EXPERTS, GATE_PAD), F32),
                        pltpu.VMEM((GLA_HEADS, GLA_DV, GLA_DK), F32)],
        compiler_params=_cparams("arbitrary"),
        name="post_mixer",
    )(x2, o_i, a_b, vb, dec, s0b, sr, yc, gnw, wo, mod4, nw2, mod4, mod4, rwt, rb, tri)


def _plan_kernel(cnt_ref, start_ref, te_ref, first_ref, next_ref, nu_ref):
    n_tiles = te_ref.shape[0]
    shift = EXPERT_TILE.bit_length() - 1

    def clear(t, carry):
        first_ref[t] = 0
        next_ref[t] = -1
        return carry

    lax.fori_loop(0, n_tiles, clear, 0)

    def group(e, tile0):
        nt = lax.shift_right_logical(cnt_ref[e] + (EXPERT_TILE - 1), shift)
        start_ref[e] = tile0 * EXPERT_TILE

        def fill(t, carry):
            te_ref[tile0 + t] = e
            return carry

        lax.fori_loop(0, nt, fill, 0)

        @pl.when(nt > 0)
        def _():
            first_ref[tile0] = 1

        return tile0 + nt

    n_used = lax.fori_loop(0, N_EXPERTS, group, 0)
    nu_ref[0] = n_used
    last = te_ref[n_used - 1]

    def tail(t, carry):
        te_ref[t] = last
        return carry

    lax.fori_loop(n_used, n_tiles, tail, 0)

    def link(k, nxt):
        e = N_EXPERTS - 1 - k
        has = cnt_ref[e] > 0

        @pl.when(has)
        def _():
            next_ref[lax.shift_right_logical(start_ref[e], shift)] = nxt

        return jnp.where(has, e, nxt)

    lax.fori_loop(0, N_EXPERTS, link, -1)


def _plan(counts, n_tiles):
    assert EXPERT_TILE & (EXPERT_TILE - 1) == 0
    smem = pl.BlockSpec(memory_space=pltpu.SMEM)
    vec = lambda m: jax.ShapeDtypeStruct((m,), jnp.int32)
    return pl.pallas_call(
        _plan_kernel,
        out_shape=(vec(N_EXPERTS), vec(n_tiles), vec(n_tiles), vec(n_tiles), vec(1)),
        in_specs=[smem],
        out_specs=(smem, smem, smem, smem, smem),
        name="plan",
    )(counts)


def _slot_kernel(start_ref, idx_ref, rank_ref, slot_ref):
    idx = idx_ref[...]
    base = jnp.zeros(idx.shape, jnp.int32)
    for e in range(N_EXPERTS):
        base = jnp.where(idx == e, start_ref[e], base)
    slot_ref[...] = base + rank_ref[...]


def _slots(group_start, idx_t, rank_t):
    k, n = idx_t.shape
    bn = min(n, 8192)
    blk = pl.BlockSpec((k, bn), lambda i, s: (0, i))
    return pl.pallas_call(
        _slot_kernel,
        out_shape=jax.ShapeDtypeStruct((k, n), jnp.int32),
        grid_spec=pltpu.PrefetchScalarGridSpec(
            num_scalar_prefetch=1, grid=(n // bn,), in_specs=[blk, blk], out_specs=blk),
        compiler_params=_cparams("arbitrary"),
        name="slots",
    )(group_start, idx_t, rank_t)


SC_CORES = 2
SC_SUBCORES = 16
SC_WORKERS = SC_CORES * SC_SUBCORES
SCATTER_CHUNK = 128
GATHER_CHUNK = 64


def _sc_mesh():
    return plsc.VectorSubcoreMesh(core_axis_name="c", subcore_axis_name="s")


def _sc_worker():
    return lax.axis_index("s") * SC_CORES + lax.axis_index("c")


def _sc_scatter_rows(src, idx_t, n_out):
    n, w = src.shape
    k = idx_t.shape[0]
    c = SCATTER_CHUNK
    per = n // SC_WORKERS
    nch = per // c
    assert per * SC_WORKERS == n and nch * c == per

    def body(src_hbm, idx_hbm, out_hbm, idx_v, rows_v):
        wid = _sc_worker()
        for kk in range(k):
            pltpu.sync_copy(idx_hbm.at[kk, wid], idx_v.at[kk])

        @pl.loop(0, nch)
        def _(j):
            pltpu.sync_copy(src_hbm.at[pl.ds(wid * per + j * c, c)], rows_v)
            for kk in range(k):
                pltpu.sync_copy(rows_v, out_hbm.at[idx_v.at[kk, j]])

    return pl.kernel(
        body,
        out_type=jax.ShapeDtypeStruct((n_out, w), src.dtype),
        mesh=_sc_mesh(),
        scratch_types=[pltpu.VMEM((k, nch, c), jnp.int32), pltpu.VMEM((c, w), src.dtype)],
        name="sc_dispatch",
    )(src, idx_t.reshape(k, SC_WORKERS, nch, c))


def _sc_gather_rows(table, idx):
    w = table.shape[1]
    m = idx.shape[0]
    c = GATHER_CHUNK
    per = m // SC_WORKERS
    nch = per // c
    assert per * SC_WORKERS == m and nch * c == per and nch % 2 == 0

    def body(table_hbm, idx_hbm, out_hbm, idx_v, rows_v, sem0, sem1):
        wid = _sc_worker()
        base = wid * per
        sems = (sem0, sem1)
        pltpu.sync_copy(idx_hbm.at[wid], idx_v)

        def gather(j, b):
            return pltpu.make_async_copy(table_hbm.at[idx_v.at[j]], rows_v.at[b], sems[b])

        gather(0, 0).start()

        @pl.loop(0, nch, step=2)
        def _(g):
            for b in range(2):
                j = g + b
                gather(j, b).wait()

                @pl.when(j + 1 < nch)
                def _():
                    gather(j + 1, 1 - b).start()

                pltpu.sync_copy(rows_v.at[b], out_hbm.at[pl.ds(base + j * c, c)])

    return pl.kernel(
        body,
        out_type=jax.ShapeDtypeStruct((m, w), table.dtype),
        mesh=_sc_mesh(),
        scratch_types=[pltpu.VMEM((nch, c), jnp.int32), pltpu.VMEM((2, c, w), table.dtype),
                       pltpu.SemaphoreType.DMA, pltpu.SemaphoreType.DMA],
        name="sc_combine_gather",
    )(table, idx.reshape(SC_WORKERS, nch, c))


PERM_BLOCK = 256


def _expert_kernel(te_ref, first_ref, next_ref, nu_ref, x_ref, wgu_hbm, wdn_hbm, perm_ref, bg_ref,
                   bu_ref, bd_ref, y_ref, wgu_f, wdn_f, wg_s, wu_s, wd_s, sem):
    i = pl.program_id(0)
    used = i < nu_ref[0]

    def fetch(e):
        return (pltpu.make_async_copy(wgu_hbm.at[e], wgu_f, sem.at[0]),
                pltpu.make_async_copy(wdn_hbm.at[e], wdn_f, sem.at[1]))

    @pl.when(i == 0)
    def _():
        for cp in fetch(te_ref[0]):
            cp.start()

    @pl.when(used & (first_ref[i] == 1))
    def _():
        for cp in fetch(te_ref[i]):
            cp.wait()
        half = PERM_BLOCK // 2
        perm = perm_ref[...]
        for b in range(wgu_f.shape[1] // PERM_BLOCK):
            blk = wgu_f[:, b * PERM_BLOCK:(b + 1) * PERM_BLOCK].astype(BF16)
            t = _dot(blk, perm)
            wg_s[:, b * half:(b + 1) * half] = t[:, :half].astype(BF16)
            wu_s[:, b * half:(b + 1) * half] = t[:, half:].astype(BF16)
        wd_s[...] = wdn_f[...].astype(BF16)

        @pl.when(next_ref[i] >= 0)
        def _():
            for cp in fetch(next_ref[i]):
                cp.start()

    @pl.when(used)
    def _():
        x = _unpack_halves(x_ref[...]).astype(BF16)
        g = _dot(x, wg_s[...]) + bg_ref[...]
        u = _dot(x, wu_s[...]) + bu_ref[...]
        gate = jnp.minimum(g, SWIGLU_LIMIT)
        up = jnp.clip(u, -SWIGLU_LIMIT, SWIGLU_LIMIT)
        act = (up + 1.0) * (gate * _sigmoid(gate, SWIGLU_ALPHA))
        y = _dot(act.astype(BF16), wd_s[...]) + bd_ref[...]
        y_ref[...] = _pack_halves(y)

    @pl.when(jnp.logical_not(used))
    def _():
        y_ref[...] = jnp.zeros_like(y_ref)


def _experts(tile_expert, tile_first, tile_next, n_used, xs, w_gu, w_dn, bg, bu, bd):
    n_slots, half = xs.shape
    tm = EXPERT_TILE
    n_tiles = n_slots // tm
    dm, f, d = w_gu.shape[1], w_dn.shape[1], w_dn.shape[2]
    ph = PERM_BLOCK // 2
    j = jnp.arange(PERM_BLOCK)
    perm = (jnp.arange(PERM_BLOCK)[None, :] == (j // 2 + (j % 2) * ph)[:, None]).astype(BF16)
    row_in = pl.BlockSpec((tm, half), lambda i, te, tf, tn, nu: (jnp.minimum(i, nu[0] - 1), 0))
    row_out = pl.BlockSpec((tm, half),
                           lambda i, te, tf, tn, nu: (jnp.where(i < nu[0], i, n_tiles - 1), 0))
    bspec = lambda a: pl.BlockSpec((None,) + a.shape[1:], lambda i, te, tf, tn, nu: (te[i], 0, 0))
    return pl.pallas_call(
        _expert_kernel,
        out_shape=jax.ShapeDtypeStruct((n_slots, half), jnp.uint32),
        grid_spec=pltpu.PrefetchScalarGridSpec(
            num_scalar_prefetch=4, grid=(n_tiles,),
            in_specs=[row_in, pl.BlockSpec(memory_space=pl.ANY), pl.BlockSpec(memory_space=pl.ANY),
                      pl.BlockSpec((PERM_BLOCK, PERM_BLOCK), lambda i, te, tf, tn, nu: (0, 0)),
                      bspec(bg), bspec(bu), bspec(bd)],
            out_specs=row_out,
            scratch_shapes=[pltpu.VMEM((dm, 2 * f), F32), pltpu.VMEM((f, d), F32),
                            pltpu.VMEM((dm, f), BF16), pltpu.VMEM((dm, f), BF16),
                            pltpu.VMEM((f, d), BF16), pltpu.SemaphoreType.DMA((2,))]),
        compiler_params=_cparams("arbitrary"),
        name="experts",
    )(tile_expert, tile_first, tile_next, n_used, xs, w_gu, w_dn, perm, bg, bu, bd)


def _combine_kernel(y4_ref, h_ref, wt_ref, g2_ref, fnw_ref, *rest):
    o_ref = rest[-1]
    wt = wt_ref[...]
    acc = jnp.zeros(h_ref.shape, F32)
    for kk in range(TOP_K):
        acc = acc + wt[:, kk:kk + 1] * _unpack_halves(y4_ref[kk])
    h = h_ref[...] + g2_ref[...] * acc
    o_ref[...] = _rms(h) * fnw_ref[...]


def _combine(y4, h, wt, mod4, fnw, seq, row0, n_total, prev_out):
    n, d = h.shape
    tt = COMBINE_TILE
    tpb = seq // tt
    t0 = row0 // tt
    in_specs = [pl.BlockSpec((TOP_K, tt, d // 2), lambda i: (0, i, 0)),
                pl.BlockSpec((tt, d), lambda i: (i, 0)),
                pl.BlockSpec((tt, GATE_PAD), lambda i: (i, 0)),
                _mod_spec(d, 5, lambda i: i // tpb + row0 // seq),
                pl.BlockSpec((1, d), lambda i: (0, 0))]
    args = [y4, h, wt, mod4, fnw]
    aliases = {}
    if prev_out is not None:
        in_specs.append(pl.BlockSpec(memory_space=pl.ANY))
        args.append(prev_out)
        aliases = {len(args) - 1: 0}
    return pl.pallas_call(
        _combine_kernel,
        out_shape=jax.ShapeDtypeStruct((n_total, d), F32),
        grid=(n // tt,),
        in_specs=in_specs,
        out_specs=pl.BlockSpec((tt, d), lambda i: (i + t0, 0)),
        input_output_aliases=aliases,
        compiler_params=_cparams("arbitrary"),
        name="combine",
    )(*args)


def kernel(x, c, ctx, c_ctx, w_ada, b_ada, norm1_w, w_in, a_up_f, a_bias_f, a_up_b, a_bias_b,
           gla_norm_w, conv_w, w_out, norm2_w, router_w, router_b, w_gu, b_gu, w_dn, b_dn,
           final_norm_w):
    bsz, seq, d = x.shape
    n = bsz * seq
    assert w_ada.shape[0] == 1 and d == D_MODEL
    assert seq % TOKEN_TILE == 0 and seq % POST_TILE == 0 and seq % GRID_W == 0
    assert ctx.shape[1] & (ctx.shape[1] - 1) == 0

    w_in0 = w_in[0]
    v_end = 2 * GLA_QK + GLA_WIDTH
    r_end = v_end + GLA_WIDTH
    g_end = r_end + 2 * GATE_RANK
    wm = jnp.concatenate([w_in0[:, :r_end], w_in0[:, g_end:]], axis=1).astype(BF16)
    wg_low = jnp.pad(w_in0[:, r_end:g_end], ((0, 0), (0, GATE_PAD - 2 * GATE_RANK))).astype(BF16)
    afb = jnp.zeros((GATE_PAD, 2 * GLA_QK), F32)
    afb = afb.at[:GATE_RANK, :GLA_QK].set(a_up_f[0]).at[GATE_RANK:2 * GATE_RANK, GLA_QK:].set(a_up_b[0])
    afb = afb.astype(BF16)
    bias_fb = jnp.concatenate([a_bias_f[0], a_bias_b[0]])[None]
    wkv = wm[:, GLA_QK:v_end]
    row = lambda a: a.reshape(1, -1)

    mod = _modulation(jnp.concatenate([c, c_ctx[None]], axis=0), w_ada[0], row(b_ada[0]))
    mod4 = mod.reshape(bsz + 1, 6, 1, d)

    s0f, s0b = _context_states(ctx, mod4, row(norm1_w[0]), wkv, wg_low, afb, bias_fb)

    x2 = x.reshape(n, d)
    tri = jnp.triu(jnp.ones((POST_TILE, POST_TILE), BF16), k=1)
    gnw = jnp.tile(gla_norm_w[0], GLA_HEADS)[None]
    wo = w_out[0].astype(BF16)
    bg_e, bu_e, bd_e = b_gu[0][:, None, 0::2], b_gu[0][:, None, 1::2], b_dn[0][:, None, :]

    if bsz % 8 == 0:
        chunk_batches = (5 * bsz // 8, 3 * bsz // 8)
    elif bsz % 2 == 0:
        chunk_batches = (bsz // 2, bsz // 2)
    else:
        chunk_batches = (bsz,)
    routed = []
    b0 = 0
    for cb in chunk_batches:
        row0 = b0 * seq
        b0 += cb
        nc = cb * seq
        n_tiles = nc * TOP_K // EXPERT_TILE + N_EXPERTS
        a_b, vb, o_i, sr, yc, dec = _projection(
            x2, row0, nc, mod4, row(norm1_w[0]), wm, wg_low, afb, bias_fb, conv_w[0], s0f, seq)
        h, xw, idx_t, rank_t, wt, cnt = _post_mixer(
            x2, row0, o_i, a_b, vb, dec, s0b, sr, yc, gnw, wo, mod4, row(norm2_w[0]),
            router_w[0].T, router_b[0][:, None], tri, seq)

        group_start, tile_expert, tile_first, tile_next, n_used = _plan(cnt[:, 0], n_tiles)
        slot_t = _slots(group_start, idx_t, rank_t)
        xs = _sc_scatter_rows(xw, slot_t, n_tiles * EXPERT_TILE)
        routed.append((h, wt, slot_t.reshape(-1), xs, tile_expert, tile_first, tile_next, n_used, row0))

    gathered = []
    for h, wt, slot_flat, xs, tile_expert, tile_first, tile_next, n_used, row0 in routed:
        ys = _experts(tile_expert, tile_first, tile_next, n_used, xs, w_gu[0], w_dn[0],
                      bg_e, bu_e, bd_e)
        y4 = _sc_gather_rows(ys, slot_flat).reshape(TOP_K, h.shape[0], d // 2)
        gathered.append((y4, h, wt, row0))

    out = None
    for y4, h, wt, row0 in gathered:
        out = _combine(y4, h, wt, mod4, row(final_norm_w), seq, row0, n, out)
    return out.reshape(bsz, seq, d)
```

```python
import functools

import jax
import jax.numpy as jnp
from jax import lax
from jax.experimental import pallas as pl
from jax.experimental.pallas import tpu as pltpu
from jax.experimental.pallas import tpu_sc as plsc

D_MODEL = 1024
GLA_HEADS = 4
GLA_DK = 64
GLA_DV = 128
GLA_QK = GLA_HEADS * GLA_DK
GLA_WIDTH = GLA_HEADS * GLA_DV
CONV_WIDTH = 512
GATE_RANK = 16
GATE_PAD = 128
GATE_NORMALIZER = 16.0
CHUNK = 64
GRID_W = 64
N_EXPERTS = 32
TOP_K = 4
SWIGLU_LIMIT = 7.0
SWIGLU_ALPHA = 1.702
EPS = 1e-6

TOKEN_TILE = 1024
POST_TILE = 1024
PROJ_SUB = 1024
PROJ_COLS = 1024
INTRA_GROUP = 256
EXPERT_TILE = 512
COMBINE_TILE = 1024
VMEM_LIMIT = 56 * 1024 * 1024

F32 = jnp.float32
BF16 = jnp.bfloat16
HI_MASK = 0xFFFF0000
LOG2_E = 1.4426950408889634


def _cparams(*sem):
    return pltpu.CompilerParams(dimension_semantics=sem, vmem_limit_bytes=VMEM_LIMIT)


def _mod_spec(d, component, batch_of_step):
    return pl.BlockSpec((None, None, 1, d), lambda i: (batch_of_step(i), component, 0, 0))


def _rms(x):
    return x * lax.rsqrt(jnp.mean(x * x, axis=-1, keepdims=True) + EPS)


def _sigmoid(x, scale=1.0):
    return 1.0 / (1.0 + jnp.exp2(x * (-scale * LOG2_E)))


def _log_sigmoid(z):
    return jnp.minimum(z, 0.0) - jnp.log(1.0 + jnp.exp(-jnp.abs(z)))


def _dot(a, b):
    return jnp.dot(a, b, preferred_element_type=F32)


def _dot_nt(a, b):
    return lax.dot_general(a, b, (((1,), (1,)), ((), ())), preferred_element_type=F32)


def _dot_tn(a, b):
    return lax.dot_general(a, b, (((0,), (0,)), ((), ())), preferred_element_type=F32)


def _split_bf16(a):
    hi = a.astype(BF16)
    lo = (a - hi.astype(F32)).astype(BF16)
    return hi, lo


def _dot3(a, b, dot):
    a_hi, a_lo = _split_bf16(a)
    b_hi, b_lo = _split_bf16(b)
    return dot(a_hi, b_hi) + (dot(a_hi, b_lo) + dot(a_lo, b_hi))


def _segment_scan(x, seg, reverse, after_step=None):
    n = x.shape[0]
    row = lax.broadcasted_iota(jnp.int32, x.shape, 0) & (seg - 1)
    s = 1
    while s < seg:
        if reverse:
            shifted = pltpu.roll(x, n - s, 0)
            x = x + jnp.where(row < seg - s, shifted, 0.0)
        else:
            shifted = pltpu.roll(x, s, 0)
            x = x + jnp.where(row >= s, shifted, 0.0)
        if after_step is not None:
            after_step()
        s *= 2
    return x


def _pack_halves(x):
    n = x.shape[1] // 2
    bits = pltpu.bitcast(x.astype(BF16).astype(F32), jnp.uint32)
    return (bits[:, :n] & jnp.uint32(HI_MASK)) | (bits[:, n:] >> 16)


def _unpack_halves(w):
    hi = pltpu.bitcast(w & jnp.uint32(HI_MASK), F32)
    lo = pltpu.bitcast(w << 16, F32)
    return jnp.concatenate([hi, lo], axis=1)


def _mod_kernel(c_ref, w_ref, b_ref, o_ref):
    c = c_ref[...]
    s = c * _sigmoid(c)
    o_ref[...] = _dot3(s, w_ref[...], _dot) + b_ref[...]


def _modulation(cc, w_ada, b_ada):
    rows, d = cc.shape
    n = w_ada.shape[1]
    bn = 1536
    return pl.pallas_call(
        _mod_kernel,
        out_shape=jax.ShapeDtypeStruct((rows, n), F32),
        grid=(n // bn,),
        in_specs=[pl.BlockSpec((rows, d), lambda j: (0, 0)),
                  pl.BlockSpec((d, bn), lambda j: (0, j)),
                  pl.BlockSpec((1, bn), lambda j: (0, j))],
        out_specs=pl.BlockSpec((rows, bn), lambda j: (0, j)),
        compiler_params=_cparams("arbitrary"),
        name="modulation",
    )(cc, w_ada, b_ada)


def _ctx_kernel(x_ref, sh_ref, sc_ref, nw_ref, wkv_ref, wg_ref, afb_ref, bias_ref, sf_ref, sb_ref):
    x = x_ref[...]
    n = x.shape[0]
    hn = _rms(x) * nw_ref[...] * (1.0 + sc_ref[...]) + sh_ref[...]
    hb = hn.astype(BF16)
    kv = _dot(hb, wkv_ref[...])
    g = _dot(hb, wg_ref[...])
    z = _dot(g.astype(BF16), afb_ref[...]) + bias_ref[...]
    la = _log_sigmoid(z) * (1.0 / GATE_NORMALIZER)
    p = _segment_scan(la, n, False)
    p_f, p_b = p[:, :GLA_QK], p[:, GLA_QK:]
    w_f = jnp.exp(p_f[n - 1:n, :] - p_f)
    w_b = jnp.exp(p_b - la[:, GLA_QK:])
    k = kv[:, :GLA_QK]
    vb = kv[:, GLA_QK:].astype(BF16)
    ke_f = (k * w_f).astype(BF16)
    ke_b = (k * w_b).astype(BF16)
    for h in range(GLA_HEADS):
        vh = vb[:, h * GLA_DV:(h + 1) * GLA_DV]
        sf_ref[h] = _dot_tn(ke_f[:, h * GLA_DK:(h + 1) * GLA_DK], vh)
        sb_ref[h] = _dot_tn(ke_b[:, h * GLA_DK:(h + 1) * GLA_DK], vh)


def _context_states(ctx, mod4, nw1, wkv, wg, afb, bias_fb):
    bsz, n, d = ctx.shape
    ctx_row = lambda b: bsz
    full = lambda a: pl.BlockSpec(a.shape, lambda b: (0,) * a.ndim)
    st = jax.ShapeDtypeStruct((bsz, GLA_HEADS, GLA_DK, GLA_DV), F32)
    st_spec = pl.BlockSpec((None, GLA_HEADS, GLA_DK, GLA_DV), lambda b: (b, 0, 0, 0))
    return pl.pallas_call(
        _ctx_kernel,
        out_shape=(st, st),
        grid=(bsz,),
        in_specs=[pl.BlockSpec((None, n, d), lambda b: (b, 0, 0)),
                  _mod_spec(d, 0, ctx_row), _mod_spec(d, 1, ctx_row), full(nw1), full(wkv), full(wg),
                  full(afb), full(bias_fb)],
        out_specs=(st_spec, st_spec),
        compiler_params=_cparams("arbitrary"),
        name="context_states",
    )(ctx, mod4, mod4, nw1, wkv, wg, afb, bias_fb)


def _chunk_scan(st_ref, qd, ke, v, dec, descending):
    nc = qd.shape[0] // CHUNK
    order = range(nc - 1, -1, -1) if descending else range(nc)
    rows = [slice(c * CHUNK, (c + 1) * CHUNK) for c in range(nc)]
    dec_t = jnp.concatenate([dec, jnp.zeros((GATE_PAD - nc, dec.shape[1]), F32)], axis=0).T
    outs = []
    for h in range(GLA_HEADS):
        lanes = slice(h * GLA_DK, (h + 1) * GLA_DK)
        vl = slice(h * GLA_DV, (h + 1) * GLA_DV)
        kv = [_dot_tn(ke[rows[c], lanes], v[rows[c], vl]) for c in range(nc)]
        s = st_ref[h]
        start = [None] * nc
        for c in order:
            start[c] = s
            s = s * dec_t[lanes, c:c + 1] + kv[c]
        st_ref[h] = s
        outs.append(jnp.concatenate(
            [_dot(qd[rows[c], lanes], start[c].astype(BF16)) for c in range(nc)], axis=0))
    return jnp.concatenate(outs, axis=1)


def _proj_kernel(tpb, x_ref, sh_ref, sc_ref, nw_ref, wm_ref, wg_ref, afb_ref, bias_ref, cw_ref,
                 s0_ref, ab_ref, v_ref, oi_ref, sr_ref, yc_ref, dec_ref, p_scr, st_ref):
    @pl.when(pl.program_id(0) % tpb == 0)
    def _():
        st_ref[...] = s0_ref[...]

    subs = range(x_ref.shape[0] // PROJ_SUB)
    nc = PROJ_SUB // CHUNK
    nblk = wm_ref.shape[1] // PROJ_COLS
    factors = []
    for sub in subs:
        x = x_ref[sub * PROJ_SUB:(sub + 1) * PROJ_SUB, :]
        hn = _rms(x) * nw_ref[...] * (1.0 + sc_ref[...]) + sh_ref[...]
        hb = hn.astype(BF16)
        g = _dot(hb, wg_ref[...])
        z = _dot(g.astype(BF16), afb_ref[...]) + bias_ref[...]

        todo = iter(range(nblk))

        def wide_block():
            j = next(todo, None)
            if j is not None:
                cols = slice(j * PROJ_COLS, (j + 1) * PROJ_COLS)
                p_scr[sub, :, cols] = _dot(hb, wm_ref[:, cols])

        la = _log_sigmoid(z) * (1.0 / GATE_NORMALIZER)
        wide_block()
        b_f = _segment_scan(la[:, :GLA_QK], CHUNK, False, wide_block)
        b_b = _segment_scan(la[:, GLA_QK:], CHUNK, True, wide_block)
        b_f3 = b_f.reshape(nc, CHUNK, GLA_QK)
        tot_f = b_f3[:, CHUNK - 1:CHUNK, :]
        b_b3 = b_b.reshape(nc, CHUNK, GLA_QK)
        tot_b = b_b3[:, 0:1, :]
        fac = []
        for arg in (b_f, -b_f, (tot_f - b_f3).reshape(PROJ_SUB, GLA_QK),
                    b_b, -b_b, (tot_b - b_b3).reshape(PROJ_SUB, GLA_QK)):
            fac.append(jnp.exp(arg))
            wide_block()
        fac += [jnp.exp(tot_f).reshape(nc, GLA_QK), jnp.exp(tot_b).reshape(nc, GLA_QK)]
        factors.append(fac)
        for _ in range(nblk):
            wide_block()
    for sub in subs:
        _proj_rows(sub, factors[sub], cw_ref, ab_ref, v_ref, oi_ref, sr_ref, yc_ref,
                   dec_ref, p_scr.at[sub], st_ref)


def _proj_rows(sub, factors, cw_ref, ab_ref, v_ref, oi_ref, sr_ref, yc_ref, dec_ref, p, st_ref):
    tm = PROJ_SUB
    nc = tm // CHUNK
    rs = slice(sub * tm, (sub + 1) * tm)
    cs = slice(sub * nc, (sub + 1) * nc)
    e_f, ei_f, ee_f, e_b, ei_b, ee_b, dec_f, dec_b = factors

    q = p[:, 0:GLA_QK] * (GLA_DK ** -0.5)
    k = p[:, GLA_QK:2 * GLA_QK]
    vb = p[:, 2 * GLA_QK:2 * GLA_QK + GLA_WIDTH].astype(BF16)
    v_ref[rs, :] = vb
    qd_f = (q * e_f).astype(BF16)
    ki_f = (k * ei_f).astype(BF16)
    ke_f = (k * ee_f).astype(BF16)
    qd_b = (q * e_b).astype(BF16)
    ki_b = (k * ei_b).astype(BF16)
    ke_b = (k * ee_b).astype(BF16)

    ab_ref[rs, :GLA_QK] = qd_b
    ab_ref[rs, GLA_QK:] = ke_b
    dec_ref[cs, :GLA_QK] = dec_f
    dec_ref[cs, GLA_QK:] = dec_b

    inter_f = _chunk_scan(st_ref, qd_f, ke_f, vb, dec_f, False)

    gi = INTRA_GROUP
    ri = lax.broadcasted_iota(jnp.int32, (gi, gi), 0)
    ci = lax.broadcasted_iota(jnp.int32, (gi, gi), 1)
    same = (ri // CHUNK) == (ci // CHUNK)
    lower = same & (ci <= ri)
    upper = same & (ci >= ri)
    for gidx in range(tm // gi):
        rows = slice(gidx * gi, (gidx + 1) * gi)
        orows = slice(sub * tm + gidx * gi, sub * tm + (gidx + 1) * gi)
        for h in range(GLA_HEADS):
            lanes = slice(h * GLA_DK, (h + 1) * GLA_DK)
            s_f = _dot_nt(qd_f[rows, lanes], ki_f[rows, lanes])
            s_b = _dot_nt(qd_b[rows, lanes], ki_b[rows, lanes])
            s = jnp.where(lower, s_f, 0.0) + jnp.where(upper, s_b, 0.0)
            vl = slice(h * GLA_DV, (h + 1) * GLA_DV)
            oi_ref[orows, vl] = _dot(s.astype(BF16), vb[rows, vl]) + inter_f[rows, vl]

    r = p[:, 1024:1536]
    sr_ref[rs, :] = (r * _sigmoid(r)).astype(BF16)

    u = p[:, 2048:2560] * p[:, 2560:3072]
    row = lax.broadcasted_iota(jnp.int32, u.shape, 0) & (GRID_W - 1)
    prev = jnp.where(row >= 1, pltpu.roll(u, 1, 0), 0.0)
    nxt = jnp.where(row < GRID_W - 1, pltpu.roll(u, tm - 1, 0), 0.0)
    cw = cw_ref[...]
    conv = prev * cw[0:1, :] + u * cw[1:2, :] + nxt * cw[2:3, :]
    yc_ref[rs, :] = (p[:, 1536:2048] * conv).astype(BF16)


def _projection(x2, row0, n, mod4, nw1, wm, wg, afb, bias_fb, conv_w, s0f, seq):
    d = x2.shape[1]
    tm = TOKEN_TILE
    tpb = seq // tm
    t0 = row0 // tm
    batch = lambda i: i // tpb + row0 // seq
    full = lambda a: pl.BlockSpec(a.shape, lambda i: (0,) * a.ndim)
    tok = lambda w: pl.BlockSpec((tm, w), lambda i: (i, 0))
    st_spec = pl.BlockSpec((None, GLA_HEADS, GLA_DK, GLA_DV), lambda i: (batch(i), 0, 0, 0))
    out_shape = (
        jax.ShapeDtypeStruct((n, 2 * GLA_QK), BF16),
        jax.ShapeDtypeStruct((n, GLA_WIDTH), BF16),
        jax.ShapeDtypeStruct((n, GLA_WIDTH), F32),
        jax.ShapeDtypeStruct((n, GLA_WIDTH), BF16),
        jax.ShapeDtypeStruct((n, CONV_WIDTH), BF16),
        jax.ShapeDtypeStruct((n // CHUNK, 2 * GLA_QK), F32),
    )
    out_specs = (tok(2 * GLA_QK), tok(GLA_WIDTH), tok(GLA_WIDTH), tok(GLA_WIDTH),
                 tok(CONV_WIDTH), pl.BlockSpec((tm // CHUNK, 2 * GLA_QK), lambda i: (i, 0)))
    return pl.pallas_call(
        functools.partial(_proj_kernel, tpb),
        out_shape=out_shape,
        grid=(n // tm,),
        in_specs=[pl.BlockSpec((tm, d), lambda i: (i + t0, 0)), _mod_spec(d, 0, batch),
                  _mod_spec(d, 1, batch), full(nw1),
                  full(wm), full(wg), full(afb), full(bias_fb), full(conv_w), st_spec],
        out_specs=out_specs,
        scratch_shapes=[pltpu.VMEM((tm // PROJ_SUB, PROJ_SUB, wm.shape[1]), F32),
                        pltpu.VMEM((GLA_HEADS, GLA_DK, GLA_DV), F32)],
        compiler_params=_cparams("arbitrary"),
        name="projection",
    )(x2, mod4, mod4, nw1, wm, wg, afb, bias_fb, conv_w, s0f)


def _post_kernel(tpb, x_ref, oi_ref, ab_ref, v_ref, dec_ref, s0_ref, sr_ref, yc_ref, gnw_ref, wo_ref,
                 g1_ref, nw2_ref, sh2_ref, sc2_ref, rwt_ref, rb_ref, tri_ref,
                 h_ref, xw_ref, idx_ref, rank_ref, wt_ref, cnt_ref, run_ref, st_ref):
    @pl.when(pl.program_id(0) == 0)
    def _():
        run_ref[...] = jnp.zeros_like(run_ref)

    @pl.when(pl.program_id(0) % tpb == 0)
    def _():
        st_ref[...] = s0_ref[...]

    tm = x_ref.shape[0]
    inter_b = _chunk_scan(st_ref, ab_ref[:, :GLA_QK], ab_ref[:, GLA_QK:], v_ref[...],
                          dec_ref[:, GLA_QK:], True)
    o = oi_ref[...] + inter_b
    normed = jnp.concatenate(
        [_rms(o[:, h * GLA_DV:(h + 1) * GLA_DV]) for h in range(GLA_HEADS)], axis=1)
    y_gla = (normed * gnw_ref[...] * sr_ref[...].astype(F32)).astype(BF16)
    wo = wo_ref[...]
    y = _dot(y_gla, wo[:GLA_WIDTH, :]) + _dot(yc_ref[...], wo[GLA_WIDTH:, :])
    h = x_ref[...] + g1_ref[...] * y
    h_ref[...] = h
    hn = _rms(h) * nw2_ref[...] * (1.0 + sc2_ref[...]) + sh2_ref[...]
    xw_ref[...] = _pack_halves(hn)

    logits = _dot3(rwt_ref[...], hn, _dot_nt) + rb_ref[...]
    eid = lax.broadcasted_iota(jnp.int32, logits.shape, 0)
    vals, idxs = [], []
    work = logits
    for _ in range(TOP_K):
        m = jnp.max(work, axis=0, keepdims=True)
        i = jnp.min(jnp.where(work == m, eid, N_EXPERTS), axis=0, keepdims=True)
        vals.append(m)
        idxs.append(i)
        work = jnp.where(eid == i, -jnp.inf, work)
    exps = [jnp.exp(v - vals[0]) for v in vals]
    denom = exps[0] + exps[1] + exps[2] + exps[3]
    wts = [e / denom for e in exps]

    onehot = jnp.zeros(logits.shape, F32)
    for i in idxs:
        onehot = onehot + jnp.where(eid == i, 1.0, 0.0)
    before = _dot(onehot.astype(BF16), tri_ref[...]) + run_ref[:, 0:1]
    for kk in range(TOP_K):
        idx_ref[kk:kk + 1, :] = idxs[kk]
        rk = jnp.sum(jnp.where(eid == idxs[kk], before, 0.0), axis=0, keepdims=True)
        rank_ref[kk:kk + 1, :] = rk.astype(jnp.int32)
    run = run_ref[...] + jnp.sum(onehot, axis=1, keepdims=True)
    run_ref[...] = run
    cnt_ref[...] = run.astype(jnp.int32)

    sub = lax.broadcasted_iota(jnp.int32, (GATE_PAD, tm), 0)
    wpad = jnp.zeros((GATE_PAD, tm), F32)
    for kk in range(TOP_K):
        wpad = wpad + jnp.where(sub == kk, wts[kk], 0.0)
    wt_ref[...] = wpad.T


def _post_mixer(x2, row0, o_i, a_b, vb, dec, s0b, sr, yc, gnw, wo, mod4, nw2, rwt, rb, tri, seq):
    n, d = o_i.shape[0], x2.shape[1]
    tm = POST_TILE
    tpb = seq // tm
    t0 = row0 // tm
    rev = lambda i: (i // tpb) * tpb + (tpb - 1 - i % tpb)
    batch = lambda i: i // tpb + row0 // seq
    full = lambda a: pl.BlockSpec(a.shape, lambda i: (0,) * a.ndim)
    tok = lambda w: pl.BlockSpec((tm, w), lambda i: (rev(i), 0))
    lane_tok = pl.BlockSpec((TOP_K, tm), lambda i: (0, rev(i)))
    st_spec = pl.BlockSpec((None, GLA_HEADS, GLA_DK, GLA_DV), lambda i: (batch(i), 0, 0, 0))
    dec_spec = pl.BlockSpec((tm // CHUNK, 2 * GLA_QK), lambda i: (rev(i), 0))
    out_shape = (
        jax.ShapeDtypeStruct((n, d), F32),
        jax.ShapeDtypeStruct((n, d // 2), jnp.uint32),
        jax.ShapeDtypeStruct((TOP_K, n), jnp.int32),
        jax.ShapeDtypeStruct((TOP_K, n), jnp.int32),
        jax.ShapeDtypeStruct((n, GATE_PAD), F32),
        jax.ShapeDtypeStruct((N_EXPERTS, GATE_PAD), jnp.int32),
    )
    out_specs = (tok(d), tok(d // 2), lane_tok, lane_tok, tok(GATE_PAD),
                 pl.BlockSpec((N_EXPERTS, GATE_PAD), lambda i: (0, 0)))
    return pl.pallas_call(
        functools.partial(_post_kernel, tpb),
        out_shape=out_shape,
        grid=(n // tm,),
        in_specs=[pl.BlockSpec((tm, d), lambda i: (rev(i) + t0, 0)), tok(GLA_WIDTH),
                  tok(2 * GLA_QK), tok(GLA_WIDTH), dec_spec, st_spec, tok(GLA_WIDTH),
                  tok(CONV_WIDTH), full(gnw), full(wo), _mod_spec(d, 2, batch), full(nw2),
                  _mod_spec(d, 3, batch), _mod_spec(d, 4, batch),
                  full(rwt), full(rb), full(tri)],
        out_specs=out_specs,
        scratch_shapes=[pltpu.VMEM((N_EXPERTS, GATE_PAD), F32),
                        pltpu.VMEM((GLA_HEADS, GLA_DK, GLA_DV), F32)],
        compiler_params=_cparams("arbitrary"),
        name="post_mixer",
    )(x2, o_i, a_b, vb, dec, s0b, sr, yc, gnw, wo, mod4, nw2, mod4, mod4, rwt, rb, tri)


def _plan_kernel(cnt_ref, start_ref, te_ref, first_ref, next_ref, nu_ref):
    n_tiles = te_ref.shape[0]
    shift = EXPERT_TILE.bit_length() - 1

    def clear(t, carry):
        first_ref[t] = 0
        next_ref[t] = -1
        return carry

    lax.fori_loop(0, n_tiles, clear, 0)

    def group(e, tile0):
        nt = lax.shift_right_logical(cnt_ref[e] + (EXPERT_TILE - 1), shift)
        start_ref[e] = tile0 * EXPERT_TILE

        def fill(t, carry):
            te_ref[tile0 + t] = e
            return carry

        lax.fori_loop(0, nt, fill, 0)

        @pl.when(nt > 0)
        def _():
            first_ref[tile0] = 1

        return tile0 + nt

    n_used = lax.fori_loop(0, N_EXPERTS, group, 0)
    nu_ref[0] = n_used
    last = te_ref[n_used - 1]

    def tail(t, carry):
        te_ref[t] = last
        return carry

    lax.fori_loop(n_used, n_tiles, tail, 0)

    def link(k, nxt):
        e = N_EXPERTS - 1 - k
        has = cnt_ref[e] > 0

        @pl.when(has)
        def _():
            next_ref[lax.shift_right_logical(start_ref[e], shift)] = nxt

        return jnp.where(has, e, nxt)

    lax.fori_loop(0, N_EXPERTS, link, -1)


def _plan(counts, n_tiles):
    assert EXPERT_TILE & (EXPERT_TILE - 1) == 0
    smem = pl.BlockSpec(memory_space=pltpu.SMEM)
    vec = lambda m: jax.ShapeDtypeStruct((m,), jnp.int32)
    return pl.pallas_call(
        _plan_kernel,
        out_shape=(vec(N_EXPERTS), vec(n_tiles), vec(n_tiles), vec(n_tiles), vec(1)),
        in_specs=[smem],
        out_specs=(smem, smem, smem, smem, smem),
        name="plan",
    )(counts)


def _slot_kernel(start_ref, idx_ref, rank_ref, slot_ref):
    idx = idx_ref[...]
    base = jnp.zeros(idx.shape, jnp.int32)
    for e in range(N_EXPERTS):
        base = jnp.where(idx == e, start_ref[e], base)
    slot_ref[...] = base + rank_ref[...]


def _slots(group_start, idx_t, rank_t):
    k, n = idx_t.shape
    bn = min(n, 8192)
    blk = pl.BlockSpec((k, bn), lambda i, s: (0, i))
    return pl.pallas_call(
        _slot_kernel,
        out_shape=jax.ShapeDtypeStruct((k, n), jnp.int32),
        grid_spec=pltpu.PrefetchScalarGridSpec(
            num_scalar_prefetch=1, grid=(n // bn,), in_specs=[blk, blk], out_specs=blk),
        compiler_params=_cparams("arbitrary"),
        name="slots",
    )(group_start, idx_t, rank_t)


SC_CORES = 2
SC_SUBCORES = 16
SC_WORKERS = SC_CORES * SC_SUBCORES
SCATTER_CHUNK = 128
GATHER_CHUNK = 64


def _sc_mesh():
    return plsc.VectorSubcoreMesh(core_axis_name="c", subcore_axis_name="s")


def _sc_worker():
    return lax.axis_index("s") * SC_CORES + lax.axis_index("c")


def _sc_scatter_rows(src, idx_t, n_out):
    n, w = src.shape
    k = idx_t.shape[0]
    c = SCATTER_CHUNK
    per = n // SC_WORKERS
    nch = per // c
    assert per * SC_WORKERS == n and nch * c == per

    def body(src_hbm, idx_hbm, out_hbm, idx_v, rows_v):
        wid = _sc_worker()
        for kk in range(k):
            pltpu.sync_copy(idx_hbm.at[kk, wid], idx_v.at[kk])

        @pl.loop(0, nch)
        def _(j):
            pltpu.sync_copy(src_hbm.at[pl.ds(wid * per + j * c, c)], rows_v)
            for kk in range(k):
                pltpu.sync_copy(rows_v, out_hbm.at[idx_v.at[kk, j]])

    return pl.kernel(
        body,
        out_type=jax.ShapeDtypeStruct((n_out, w), src.dtype),
        mesh=_sc_mesh(),
        scratch_types=[pltpu.VMEM((k, nch, c), jnp.int32), pltpu.VMEM((c, w), src.dtype)],
        name="sc_dispatch",
    )(src, idx_t.reshape(k, SC_WORKERS, nch, c))


def _sc_gather_rows(table, idx):
    w = table.shape[1]
    m = idx.shape[0]
    c = GATHER_CHUNK
    per = m // SC_WORKERS
    nch = per // c
    assert per * SC_WORKERS == m and nch * c == per and nch % 2 == 0

    def body(table_hbm, idx_hbm, out_hbm, idx_v, rows_v, sem0, sem1):
        wid = _sc_worker()
        base = wid * per
        sems = (sem0, sem1)
        pltpu.sync_copy(idx_hbm.at[wid], idx_v)

        def gather(j, b):
            return pltpu.make_async_copy(table_hbm.at[idx_v.at[j]], rows_v.at[b], sems[b])

        gather(0, 0).start()

        @pl.loop(0, nch, step=2)
        def _(g):
            for b in range(2):
                j = g + b
                gather(j, b).wait()

                @pl.when(j + 1 < nch)
                def _():
                    gather(j + 1, 1 - b).start()

                pltpu.sync_copy(rows_v.at[b], out_hbm.at[pl.ds(base + j * c, c)])

    return pl.kernel(
        body,
        out_type=jax.ShapeDtypeStruct((m, w), table.dtype),
        mesh=_sc_mesh(),
        scratch_types=[pltpu.VMEM((nch, c), jnp.int32), pltpu.VMEM((2, c, w), table.dtype),
                       pltpu.SemaphoreType.DMA, pltpu.SemaphoreType.DMA],
        name="sc_combine_gather",
    )(table, idx.reshape(SC_WORKERS, nch, c))


PERM_BLOCK = 256


def _expert_kernel(te_ref, first_ref, next_ref, nu_ref, x_ref, wgu_hbm, wdn_hbm, perm_ref, bg_ref,
                   bu_ref, bd_ref, y_ref, wgu_f, wdn_f, wg_s, wu_s, wd_s, sem):
    i = pl.program_id(0)
    used = i < nu_ref[0]

    def fetch(e):
        return (pltpu.make_async_copy(wgu_hbm.at[e], wgu_f, sem.at[0]),
                pltpu.make_async_copy(wdn_hbm.at[e], wdn_f, sem.at[1]))

    @pl.when(i == 0)
    def _():
        for cp in fetch(te_ref[0]):
            cp.start()

    @pl.when(used & (first_ref[i] == 1))
    def _():
        for cp in fetch(te_ref[i]):
            cp.wait()
        half = PERM_BLOCK // 2
        perm = perm_ref[...]
        for b in range(wgu_f.shape[1] // PERM_BLOCK):
            blk = wgu_f[:, b * PERM_BLOCK:(b + 1) * PERM_BLOCK].astype(BF16)
            t = _dot(blk, perm)
            wg_s[:, b * half:(b + 1) * half] = t[:, :half].astype(BF16)
            wu_s[:, b * half:(b + 1) * half] = t[:, half:].astype(BF16)
        wd_s[...] = wdn_f[...].astype(BF16)

        @pl.when(next_ref[i] >= 0)
        def _():
            for cp in fetch(next_ref[i]):
                cp.start()

    @pl.when(used)
    def _():
        x = _unpack_halves(x_ref[...]).astype(BF16)
        g = _dot(x, wg_s[...]) + bg_ref[...]
        u = _dot(x, wu_s[...]) + bu_ref[...]
        gate = jnp.minimum(g, SWIGLU_LIMIT)
        up = jnp.clip(u, -SWIGLU_LIMIT, SWIGLU_LIMIT)
        act = (up + 1.0) * (gate * _sigmoid(gate, SWIGLU_ALPHA))
        y = _dot(act.astype(BF16), wd_s[...]) + bd_ref[...]
        y_ref[...] = _pack_halves(y)

    @pl.when(jnp.logical_not(used))
    def _():
        y_ref[...] = jnp.zeros_like(y_ref)


def _experts(tile_expert, tile_first, tile_next, n_used, xs, w_gu, w_dn, bg, bu, bd):
    n_slots, half = xs.shape
    tm = EXPERT_TILE
    n_tiles = n_slots // tm
    dm, f, d = w_gu.shape[1], w_dn.shape[1], w_dn.shape[2]
    ph = PERM_BLOCK // 2
    j = jnp.arange(PERM_BLOCK)
    perm = (jnp.arange(PERM_BLOCK)[None, :] == (j // 2 + (j % 2) * ph)[:, None]).astype(BF16)
    row_in = pl.BlockSpec((tm, half), lambda i, te, tf, tn, nu: (jnp.minimum(i, nu[0] - 1), 0))
    row_out = pl.BlockSpec((tm, half),
                           lambda i, te, tf, tn, nu: (jnp.where(i < nu[0], i, n_tiles - 1), 0))
    bspec = lambda a: pl.BlockSpec((None,) + a.shape[1:], lambda i, te, tf, tn, nu: (te[i], 0, 0))
    return pl.pallas_call(
        _expert_kernel,
        out_shape=jax.ShapeDtypeStruct((n_slots, half), jnp.uint32),
        grid_spec=pltpu.PrefetchScalarGridSpec(
            num_scalar_prefetch=4, grid=(n_tiles,),
            in_specs=[row_in, pl.BlockSpec(memory_space=pl.ANY), pl.BlockSpec(memory_space=pl.ANY),
                      pl.BlockSpec((PERM_BLOCK, PERM_BLOCK), lambda i, te, tf, tn, nu: (0, 0)),
                      bspec(bg), bspec(bu), bspec(bd)],
            out_specs=row_out,
            scratch_shapes=[pltpu.VMEM((dm, 2 * f), F32), pltpu.VMEM((f, d), F32),
                            pltpu.VMEM((dm, f), BF16), pltpu.VMEM((dm, f), BF16),
                            pltpu.VMEM((f, d), BF16), pltpu.SemaphoreType.DMA((2,))]),
        compiler_params=_cparams("arbitrary"),
        name="experts",
    )(tile_expert, tile_first, tile_next, n_used, xs, w_gu, w_dn, perm, bg, bu, bd)


def _combine_kernel(y4_ref, h_ref, wt_ref, g2_ref, fnw_ref, *rest):
    o_ref = rest[-1]
    wt = wt_ref[...]
    acc = jnp.zeros(h_ref.shape, F32)
    for kk in range(TOP_K):
        acc = acc + wt[:, kk:kk + 1] * _unpack_halves(y4_ref[kk])
    h = h_ref[...] + g2_ref[...] * acc
    o_ref[...] = _rms(h) * fnw_ref[...]


def _combine(y4, h, wt, mod4, fnw, seq, row0, n_total, prev_out):
    n, d = h.shape
    tt = COMBINE_TILE
    tpb = seq // tt
    t0 = row0 // tt
    in_specs = [pl.BlockSpec((TOP_K, tt, d // 2), lambda i: (0, i, 0)),
                pl.BlockSpec((tt, d), lambda i: (i, 0)),
                pl.BlockSpec((tt, GATE_PAD), lambda i: (i, 0)),
                _mod_spec(d, 5, lambda i: i // tpb + row0 // seq),
                pl.BlockSpec((1, d), lambda i: (0, 0))]
    args = [y4, h, wt, mod4, fnw]
    aliases = {}
    if prev_out is not None:
        in_specs.append(pl.BlockSpec(memory_space=pl.ANY))
        args.append(prev_out)
        aliases = {len(args) - 1: 0}
    return pl.pallas_call(
        _combine_kernel,
        out_shape=jax.ShapeDtypeStruct((n_total, d), F32),
        grid=(n // tt,),
        in_specs=in_specs,
        out_specs=pl.BlockSpec((tt, d), lambda i: (i + t0, 0)),
        input_output_aliases=aliases,
        compiler_params=_cparams("arbitrary"),
        name="combine",
    )(*args)


def kernel(x, c, ctx, c_ctx, w_ada, b_ada, norm1_w, w_in, a_up_f, a_bias_f, a_up_b, a_bias_b,
           gla_norm_w, conv_w, w_out, norm2_w, router_w, router_b, w_gu, b_gu, w_dn, b_dn,
           final_norm_w):
    bsz, seq, d = x.shape
    n = bsz * seq
    assert w_ada.shape[0] == 1 and d == D_MODEL
    assert seq % TOKEN_TILE == 0 and seq % POST_TILE == 0 and seq % GRID_W == 0
    assert ctx.shape[1] & (ctx.shape[1] - 1) == 0

    w_in0 = w_in[0]
    v_end = 2 * GLA_QK + GLA_WIDTH
    r_end = v_end + GLA_WIDTH
    g_end = r_end + 2 * GATE_RANK
    wm = jnp.concatenate([w_in0[:, :r_end], w_in0[:, g_end:]], axis=1).astype(BF16)
    wg_low = jnp.pad(w_in0[:, r_end:g_end], ((0, 0), (0, GATE_PAD - 2 * GATE_RANK))).astype(BF16)
    afb = jnp.zeros((GATE_PAD, 2 * GLA_QK), F32)
    afb = afb.at[:GATE_RANK, :GLA_QK].set(a_up_f[0]).at[GATE_RANK:2 * GATE_RANK, GLA_QK:].set(a_up_b[0])
    afb = afb.astype(BF16)
    bias_fb = jnp.concatenate([a_bias_f[0], a_bias_b[0]])[None]
    wkv = wm[:, GLA_QK:v_end]
    row = lambda a: a.reshape(1, -1)

    mod = _modulation(jnp.concatenate([c, c_ctx[None]], axis=0), w_ada[0], row(b_ada[0]))
    mod4 = mod.reshape(bsz + 1, 6, 1, d)

    s0f, s0b = _context_states(ctx, mod4, row(norm1_w[0]), wkv, wg_low, afb, bias_fb)

    x2 = x.reshape(n, d)
    tri = jnp.triu(jnp.ones((POST_TILE, POST_TILE), BF16), k=1)
    gnw = jnp.tile(gla_norm_w[0], GLA_HEADS)[None]
    wo = w_out[0].astype(BF16)
    bg_e, bu_e, bd_e = b_gu[0][:, None, 0::2], b_gu[0][:, None, 1::2], b_dn[0][:, None, :]

    if bsz % 8 == 0:
        chunk_batches = (5 * bsz // 8, 3 * bsz // 8)
    elif bsz % 2 == 0:
        chunk_batches = (bsz // 2, bsz // 2)
    else:
        chunk_batches = (bsz,)
    routed = []
    b0 = 0
    for cb in chunk_batches:
        row0 = b0 * seq
        b0 += cb
        nc = cb * seq
        n_tiles = nc * TOP_K // EXPERT_TILE + N_EXPERTS
        a_b, vb, o_i, sr, yc, dec = _projection(
            x2, row0, nc, mod4, row(norm1_w[0]), wm, wg_low, afb, bias_fb, conv_w[0], s0f, seq)
        h, xw, idx_t, rank_t, wt, cnt = _post_mixer(
            x2, row0, o_i, a_b, vb, dec, s0b, sr, yc, gnw, wo, mod4, row(norm2_w[0]),
            router_w[0].T, router_b[0][:, None], tri, seq)

        group_start, tile_expert, tile_first, tile_next, n_used = _plan(cnt[:, 0], n_tiles)
        slot_t = _slots(group_start, idx_t, rank_t)
        xs = _sc_scatter_rows(xw, slot_t, n_tiles * EXPERT_TILE)
        routed.append((h, wt, slot_t.reshape(-1), xs, tile_expert, tile_first, tile_next, n_used, row0))

    gathered = []
    for h, wt, slot_flat, xs, tile_expert, tile_first, tile_next, n_used, row0 in routed:
        ys = _experts(tile_expert, tile_first, tile_next, n_used, xs, w_gu[0], w_dn[0],
                      bg_e, bu_e, bd_e)
        y4 = _sc_gather_rows(ys, slot_flat).reshape(TOP_K, h.shape[0], d // 2)
        gathered.append((y4, h, wt, row0))

    out = None
    for y4, h, wt, row0 in gathered:
        out = _combine(y4, h, wt, mod4, row(final_norm_w), seq, row0, n, out)
    return out.reshape(bsz, seq, d)
```

```python
import functools

import jax
import jax.numpy as jnp
from jax import lax
from jax.experimental import pallas as pl
from jax.experimental.pallas import tpu as pltpu
from jax.experimental.pallas import tpu_sc as plsc

D_MODEL = 1024
GLA_HEADS = 4
GLA_DK = 64
GLA_DV = 128
GLA_QK = GLA_HEADS * GLA_DK
GLA_WIDTH = GLA_HEADS * GLA_DV
CONV_WIDTH = 512
GATE_RANK = 16
GATE_PAD = 128
GATE_NORMALIZER = 16.0
CHUNK = 64
GRID_W = 64
N_EXPERTS = 32
TOP_K = 4
SWIGLU_LIMIT = 7.0
SWIGLU_ALPHA = 1.702
EPS = 1e-6

TOKEN_TILE = 1024
POST_TILE = 1024
PROJ_SUB = 1024
PROJ_COLS = 1024
INTRA_GROUP = 256
EXPERT_TILE = 512
COMBINE_TILE = 1024
VMEM_LIMIT = 56 * 1024 * 1024

F32 = jnp.float32
BF16 = jnp.bfloat16
HI_MASK = 0xFFFF0000
LOG2_E = 1.4426950408889634


def _cparams(*sem):
    return pltpu.CompilerParams(dimension_semantics=sem, vmem_limit_bytes=VMEM_LIMIT)


def _mod_spec(d, component, batch_of_step):
    return pl.BlockSpec((None, None, 1, d), lambda i: (batch_of_step(i), component, 0, 0))


def _rms(x):
    return x * lax.rsqrt(jnp.mean(x * x, axis=-1, keepdims=True) + EPS)


def _sigmoid(x, scale=1.0):
    return 1.0 / (1.0 + jnp.exp2(x * (-scale * LOG2_E)))


def _log_sigmoid(z):
    return jnp.minimum(z, 0.0) - jnp.log(1.0 + jnp.exp(-jnp.abs(z)))


def _dot(a, b):
    return jnp.dot(a, b, preferred_element_type=F32)


def _dot_nt(a, b):
    return lax.dot_general(a, b, (((1,), (1,)), ((), ())), preferred_element_type=F32)


def _dot_tn(a, b):
    return lax.dot_general(a, b, (((0,), (0,)), ((), ())), preferred_element_type=F32)


def _split_bf16(a):
    hi = a.astype(BF16)
    lo = (a - hi.astype(F32)).astype(BF16)
    return hi, lo


def _dot3(a, b, dot):
    a_hi, a_lo = _split_bf16(a)
    b_hi, b_lo = _split_bf16(b)
    return dot(a_hi, b_hi) + (dot(a_hi, b_lo) + dot(a_lo, b_hi))


def _segment_scan(x, seg, reverse, after_step=None):
    n = x.shape[0]
    row = lax.broadcasted_iota(jnp.int32, x.shape, 0) & (seg - 1)
    s = 1
    while s < seg:
        if reverse:
            shifted = pltpu.roll(x, n - s, 0)
            x = x + jnp.where(row < seg - s, shifted, 0.0)
        else:
            shifted = pltpu.roll(x, s, 0)
            x = x + jnp.where(row >= s, shifted, 0.0)
        if after_step is not None:
            after_step()
        s *= 2
    return x


def _pack_halves(x):
    n = x.shape[1] // 2
    bits = pltpu.bitcast(x.astype(BF16).astype(F32), jnp.uint32)
    return (bits[:, :n] & jnp.uint32(HI_MASK)) | (bits[:, n:] >> 16)


def _unpack_halves(w):
    hi = pltpu.bitcast(w & jnp.uint32(HI_MASK), F32)
    lo = pltpu.bitcast(w << 16, F32)
    return jnp.concatenate([hi, lo], axis=1)


def _mod_kernel(c_ref, w_ref, b_ref, o_ref):
    c = c_ref[...]
    s = c * _sigmoid(c)
    o_ref[...] = _dot3(s, w_ref[...], _dot) + b_ref[...]


def _modulation(cc, w_ada, b_ada):
    rows, d = cc.shape
    n = w_ada.shape[1]
    bn = 1536
    return pl.pallas_call(
        _mod_kernel,
        out_shape=jax.ShapeDtypeStruct((rows, n), F32),
        grid=(n // bn,),
        in_specs=[pl.BlockSpec((rows, d), lambda j: (0, 0)),
                  pl.BlockSpec((d, bn), lambda j: (0, j)),
                  pl.BlockSpec((1, bn), lambda j: (0, j))],
        out_specs=pl.BlockSpec((rows, bn), lambda j: (0, j)),
        compiler_params=_cparams("arbitrary"),
        name="modulation",
    )(cc, w_ada, b_ada)


def _ctx_kernel(x_ref, sh_ref, sc_ref, nw_ref, wkv_ref, wg_ref, afb_ref, bias_ref, sf_ref, sb_ref):
    x = x_ref[...]
    n = x.shape[0]
    hn = _rms(x) * nw_ref[...] * (1.0 + sc_ref[...]) + sh_ref[...]
    hb = hn.astype(BF16)
    kv = _dot(hb, wkv_ref[...])
    g = _dot(hb, wg_ref[...])
    z = _dot(g.astype(BF16), afb_ref[...]) + bias_ref[...]
    la = _log_sigmoid(z) * (1.0 / GATE_NORMALIZER)
    p = _segment_scan(la, n, False)
    p_f, p_b = p[:, :GLA_QK], p[:, GLA_QK:]
    w_f = jnp.exp(p_f[n - 1:n, :] - p_f)
    w_b = jnp.exp(p_b - la[:, GLA_QK:])
    k = kv[:, :GLA_QK]
    vb = kv[:, GLA_QK:].astype(BF16)
    ke_f = (k * w_f).astype(BF16)
    ke_b = (k * w_b).astype(BF16)
    for h in range(GLA_HEADS):
        vh = vb[:, h * GLA_DV:(h + 1) * GLA_DV]
        sf_ref[h] = _dot_tn(ke_f[:, h * GLA_DK:(h + 1) * GLA_DK], vh)
        sb_ref[h] = _dot_tn(ke_b[:, h * GLA_DK:(h + 1) * GLA_DK], vh)


def _context_states(ctx, mod4, nw1, wkv, wg, afb, bias_fb):
    bsz, n, d = ctx.shape
    ctx_row = lambda b: bsz
    full = lambda a: pl.BlockSpec(a.shape, lambda b: (0,) * a.ndim)
    st = jax.ShapeDtypeStruct((bsz, GLA_HEADS, GLA_DK, GLA_DV), F32)
    st_spec = pl.BlockSpec((None, GLA_HEADS, GLA_DK, GLA_DV), lambda b: (b, 0, 0, 0))
    return pl.pallas_call(
        _ctx_kernel,
        out_shape=(st, st),
        grid=(bsz,),
        in_specs=[pl.BlockSpec((None, n, d), lambda b: (b, 0, 0)),
                  _mod_spec(d, 0, ctx_row), _mod_spec(d, 1, ctx_row), full(nw1), full(wkv), full(wg),
                  full(afb), full(bias_fb)],
        out_specs=(st_spec, st_spec),
        compiler_params=_cparams("arbitrary"),
        name="context_states",
    )(ctx, mod4, mod4, nw1, wkv, wg, afb, bias_fb)


def _chunk_scan(st_ref, qd, ke, v, dec, descending):
    nc = qd.shape[0] // CHUNK
    order = range(nc - 1, -1, -1) if descending else range(nc)
    rows = [slice(c * CHUNK, (c + 1) * CHUNK) for c in range(nc)]
    dec_t = jnp.concatenate([dec, jnp.zeros((GATE_PAD - nc, dec.shape[1]), F32)], axis=0).T
    outs = []
    for h in range(GLA_HEADS):
        lanes = slice(h * GLA_DK, (h + 1) * GLA_DK)
        vl = slice(h * GLA_DV, (h + 1) * GLA_DV)
        kv = [_dot_tn(ke[rows[c], lanes], v[rows[c], vl]) for c in range(nc)]
        s = st_ref[h]
        start = [None] * nc
        for c in order:
            start[c] = s
            s = s * dec_t[lanes, c:c + 1] + kv[c]
        st_ref[h] = s
        outs.append(jnp.concatenate(
            [_dot(qd[rows[c], lanes], start[c].astype(BF16)) for c in range(nc)], axis=0))
    return jnp.concatenate(outs, axis=1)


def _proj_kernel(tpb, x_ref, sh_ref, sc_ref, nw_ref, wm_ref, wg_ref, afb_ref, bias_ref, cw_ref,
                 s0_ref, ab_ref, v_ref, oi_ref, sr_ref, yc_ref, dec_ref, p_scr, st_ref):
    @pl.when(pl.program_id(0) % tpb == 0)
    def _():
        st_ref[...] = s0_ref[...]

    subs = range(x_ref.shape[0] // PROJ_SUB)
    nc = PROJ_SUB // CHUNK
    nblk = wm_ref.shape[1] // PROJ_COLS
    factors = []
    for sub in subs:
        x = x_ref[sub * PROJ_SUB:(sub + 1) * PROJ_SUB, :]
        hn = _rms(x) * nw_ref[...] * (1.0 + sc_ref[...]) + sh_ref[...]
        hb = hn.astype(BF16)
        g = _dot(hb, wg_ref[...])
        z = _dot(g.astype(BF16), afb_ref[...]) + bias_ref[...]

        todo = iter(range(nblk))

        def wide_block():
            j = next(todo, None)
            if j is not None:
                cols = slice(j * PROJ_COLS, (j + 1) * PROJ_COLS)
                p_scr[sub, :, cols] = _dot(hb, wm_ref[:, cols])

        la = _log_sigmoid(z) * (1.0 / GATE_NORMALIZER)
        wide_block()
        b_f = _segment_scan(la[:, :GLA_QK], CHUNK, False, wide_block)
        b_b = _segment_scan(la[:, GLA_QK:], CHUNK, True, wide_block)
        b_f3 = b_f.reshape(nc, CHUNK, GLA_QK)
        tot_f = b_f3[:, CHUNK - 1:CHUNK, :]
        b_b3 = b_b.reshape(nc, CHUNK, GLA_QK)
        tot_b = b_b3[:, 0:1, :]
        fac = []
        for arg in (b_f, -b_f, (tot_f - b_f3).reshape(PROJ_SUB, GLA_QK),
                    b_b, -b_b, (tot_b - b_b3).reshape(PROJ_SUB, GLA_QK)):
            fac.append(jnp.exp(arg))
            wide_block()
        fac += [jnp.exp(tot_f).reshape(nc, GLA_QK), jnp.exp(tot_b).reshape(nc, GLA_QK)]
        factors.append(fac)
        for _ in range(nblk):
            wide_block()
    for sub in subs:
        _proj_rows(sub, factors[sub], cw_ref, ab_ref, v_ref, oi_ref, sr_ref, yc_ref,
                   dec_ref, p_scr.at[sub], st_ref)


def _proj_rows(sub, factors, cw_ref, ab_ref, v_ref, oi_ref, sr_ref, yc_ref, dec_ref, p, st_ref):
    tm = PROJ_SUB
    nc = tm // CHUNK
    rs = slice(sub * tm, (sub + 1) * tm)
    cs = slice(sub * nc, (sub + 1) * nc)
    e_f, ei_f, ee_f, e_b, ei_b, ee_b, dec_f, dec_b = factors

    q = p[:, 0:GLA_QK] * (GLA_DK ** -0.5)
    k = p[:, GLA_QK:2 * GLA_QK]
    vb = p[:, 2 * GLA_QK:2 * GLA_QK + GLA_WIDTH].astype(BF16)
    v_ref[rs, :] = vb
    qd_f = (q * e_f).astype(BF16)
    ki_f = (k * ei_f).astype(BF16)
    ke_f = (k * ee_f).astype(BF16)
    qd_b = (q * e_b).astype(BF16)
    ki_b = (k * ei_b).astype(BF16)
    ke_b = (k * ee_b).astype(BF16)

    ab_ref[rs, :GLA_QK] = qd_b
    ab_ref[rs, GLA_QK:] = ke_b
    dec_ref[cs, :GLA_QK] = dec_f
    dec_ref[cs, GLA_QK:] = dec_b

    inter_f = _chunk_scan(st_ref, qd_f, ke_f, vb, dec_f, False)

    gi = INTRA_GROUP
    ri = lax.broadcasted_iota(jnp.int32, (gi, gi), 0)
    ci = lax.broadcasted_iota(jnp.int32, (gi, gi), 1)
    same = (ri // CHUNK) == (ci // CHUNK)
    lower = same & (ci <= ri)
    upper = same & (ci >= ri)
    for gidx in range(tm // gi):
        rows = slice(gidx * gi, (gidx + 1) * gi)
        orows = slice(sub * tm + gidx * gi, sub * tm + (gidx + 1) * gi)
        for h in range(GLA_HEADS):
            lanes = slice(h * GLA_DK, (h + 1) * GLA_DK)
            s_f = _dot_nt(qd_f[rows, lanes], ki_f[rows, lanes])
            s_b = _dot_nt(qd_b[rows, lanes], ki_b[rows, lanes])
            s = jnp.where(lower, s_f, 0.0) + jnp.where(upper, s_b, 0.0)
            vl = slice(h * GLA_DV, (h + 1) * GLA_DV)
            oi_ref[orows, vl] = (_dot(s.astype(BF16), vb[rows, vl]) + inter_f[rows, vl]).astype(BF16)

    r = p[:, 1024:1536]
    sr_ref[rs, :] = (r * _sigmoid(r)).astype(BF16)

    u = p[:, 2048:2560] * p[:, 2560:3072]
    row = lax.broadcasted_iota(jnp.int32, u.shape, 0) & (GRID_W - 1)
    prev = jnp.where(row >= 1, pltpu.roll(u, 1, 0), 0.0)
    nxt = jnp.where(row < GRID_W - 1, pltpu.roll(u, tm - 1, 0), 0.0)
    cw = cw_ref[...]
    conv = prev * cw[0:1, :] + u * cw[1:2, :] + nxt * cw[2:3, :]
    yc_ref[rs, :] = (p[:, 1536:2048] * conv).astype(BF16)


def _projection(x2, row0, n, mod4, nw1, wm, wg, afb, bias_fb, conv_w, s0f, seq):
    d = x2.shape[1]
    tm = TOKEN_TILE
    tpb = seq // tm
    t0 = row0 // tm
    batch = lambda i: i // tpb + row0 // seq
    full = lambda a: pl.BlockSpec(a.shape, lambda i: (0,) * a.ndim)
    tok = lambda w: pl.BlockSpec((tm, w), lambda i: (i, 0))
    st_spec = pl.BlockSpec((None, GLA_HEADS, GLA_DK, GLA_DV), lambda i: (batch(i), 0, 0, 0))
    out_shape = (
        jax.ShapeDtypeStruct((n, 2 * GLA_QK), BF16),
        jax.ShapeDtypeStruct((n, GLA_WIDTH), BF16),
        jax.ShapeDtypeStruct((n, GLA_WIDTH), BF16),
        jax.ShapeDtypeStruct((n, GLA_WIDTH), BF16),
        jax.ShapeDtypeStruct((n, CONV_WIDTH), BF16),
        jax.ShapeDtypeStruct((n // CHUNK, 2 * GLA_QK), F32),
    )
    out_specs = (tok(2 * GLA_QK), tok(GLA_WIDTH), tok(GLA_WIDTH), tok(GLA_WIDTH),
                 tok(CONV_WIDTH), pl.BlockSpec((tm // CHUNK, 2 * GLA_QK), lambda i: (i, 0)))
    return pl.pallas_call(
        functools.partial(_proj_kernel, tpb),
        out_shape=out_shape,
        grid=(n // tm,),
        in_specs=[pl.BlockSpec((tm, d), lambda i: (i + t0, 0)), _mod_spec(d, 0, batch),
                  _mod_spec(d, 1, batch), full(nw1),
                  full(wm), full(wg), full(afb), full(bias_fb), full(conv_w), st_spec],
        out_specs=out_specs,
        scratch_shapes=[pltpu.VMEM((tm // PROJ_SUB, PROJ_SUB, wm.shape[1]), F32),
                        pltpu.VMEM((GLA_HEADS, GLA_DK, GLA_DV), F32)],
        compiler_params=_cparams("arbitrary"),
        name="projection",
    )(x2, mod4, mod4, nw1, wm, wg, afb, bias_fb, conv_w, s0f)


def _post_kernel(tpb, x_ref, oi_ref, ab_ref, v_ref, dec_ref, s0_ref, sr_ref, yc_ref, gnw_ref, wo_ref,
                 g1_ref, nw2_ref, sh2_ref, sc2_ref, rwt_ref, rb_ref, tri_ref,
                 h_ref, xw_ref, idx_ref, rank_ref, wt_ref, cnt_ref, run_ref, st_ref):
    @pl.when(pl.program_id(0) == 0)
    def _():
        run_ref[...] = jnp.zeros_like(run_ref)

    @pl.when(pl.program_id(0) % tpb == 0)
    def _():
        st_ref[...] = s0_ref[...]

    tm = x_ref.shape[0]
    inter_b = _chunk_scan(st_ref, ab_ref[:, :GLA_QK], ab_ref[:, GLA_QK:], v_ref[...],
                          dec_ref[:, GLA_QK:], True)
    o = oi_ref[...].astype(F32) + inter_b
    normed = jnp.concatenate(
        [_rms(o[:, h * GLA_DV:(h + 1) * GLA_DV]) for h in range(GLA_HEADS)], axis=1)
    y_gla = (normed * gnw_ref[...] * sr_ref[...].astype(F32)).astype(BF16)
    wo = wo_ref[...]
    y = _dot(y_gla, wo[:GLA_WIDTH, :]) + _dot(yc_ref[...], wo[GLA_WIDTH:, :])
    h = x_ref[...] + g1_ref[...] * y
    h_ref[...] = h
    hn = _rms(h) * nw2_ref[...] * (1.0 + sc2_ref[...]) + sh2_ref[...]
    xw_ref[...] = _pack_halves(hn)

    logits = _dot3(rwt_ref[...], hn, _dot_nt) + rb_ref[...]
    eid = lax.broadcasted_iota(jnp.int32, logits.shape, 0)
    vals, idxs = [], []
    work = logits
    for _ in range(TOP_K):
        m = jnp.max(work, axis=0, keepdims=True)
        i = jnp.min(jnp.where(work == m, eid, N_EXPERTS), axis=0, keepdims=True)
        vals.append(m)
        idxs.append(i)
        work = jnp.where(eid == i, -jnp.inf, work)
    exps = [jnp.exp(v - vals[0]) for v in vals]
    denom = exps[0] + exps[1] + exps[2] + exps[3]
    wts = [e / denom for e in exps]

    onehot = jnp.zeros(logits.shape, F32)
    for i in idxs:
        onehot = onehot + jnp.where(eid == i, 1.0, 0.0)
    before = _dot(onehot.astype(BF16), tri_ref[...]) + run_ref[:, 0:1]
    for kk in range(TOP_K):
        idx_ref[kk:kk + 1, :] = idxs[kk]
        rk = jnp.sum(jnp.where(eid == idxs[kk], before, 0.0), axis=0, keepdims=True)
        rank_ref[kk:kk + 1, :] = rk.astype(jnp.int32)
    run = run_ref[...] + jnp.sum(onehot, axis=1, keepdims=True)
    run_ref[...] = run
    cnt_ref[...] = run.astype(jnp.int32)

    sub = lax.broadcasted_iota(jnp.int32, (GATE_PAD, tm), 0)
    wpad = jnp.zeros((GATE_PAD, tm), F32)
    for kk in range(TOP_K):
        wpad = wpad + jnp.where(sub == kk, wts[kk], 0.0)
    wt_ref[...] = wpad.T


def _post_mixer(x2, row0, o_i, a_b, vb, dec, s0b, sr, yc, gnw, wo, mod4, nw2, rwt, rb, tri, seq):
    n, d = o_i.shape[0], x2.shape[1]
    tm = POST_TILE
    tpb = seq // tm
    t0 = row0 // tm
    rev = lambda i: (i // tpb) * tpb + (tpb - 1 - i % tpb)
    batch = lambda i: i // tpb + row0 // seq
    full = lambda a: pl.BlockSpec(a.shape, lambda i: (0,) * a.ndim)
    tok = lambda w: pl.BlockSpec((tm, w), lambda i: (rev(i), 0))
    lane_tok = pl.BlockSpec((TOP_K, tm), lambda i: (0, rev(i)))
    st_spec = pl.BlockSpec((None, GLA_HEADS, GLA_DK, GLA_DV), lambda i: (batch(i), 0, 0, 0))
    dec_spec = pl.BlockSpec((tm // CHUNK, 2 * GLA_QK), lambda i: (rev(i), 0))
    out_shape = (
        jax.ShapeDtypeStruct((n, d), F32),
        jax.ShapeDtypeStruct((n, d // 2), jnp.uint32),
        jax.ShapeDtypeStruct((TOP_K, n), jnp.int32),
        jax.ShapeDtypeStruct((TOP_K, n), jnp.int32),
        jax.ShapeDtypeStruct((n, GATE_PAD), F32),
        jax.ShapeDtypeStruct((N_EXPERTS, GATE_PAD), jnp.int32),
    )
    out_specs = (tok(d), tok(d // 2), lane_tok, lane_tok, tok(GATE_PAD),
                 pl.BlockSpec((N_EXPERTS, GATE_PAD), lambda i: (0, 0)))
    return pl.pallas_call(
        functools.partial(_post_kernel, tpb),
        out_shape=out_shape,
        grid=(n // tm,),
        in_specs=[pl.BlockSpec((tm, d), lambda i: (rev(i) + t0, 0)), tok(GLA_WIDTH),
                  tok(2 * GLA_QK), tok(GLA_WIDTH), dec_spec, st_spec, tok(GLA_WIDTH),
                  tok(CONV_WIDTH), full(gnw), full(wo), _mod_spec(d, 2, batch), full(nw2),
                  _mod_spec(d, 3, batch), _mod_spec(d, 4, batch),
                  full(rwt), full(rb), full(tri)],
        out_specs=out_specs,
        scratch_shapes=[pltpu.VMEM((N_EXPERTS, GATE_PAD), F32),
                        pltpu.VMEM((GLA_HEADS, GLA_DK, GLA_DV), F32)],
        compiler_params=_cparams("arbitrary"),
        name="post_mixer",
    )(x2, o_i, a_b, vb, dec, s0b, sr, yc, gnw, wo, mod4, nw2, mod4, mod4, rwt, rb, tri)


def _plan_kernel(cnt_ref, start_ref, te_ref, first_ref, next_ref, nu_ref):
    n_tiles = te_ref.shape[0]
    shift = EXPERT_TILE.bit_length() - 1

    def clear(t, carry):
        first_ref[t] = 0
        next_ref[t] = -1
        return carry

    lax.fori_loop(0, n_tiles, clear, 0)

    def group(e, tile0):
        nt = lax.shift_right_logical(cnt_ref[e] + (EXPERT_TILE - 1), shift)
        start_ref[e] = tile0 * EXPERT_TILE

        def fill(t, carry):
            te_ref[tile0 + t] = e
            return carry

        lax.fori_loop(0, nt, fill, 0)

        @pl.when(nt > 0)
        def _():
            first_ref[tile0] = 1

        return tile0 + nt

    n_used = lax.fori_loop(0, N_EXPERTS, group, 0)
    nu_ref[0] = n_used
    last = te_ref[n_used - 1]

    def tail(t, carry):
        te_ref[t] = last
        return carry

    lax.fori_loop(n_used, n_tiles, tail, 0)

    def link(k, nxt):
        e = N_EXPERTS - 1 - k
        has = cnt_ref[e] > 0

        @pl.when(has)
        def _():
            next_ref[lax.shift_right_logical(start_ref[e], shift)] = nxt

        return jnp.where(has, e, nxt)

    lax.fori_loop(0, N_EXPERTS, link, -1)


def _plan(counts, n_tiles):
    assert EXPERT_TILE & (EXPERT_TILE - 1) == 0
    smem = pl.BlockSpec(memory_space=pltpu.SMEM)
    vec = lambda m: jax.ShapeDtypeStruct((m,), jnp.int32)
    return pl.pallas_call(
        _plan_kernel,
        out_shape=(vec(N_EXPERTS), vec(n_tiles), vec(n_tiles), vec(n_tiles), vec(1)),
        in_specs=[smem],
        out_specs=(smem, smem, smem, smem, smem),
        name="plan",
    )(counts)


def _slot_kernel(start_ref, idx_ref, rank_ref, slot_ref):
    idx = idx_ref[...]
    base = jnp.zeros(idx.shape, jnp.int32)
    for e in range(N_EXPERTS):
        base = jnp.where(idx == e, start_ref[e], base)
    slot_ref[...] = base + rank_ref[...]


def _slots(group_start, idx_t, rank_t):
    k, n = idx_t.shape
    bn = min(n, 8192)
    blk = pl.BlockSpec((k, bn), lambda i, s: (0, i))
    return pl.pallas_call(
        _slot_kernel,
        out_shape=jax.ShapeDtypeStruct((k, n), jnp.int32),
        grid_spec=pltpu.PrefetchScalarGridSpec(
            num_scalar_prefetch=1, grid=(n // bn,), in_specs=[blk, blk], out_specs=blk),
        compiler_params=_cparams("arbitrary"),
        name="slots",
    )(group_start, idx_t, rank_t)


SC_CORES = 2
SC_SUBCORES = 16
SC_WORKERS = SC_CORES * SC_SUBCORES
SCATTER_CHUNK = 128
GATHER_CHUNK = 64


def _sc_mesh():
    return plsc.VectorSubcoreMesh(core_axis_name="c", subcore_axis_name="s")


def _sc_worker():
    return lax.axis_index("s") * SC_CORES + lax.axis_index("c")


def _sc_scatter_rows(src, idx_t, n_out):
    n, w = src.shape
    k = idx_t.shape[0]
    c = SCATTER_CHUNK
    per = n // SC_WORKERS
    nch = per // c
    assert per * SC_WORKERS == n and nch * c == per

    def body(src_hbm, idx_hbm, out_hbm, idx_v, rows_v):
        wid = _sc_worker()
        for kk in range(k):
            pltpu.sync_copy(idx_hbm.at[kk, wid], idx_v.at[kk])

        @pl.loop(0, nch)
        def _(j):
            pltpu.sync_copy(src_hbm.at[pl.ds(wid * per + j * c, c)], rows_v)
            for kk in range(k):
                pltpu.sync_copy(rows_v, out_hbm.at[idx_v.at[kk, j]])

    return pl.kernel(
        body,
        out_type=jax.ShapeDtypeStruct((n_out, w), src.dtype),
        mesh=_sc_mesh(),
        scratch_types=[pltpu.VMEM((k, nch, c), jnp.int32), pltpu.VMEM((c, w), src.dtype)],
        name="sc_dispatch",
    )(src, idx_t.reshape(k, SC_WORKERS, nch, c))


def _sc_gather_rows(table, idx):
    w = table.shape[1]
    m = idx.shape[0]
    c = GATHER_CHUNK
    per = m // SC_WORKERS
    nch = per // c
    assert per * SC_WORKERS == m and nch * c == per and nch % 2 == 0

    def body(table_hbm, idx_hbm, out_hbm, idx_v, rows_v, sem0, sem1):
        wid = _sc_worker()
        base = wid * per
        sems = (sem0, sem1)
        pltpu.sync_copy(idx_hbm.at[wid], idx_v)

        def gather(j, b):
            return pltpu.make_async_copy(table_hbm.at[idx_v.at[j]], rows_v.at[b], sems[b])

        gather(0, 0).start()

        @pl.loop(0, nch, step=2)
        def _(g):
            for b in range(2):
                j = g + b
                gather(j, b).wait()

                @pl.when(j + 1 < nch)
                def _():
                    gather(j + 1, 1 - b).start()

                pltpu.sync_copy(rows_v.at[b], out_hbm.at[pl.ds(base + j * c, c)])

    return pl.kernel(
        body,
        out_type=jax.ShapeDtypeStruct((m, w), table.dtype),
        mesh=_sc_mesh(),
        scratch_types=[pltpu.VMEM((nch, c), jnp.int32), pltpu.VMEM((2, c, w), table.dtype),
                       pltpu.SemaphoreType.DMA, pltpu.SemaphoreType.DMA],
        name="sc_combine_gather",
    )(table, idx.reshape(SC_WORKERS, nch, c))


PERM_BLOCK = 256


def _expert_kernel(te_ref, first_ref, next_ref, nu_ref, x_ref, wgu_hbm, wdn_hbm, perm_ref, bg_ref,
                   bu_ref, bd_ref, y_ref, wgu_f, wdn_f, wg_s, wu_s, wd_s, sem):
    i = pl.program_id(0)
    used = i < nu_ref[0]

    def fetch(e):
        return (pltpu.make_async_copy(wgu_hbm.at[e], wgu_f, sem.at[0]),
                pltpu.make_async_copy(wdn_hbm.at[e], wdn_f, sem.at[1]))

    @pl.when(i == 0)
    def _():
        for cp in fetch(te_ref[0]):
            cp.start()

    @pl.when(used & (first_ref[i] == 1))
    def _():
        for cp in fetch(te_ref[i]):
            cp.wait()
        half = PERM_BLOCK // 2
        perm = perm_ref[...]
        for b in range(wgu_f.shape[1] // PERM_BLOCK):
            blk = wgu_f[:, b * PERM_BLOCK:(b + 1) * PERM_BLOCK].astype(BF16)
            t = _dot(blk, perm)
            wg_s[:, b * half:(b + 1) * half] = t[:, :half].astype(BF16)
            wu_s[:, b * half:(b + 1) * half] = t[:, half:].astype(BF16)
        wd_s[...] = wdn_f[...].astype(BF16)

        @pl.when(next_ref[i] >= 0)
        def _():
            for cp in fetch(next_ref[i]):
                cp.start()

    @pl.when(used)
    def _():
        x = _unpack_halves(x_ref[...]).astype(BF16)
        g = _dot(x, wg_s[...]) + bg_ref[...]
        u = _dot(x, wu_s[...]) + bu_ref[...]
        gate = jnp.minimum(g, SWIGLU_LIMIT)
        up = jnp.clip(u, -SWIGLU_LIMIT, SWIGLU_LIMIT)
        act = (up + 1.0) * (gate * _sigmoid(gate, SWIGLU_ALPHA))
        y = _dot(act.astype(BF16), wd_s[...]) + bd_ref[...]
        y_ref[...] = _pack_halves(y)

    @pl.when(jnp.logical_not(used))
    def _():
        y_ref[...] = jnp.zeros_like(y_ref)


def _experts(tile_expert, tile_first, tile_next, n_used, xs, w_gu, w_dn, bg, bu, bd):
    n_slots, half = xs.shape
    tm = EXPERT_TILE
    n_tiles = n_slots // tm
    dm, f, d = w_gu.shape[1], w_dn.shape[1], w_dn.shape[2]
    ph = PERM_BLOCK // 2
    j = jnp.arange(PERM_BLOCK)
    perm = (jnp.arange(PERM_BLOCK)[None, :] == (j // 2 + (j % 2) * ph)[:, None]).astype(BF16)
    row_in = pl.BlockSpec((tm, half), lambda i, te, tf, tn, nu: (jnp.minimum(i, nu[0] - 1), 0))
    row_out = pl.BlockSpec((tm, half),
                           lambda i, te, tf, tn, nu: (jnp.where(i < nu[0], i, n_tiles - 1), 0))
    bspec = lambda a: pl.BlockSpec((None,) + a.shape[1:], lambda i, te, tf, tn, nu: (te[i], 0, 0))
    return pl.pallas_call(
        _expert_kernel,
        out_shape=jax.ShapeDtypeStruct((n_slots, half), jnp.uint32),
        grid_spec=pltpu.PrefetchScalarGridSpec(
            num_scalar_prefetch=4, grid=(n_tiles,),
            in_specs=[row_in, pl.BlockSpec(memory_space=pl.ANY), pl.BlockSpec(memory_space=pl.ANY),
                      pl.BlockSpec((PERM_BLOCK, PERM_BLOCK), lambda i, te, tf, tn, nu: (0, 0)),
                      bspec(bg), bspec(bu), bspec(bd)],
            out_specs=row_out,
            scratch_shapes=[pltpu.VMEM((dm, 2 * f), F32), pltpu.VMEM((f, d), F32),
                            pltpu.VMEM((dm, f), BF16), pltpu.VMEM((dm, f), BF16),
                            pltpu.VMEM((f, d), BF16), pltpu.SemaphoreType.DMA((2,))]),
        compiler_params=_cparams("arbitrary"),
        name="experts",
    )(tile_expert, tile_first, tile_next, n_used, xs, w_gu, w_dn, perm, bg, bu, bd)


def _combine_kernel(y4_ref, h_ref, wt_ref, g2_ref, fnw_ref, *rest):
    o_ref = rest[-1]
    wt = wt_ref[...]
    acc = jnp.zeros(h_ref.shape, F32)
    for kk in range(TOP_K):
        acc = acc + wt[:, kk:kk + 1] * _unpack_halves(y4_ref[kk])
    h = h_ref[...] + g2_ref[...] * acc
    o_ref[...] = _rms(h) * fnw_ref[...]


def _combine(y4, h, wt, mod4, fnw, seq, row0, n_total, prev_out):
    n, d = h.shape
    tt = COMBINE_TILE
    tpb = seq // tt
    t0 = row0 // tt
    in_specs = [pl.BlockSpec((TOP_K, tt, d // 2), lambda i: (0, i, 0)),
                pl.BlockSpec((tt, d), lambda i: (i, 0)),
                pl.BlockSpec((tt, GATE_PAD), lambda i: (i, 0)),
                _mod_spec(d, 5, lambda i: i // tpb + row0 // seq),
                pl.BlockSpec((1, d), lambda i: (0, 0))]
    args = [y4, h, wt, mod4, fnw]
    aliases = {}
    if prev_out is not None:
        in_specs.append(pl.BlockSpec(memory_space=pl.ANY))
        args.append(prev_out)
        aliases = {len(args) - 1: 0}
    return pl.pallas_call(
        _combine_kernel,
        out_shape=jax.ShapeDtypeStruct((n_total, d), F32),
        grid=(n // tt,),
        in_specs=in_specs,
        out_specs=pl.BlockSpec((tt, d), lambda i: (i + t0, 0)),
        input_output_aliases=aliases,
        compiler_params=_cparams("arbitrary"),
        name="combine",
    )(*args)


def kernel(x, c, ctx, c_ctx, w_ada, b_ada, norm1_w, w_in, a_up_f, a_bias_f, a_up_b, a_bias_b,
           gla_norm_w, conv_w, w_out, norm2_w, router_w, router_b, w_gu, b_gu, w_dn, b_dn,
           final_norm_w):
    bsz, seq, d = x.shape
    n = bsz * seq
    assert w_ada.shape[0] == 1 and d == D_MODEL
    assert seq % TOKEN_TILE == 0 and seq % POST_TILE == 0 and seq % GRID_W == 0
    assert ctx.shape[1] & (ctx.shape[1] - 1) == 0

    w_in0 = w_in[0]
    v_end = 2 * GLA_QK + GLA_WIDTH
    r_end = v_end + GLA_WIDTH
    g_end = r_end + 2 * GATE_RANK
    wm = jnp.concatenate([w_in0[:, :r_end], w_in0[:, g_end:]], axis=1).astype(BF16)
    wg_low = jnp.pad(w_in0[:, r_end:g_end], ((0, 0), (0, GATE_PAD - 2 * GATE_RANK))).astype(BF16)
    afb = jnp.zeros((GATE_PAD, 2 * GLA_QK), F32)
    afb = afb.at[:GATE_RANK, :GLA_QK].set(a_up_f[0]).at[GATE_RANK:2 * GATE_RANK, GLA_QK:].set(a_up_b[0])
    afb = afb.astype(BF16)
    bias_fb = jnp.concatenate([a_bias_f[0], a_bias_b[0]])[None]
    wkv = wm[:, GLA_QK:v_end]
    row = lambda a: a.reshape(1, -1)

    mod = _modulation(jnp.concatenate([c, c_ctx[None]], axis=0), w_ada[0], row(b_ada[0]))
    mod4 = mod.reshape(bsz + 1, 6, 1, d)

    s0f, s0b = _context_states(ctx, mod4, row(norm1_w[0]), wkv, wg_low, afb, bias_fb)

    x2 = x.reshape(n, d)
    tri = jnp.triu(jnp.ones((POST_TILE, POST_TILE), BF16), k=1)
    gnw = jnp.tile(gla_norm_w[0], GLA_HEADS)[None]
    wo = w_out[0].astype(BF16)
    bg_e, bu_e, bd_e = b_gu[0][:, None, 0::2], b_gu[0][:, None, 1::2], b_dn[0][:, None, :]

    if bsz % 8 == 0:
        chunk_batches = (5 * bsz // 8, 3 * bsz // 8)
    elif bsz % 2 == 0:
        chunk_batches = (bsz // 2, bsz // 2)
    else:
        chunk_batches = (bsz,)
    routed = []
    b0 = 0
    for cb in chunk_batches:
        row0 = b0 * seq
        b0 += cb
        nc = cb * seq
        n_tiles = nc * TOP_K // EXPERT_TILE + N_EXPERTS
        a_b, vb, o_i, sr, yc, dec = _projection(
            x2, row0, nc, mod4, row(norm1_w[0]), wm, wg_low, afb, bias_fb, conv_w[0], s0f, seq)
        h, xw, idx_t, rank_t, wt, cnt = _post_mixer(
            x2, row0, o_i, a_b, vb, dec, s0b, sr, yc, gnw, wo, mod4, row(norm2_w[0]),
            router_w[0].T, router_b[0][:, None], tri, seq)

        group_start, tile_expert, tile_first, tile_next, n_used = _plan(cnt[:, 0], n_tiles)
        slot_t = _slots(group_start, idx_t, rank_t)
        xs = _sc_scatter_rows(xw, slot_t, n_tiles * EXPERT_TILE)
        routed.append((h, wt, slot_t.reshape(-1), xs, tile_expert, tile_first, tile_next, n_used, row0))

    gathered = []
    for h, wt, slot_flat, xs, tile_expert, tile_first, tile_next, n_used, row0 in routed:
        ys = _experts(tile_expert, tile_first, tile_next, n_used, xs, w_gu[0], w_dn[0],
                      bg_e, bu_e, bd_e)
        y4 = _sc_gather_rows(ys, slot_flat).reshape(TOP_K, h.shape[0], d // 2)
        gathered.append((y4, h, wt, row0))

    out = None
    for y4, h, wt, row0 in gathered:
        out = _combine(y4, h, wt, mod4, row(final_norm_w), seq, row0, n, out)
    return out.reshape(bsz, seq, d)
```

```python
import functools

import jax
import jax.numpy as jnp
from jax import lax
from jax.experimental import pallas as pl
from jax.experimental.pallas import tpu as pltpu
from jax.experimental.pallas import tpu_sc as plsc

D_MODEL = 1024
GLA_HEADS = 4
GLA_DK = 64
GLA_DV = 128
GLA_QK = GLA_HEADS * GLA_DK
GLA_WIDTH = GLA_HEADS * GLA_DV
CONV_WIDTH = 512
GATE_RANK = 16
GATE_PAD = 128
GATE_NORMALIZER = 16.0
CHUNK = 64
GRID_W = 64
N_EXPERTS = 32
TOP_K = 4
SWIGLU_LIMIT = 7.0
SWIGLU_ALPHA = 1.702
EPS = 1e-6

TOKEN_TILE = 1024
POST_TILE = 1024
PREFIX_BLOCK = 256
PROJ_SUB = 1024
PROJ_COLS = 1024
INTRA_GROUP = 256
EXPERT_TILE = 512
COMBINE_TILE = 1024
VMEM_LIMIT = 56 * 1024 * 1024

F32 = jnp.float32
BF16 = jnp.bfloat16
HI_MASK = 0xFFFF0000
LOG2_E = 1.4426950408889634


def _cparams(*sem):
    return pltpu.CompilerParams(dimension_semantics=sem, vmem_limit_bytes=VMEM_LIMIT)


def _mod_spec(d, component, batch_of_step):
    return pl.BlockSpec((None, None, 1, d), lambda i: (batch_of_step(i), component, 0, 0))


def _rms(x):
    return x * lax.rsqrt(jnp.mean(x * x, axis=-1, keepdims=True) + EPS)


def _sigmoid(x, scale=1.0):
    return 1.0 / (1.0 + jnp.exp2(x * (-scale * LOG2_E)))


def _log_sigmoid(z):
    return jnp.minimum(z, 0.0) - jnp.log(1.0 + jnp.exp(-jnp.abs(z)))


def _dot(a, b):
    return jnp.dot(a, b, preferred_element_type=F32)


def _dot_nt(a, b):
    return lax.dot_general(a, b, (((1,), (1,)), ((), ())), preferred_element_type=F32)


def _dot_tn(a, b):
    return lax.dot_general(a, b, (((0,), (0,)), ((), ())), preferred_element_type=F32)


def _split_bf16(a):
    hi = a.astype(BF16)
    lo = (a - hi.astype(F32)).astype(BF16)
    return hi, lo


def _dot3(a, b, dot):
    a_hi, a_lo = _split_bf16(a)
    b_hi, b_lo = _split_bf16(b)
    return dot(a_hi, b_hi) + (dot(a_hi, b_lo) + dot(a_lo, b_hi))


def _segment_scan(x, seg, reverse, after_step=None):
    n = x.shape[0]
    row = lax.broadcasted_iota(jnp.int32, x.shape, 0) & (seg - 1)
    s = 1
    while s < seg:
        if reverse:
            shifted = pltpu.roll(x, n - s, 0)
            x = x + jnp.where(row < seg - s, shifted, 0.0)
        else:
            shifted = pltpu.roll(x, s, 0)
            x = x + jnp.where(row >= s, shifted, 0.0)
        if after_step is not None:
            after_step()
        s *= 2
    return x


def _pack_halves(x):
    n = x.shape[1] // 2
    bits = pltpu.bitcast(x.astype(BF16).astype(F32), jnp.uint32)
    return (bits[:, :n] & jnp.uint32(HI_MASK)) | (bits[:, n:] >> 16)


def _unpack_halves(w):
    hi = pltpu.bitcast(w & jnp.uint32(HI_MASK), F32)
    lo = pltpu.bitcast(w << 16, F32)
    return jnp.concatenate([hi, lo], axis=1)


def _mod_kernel(c_ref, w_ref, b_ref, o_ref):
    c = c_ref[...]
    s = c * _sigmoid(c)
    o_ref[...] = _dot3(s, w_ref[...], _dot) + b_ref[...]


def _modulation(cc, w_ada, b_ada):
    rows, d = cc.shape
    n = w_ada.shape[1]
    bn = 1536
    return pl.pallas_call(
        _mod_kernel,
        out_shape=jax.ShapeDtypeStruct((rows, n), F32),
        grid=(n // bn,),
        in_specs=[pl.BlockSpec((rows, d), lambda j: (0, 0)),
                  pl.BlockSpec((d, bn), lambda j: (0, j)),
                  pl.BlockSpec((1, bn), lambda j: (0, j))],
        out_specs=pl.BlockSpec((rows, bn), lambda j: (0, j)),
        compiler_params=_cparams("arbitrary"),
        name="modulation",
    )(cc, w_ada, b_ada)


def _ctx_kernel(x_ref, sh_ref, sc_ref, nw_ref, wkv_ref, wg_ref, afb_ref, bias_ref, sf_ref, sb_ref):
    x = x_ref[...]
    n = x.shape[0]
    hn = _rms(x) * nw_ref[...] * (1.0 + sc_ref[...]) + sh_ref[...]
    hb = hn.astype(BF16)
    kv = _dot(hb, wkv_ref[...])
    g = _dot(hb, wg_ref[...])
    z = _dot(g.astype(BF16), afb_ref[...]) + bias_ref[...]
    la = _log_sigmoid(z) * (1.0 / GATE_NORMALIZER)
    p = _segment_scan(la, n, False)
    p_f, p_b = p[:, :GLA_QK], p[:, GLA_QK:]
    w_f = jnp.exp(p_f[n - 1:n, :] - p_f)
    w_b = jnp.exp(p_b - la[:, GLA_QK:])
    k = kv[:, :GLA_QK]
    vb = kv[:, GLA_QK:].astype(BF16)
    ke_f = (k * w_f).astype(BF16)
    ke_b = (k * w_b).astype(BF16)
    for h in range(GLA_HEADS):
        vh = vb[:, h * GLA_DV:(h + 1) * GLA_DV]
        sf_ref[h] = _dot_tn(ke_f[:, h * GLA_DK:(h + 1) * GLA_DK], vh)
        sb_ref[h] = _dot_tn(ke_b[:, h * GLA_DK:(h + 1) * GLA_DK], vh)


def _context_states(ctx, mod4, nw1, wkv, wg, afb, bias_fb):
    bsz, n, d = ctx.shape
    ctx_row = lambda b: bsz
    full = lambda a: pl.BlockSpec(a.shape, lambda b: (0,) * a.ndim)
    st = jax.ShapeDtypeStruct((bsz, GLA_HEADS, GLA_DK, GLA_DV), F32)
    st_spec = pl.BlockSpec((None, GLA_HEADS, GLA_DK, GLA_DV), lambda b: (b, 0, 0, 0))
    return pl.pallas_call(
        _ctx_kernel,
        out_shape=(st, st),
        grid=(bsz,),
        in_specs=[pl.BlockSpec((None, n, d), lambda b: (b, 0, 0)),
                  _mod_spec(d, 0, ctx_row), _mod_spec(d, 1, ctx_row), full(nw1), full(wkv), full(wg),
                  full(afb), full(bias_fb)],
        out_specs=(st_spec, st_spec),
        compiler_params=_cparams("arbitrary"),
        name="context_states",
    )(ctx, mod4, mod4, nw1, wkv, wg, afb, bias_fb)


def _chunk_scan(st_ref, qd, ke, v, dec, descending):
    nc = qd.shape[0] // CHUNK
    order = range(nc - 1, -1, -1) if descending else range(nc)
    rows = [slice(c * CHUNK, (c + 1) * CHUNK) for c in range(nc)]
    dec_t = jnp.concatenate([dec, jnp.zeros((GATE_PAD - nc, dec.shape[1]), F32)], axis=0).T
    outs = []
    for h in range(GLA_HEADS):
        lanes = slice(h * GLA_DK, (h + 1) * GLA_DK)
        vl = slice(h * GLA_DV, (h + 1) * GLA_DV)
        kv = [_dot_tn(ke[rows[c], lanes], v[rows[c], vl]) for c in range(nc)]
        s = st_ref[h]
        start = [None] * nc
        for c in order:
            start[c] = s
            s = s * dec_t[lanes, c:c + 1] + kv[c]
        st_ref[h] = s
        outs.append(jnp.concatenate(
            [_dot(qd[rows[c], lanes], start[c].astype(BF16)) for c in range(nc)], axis=0))
    return jnp.concatenate(outs, axis=1)


def _proj_kernel(tpb, x_ref, sh_ref, sc_ref, nw_ref, wm_ref, wg_ref, afb_ref, bias_ref, cw_ref,
                 s0_ref, ab_ref, v_ref, oi_ref, sr_ref, yc_ref, dec_ref, p_scr, st_ref):
    @pl.when(pl.program_id(0) % tpb == 0)
    def _():
        st_ref[...] = s0_ref[...]

    subs = range(x_ref.shape[0] // PROJ_SUB)
    nc = PROJ_SUB // CHUNK
    nblk = wm_ref.shape[1] // PROJ_COLS
    factors = []
    for sub in subs:
        x = x_ref[sub * PROJ_SUB:(sub + 1) * PROJ_SUB, :]
        hn = _rms(x) * nw_ref[...] * (1.0 + sc_ref[...]) + sh_ref[...]
        hb = hn.astype(BF16)
        g = _dot(hb, wg_ref[...])
        z = _dot(g.astype(BF16), afb_ref[...]) + bias_ref[...]

        todo = iter(range(nblk))

        def wide_block():
            j = next(todo, None)
            if j is not None:
                cols = slice(j * PROJ_COLS, (j + 1) * PROJ_COLS)
                p_scr[sub, :, cols] = _dot(hb, wm_ref[:, cols])

        la = _log_sigmoid(z) * (1.0 / GATE_NORMALIZER)
        wide_block()
        b_f = _segment_scan(la[:, :GLA_QK], CHUNK, False, wide_block)
        b_b = _segment_scan(la[:, GLA_QK:], CHUNK, True, wide_block)
        b_f3 = b_f.reshape(nc, CHUNK, GLA_QK)
        tot_f = b_f3[:, CHUNK - 1:CHUNK, :]
        b_b3 = b_b.reshape(nc, CHUNK, GLA_QK)
        tot_b = b_b3[:, 0:1, :]
        fac = []
        for arg in (b_f, -b_f, (tot_f - b_f3).reshape(PROJ_SUB, GLA_QK),
                    b_b, -b_b, (tot_b - b_b3).reshape(PROJ_SUB, GLA_QK)):
            fac.append(jnp.exp(arg))
            wide_block()
        fac += [jnp.exp(tot_f).reshape(nc, GLA_QK), jnp.exp(tot_b).reshape(nc, GLA_QK)]
        factors.append(fac)
        for _ in range(nblk):
            wide_block()
    for sub in subs:
        _proj_rows(sub, factors[sub], cw_ref, ab_ref, v_ref, oi_ref, sr_ref, yc_ref,
                   dec_ref, p_scr.at[sub], st_ref)


def _proj_rows(sub, factors, cw_ref, ab_ref, v_ref, oi_ref, sr_ref, yc_ref, dec_ref, p, st_ref):
    tm = PROJ_SUB
    nc = tm // CHUNK
    rs = slice(sub * tm, (sub + 1) * tm)
    cs = slice(sub * nc, (sub + 1) * nc)
    e_f, ei_f, ee_f, e_b, ei_b, ee_b, dec_f, dec_b = factors

    q = p[:, 0:GLA_QK] * (GLA_DK ** -0.5)
    k = p[:, GLA_QK:2 * GLA_QK]
    vb = p[:, 2 * GLA_QK:2 * GLA_QK + GLA_WIDTH].astype(BF16)
    v_ref[rs, :] = vb
    qd_f = (q * e_f).astype(BF16)
    ki_f = (k * ei_f).astype(BF16)
    ke_f = (k * ee_f).astype(BF16)
    qd_b = (q * e_b).astype(BF16)
    ki_b = (k * ei_b).astype(BF16)
    ke_b = (k * ee_b).astype(BF16)

    ab_ref[rs, :GLA_QK] = qd_b
    ab_ref[rs, GLA_QK:] = ke_b
    dec_ref[cs, :GLA_QK] = dec_f
    dec_ref[cs, GLA_QK:] = dec_b

    inter_f = _chunk_scan(st_ref, qd_f, ke_f, vb, dec_f, False)

    gi = INTRA_GROUP
    ri = lax.broadcasted_iota(jnp.int32, (gi, gi), 0)
    ci = lax.broadcasted_iota(jnp.int32, (gi, gi), 1)
    same = (ri // CHUNK) == (ci // CHUNK)
    lower = same & (ci <= ri)
    upper = same & (ci >= ri)
    for gidx in range(tm // gi):
        rows = slice(gidx * gi, (gidx + 1) * gi)
        orows = slice(sub * tm + gidx * gi, sub * tm + (gidx + 1) * gi)
        for h in range(GLA_HEADS):
            lanes = slice(h * GLA_DK, (h + 1) * GLA_DK)
            s_f = _dot_nt(qd_f[rows, lanes], ki_f[rows, lanes])
            s_b = _dot_nt(qd_b[rows, lanes], ki_b[rows, lanes])
            s = jnp.where(lower, s_f, 0.0) + jnp.where(upper, s_b, 0.0)
            vl = slice(h * GLA_DV, (h + 1) * GLA_DV)
            oi_ref[orows, vl] = _dot(s.astype(BF16), vb[rows, vl]) + inter_f[rows, vl]

    r = p[:, 1024:1536]
    sr_ref[rs, :] = (r * _sigmoid(r)).astype(BF16)

    u = p[:, 2048:2560] * p[:, 2560:3072]
    row = lax.broadcasted_iota(jnp.int32, u.shape, 0) & (GRID_W - 1)
    prev = jnp.where(row >= 1, pltpu.roll(u, 1, 0), 0.0)
    nxt = jnp.where(row < GRID_W - 1, pltpu.roll(u, tm - 1, 0), 0.0)
    cw = cw_ref[...]
    conv = prev * cw[0:1, :] + u * cw[1:2, :] + nxt * cw[2:3, :]
    yc_ref[rs, :] = (p[:, 1536:2048] * conv).astype(BF16)


def _projection(x2, row0, n, mod4, nw1, wm, wg, afb, bias_fb, conv_w, s0f, seq):
    d = x2.shape[1]
    tm = TOKEN_TILE
    tpb = seq // tm
    t0 = row0 // tm
    batch = lambda i: i // tpb + row0 // seq
    full = lambda a: pl.BlockSpec(a.shape, lambda i: (0,) * a.ndim)
    tok = lambda w: pl.BlockSpec((tm, w), lambda i: (i, 0))
    st_spec = pl.BlockSpec((None, GLA_HEADS, GLA_DK, GLA_DV), lambda i: (batch(i), 0, 0, 0))
    out_shape = (
        jax.ShapeDtypeStruct((n, 2 * GLA_QK), BF16),
        jax.ShapeDtypeStruct((n, GLA_WIDTH), BF16),
        jax.ShapeDtypeStruct((n, GLA_WIDTH), F32),
        jax.ShapeDtypeStruct((n, GLA_WIDTH), BF16),
        jax.ShapeDtypeStruct((n, CONV_WIDTH), BF16),
        jax.ShapeDtypeStruct((n // CHUNK, 2 * GLA_QK), F32),
    )
    out_specs = (tok(2 * GLA_QK), tok(GLA_WIDTH), tok(GLA_WIDTH), tok(GLA_WIDTH),
                 tok(CONV_WIDTH), pl.BlockSpec((tm // CHUNK, 2 * GLA_QK), lambda i: (i, 0)))
    return pl.pallas_call(
        functools.partial(_proj_kernel, tpb),
        out_shape=out_shape,
        grid=(n // tm,),
        in_specs=[pl.BlockSpec((tm, d), lambda i: (i + t0, 0)), _mod_spec(d, 0, batch),
                  _mod_spec(d, 1, batch), full(nw1),
                  full(wm), full(wg), full(afb), full(bias_fb), full(conv_w), st_spec],
        out_specs=out_specs,
        scratch_shapes=[pltpu.VMEM((tm // PROJ_SUB, PROJ_SUB, wm.shape[1]), F32),
                        pltpu.VMEM((GLA_HEADS, GLA_DK, GLA_DV), F32)],
        compiler_params=_cparams("arbitrary"),
        name="projection",
    )(x2, mod4, mod4, nw1, wm, wg, afb, bias_fb, conv_w, s0f)


def _post_kernel(tpb, x_ref, oi_ref, ab_ref, v_ref, dec_ref, s0_ref, sr_ref, yc_ref, gnw_ref, wo_ref,
                 g1_ref, nw2_ref, sh2_ref, sc2_ref, rwt_ref, rb_ref, tri_ref,
                 h_ref, xw_ref, idx_ref, rank_ref, wt_ref, cnt_ref, run_ref, st_ref):
    @pl.when(pl.program_id(0) == 0)
    def _():
        run_ref[...] = jnp.zeros_like(run_ref)

    @pl.when(pl.program_id(0) % tpb == 0)
    def _():
        st_ref[...] = s0_ref[...]

    tm = x_ref.shape[0]
    inter_b = _chunk_scan(st_ref, ab_ref[:, :GLA_QK], ab_ref[:, GLA_QK:], v_ref[...],
                          dec_ref[:, GLA_QK:], True)
    o = oi_ref[...] + inter_b
    normed = jnp.concatenate(
        [_rms(o[:, h * GLA_DV:(h + 1) * GLA_DV]) for h in range(GLA_HEADS)], axis=1)
    y_gla = (normed * gnw_ref[...] * sr_ref[...].astype(F32)).astype(BF16)
    wo = wo_ref[...]
    y = _dot(y_gla, wo[:GLA_WIDTH, :]) + _dot(yc_ref[...], wo[GLA_WIDTH:, :])
    h = x_ref[...] + g1_ref[...] * y
    h_ref[...] = h
    hn = _rms(h) * nw2_ref[...] * (1.0 + sc2_ref[...]) + sh2_ref[...]
    xw_ref[...] = _pack_halves(hn)

    rw_hi, rw_lo = _split_bf16(rwt_ref[...])
    hn_hi, hn_lo = _split_bf16(hn)
    both = _dot_nt(jnp.concatenate([rw_hi, rw_lo], axis=0), hn_hi)
    logits = (both[:N_EXPERTS] + (both[N_EXPERTS:] + _dot_nt(rw_hi, hn_lo))) + rb_ref[...]
    eid = lax.broadcasted_iota(jnp.int32, logits.shape, 0)
    vals, idxs = [], []
    work = logits
    for _ in range(TOP_K):
        m = jnp.max(work, axis=0, keepdims=True)
        i = jnp.min(jnp.where(work == m, eid, N_EXPERTS), axis=0, keepdims=True)
        vals.append(m)
        idxs.append(i)
        work = jnp.where(eid == i, -jnp.inf, work)
    exps = [jnp.exp(v - vals[0]) for v in vals]
    denom = exps[0] + exps[1] + exps[2] + exps[3]
    wts = [e / denom for e in exps]

    onehot = jnp.zeros(logits.shape, F32)
    for i in idxs:
        onehot = onehot + jnp.where(eid == i, 1.0, 0.0)
    pb = tri_ref.shape[0]
    seen = run_ref[:, 0:1]
    parts = []
    for blk in range(tm // pb):
        cols = slice(blk * pb, (blk + 1) * pb)
        parts.append(_dot(onehot[:, cols].astype(BF16), tri_ref[...]) + seen)
        seen = seen + jnp.sum(onehot[:, cols], axis=1, keepdims=True)
    before = jnp.concatenate(parts, axis=1)
    for kk in range(TOP_K):
        idx_ref[kk:kk + 1, :] = idxs[kk]
        rk = jnp.sum(jnp.where(eid == idxs[kk], before, 0.0), axis=0, keepdims=True)
        rank_ref[kk:kk + 1, :] = rk.astype(jnp.int32)
    run = run_ref[...] + jnp.sum(onehot, axis=1, keepdims=True)
    run_ref[...] = run
    cnt_ref[...] = run.astype(jnp.int32)

    sub = lax.broadcasted_iota(jnp.int32, (GATE_PAD, tm), 0)
    wpad = jnp.zeros((GATE_PAD, tm), F32)
    for kk in range(TOP_K):
        wpad = wpad + jnp.where(sub == kk, wts[kk], 0.0)
    wt_ref[...] = wpad.T


def _post_mixer(x2, row0, o_i, a_b, vb, dec, s0b, sr, yc, gnw, wo, mod4, nw2, rwt, rb, tri, seq):
    n, d = o_i.shape[0], x2.shape[1]
    tm = POST_TILE
    tpb = seq // tm
    t0 = row0 // tm
    rev = lambda i: (i // tpb) * tpb + (tpb - 1 - i % tpb)
    batch = lambda i: i // tpb + row0 // seq
    full = lambda a: pl.BlockSpec(a.shape, lambda i: (0,) * a.ndim)
    tok = lambda w: pl.BlockSpec((tm, w), lambda i: (rev(i), 0))
    lane_tok = pl.BlockSpec((TOP_K, tm), lambda i: (0, rev(i)))
    st_spec = pl.BlockSpec((None, GLA_HEADS, GLA_DK, GLA_DV), lambda i: (batch(i), 0, 0, 0))
    dec_spec = pl.BlockSpec((tm // CHUNK, 2 * GLA_QK), lambda i: (rev(i), 0))
    out_shape = (
        jax.ShapeDtypeStruct((n, d), F32),
        jax.ShapeDtypeStruct((n, d // 2), jnp.uint32),
        jax.ShapeDtypeStruct((TOP_K, n), jnp.int32),
        jax.ShapeDtypeStruct((TOP_K, n), jnp.int32),
        jax.ShapeDtypeStruct((n, GATE_PAD), F32),
        jax.ShapeDtypeStruct((N_EXPERTS, GATE_PAD), jnp.int32),
    )
    out_specs = (tok(d), tok(d // 2), lane_tok, lane_tok, tok(GATE_PAD),
                 pl.BlockSpec((N_EXPERTS, GATE_PAD), lambda i: (0, 0)))
    return pl.pallas_call(
        functools.partial(_post_kernel, tpb),
        out_shape=out_shape,
        grid=(n // tm,),
        in_specs=[pl.BlockSpec((tm, d), lambda i: (rev(i) + t0, 0)), tok(GLA_WIDTH),
                  tok(2 * GLA_QK), tok(GLA_WIDTH), dec_spec, st_spec, tok(GLA_WIDTH),
                  tok(CONV_WIDTH), full(gnw), full(wo), _mod_spec(d, 2, batch), full(nw2),
                  _mod_spec(d, 3, batch), _mod_spec(d, 4, batch),
                  full(rwt), full(rb), full(tri)],
        out_specs=out_specs,
        scratch_shapes=[pltpu.VMEM((N_EXPERTS, GATE_PAD), F32),
                        pltpu.VMEM((GLA_HEADS, GLA_DK, GLA_DV), F32)],
        compiler_params=_cparams("arbitrary"),
        name="post_mixer",
    )(x2, o_i, a_b, vb, dec, s0b, sr, yc, gnw, wo, mod4, nw2, mod4, mod4, rwt, rb, tri)


def _plan_kernel(cnt_ref, start_ref, te_ref, first_ref, next_ref, nu_ref):
    n_tiles = te_ref.shape[0]
    shift = EXPERT_TILE.bit_length() - 1

    def clear(t, carry):
        first_ref[t] = 0
        next_ref[t] = -1
        return carry

    lax.fori_loop(0, n_tiles, clear, 0)

    def group(e, tile0):
        nt = lax.shift_right_logical(cnt_ref[e] + (EXPERT_TILE - 1), shift)
        start_ref[e] = tile0 * EXPERT_TILE

        def fill(t, carry):
            te_ref[tile0 + t] = e
            return carry

        lax.fori_loop(0, nt, fill, 0)

        @pl.when(nt > 0)
        def _():
            first_ref[tile0] = 1

        return tile0 + nt

    n_used = lax.fori_loop(0, N_EXPERTS, group, 0)
    nu_ref[0] = n_used
    last = te_ref[n_used - 1]

    def tail(t, carry):
        te_ref[t] = last
        return carry

    lax.fori_loop(n_used, n_tiles, tail, 0)

    def link(k, nxt):
        e = N_EXPERTS - 1 - k
        has = cnt_ref[e] > 0

        @pl.when(has)
        def _():
            next_ref[lax.shift_right_logical(start_ref[e], shift)] = nxt

        return jnp.where(has, e, nxt)

    lax.fori_loop(0, N_EXPERTS, link, -1)


def _plan(counts, n_tiles):
    assert EXPERT_TILE & (EXPERT_TILE - 1) == 0
    smem = pl.BlockSpec(memory_space=pltpu.SMEM)
    vec = lambda m: jax.ShapeDtypeStruct((m,), jnp.int32)
    return pl.pallas_call(
        _plan_kernel,
        out_shape=(vec(N_EXPERTS), vec(n_tiles), vec(n_tiles), vec(n_tiles), vec(1)),
        in_specs=[smem],
        out_specs=(smem, smem, smem, smem, smem),
        name="plan",
    )(counts)


def _slot_kernel(start_ref, idx_ref, rank_ref, slot_ref):
    idx = idx_ref[...]
    base = jnp.zeros(idx.shape, jnp.int32)
    for e in range(N_EXPERTS):
        base = jnp.where(idx == e, start_ref[e], base)
    slot_ref[...] = base + rank_ref[...]


def _slots(group_start, idx_t, rank_t):
    k, n = idx_t.shape
    bn = min(n, 8192)
    blk = pl.BlockSpec((k, bn), lambda i, s: (0, i))
    return pl.pallas_call(
        _slot_kernel,
        out_shape=jax.ShapeDtypeStruct((k, n), jnp.int32),
        grid_spec=pltpu.PrefetchScalarGridSpec(
            num_scalar_prefetch=1, grid=(n // bn,), in_specs=[blk, blk], out_specs=blk),
        compiler_params=_cparams("arbitrary"),
        name="slots",
    )(group_start, idx_t, rank_t)


SC_CORES = 2
SC_SUBCORES = 16
SC_WORKERS = SC_CORES * SC_SUBCORES
SCATTER_CHUNK = 128
GATHER_CHUNK = 64


def _sc_mesh():
    return plsc.VectorSubcoreMesh(core_axis_name="c", subcore_axis_name="s")


def _sc_worker():
    return lax.axis_index("s") * SC_CORES + lax.axis_index("c")


def _sc_scatter_rows(src, idx_t, n_out):
    n, w = src.shape
    k = idx_t.shape[0]
    c = SCATTER_CHUNK
    per = n // SC_WORKERS
    nch = per // c
    assert per * SC_WORKERS == n and nch * c == per

    def body(src_hbm, idx_hbm, out_hbm, idx_v, rows_v):
        wid = _sc_worker()
        for kk in range(k):
            pltpu.sync_copy(idx_hbm.at[kk, wid], idx_v.at[kk])

        @pl.loop(0, nch)
        def _(j):
            pltpu.sync_copy(src_hbm.at[pl.ds(wid * per + j * c, c)], rows_v)
            for kk in range(k):
                pltpu.sync_copy(rows_v, out_hbm.at[idx_v.at[kk, j]])

    return pl.kernel(
        body,
        out_type=jax.ShapeDtypeStruct((n_out, w), src.dtype),
        mesh=_sc_mesh(),
        scratch_types=[pltpu.VMEM((k, nch, c), jnp.int32), pltpu.VMEM((c, w), src.dtype)],
        name="sc_dispatch",
    )(src, idx_t.reshape(k, SC_WORKERS, nch, c))


def _sc_gather_rows(table, idx):
    w = table.shape[1]
    m = idx.shape[0]
    c = GATHER_CHUNK
    per = m // SC_WORKERS
    nch = per // c
    assert per * SC_WORKERS == m and nch * c == per and nch % 2 == 0

    def body(table_hbm, idx_hbm, out_hbm, idx_v, rows_v, sem0, sem1):
        wid = _sc_worker()
        base = wid * per
        sems = (sem0, sem1)
        pltpu.sync_copy(idx_hbm.at[wid], idx_v)

        def gather(j, b):
            return pltpu.make_async_copy(table_hbm.at[idx_v.at[j]], rows_v.at[b], sems[b])

        gather(0, 0).start()

        @pl.loop(0, nch, step=2)
        def _(g):
            for b in range(2):
                j = g + b
                gather(j, b).wait()

                @pl.when(j + 1 < nch)
                def _():
                    gather(j + 1, 1 - b).start()

                pltpu.sync_copy(rows_v.at[b], out_hbm.at[pl.ds(base + j * c, c)])

    return pl.kernel(
        body,
        out_type=jax.ShapeDtypeStruct((m, w), table.dtype),
        mesh=_sc_mesh(),
        scratch_types=[pltpu.VMEM((nch, c), jnp.int32), pltpu.VMEM((2, c, w), table.dtype),
                       pltpu.SemaphoreType.DMA, pltpu.SemaphoreType.DMA],
        name="sc_combine_gather",
    )(table, idx.reshape(SC_WORKERS, nch, c))


PERM_BLOCK = 256


def _expert_kernel(te_ref, first_ref, next_ref, nu_ref, x_ref, wgu_hbm, wdn_hbm, perm_ref, bg_ref,
                   bu_ref, bd_ref, y_ref, wgu_f, wdn_f, wg_s, wu_s, wd_s, sem):
    i = pl.program_id(0)
    used = i < nu_ref[0]

    def fetch(e):
        return (pltpu.make_async_copy(wgu_hbm.at[e], wgu_f, sem.at[0]),
                pltpu.make_async_copy(wdn_hbm.at[e], wdn_f, sem.at[1]))

    @pl.when(i == 0)
    def _():
        for cp in fetch(te_ref[0]):
            cp.start()

    @pl.when(used & (first_ref[i] == 1))
    def _():
        for cp in fetch(te_ref[i]):
            cp.wait()
        half = PERM_BLOCK // 2
        perm = perm_ref[...]
        for b in range(wgu_f.shape[1] // PERM_BLOCK):
            blk = wgu_f[:, b * PERM_BLOCK:(b + 1) * PERM_BLOCK].astype(BF16)
            t = _dot(blk, perm)
            wg_s[:, b * half:(b + 1) * half] = t[:, :half].astype(BF16)
            wu_s[:, b * half:(b + 1) * half] = t[:, half:].astype(BF16)
        wd_s[...] = wdn_f[...].astype(BF16)

        @pl.when(next_ref[i] >= 0)
        def _():
            for cp in fetch(next_ref[i]):
                cp.start()

    @pl.when(used)
    def _():
        x = _unpack_halves(x_ref[...]).astype(BF16)
        g = _dot(x, wg_s[...]) + bg_ref[...]
        u = _dot(x, wu_s[...]) + bu_ref[...]
        gate = jnp.minimum(g, SWIGLU_LIMIT)
        up = jnp.clip(u, -SWIGLU_LIMIT, SWIGLU_LIMIT)
        act = (up + 1.0) * (gate * _sigmoid(gate, SWIGLU_ALPHA))
        y = _dot(act.astype(BF16), wd_s[...]) + bd_ref[...]
        y_ref[...] = _pack_halves(y)

    @pl.when(jnp.logical_not(used))
    def _():
        y_ref[...] = jnp.zeros_like(y_ref)


def _experts(tile_expert, tile_first, tile_next, n_used, xs, w_gu, w_dn, bg, bu, bd):
    n_slots, half = xs.shape
    tm = EXPERT_TILE
    n_tiles = n_slots // tm
    dm, f, d = w_gu.shape[1], w_dn.shape[1], w_dn.shape[2]
    ph = PERM_BLOCK // 2
    j = jnp.arange(PERM_BLOCK)
    perm = (jnp.arange(PERM_BLOCK)[None, :] == (j // 2 + (j % 2) * ph)[:, None]).astype(BF16)
    row_in = pl.BlockSpec((tm, half), lambda i, te, tf, tn, nu: (jnp.minimum(i, nu[0] - 1), 0))
    row_out = pl.BlockSpec((tm, half),
                           lambda i, te, tf, tn, nu: (jnp.where(i < nu[0], i, n_tiles - 1), 0))
    bspec = lambda a: pl.BlockSpec((None,) + a.shape[1:], lambda i, te, tf, tn, nu: (te[i], 0, 0))
    return pl.pallas_call(
        _expert_kernel,
        out_shape=jax.ShapeDtypeStruct((n_slots, half), jnp.uint32),
        grid_spec=pltpu.PrefetchScalarGridSpec(
            num_scalar_prefetch=4, grid=(n_tiles,),
            in_specs=[row_in, pl.BlockSpec(memory_space=pl.ANY), pl.BlockSpec(memory_space=pl.ANY),
                      pl.BlockSpec((PERM_BLOCK, PERM_BLOCK), lambda i, te, tf, tn, nu: (0, 0)),
                      bspec(bg), bspec(bu), bspec(bd)],
            out_specs=row_out,
            scratch_shapes=[pltpu.VMEM((dm, 2 * f), F32), pltpu.VMEM((f, d), F32),
                            pltpu.VMEM((dm, f), BF16), pltpu.VMEM((dm, f), BF16),
                            pltpu.VMEM((f, d), BF16), pltpu.SemaphoreType.DMA((2,))]),
        compiler_params=_cparams("arbitrary"),
        name="experts",
    )(tile_expert, tile_first, tile_next, n_used, xs, w_gu, w_dn, perm, bg, bu, bd)


def _combine_kernel(y4_ref, h_ref, wt_ref, g2_ref, fnw_ref, *rest):
    o_ref = rest[-1]
    wt = wt_ref[...]
    acc = jnp.zeros(h_ref.shape, F32)
    for kk in range(TOP_K):
        acc = acc + wt[:, kk:kk + 1] * _unpack_halves(y4_ref[kk])
    h = h_ref[...] + g2_ref[...] * acc
    o_ref[...] = _rms(h) * fnw_ref[...]


def _combine(y4, h, wt, mod4, fnw, seq, row0, n_total, prev_out):
    n, d = h.shape
    tt = COMBINE_TILE
    tpb = seq // tt
    t0 = row0 // tt
    in_specs = [pl.BlockSpec((TOP_K, tt, d // 2), lambda i: (0, i, 0)),
                pl.BlockSpec((tt, d), lambda i: (i, 0)),
                pl.BlockSpec((tt, GATE_PAD), lambda i: (i, 0)),
                _mod_spec(d, 5, lambda i: i // tpb + row0 // seq),
                pl.BlockSpec((1, d), lambda i: (0, 0))]
    args = [y4, h, wt, mod4, fnw]
    aliases = {}
    if prev_out is not None:
        in_specs.append(pl.BlockSpec(memory_space=pl.ANY))
        args.append(prev_out)
        aliases = {len(args) - 1: 0}
    return pl.pallas_call(
        _combine_kernel,
        out_shape=jax.ShapeDtypeStruct((n_total, d), F32),
        grid=(n // tt,),
        in_specs=in_specs,
        out_specs=pl.BlockSpec((tt, d), lambda i: (i + t0, 0)),
        input_output_aliases=aliases,
        compiler_params=_cparams("arbitrary"),
        name="combine",
    )(*args)


def kernel(x, c, ctx, c_ctx, w_ada, b_ada, norm1_w, w_in, a_up_f, a_bias_f, a_up_b, a_bias_b,
           gla_norm_w, conv_w, w_out, norm2_w, router_w, router_b, w_gu, b_gu, w_dn, b_dn,
           final_norm_w):
    bsz, seq, d = x.shape
    n = bsz * seq
    assert w_ada.shape[0] == 1 and d == D_MODEL
    assert seq % TOKEN_TILE == 0 and seq % POST_TILE == 0 and seq % GRID_W == 0
    assert ctx.shape[1] & (ctx.shape[1] - 1) == 0

    w_in0 = w_in[0]
    v_end = 2 * GLA_QK + GLA_WIDTH
    r_end = v_end + GLA_WIDTH
    g_end = r_end + 2 * GATE_RANK
    wm = jnp.concatenate([w_in0[:, :r_end], w_in0[:, g_end:]], axis=1).astype(BF16)
    wg_low = jnp.pad(w_in0[:, r_end:g_end], ((0, 0), (0, GATE_PAD - 2 * GATE_RANK))).astype(BF16)
    afb = jnp.zeros((GATE_PAD, 2 * GLA_QK), F32)
    afb = afb.at[:GATE_RANK, :GLA_QK].set(a_up_f[0]).at[GATE_RANK:2 * GATE_RANK, GLA_QK:].set(a_up_b[0])
    afb = afb.astype(BF16)
    bias_fb = jnp.concatenate([a_bias_f[0], a_bias_b[0]])[None]
    wkv = wm[:, GLA_QK:v_end]
    row = lambda a: a.reshape(1, -1)

    mod = _modulation(jnp.concatenate([c, c_ctx[None]], axis=0), w_ada[0], row(b_ada[0]))
    mod4 = mod.reshape(bsz + 1, 6, 1, d)

    s0f, s0b = _context_states(ctx, mod4, row(norm1_w[0]), wkv, wg_low, afb, bias_fb)

    x2 = x.reshape(n, d)
    tri = jnp.triu(jnp.ones((PREFIX_BLOCK, PREFIX_BLOCK), BF16), k=1)
    gnw = jnp.tile(gla_norm_w[0], GLA_HEADS)[None]
    wo = w_out[0].astype(BF16)
    bg_e, bu_e, bd_e = b_gu[0][:, None, 0::2], b_gu[0][:, None, 1::2], b_dn[0][:, None, :]

    if bsz % 8 == 0:
        chunk_batches = (5 * bsz // 8, 3 * bsz // 8)
    elif bsz % 2 == 0:
        chunk_batches = (bsz // 2, bsz // 2)
    else:
        chunk_batches = (bsz,)
    routed = []
    b0 = 0
    for cb in chunk_batches:
        row0 = b0 * seq
        b0 += cb
        nc = cb * seq
        n_tiles = nc * TOP_K // EXPERT_TILE + N_EXPERTS
        a_b, vb, o_i, sr, yc, dec = _projection(
            x2, row0, nc, mod4, row(norm1_w[0]), wm, wg_low, afb, bias_fb, conv_w[0], s0f, seq)
        h, xw, idx_t, rank_t, wt, cnt = _post_mixer(
            x2, row0, o_i, a_b, vb, dec, s0b, sr, yc, gnw, wo, mod4, row(norm2_w[0]),
            router_w[0].T, router_b[0][:, None], tri, seq)

        group_start, tile_expert, tile_first, tile_next, n_used = _plan(cnt[:, 0], n_tiles)
        slot_t = _slots(group_start, idx_t, rank_t)
        xs = _sc_scatter_rows(xw, slot_t, n_tiles * EXPERT_TILE)
        routed.append((h, wt, slot_t.reshape(-1), xs, tile_expert, tile_first, tile_next, n_used, row0))

    gathered = []
    for h, wt, slot_flat, xs, tile_expert, tile_first, tile_next, n_used, row0 in routed:
        ys = _experts(tile_expert, tile_first, tile_next, n_used, xs, w_gu[0], w_dn[0],
                      bg_e, bu_e, bd_e)
        y4 = _sc_gather_rows(ys, slot_flat).reshape(TOP_K, h.shape[0], d // 2)
        gathered.append((y4, h, wt, row0))

    out = None
    for y4, h, wt, row0 in gathered:
        out = _combine(y4, h, wt, mod4, row(final_norm_w), seq, row0, n, out)
    return out.reshape(bsz, seq, d)
```

```python
import functools

import jax
import jax.numpy as jnp
from jax import lax
from jax.experimental import pallas as pl
from jax.experimental.pallas import tpu as pltpu
from jax.experimental.pallas import tpu_sc as plsc

D_MODEL = 1024
GLA_HEADS = 4
GLA_DK = 64
GLA_DV = 128
GLA_QK = GLA_HEADS * GLA_DK
GLA_WIDTH = GLA_HEADS * GLA_DV
CONV_WIDTH = 512
GATE_RANK = 16
GATE_PAD = 128
GATE_NORMALIZER = 16.0
CHUNK = 64
GRID_W = 64
N_EXPERTS = 32
TOP_K = 4
SWIGLU_LIMIT = 7.0
SWIGLU_ALPHA = 1.702
EPS = 1e-6

TOKEN_TILE = 1024
POST_TILE = 1024
PREFIX_BLOCK = 256
PROJ_SUB = 1024
PROJ_COLS = 1024
INTRA_GROUP = 256
EXPERT_TILE = 512
COMBINE_TILE = 1024
VMEM_LIMIT = 56 * 1024 * 1024

F32 = jnp.float32
BF16 = jnp.bfloat16
HI_MASK = 0xFFFF0000
LOG2_E = 1.4426950408889634


def _cparams(*sem):
    return pltpu.CompilerParams(dimension_semantics=sem, vmem_limit_bytes=VMEM_LIMIT)


def _mod_spec(d, component, batch_of_step):
    return pl.BlockSpec((None, None, 1, d), lambda i: (batch_of_step(i), component, 0, 0))


def _rms(x):
    return x * lax.rsqrt(jnp.mean(x * x, axis=-1, keepdims=True) + EPS)


def _sigmoid(x, scale=1.0):
    return 1.0 / (1.0 + jnp.exp2(x * (-scale * LOG2_E)))


def _log_sigmoid(z):
    return jnp.minimum(z, 0.0) - jnp.log(1.0 + jnp.exp(-jnp.abs(z)))


def _dot(a, b):
    return jnp.dot(a, b, preferred_element_type=F32)


def _dot_nt(a, b):
    return lax.dot_general(a, b, (((1,), (1,)), ((), ())), preferred_element_type=F32)


def _dot_tn(a, b):
    return lax.dot_general(a, b, (((0,), (0,)), ((), ())), preferred_element_type=F32)


def _split_bf16(a):
    hi = a.astype(BF16)
    lo = (a - hi.astype(F32)).astype(BF16)
    return hi, lo


def _dot3(a, b, dot):
    a_hi, a_lo = _split_bf16(a)
    b_hi, b_lo = _split_bf16(b)
    return dot(a_hi, b_hi) + (dot(a_hi, b_lo) + dot(a_lo, b_hi))


def _segment_scan(x, seg, reverse, after_step=None):
    n = x.shape[0]
    row = lax.broadcasted_iota(jnp.int32, x.shape, 0) & (seg - 1)
    s = 1
    while s < seg:
        if reverse:
            shifted = pltpu.roll(x, n - s, 0)
            x = x + jnp.where(row < seg - s, shifted, 0.0)
        else:
            shifted = pltpu.roll(x, s, 0)
            x = x + jnp.where(row >= s, shifted, 0.0)
        if after_step is not None:
            after_step()
        s *= 2
    return x


def _pack_halves(x):
    n = x.shape[1] // 2
    bits = pltpu.bitcast(x.astype(BF16).astype(F32), jnp.uint32)
    return (bits[:, :n] & jnp.uint32(HI_MASK)) | (bits[:, n:] >> 16)


def _unpack_halves(w):
    hi = pltpu.bitcast(w & jnp.uint32(HI_MASK), F32)
    lo = pltpu.bitcast(w << 16, F32)
    return jnp.concatenate([hi, lo], axis=1)


def _mod_kernel(c_ref, w_ref, b_ref, o_ref):
    c = c_ref[...]
    s = c * _sigmoid(c)
    o_ref[...] = _dot3(s, w_ref[...], _dot) + b_ref[...]


def _modulation(cc, w_ada, b_ada):
    rows, d = cc.shape
    n = w_ada.shape[1]
    bn = 1536
    return pl.pallas_call(
        _mod_kernel,
        out_shape=jax.ShapeDtypeStruct((rows, n), F32),
        grid=(n // bn,),
        in_specs=[pl.BlockSpec((rows, d), lambda j: (0, 0)),
                  pl.BlockSpec((d, bn), lambda j: (0, j)),
                  pl.BlockSpec((1, bn), lambda j: (0, j))],
        out_specs=pl.BlockSpec((rows, bn), lambda j: (0, j)),
        compiler_params=_cparams("arbitrary"),
        name="modulation",
    )(cc, w_ada, b_ada)


def _ctx_kernel(x_ref, sh_ref, sc_ref, nw_ref, wkv_ref, wg_ref, afb_ref, bias_ref, sf_ref, sb_ref):
    nb, n, d = x_ref.shape
    x = x_ref[...].reshape(nb * n, d)
    hn = _rms(x) * nw_ref[...] * (1.0 + sc_ref[...]) + sh_ref[...]
    hb = hn.astype(BF16)
    kv = _dot(hb, wkv_ref[...])
    g = _dot(hb, wg_ref[...])
    z = _dot(g.astype(BF16), afb_ref[...]) + bias_ref[...]
    la = _log_sigmoid(z) * (1.0 / GATE_NORMALIZER)
    p = _segment_scan(la, n, False)
    p_f, p_b = p[:, :GLA_QK], p[:, GLA_QK:]
    p_f3 = p_f.reshape(nb, n, GLA_QK)
    w_f = jnp.exp(p_f3[:, n - 1:n, :] - p_f3).reshape(nb * n, GLA_QK)
    w_b = jnp.exp(p_b - la[:, GLA_QK:])
    k = kv[:, :GLA_QK]
    vb = kv[:, GLA_QK:].astype(BF16)
    ke_f = (k * w_f).astype(BF16)
    ke_b = (k * w_b).astype(BF16)
    for b in range(nb):
        rows = slice(b * n, (b + 1) * n)
        for h in range(GLA_HEADS):
            vh = vb[rows, h * GLA_DV:(h + 1) * GLA_DV]
            sf_ref[b, h] = _dot_tn(ke_f[rows, h * GLA_DK:(h + 1) * GLA_DK], vh)
            sb_ref[b, h] = _dot_tn(ke_b[rows, h * GLA_DK:(h + 1) * GLA_DK], vh)


def _context_states(ctx, mod4, nw1, wkv, wg, afb, bias_fb):
    bsz, n, d = ctx.shape
    ctx_row = lambda b: bsz
    full = lambda a: pl.BlockSpec(a.shape, lambda b: (0,) * a.ndim)
    st = jax.ShapeDtypeStruct((bsz, GLA_HEADS, GLA_DK, GLA_DV), F32)
    nb = 4 if bsz % 4 == 0 else 1
    st_spec = pl.BlockSpec((nb, GLA_HEADS, GLA_DK, GLA_DV), lambda b: (b, 0, 0, 0))
    return pl.pallas_call(
        _ctx_kernel,
        out_shape=(st, st),
        grid=(bsz // nb,),
        in_specs=[pl.BlockSpec((nb, n, d), lambda b: (b, 0, 0)),
                  _mod_spec(d, 0, ctx_row), _mod_spec(d, 1, ctx_row), full(nw1), full(wkv), full(wg),
                  full(afb), full(bias_fb)],
        out_specs=(st_spec, st_spec),
        compiler_params=_cparams("arbitrary"),
        name="context_states",
    )(ctx, mod4, mod4, nw1, wkv, wg, afb, bias_fb)


def _chunk_scan(st_ref, qd, ke, v, dec, descending):
    nc = qd.shape[0] // CHUNK
    order = range(nc - 1, -1, -1) if descending else range(nc)
    rows = [slice(c * CHUNK, (c + 1) * CHUNK) for c in range(nc)]
    dec_t = jnp.concatenate([dec, jnp.zeros((GATE_PAD - nc, dec.shape[1]), F32)], axis=0).T
    outs = []
    for h in range(GLA_HEADS):
        lanes = slice(h * GLA_DK, (h + 1) * GLA_DK)
        vl = slice(h * GLA_DV, (h + 1) * GLA_DV)
        kv = [_dot_tn(ke[rows[c], lanes], v[rows[c], vl]) for c in range(nc)]
        s = st_ref[h]
        start = [None] * nc
        for c in order:
            start[c] = s
            s = s * dec_t[lanes, c:c + 1] + kv[c]
        st_ref[h] = s
        outs.append(jnp.concatenate(
            [_dot(qd[rows[c], lanes], start[c].astype(BF16)) for c in range(nc)], axis=0))
    return jnp.concatenate(outs, axis=1)


def _proj_kernel(tpb, x_ref, sh_ref, sc_ref, nw_ref, wm_ref, wg_ref, afb_ref, bias_ref, cw_ref,
                 s0_ref, ab_ref, v_ref, oi_ref, sr_ref, yc_ref, dec_ref, p_scr, st_ref):
    @pl.when(pl.program_id(0) % tpb == 0)
    def _():
        st_ref[...] = s0_ref[...]

    subs = range(x_ref.shape[0] // PROJ_SUB)
    nc = PROJ_SUB // CHUNK
    nblk = wm_ref.shape[1] // PROJ_COLS
    factors = []
    for sub in subs:
        x = x_ref[sub * PROJ_SUB:(sub + 1) * PROJ_SUB, :]
        hn = _rms(x) * nw_ref[...] * (1.0 + sc_ref[...]) + sh_ref[...]
        hb = hn.astype(BF16)
        g = _dot(hb, wg_ref[...])
        z = _dot(g.astype(BF16), afb_ref[...]) + bias_ref[...]

        todo = iter(range(nblk))

        def wide_block():
            j = next(todo, None)
            if j is not None:
                cols = slice(j * PROJ_COLS, (j + 1) * PROJ_COLS)
                p_scr[sub, :, cols] = _dot(hb, wm_ref[:, cols])

        la = _log_sigmoid(z) * (1.0 / GATE_NORMALIZER)
        wide_block()
        b_f = _segment_scan(la[:, :GLA_QK], CHUNK, False, wide_block)
        b_b = _segment_scan(la[:, GLA_QK:], CHUNK, True, wide_block)
        b_f3 = b_f.reshape(nc, CHUNK, GLA_QK)
        tot_f = b_f3[:, CHUNK - 1:CHUNK, :]
        b_b3 = b_b.reshape(nc, CHUNK, GLA_QK)
        tot_b = b_b3[:, 0:1, :]
        fac = []
        for arg in (b_f, -b_f, (tot_f - b_f3).reshape(PROJ_SUB, GLA_QK),
                    b_b, -b_b, (tot_b - b_b3).reshape(PROJ_SUB, GLA_QK)):
            fac.append(jnp.exp(arg))
            wide_block()
        fac += [jnp.exp(tot_f).reshape(nc, GLA_QK), jnp.exp(tot_b).reshape(nc, GLA_QK)]
        factors.append(fac)
        for _ in range(nblk):
            wide_block()
    for sub in subs:
        _proj_rows(sub, factors[sub], cw_ref, ab_ref, v_ref, oi_ref, sr_ref, yc_ref,
                   dec_ref, p_scr.at[sub], st_ref)


def _proj_rows(sub, factors, cw_ref, ab_ref, v_ref, oi_ref, sr_ref, yc_ref, dec_ref, p, st_ref):
    tm = PROJ_SUB
    nc = tm // CHUNK
    rs = slice(sub * tm, (sub + 1) * tm)
    cs = slice(sub * nc, (sub + 1) * nc)
    e_f, ei_f, ee_f, e_b, ei_b, ee_b, dec_f, dec_b = factors

    q = p[:, 0:GLA_QK] * (GLA_DK ** -0.5)
    k = p[:, GLA_QK:2 * GLA_QK]
    vb = p[:, 2 * GLA_QK:2 * GLA_QK + GLA_WIDTH].astype(BF16)
    v_ref[rs, :] = vb
    qd_f = (q * e_f).astype(BF16)
    ki_f = (k * ei_f).astype(BF16)
    ke_f = (k * ee_f).astype(BF16)
    qd_b = (q * e_b).astype(BF16)
    ki_b = (k * ei_b).astype(BF16)
    ke_b = (k * ee_b).astype(BF16)

    ab_ref[rs, :GLA_QK] = qd_b
    ab_ref[rs, GLA_QK:] = ke_b
    dec_ref[cs, :GLA_QK] = dec_f
    dec_ref[cs, GLA_QK:] = dec_b

    inter_f = _chunk_scan(st_ref, qd_f, ke_f, vb, dec_f, False)

    gi = INTRA_GROUP
    ri = lax.broadcasted_iota(jnp.int32, (gi, gi), 0)
    ci = lax.broadcasted_iota(jnp.int32, (gi, gi), 1)
    same = (ri // CHUNK) == (ci // CHUNK)
    lower = same & (ci <= ri)
    upper = same & (ci >= ri)
    for gidx in range(tm // gi):
        rows = slice(gidx * gi, (gidx + 1) * gi)
        orows = slice(sub * tm + gidx * gi, sub * tm + (gidx + 1) * gi)
        for h in range(GLA_HEADS):
            lanes = slice(h * GLA_DK, (h + 1) * GLA_DK)
            s_f = _dot_nt(qd_f[rows, lanes], ki_f[rows, lanes])
            s_b = _dot_nt(qd_b[rows, lanes], ki_b[rows, lanes])
            s = jnp.where(lower, s_f, 0.0) + jnp.where(upper, s_b, 0.0)
            vl = slice(h * GLA_DV, (h + 1) * GLA_DV)
            oi_ref[orows, vl] = _dot(s.astype(BF16), vb[rows, vl]) + inter_f[rows, vl]

    r = p[:, 1024:1536]
    sr_ref[rs, :] = (r * _sigmoid(r)).astype(BF16)

    u = p[:, 2048:2560] * p[:, 2560:3072]
    row = lax.broadcasted_iota(jnp.int32, u.shape, 0) & (GRID_W - 1)
    prev = jnp.where(row >= 1, pltpu.roll(u, 1, 0), 0.0)
    nxt = jnp.where(row < GRID_W - 1, pltpu.roll(u, tm - 1, 0), 0.0)
    cw = cw_ref[...]
    conv = prev * cw[0:1, :] + u * cw[1:2, :] + nxt * cw[2:3, :]
    yc_ref[rs, :] = (p[:, 1536:2048] * conv).astype(BF16)


def _projection(x2, row0, n, mod4, nw1, wm, wg, afb, bias_fb, conv_w, s0f, seq):
    d = x2.shape[1]
    tm = TOKEN_TILE
    tpb = seq // tm
    t0 = row0 // tm
    batch = lambda i: i // tpb + row0 // seq
    full = lambda a: pl.BlockSpec(a.shape, lambda i: (0,) * a.ndim)
    tok = lambda w: pl.BlockSpec((tm, w), lambda i: (i, 0))
    st_spec = pl.BlockSpec((None, GLA_HEADS, GLA_DK, GLA_DV), lambda i: (batch(i), 0, 0, 0))
    out_shape = (
        jax.ShapeDtypeStruct((n, 2 * GLA_QK), BF16),
        jax.ShapeDtypeStruct((n, GLA_WIDTH), BF16),
        jax.ShapeDtypeStruct((n, GLA_WIDTH), F32),
        jax.ShapeDtypeStruct((n, GLA_WIDTH), BF16),
        jax.ShapeDtypeStruct((n, CONV_WIDTH), BF16),
        jax.ShapeDtypeStruct((n // CHUNK, 2 * GLA_QK), F32),
    )
    out_specs = (tok(2 * GLA_QK), tok(GLA_WIDTH), tok(GLA_WIDTH), tok(GLA_WIDTH),
                 tok(CONV_WIDTH), pl.BlockSpec((tm // CHUNK, 2 * GLA_QK), lambda i: (i, 0)))
    return pl.pallas_call(
        functools.partial(_proj_kernel, tpb),
        out_shape=out_shape,
        grid=(n // tm,),
        in_specs=[pl.BlockSpec((tm, d), lambda i: (i + t0, 0)), _mod_spec(d, 0, batch),
                  _mod_spec(d, 1, batch), full(nw1),
                  full(wm), full(wg), full(afb), full(bias_fb), full(conv_w), st_spec],
        out_specs=out_specs,
        scratch_shapes=[pltpu.VMEM((tm // PROJ_SUB, PROJ_SUB, wm.shape[1]), F32),
                        pltpu.VMEM((GLA_HEADS, GLA_DK, GLA_DV), F32)],
        compiler_params=_cparams("arbitrary"),
        name="projection",
    )(x2, mod4, mod4, nw1, wm, wg, afb, bias_fb, conv_w, s0f)


def _post_kernel(tpb, x_ref, oi_ref, ab_ref, v_ref, dec_ref, s0_ref, sr_ref, yc_ref, gnw_ref, wo_ref,
                 g1_ref, nw2_ref, sh2_ref, sc2_ref, rwt_ref, rb_ref, tri_ref,
                 h_ref, xw_ref, idx_ref, rank_ref, wt_ref, cnt_ref, run_ref, st_ref):
    @pl.when(pl.program_id(0) == 0)
    def _():
        run_ref[...] = jnp.zeros_like(run_ref)

    @pl.when(pl.program_id(0) % tpb == 0)
    def _():
        st_ref[...] = s0_ref[...]

    tm = x_ref.shape[0]
    inter_b = _chunk_scan(st_ref, ab_ref[:, :GLA_QK], ab_ref[:, GLA_QK:], v_ref[...],
                          dec_ref[:, GLA_QK:], True)
    o = oi_ref[...] + inter_b
    normed = jnp.concatenate(
        [_rms(o[:, h * GLA_DV:(h + 1) * GLA_DV]) for h in range(GLA_HEADS)], axis=1)
    y_gla = (normed * gnw_ref[...] * sr_ref[...].astype(F32)).astype(BF16)
    wo = wo_ref[...]
    y = _dot(y_gla, wo[:GLA_WIDTH, :]) + _dot(yc_ref[...], wo[GLA_WIDTH:, :])
    h = x_ref[...] + g1_ref[...] * y
    h_ref[...] = h
    hn = _rms(h) * nw2_ref[...] * (1.0 + sc2_ref[...]) + sh2_ref[...]
    xw_ref[...] = _pack_halves(hn)

    rw_hi, rw_lo = _split_bf16(rwt_ref[...])
    hn_hi, hn_lo = _split_bf16(hn)
    both = _dot_nt(jnp.concatenate([rw_hi, rw_lo], axis=0), hn_hi)
    logits = (both[:N_EXPERTS] + (both[N_EXPERTS:] + _dot_nt(rw_hi, hn_lo))) + rb_ref[...]
    eid = lax.broadcasted_iota(jnp.int32, logits.shape, 0)
    vals, idxs = [], []
    work = logits
    for _ in range(TOP_K):
        m = jnp.max(work, axis=0, keepdims=True)
        i = jnp.min(jnp.where(work == m, eid, N_EXPERTS), axis=0, keepdims=True)
        vals.append(m)
        idxs.append(i)
        work = jnp.where(eid == i, -jnp.inf, work)
    exps = [jnp.exp(v - vals[0]) for v in vals]
    denom = exps[0] + exps[1] + exps[2] + exps[3]
    wts = [e / denom for e in exps]

    onehot = jnp.zeros(logits.shape, F32)
    for i in idxs:
        onehot = onehot + jnp.where(eid == i, 1.0, 0.0)
    pb = tri_ref.shape[0]
    seen = run_ref[:, 0:1]
    parts = []
    for blk in range(tm // pb):
        cols = slice(blk * pb, (blk + 1) * pb)
        parts.append(_dot(onehot[:, cols].astype(BF16), tri_ref[...]) + seen)
        seen = seen + jnp.sum(onehot[:, cols], axis=1, keepdims=True)
    before = jnp.concatenate(parts, axis=1)
    for kk in range(TOP_K):
        idx_ref[kk:kk + 1, :] = idxs[kk]
        rk = jnp.sum(jnp.where(eid == idxs[kk], before, 0.0), axis=0, keepdims=True)
        rank_ref[kk:kk + 1, :] = rk.astype(jnp.int32)
    run = run_ref[...] + jnp.sum(onehot, axis=1, keepdims=True)
    run_ref[...] = run
    cnt_ref[...] = run.astype(jnp.int32)

    sub = lax.broadcasted_iota(jnp.int32, (GATE_PAD, tm), 0)
    wpad = jnp.zeros((GATE_PAD, tm), F32)
    for kk in range(TOP_K):
        wpad = wpad + jnp.where(sub == kk, wts[kk], 0.0)
    wt_ref[...] = wpad.T


def _post_mixer(x2, row0, o_i, a_b, vb, dec, s0b, sr, yc, gnw, wo, mod4, nw2, rwt, rb, tri, seq):
    n, d = o_i.shape[0], x2.shape[1]
    tm = POST_TILE
    tpb = seq // tm
    t0 = row0 // tm
    rev = lambda i: (i // tpb) * tpb + (tpb - 1 - i % tpb)
    batch = lambda i: i // tpb + row0 // seq
    full = lambda a: pl.BlockSpec(a.shape, lambda i: (0,) * a.ndim)
    tok = lambda w: pl.BlockSpec((tm, w), lambda i: (rev(i), 0))
    lane_tok = pl.BlockSpec((TOP_K, tm), lambda i: (0, rev(i)))
    st_spec = pl.BlockSpec((None, GLA_HEADS, GLA_DK, GLA_DV), lambda i: (batch(i), 0, 0, 0))
    dec_spec = pl.BlockSpec((tm // CHUNK, 2 * GLA_QK), lambda i: (rev(i), 0))
    out_shape = (
        jax.ShapeDtypeStruct((n, d), F32),
        jax.ShapeDtypeStruct((n, d // 2), jnp.uint32),
        jax.ShapeDtypeStruct((TOP_K, n), jnp.int32),
        jax.ShapeDtypeStruct((TOP_K, n), jnp.int32),
        jax.ShapeDtypeStruct((n, GATE_PAD), F32),
        jax.ShapeDtypeStruct((N_EXPERTS, GATE_PAD), jnp.int32),
    )
    out_specs = (tok(d), tok(d // 2), lane_tok, lane_tok, tok(GATE_PAD),
                 pl.BlockSpec((N_EXPERTS, GATE_PAD), lambda i: (0, 0)))
    return pl.pallas_call(
        functools.partial(_post_kernel, tpb),
        out_shape=out_shape,
        grid=(n // tm,),
        in_specs=[pl.BlockSpec((tm, d), lambda i: (rev(i) + t0, 0)), tok(GLA_WIDTH),
                  tok(2 * GLA_QK), tok(GLA_WIDTH), dec_spec, st_spec, tok(GLA_WIDTH),
                  tok(CONV_WIDTH), full(gnw), full(wo), _mod_spec(d, 2, batch), full(nw2),
                  _mod_spec(d, 3, batch), _mod_spec(d, 4, batch),
                  full(rwt), full(rb), full(tri)],
        out_specs=out_specs,
        scratch_shapes=[pltpu.VMEM((N_EXPERTS, GATE_PAD), F32),
                        pltpu.VMEM((GLA_HEADS, GLA_DK, GLA_DV), F32)],
        compiler_params=_cparams("arbitrary"),
        name="post_mixer",
    )(x2, o_i, a_b, vb, dec, s0b, sr, yc, gnw, wo, mod4, nw2, mod4, mod4, rwt, rb, tri)


def _plan_kernel(cnt_ref, start_ref, te_ref, first_ref, next_ref, nu_ref):
    n_tiles = te_ref.shape[0]
    shift = EXPERT_TILE.bit_length() - 1

    def clear(t, carry):
        first_ref[t] = 0
        next_ref[t] = -1
        return carry

    lax.fori_loop(0, n_tiles, clear, 0)

    def group(e, tile0):
        nt = lax.shift_right_logical(cnt_ref[e] + (EXPERT_TILE - 1), shift)
        start_ref[e] = tile0 * EXPERT_TILE

        def fill(t, carry):
            te_ref[tile0 + t] = e
            return carry

        lax.fori_loop(0, nt, fill, 0)

        @pl.when(nt > 0)
        def _():
            first_ref[tile0] = 1

        return tile0 + nt

    n_used = lax.fori_loop(0, N_EXPERTS, group, 0)
    nu_ref[0] = n_used
    last = te_ref[n_used - 1]

    def tail(t, carry):
        te_ref[t] = last
        return carry

    lax.fori_loop(n_used, n_tiles, tail, 0)

    def link(k, nxt):
        e = N_EXPERTS - 1 - k
        has = cnt_ref[e] > 0

        @pl.when(has)
        def _():
            next_ref[lax.shift_right_logical(start_ref[e], shift)] = nxt

        return jnp.where(has, e, nxt)

    lax.fori_loop(0, N_EXPERTS, link, -1)


def _plan(counts, n_tiles):
    assert EXPERT_TILE & (EXPERT_TILE - 1) == 0
    smem = pl.BlockSpec(memory_space=pltpu.SMEM)
    vec = lambda m: jax.ShapeDtypeStruct((m,), jnp.int32)
    return pl.pallas_call(
        _plan_kernel,
        out_shape=(vec(N_EXPERTS), vec(n_tiles), vec(n_tiles), vec(n_tiles), vec(1)),
        in_specs=[smem],
        out_specs=(smem, smem, smem, smem, smem),
        name="plan",
    )(counts)


def _slot_kernel(start_ref, idx_ref, rank_ref, slot_ref):
    idx = idx_ref[...]
    base = jnp.zeros(idx.shape, jnp.int32)
    for e in range(N_EXPERTS):
        base = jnp.where(idx == e, start_ref[e], base)
    slot_ref[...] = base + rank_ref[...]


def _slots(group_start, idx_t, rank_t):
    k, n = idx_t.shape
    bn = min(n, 8192)
    blk = pl.BlockSpec((k, bn), lambda i, s: (0, i))
    return pl.pallas_call(
        _slot_kernel,
        out_shape=jax.ShapeDtypeStruct((k, n), jnp.int32),
        grid_spec=pltpu.PrefetchScalarGridSpec(
            num_scalar_prefetch=1, grid=(n // bn,), in_specs=[blk, blk], out_specs=blk),
        compiler_params=_cparams("arbitrary"),
        name="slots",
    )(group_start, idx_t, rank_t)


SC_CORES = 2
SC_SUBCORES = 16
SC_WORKERS = SC_CORES * SC_SUBCORES
SCATTER_CHUNK = 128
GATHER_CHUNK = 64


def _sc_mesh():
    return plsc.VectorSubcoreMesh(core_axis_name="c", subcore_axis_name="s")


def _sc_worker():
    return lax.axis_index("s") * SC_CORES + lax.axis_index("c")


def _sc_scatter_rows(src, idx_t, n_out):
    n, w = src.shape
    k = idx_t.shape[0]
    c = SCATTER_CHUNK
    per = n // SC_WORKERS
    nch = per // c
    assert per * SC_WORKERS == n and nch * c == per

    def body(src_hbm, idx_hbm, out_hbm, idx_v, rows_v):
        wid = _sc_worker()
        for kk in range(k):
            pltpu.sync_copy(idx_hbm.at[kk, wid], idx_v.at[kk])

        @pl.loop(0, nch)
        def _(j):
            pltpu.sync_copy(src_hbm.at[pl.ds(wid * per + j * c, c)], rows_v)
            for kk in range(k):
                pltpu.sync_copy(rows_v, out_hbm.at[idx_v.at[kk, j]])

    return pl.kernel(
        body,
        out_type=jax.ShapeDtypeStruct((n_out, w), src.dtype),
        mesh=_sc_mesh(),
        scratch_types=[pltpu.VMEM((k, nch, c), jnp.int32), pltpu.VMEM((c, w), src.dtype)],
        name="sc_dispatch",
    )(src, idx_t.reshape(k, SC_WORKERS, nch, c))


def _sc_gather_rows(table, idx):
    w = table.shape[1]
    m = idx.shape[0]
    c = GATHER_CHUNK
    per = m // SC_WORKERS
    nch = per // c
    assert per * SC_WORKERS == m and nch * c == per and nch % 2 == 0

    def body(table_hbm, idx_hbm, out_hbm, idx_v, rows_v, sem0, sem1):
        wid = _sc_worker()
        base = wid * per
        sems = (sem0, sem1)
        pltpu.sync_copy(idx_hbm.at[wid], idx_v)

        def gather(j, b):
            return pltpu.make_async_copy(table_hbm.at[idx_v.at[j]], rows_v.at[b], sems[b])

        gather(0, 0).start()

        @pl.loop(0, nch, step=2)
        def _(g):
            for b in range(2):
                j = g + b
                gather(j, b).wait()

                @pl.when(j + 1 < nch)
                def _():
                    gather(j + 1, 1 - b).start()

                pltpu.sync_copy(rows_v.at[b], out_hbm.at[pl.ds(base + j * c, c)])

    return pl.kernel(
        body,
        out_type=jax.ShapeDtypeStruct((m, w), table.dtype),
        mesh=_sc_mesh(),
        scratch_types=[pltpu.VMEM((nch, c), jnp.int32), pltpu.VMEM((2, c, w), table.dtype),
                       pltpu.SemaphoreType.DMA, pltpu.SemaphoreType.DMA],
        name="sc_combine_gather",
    )(table, idx.reshape(SC_WORKERS, nch, c))


PERM_BLOCK = 256


def _expert_kernel(te_ref, first_ref, next_ref, nu_ref, x_ref, wgu_hbm, wdn_hbm, perm_ref, bg_ref,
                   bu_ref, bd_ref, y_ref, wgu_f, wdn_f, wg_s, wu_s, wd_s, sem):
    i = pl.program_id(0)
    used = i < nu_ref[0]

    def fetch(e):
        return (pltpu.make_async_copy(wgu_hbm.at[e], wgu_f, sem.at[0]),
                pltpu.make_async_copy(wdn_hbm.at[e], wdn_f, sem.at[1]))

    @pl.when(i == 0)
    def _():
        for cp in fetch(te_ref[0]):
            cp.start()

    @pl.when(used & (first_ref[i] == 1))
    def _():
        for cp in fetch(te_ref[i]):
            cp.wait()
        half = PERM_BLOCK // 2
        perm = perm_ref[...]
        for b in range(wgu_f.shape[1] // PERM_BLOCK):
            blk = wgu_f[:, b * PERM_BLOCK:(b + 1) * PERM_BLOCK].astype(BF16)
            t = _dot(blk, perm)
            wg_s[:, b * half:(b + 1) * half] = t[:, :half].astype(BF16)
            wu_s[:, b * half:(b + 1) * half] = t[:, half:].astype(BF16)
        wd_s[...] = wdn_f[...].astype(BF16)

        @pl.when(next_ref[i] >= 0)
        def _():
            for cp in fetch(next_ref[i]):
                cp.start()

    @pl.when(used)
    def _():
        x = _unpack_halves(x_ref[...]).astype(BF16)
        g = _dot(x, wg_s[...]) + bg_ref[...]
        u = _dot(x, wu_s[...]) + bu_ref[...]
        gate = jnp.minimum(g, SWIGLU_LIMIT)
        up = jnp.clip(u, -SWIGLU_LIMIT, SWIGLU_LIMIT)
        act = (up + 1.0) * (gate * _sigmoid(gate, SWIGLU_ALPHA))
        y = _dot(act.astype(BF16), wd_s[...]) + bd_ref[...]
        y_ref[...] = _pack_halves(y)

    @pl.when(jnp.logical_not(used))
    def _():
        y_ref[...] = jnp.zeros_like(y_ref)


def _experts(tile_expert, tile_first, tile_next, n_used, xs, w_gu, w_dn, bg, bu, bd):
    n_slots, half = xs.shape
    tm = EXPERT_TILE
    n_tiles = n_slots // tm
    dm, f, d = w_gu.shape[1], w_dn.shape[1], w_dn.shape[2]
    ph = PERM_BLOCK // 2
    j = jnp.arange(PERM_BLOCK)
    perm = (jnp.arange(PERM_BLOCK)[None, :] == (j // 2 + (j % 2) * ph)[:, None]).astype(BF16)
    row_in = pl.BlockSpec((tm, half), lambda i, te, tf, tn, nu: (jnp.minimum(i, nu[0] - 1), 0))
    row_out = pl.BlockSpec((tm, half),
                           lambda i, te, tf, tn, nu: (jnp.where(i < nu[0], i, n_tiles - 1), 0))
    bspec = lambda a: pl.BlockSpec((None,) + a.shape[1:], lambda i, te, tf, tn, nu: (te[i], 0, 0))
    return pl.pallas_call(
        _expert_kernel,
        out_shape=jax.ShapeDtypeStruct((n_slots, half), jnp.uint32),
        grid_spec=pltpu.PrefetchScalarGridSpec(
            num_scalar_prefetch=4, grid=(n_tiles,),
            in_specs=[row_in, pl.BlockSpec(memory_space=pl.ANY), pl.BlockSpec(memory_space=pl.ANY),
                      pl.BlockSpec((PERM_BLOCK, PERM_BLOCK), lambda i, te, tf, tn, nu: (0, 0)),
                      bspec(bg), bspec(bu), bspec(bd)],
            out_specs=row_out,
            scratch_shapes=[pltpu.VMEM((dm, 2 * f), F32), pltpu.VMEM((f, d), F32),
                            pltpu.VMEM((dm, f), BF16), pltpu.VMEM((dm, f), BF16),
                            pltpu.VMEM((f, d), BF16), pltpu.SemaphoreType.DMA((2,))]),
        compiler_params=_cparams("arbitrary"),
        name="experts",
    )(tile_expert, tile_first, tile_next, n_used, xs, w_gu, w_dn, perm, bg, bu, bd)


def _combine_kernel(y4_ref, h_ref, wt_ref, g2_ref, fnw_ref, *rest):
    o_ref = rest[-1]
    wt = wt_ref[...]
    acc = jnp.zeros(h_ref.shape, F32)
    for kk in range(TOP_K):
        acc = acc + wt[:, kk:kk + 1] * _unpack_halves(y4_ref[kk])
    h = h_ref[...] + g2_ref[...] * acc
    o_ref[...] = _rms(h) * fnw_ref[...]


def _combine(y4, h, wt, mod4, fnw, seq, row0, n_total, prev_out):
    n, d = h.shape
    tt = COMBINE_TILE
    tpb = seq // tt
    t0 = row0 // tt
    in_specs = [pl.BlockSpec((TOP_K, tt, d // 2), lambda i: (0, i, 0)),
                pl.BlockSpec((tt, d), lambda i: (i, 0)),
                pl.BlockSpec((tt, GATE_PAD), lambda i: (i, 0)),
                _mod_spec(d, 5, lambda i: i // tpb + row0 // seq),
                pl.BlockSpec((1, d), lambda i: (0, 0))]
    args = [y4, h, wt, mod4, fnw]
    aliases = {}
    if prev_out is not None:
        in_specs.append(pl.BlockSpec(memory_space=pl.ANY))
        args.append(prev_out)
        aliases = {len(args) - 1: 0}
    return pl.pallas_call(
        _combine_kernel,
        out_shape=jax.ShapeDtypeStruct((n_total, d), F32),
        grid=(n // tt,),
        in_specs=in_specs,
        out_specs=pl.BlockSpec((tt, d), lambda i: (i + t0, 0)),
        input_output_aliases=aliases,
        compiler_params=_cparams("arbitrary"),
        name="combine",
    )(*args)


def kernel(x, c, ctx, c_ctx, w_ada, b_ada, norm1_w, w_in, a_up_f, a_bias_f, a_up_b, a_bias_b,
           gla_norm_w, conv_w, w_out, norm2_w, router_w, router_b, w_gu, b_gu, w_dn, b_dn,
           final_norm_w):
    bsz, seq, d = x.shape
    n = bsz * seq
    assert w_ada.shape[0] == 1 and d == D_MODEL
    assert seq % TOKEN_TILE == 0 and seq % POST_TILE == 0 and seq % GRID_W == 0
    assert ctx.shape[1] & (ctx.shape[1] - 1) == 0

    w_in0 = w_in[0]
    v_end = 2 * GLA_QK + GLA_WIDTH
    r_end = v_end + GLA_WIDTH
    g_end = r_end + 2 * GATE_RANK
    wm = jnp.concatenate([w_in0[:, :r_end], w_in0[:, g_end:]], axis=1).astype(BF16)
    wg_low = jnp.pad(w_in0[:, r_end:g_end], ((0, 0), (0, GATE_PAD - 2 * GATE_RANK))).astype(BF16)
    afb = jnp.zeros((GATE_PAD, 2 * GLA_QK), F32)
    afb = afb.at[:GATE_RANK, :GLA_QK].set(a_up_f[0]).at[GATE_RANK:2 * GATE_RANK, GLA_QK:].set(a_up_b[0])
    afb = afb.astype(BF16)
    bias_fb = jnp.concatenate([a_bias_f[0], a_bias_b[0]])[None]
    wkv = wm[:, GLA_QK:v_end]
    row = lambda a: a.reshape(1, -1)

    mod = _modulation(jnp.concatenate([c, c_ctx[None]], axis=0), w_ada[0], row(b_ada[0]))
    mod4 = mod.reshape(bsz + 1, 6, 1, d)

    s0f, s0b = _context_states(ctx, mod4, row(norm1_w[0]), wkv, wg_low, afb, bias_fb)

    x2 = x.reshape(n, d)
    tri = jnp.triu(jnp.ones((PREFIX_BLOCK, PREFIX_BLOCK), BF16), k=1)
    gnw = jnp.tile(gla_norm_w[0], GLA_HEADS)[None]
    wo = w_out[0].astype(BF16)
    bg_e, bu_e, bd_e = b_gu[0][:, None, 0::2], b_gu[0][:, None, 1::2], b_dn[0][:, None, :]

    if bsz % 8 == 0:
        chunk_batches = (5 * bsz // 8, 3 * bsz // 8)
    elif bsz % 2 == 0:
        chunk_batches = (bsz // 2, bsz // 2)
    else:
        chunk_batches = (bsz,)
    routed = []
    b0 = 0
    for cb in chunk_batches:
        row0 = b0 * seq
        b0 += cb
        nc = cb * seq
        n_tiles = nc * TOP_K // EXPERT_TILE + N_EXPERTS
        a_b, vb, o_i, sr, yc, dec = _projection(
            x2, row0, nc, mod4, row(norm1_w[0]), wm, wg_low, afb, bias_fb, conv_w[0], s0f, seq)
        h, xw, idx_t, rank_t, wt, cnt = _post_mixer(
            x2, row0, o_i, a_b, vb, dec, s0b, sr, yc, gnw, wo, mod4, row(norm2_w[0]),
            router_w[0].T, router_b[0][:, None], tri, seq)

        group_start, tile_expert, tile_first, tile_next, n_used = _plan(cnt[:, 0], n_tiles)
        slot_t = _slots(group_start, idx_t, rank_t)
        xs = _sc_scatter_rows(xw, slot_t, n_tiles * EXPERT_TILE)
        routed.append((h, wt, slot_t.reshape(-1), xs, tile_expert, tile_first, tile_next, n_used, row0))

    gathered = []
    for h, wt, slot_flat, xs, tile_expert, tile_first, tile_next, n_used, row0 in routed:
        ys = _experts(tile_expert, tile_first, tile_next, n_used, xs, w_gu[0], w_dn[0],
                      bg_e, bu_e, bd_e)
        y4 = _sc_gather_rows(ys, slot_flat).reshape(TOP_K, h.shape[0], d // 2)
        gathered.append((y4, h, wt, row0))

    out = None
    for y4, h, wt, row0 in gathered:
        out = _combine(y4, h, wt, mod4, row(final_norm_w), seq, row0, n, out)
    return out.reshape(bsz, seq, d)
```
